```python
import math
import jax
import jax.numpy as jnp
from jax import lax
import numpy as np

D_MODEL = 1024
BATCH = 4
SEQ = 4096
DEPTH = 4
DEC_BATCH = 128
DEC_SEQ = 1
PAST_LEN = 8192
PAGE_SIZE = 128

N_MIXERS = 3
N_DN = (DEPTH + 2) // 3
N_DIFF = (DEPTH + 1) // 3
N_SWA = DEPTH // 3
ALPHA = (2 * DEPTH) ** 0.25
BETA = (8 * DEPTH) ** -0.25
LN_EPS = 1e-5
RMS_EPS = 1e-6
F32 = jnp.float32

DN_HK = 8
DN_HV = 16
DN_DK = 128
DN_DV = 128
DN_K_W = DN_HK * DN_DK
DN_V_W = DN_HV * DN_DV
DN_CONV_DIM = 2 * DN_K_W + DN_V_W
DN_CONV_W = 4
DN_CHUNK = 64
DN_IN = DN_CONV_DIM + DN_V_W + 2 * DN_HV

DIFF_H = 8
DIFF_KVH = 4
DIFF_G = DIFF_H // DIFF_KVH
DIFF_HD = 64
DIFF_VD = 2 * DIFF_HD
DIFF_Q_W = DIFF_H * 2 * DIFF_HD
DIFF_K_W = DIFF_KVH * 2 * DIFF_HD
DIFF_V_W = DIFF_KVH * DIFF_VD
DIFF_O_W = DIFF_H * DIFF_VD
DIFF_IN = DIFF_Q_W + DIFF_K_W + DIFF_V_W + DIFF_O_W
DIFF_QBLOCK = 128

WINDOW = 128
SWA_BLOCK = WINDOW
SWA_H = 16
SWA_KVH = 2
SWA_G = SWA_H // SWA_KVH
SWA_HD = 64
SWA_Q_W = SWA_H * SWA_HD
SWA_KV_W = SWA_KVH * SWA_HD
SWA_O_W = SWA_Q_W
SWA_IN = SWA_Q_W + 2 * SWA_KV_W + SWA_O_W

kernel_name = 'hybrid_deltanet_diffattn_swa_sink_step'


def _rms(x, g):
    x = x.astype(F32)
    return x * lax.rsqrt(jnp.mean(x * x, -1, keepdims=True) + RMS_EPS) * g.astype(F32)


def _l2n(x):
    x = x.astype(F32)
    return x * lax.rsqrt(jnp.sum(x * x, -1, keepdims=True) + RMS_EPS)


def modulate(x, c, w_ada, b_ada):
    mod = jax.nn.silu(c) @ w_ada + b_ada
    shift, scale, gate = jnp.split(mod[:, None, :], 3, axis=-1)
    return x * (1 + scale) + shift, gate


def deepnorm_residual(x, y, gate, g, b):
    h = (ALPHA * x + (1 + gate) * y).astype(F32)
    hc = h - jnp.mean(h, -1, keepdims=True)
    var = jnp.mean(hc * hc, -1, keepdims=True)
    return (hc * lax.rsqrt(var + LN_EPS) * g.astype(F32) + b.astype(F32)).astype(x.dtype)


def _to_chunks(t, c):
    b, tp, h = t.shape[:3]
    t = t.reshape((b, tp // c, c, h) + t.shape[3:])
    return jnp.moveaxis(jnp.moveaxis(t, 1, 0), 3, 2)


def gated_delta_rule(q, k, v, g, beta, s0):
    t = q.shape[1]
    b = q.shape[0]
    c = DN_CHUNK if t >= DN_CHUNK else t
    pad = (-t) % c

    def padt(a):
        return jnp.pad(a, [(0, 0), (0, pad)] + [(0, 0)] * (a.ndim - 2))

    q, k, v, g, beta = [_to_chunks(padt(a), c) for a in (q, k, v, g, beta)]
    gc = jnp.cumsum(g, axis=-1)
    tril = jnp.tril(jnp.ones((c, c), bool))
    strict = jnp.tril(jnp.ones((c, c), bool), -1)
    decay = jnp.where(tril, jnp.exp(jnp.where(tril, gc[..., :, None] - gc[..., None, :], 0.0)), 0.0)
    kb = k * beta[..., None]
    lmat = jnp.where(strict, jnp.einsum('...id,...jd->...ij', kb, k) * decay, 0.0)
    amat = lmat + jnp.eye(c, dtype=lmat.dtype)
    rhs = jnp.concatenate([v * beta[..., None], kb * jnp.exp(gc)[..., None]], -1)
    sol = lax.linalg.triangular_solve(amat, rhs, left_side=True, lower=True, unit_diagonal=True)
    u, w = sol[..., :DN_DV], sol[..., DN_DV:]
    qk = jnp.einsum('...id,...jd->...ij', q, k) * decay
    qg = q * jnp.exp(gc)[..., None]
    glast = gc[..., -1]
    kd = k * jnp.exp(glast[..., None] - gc)[..., None]

    def step(s, xs):
        qk_i, qg_i, u_i, w_i, kd_i, gl_i = xs
        v_new = u_i - jnp.einsum('bhcd,bhde->bhce', w_i, s)
        o = jnp.einsum('bhcd,bhde->bhce', qg_i, s) + jnp.einsum('bhcs,bhse->bhce', qk_i, v_new)
        s = s * jnp.exp(gl_i)[..., None, None] + jnp.einsum('bhcd,bhce->bhde', kd_i, v_new)
        return s, o

    s, o = lax.scan(step, s0, (qk, qg, u, w, kd, glast))
    o = jnp.moveaxis(jnp.moveaxis(o, 3, 2), 0, 1)
    o = o.reshape(b, -1, DN_HV, DN_DV)[:, :t]
    return o, s


def dn_mixer(u, conv_prev, s0, w_in, conv_w, a_log, dt_bias, norm_g, w_out):
    b, t, _ = u.shape
    qkv, z, a, bb = jnp.split(u @ w_in, [DN_CONV_DIM, DN_CONV_DIM + DN_V_W, DN_CONV_DIM + DN_V_W + DN_HV], axis=-1)
    xc = jnp.concatenate([conv_prev.astype(qkv.dtype), qkv], axis=1)
    conv = xc[:, 0:t] * conv_w[0]
    for i in range(1, DN_CONV_W):
        conv = conv + xc[:, i:i + t] * conv_w[i]
    conv = jax.nn.silu(conv)
    q, k, v = jnp.split(conv, [DN_K_W, 2 * DN_K_W], axis=-1)
    rep = DN_HV // DN_HK
    q = jnp.repeat(_l2n(q.reshape(b, t, DN_HK, DN_DK)), rep, axis=2) * (DN_DK ** -0.5)
    k = jnp.repeat(_l2n(k.reshape(b, t, DN_HK, DN_DK)), rep, axis=2)
    v = v.reshape(b, t, DN_HV, DN_DV).astype(F32)
    g = -jnp.exp(a_log.astype(F32)) * jax.nn.softplus(a.astype(F32) + dt_bias.astype(F32))
    beta = jax.nn.sigmoid(bb.astype(F32))
    o, s = gated_delta_rule(q, k, v, g, beta, s0.astype(F32))
    o = _rms(o, norm_g) * jax.nn.silu(z.reshape(b, t, DN_HV, DN_DV).astype(F32))
    y = o.reshape(b, t, DN_V_W).astype(u.dtype) @ w_out
    return y, s.astype(s0.dtype), xc[:, -(DN_CONV_W - 1):]


def diff_project(u, w_in):
    b, t, _ = u.shape
    q, k, v, z = jnp.split(u @ w_in, [DIFF_Q_W, DIFF_Q_W + DIFF_K_W, DIFF_Q_W + DIFF_K_W + DIFF_V_W], axis=-1)
    return (q.reshape(b, t, DIFF_KVH, DIFF_G, 2, DIFF_HD), k.reshape(b, t, DIFF_KVH, 2 * DIFF_HD),
            v.reshape(b, t, DIFF_KVH, DIFF_VD), z)


def diff_lambda(lq1, lk1, lq2, lk2, lam_init):
    e1 = jnp.exp(jnp.sum(lq1.astype(F32) * lk1.astype(F32)))
    e2 = jnp.exp(jnp.sum(lq2.astype(F32) * lk2.astype(F32)))
    return e1 - e2 + lam_init


def diff_core(q, k, v, mask, lam):
    k = k.reshape(k.shape[:-1] + (2, DIFF_HD))
    s = jnp.einsum('...qhgmd,...khmd->...hgmqk', q, k, preferred_element_type=F32) * (DIFF_HD ** -0.5)
    s = jnp.where(mask[..., None, None, None, :, :], s, -jnp.inf)
    p = jax.nn.softmax(s, axis=-1)
    a = p[..., 0, :, :] - lam * p[..., 1, :, :]
    return jnp.einsum('...hgqk,...khe->...qhge', a.astype(v.dtype), v)


def diff_out(o, z, subln_g, lam_init, w_out):
    b, t = z.shape[:2]
    o = _rms(o.reshape(b, t, DIFF_H, DIFF_VD), subln_g) * (1.0 - lam_init)
    o = o.reshape(b, t, DIFF_O_W) * jax.nn.silu(z.astype(F32))
    return o.astype(z.dtype) @ w_out


def diff_prompt(u, w_in, lam, lam_init, subln_g, w_out):
    q, k, v, z = diff_project(u, w_in)
    b, t = u.shape[:2]
    nqb = t // DIFF_QBLOCK
    qb = jnp.moveaxis(q.reshape((b, nqb, DIFF_QBLOCK) + q.shape[2:]), 1, 0)
    kpos = jnp.arange(t)

    def block(args):
        q_i, i = args
        qpos = i * DIFF_QBLOCK + jnp.arange(DIFF_QBLOCK)
        return diff_core(q_i, k, v, kpos[None, :] <= qpos[:, None], lam)

    o = lax.map(block, (qb, jnp.arange(nqb)))
    o = jnp.moveaxis(o, 0, 1)
    return diff_out(o, z, subln_g, lam_init, w_out), k, v


def diff_sample(u, cache_k, cache_v, layer, page_table, w_in, lam, lam_init, subln_g, w_out):
    q, k, v, z = diff_project(u, w_in)
    t = u.shape[1]
    past = page_table.shape[1] * cache_k.shape[2]
    mask = jnp.arange(past + t)[None, :] <= (past + jnp.arange(t))[:, None]

    def seq(args):
        pt, q_i, k_i, v_i = args
        kp = cache_k[layer, pt].reshape((-1,) + cache_k.shape[3:])
        vp = cache_v[layer, pt].reshape((-1,) + cache_v.shape[3:])
        kk = jnp.concatenate([kp, k_i.astype(kp.dtype)], 0)
        vv = jnp.concatenate([vp, v_i.astype(vp.dtype)], 0)
        return diff_core(q_i, kk, vv, mask, lam)

    o = lax.map(seq, (page_table, q, k, v))
    return diff_out(o, z, subln_g, lam_init, w_out), k, v


def swa_project(u, w_in):
    b, t, _ = u.shape
    q, k, v, z = jnp.split(u @ w_in, [SWA_Q_W, SWA_Q_W + SWA_KV_W, SWA_Q_W + 2 * SWA_KV_W], axis=-1)
    return (q.reshape(b, t, SWA_KVH, SWA_G, SWA_HD), k.reshape(b, t, SWA_KVH, SWA_HD),
            v.reshape(b, t, SWA_KVH, SWA_HD), z)


def swa_core(q, k, v, mask, sinks):
    s = jnp.einsum('...qhgd,...khd->...hgqk', q, k, preferred_element_type=F32) * (SWA_HD ** -0.5)
    s = jnp.where(mask[..., None, None, :, :], s, -jnp.inf)
    sink = jnp.broadcast_to(sinks.astype(F32).reshape(SWA_KVH, SWA_G, 1, 1), s.shape[:-1] + (1,))
    p = jax.nn.softmax(jnp.concatenate([s, sink], -1), axis=-1)[..., :-1]
    return jnp.einsum('...hgqk,...khd->...qhgd', p.astype(v.dtype), v)


def swa_out(o, z, w_out):
    b, t = z.shape[:2]
    return (o.reshape(b, t, SWA_O_W) * jax.nn.silu(z)) @ w_out


def swa_prompt(u, sinks, w_in, w_out):
    q, k, v, z = swa_project(u, w_in)
    b, t = u.shape[:2]
    nb = t // SWA_BLOCK
    qb = q.reshape((b, nb, SWA_BLOCK) + q.shape[2:])

    def band(a):
        a = a.reshape((b, nb, SWA_BLOCK) + a.shape[2:])
        prev = jnp.pad(a, [(0, 0), (1, 0)] + [(0, 0)] * (a.ndim - 2))[:, :-1]
        return jnp.concatenate([prev, a], axis=2)

    i = jnp.arange(SWA_BLOCK)[:, None]
    j = jnp.arange(2 * SWA_BLOCK)[None, :]
    d = i + SWA_BLOCK - j
    mask = (d >= 0) & (d <= WINDOW) & ((jnp.arange(nb)[:, None, None] > 0) | (j >= SWA_BLOCK))
    o = swa_core(qb, band(k), band(v), mask, sinks)
    return swa_out(o, z, w_out), k[:, -WINDOW:], v[:, -WINDOW:]


def swa_sample(u, buf_k, buf_v, sinks, w_in, w_out):
    q, k, v, z = swa_project(u, w_in)
    t = u.shape[1]
    kk = jnp.concatenate([buf_k.astype(k.dtype), k], 1)
    vv = jnp.concatenate([buf_v.astype(v.dtype), v], 1)
    d = jnp.arange(t)[:, None] + WINDOW - jnp.arange(WINDOW + t)[None, :]
    mask = (d >= 0) & (d <= WINDOW)
    o = swa_core(q, kk, vv, mask, sinks)
    return swa_out(o, z, w_out), kk[:, -WINDOW:], vv[:, -WINDOW:]


def setup_inputs(seed: int = 0) -> dict:
    key = jax.random.key(seed)
    ks = iter(jax.random.split(key, 40))

    def nrm(shape, scale):
        return jax.random.normal(next(ks), shape, F32) * scale

    n_pages = PAST_LEN // PAGE_SIZE
    n_used = DEC_BATCH * n_pages
    n_pool = n_used + max(1, n_used // 4)
    page_table = jax.random.permutation(next(ks), n_pool)[:n_used].reshape(DEC_BATCH, n_pages).astype(jnp.int32)
    dt = jnp.exp(jax.random.uniform(next(ks), (N_DN, DN_HV), F32, math.log(1e-3), math.log(1e-1)))
    a_log = jnp.log(jax.random.uniform(next(ks), (N_DN, DN_HV), F32, 1.0, 16.0))
    ds = D_MODEL ** -0.5
    return {
        'x_prompt': nrm((BATCH, SEQ, D_MODEL), 1.0),
        'x_sample': nrm((DEC_BATCH, DEC_SEQ, D_MODEL), 1.0),
        'cache_diff_k': nrm((N_DIFF, n_pool, PAGE_SIZE, DIFF_KVH, 2 * DIFF_HD), 1.0),
        'cache_diff_v': nrm((N_DIFF, n_pool, PAGE_SIZE, DIFF_KVH, DIFF_VD), 1.0),
        'page_table': page_table,
        'cache_swa_k': nrm((N_SWA, DEC_BATCH, WINDOW, SWA_KVH, SWA_HD), 1.0),
        'cache_swa_v': nrm((N_SWA, DEC_BATCH, WINDOW, SWA_KVH, SWA_HD), 1.0),
        'state_dn_S': nrm((N_DN, DEC_BATCH, DN_HV, DN_DK, DN_DV), 0.1),
        'state_dn_conv': nrm((N_DN, DEC_BATCH, DN_CONV_W - 1, DN_CONV_DIM), 1.0),
        'c_prompt': nrm((BATCH, D_MODEL), 1.0),
        'c_sample': nrm((DEC_BATCH, D_MODEL), 1.0),
        'ada_w': nrm((DEPTH, D_MODEL, 3 * D_MODEL), 0.1 * ds),
        'ada_b': nrm((DEPTH, 3 * D_MODEL), 0.02),
        'ln_g': 1.0 + nrm((DEPTH, D_MODEL), 0.02),
        'ln_b': nrm((DEPTH, D_MODEL), 0.02),
        'dn_w_in': nrm((N_DN, D_MODEL, DN_IN), ds),
        'dn_conv_w': nrm((N_DN, DN_CONV_W, DN_CONV_DIM), 0.5),
        'dn_a_log': a_log,
        'dn_dt_bias': dt + jnp.log(-jnp.expm1(-dt)),
        'dn_norm_g': 1.0 + nrm((N_DN, DN_DV), 0.02),
        'dn_w_out': nrm((N_DN, DN_V_W, D_MODEL), BETA * DN_V_W ** -0.5),
        'diff_w_in': nrm((N_DIFF, D_MODEL, DIFF_IN), ds),
        'diff_lq1': nrm((N_DIFF, DIFF_HD), 0.1),
        'diff_lk1': nrm((N_DIFF, DIFF_HD), 0.1),
        'diff_lq2': nrm((N_DIFF, DIFF_HD), 0.1),
        'diff_lk2': nrm((N_DIFF, DIFF_HD), 0.1),
        'diff_subln_g': 1.0 + nrm((N_DIFF, DIFF_VD), 0.02),
        'diff_w_out': nrm((N_DIFF, DIFF_O_W, D_MODEL), BETA * DIFF_O_W ** -0.5),
        'swa_w_in': nrm((N_SWA, D_MODEL, SWA_IN), ds),
        'swa_sinks': nrm((N_SWA, SWA_H), 0.5),
        'swa_w_out': nrm((N_SWA, SWA_O_W, D_MODEL), BETA * SWA_O_W ** -0.5),
    }


def reference(x_prompt, x_sample, cache_diff_k, cache_diff_v, page_table, cache_swa_k, cache_swa_v,
              state_dn_S, state_dn_conv, c_prompt, c_sample, ada_w, ada_b, ln_g, ln_b,
              dn_w_in, dn_conv_w, dn_a_log, dn_dt_bias, dn_norm_g, dn_w_out,
              diff_w_in, diff_lq1, diff_lk1, diff_lq2, diff_lk2, diff_subln_g, diff_w_out,
              swa_w_in, swa_sinks, swa_w_out):
    xp, xs = x_prompt, x_sample
    bp = xp.shape[0]
    st = {n: [] for n in ('dk_p', 'dv_p', 'dk_s', 'dv_s', 'sk_p', 'sv_p', 'sk_s', 'sv_s',
                          'S_p', 'cv_p', 'S_s', 'cv_s')}
    for l in range(DEPTH):
        j = l // N_MIXERS
        up, gp = modulate(xp, c_prompt, ada_w[l], ada_b[l])
        us, gs = modulate(xs, c_sample, ada_w[l], ada_b[l])
        if l % N_MIXERS == 0:
            prm = (dn_w_in[j], dn_conv_w[j], dn_a_log[j], dn_dt_bias[j], dn_norm_g[j], dn_w_out[j])
            conv0 = jnp.zeros((bp, DN_CONV_W - 1, DN_CONV_DIM), up.dtype)
            s0 = jnp.zeros((bp, DN_HV, DN_DK, DN_DV), up.dtype)
            yp, s_new, c_new = dn_mixer(up, conv0, s0, *prm)
            st['S_p'].append(s_new)
            st['cv_p'].append(c_new)
            ys, s_new, c_new = dn_mixer(us, state_dn_conv[j], state_dn_S[j], *prm)
            st['S_s'].append(s_new)
            st['cv_s'].append(c_new)
        elif l % N_MIXERS == 1:
            lam_init = 0.8 - 0.6 * math.exp(-0.3 * l)
            lam = diff_lambda(diff_lq1[j], diff_lk1[j], diff_lq2[j], diff_lk2[j], lam_init)
            yp, k_new, v_new = diff_prompt(up, diff_w_in[j], lam, lam_init, diff_subln_g[j], diff_w_out[j])
            st['dk_p'].append(k_new)
            st['dv_p'].append(v_new)
            ys, k_new, v_new = diff_sample(us, cache_diff_k, cache_diff_v, j, page_table, diff_w_in[j],
                                           lam, lam_init, diff_subln_g[j], diff_w_out[j])
            st['dk_s'].append(k_new)
            st['dv_s'].append(v_new)
        else:
            yp, k_new, v_new = swa_prompt(up, swa_sinks[j], swa_w_in[j], swa_w_out[j])
            st['sk_p'].append(k_new)
            st['sv_p'].append(v_new)
            ys, k_new, v_new = swa_sample(us, cache_swa_k[j], cache_swa_v[j], swa_sinks[j], swa_w_in[j], swa_w_out[j])
            st['sk_s'].append(k_new)
            st['sv_s'].append(v_new)
        xp = deepnorm_residual(xp, yp, gp, ln_g[l], ln_b[l])
        xs = deepnorm_residual(xs, ys, gs, ln_g[l], ln_b[l])
    return (xp, xs,
            jnp.stack(st['dk_p']), jnp.stack(st['dv_p']), jnp.stack(st['dk_s']), jnp.stack(st['dv_s']),
            jnp.stack(st['sk_p']), jnp.stack(st['sv_p']), jnp.stack(st['sk_s']), jnp.stack(st['sv_s']),
            jnp.stack(st['S_p']), jnp.stack(st['cv_p']), jnp.stack(st['S_s']), jnp.stack(st['cv_s']))
```

```python
import functools
import math

import jax
import jax.numpy as jnp
from jax import lax
from jax.experimental import pallas as pl
from jax.experimental.pallas import tpu as pltpu

F32 = jnp.float32
BF16 = jnp.bfloat16

LN_EPS = 1e-5
RMS_EPS = 1e-6
N_MIXERS = 3

DN_HK = 8
DN_HV = 16
DN_DK = 128
DN_DV = 128
DN_REP = DN_HV // DN_HK
DN_CONV_W = 4
DN_CHUNK = 128
DN_HB = 4

DIFF_H = 8
DIFF_KVH = 4
DIFF_G = DIFF_H // DIFF_KVH
DIFF_HD = 64
DIFF_VD = 2 * DIFF_HD
DIFF_TQ = 512
DIFF_PAGES_PER_STEP = 8

SWA_H = 16
SWA_KVH = 2
SWA_G = SWA_H // SWA_KVH
SWA_HD = 64
WINDOW = 128
SWA_DEC_BB = 8

LANES = 128
SUBLANES = 8
VMEM_LIMIT = 48 * 1024 * 1024

_NT = (((1,), (1,)), ((), ()))
_TN = (((0,), (0,)), ((), ()))


def _silu(x):
    return x * jax.nn.sigmoid(x)


def _mm_body(*refs, n_pro, n_epi, pro, epi):
    x_ref, w_ref = refs[0], refs[1]
    pro_refs = refs[2:2 + n_pro]
    epi_refs = refs[2 + n_pro:2 + n_pro + n_epi]
    o_ref = refs[2 + n_pro + n_epi]
    xb_ref = refs[3 + n_pro + n_epi]

    @pl.when(pl.program_id(1) == 0)
    def _():
        x = x_ref[...]
        if pro is not None:
            x = pro(x, *pro_refs)
        xb_ref[...] = x.astype(xb_ref.dtype)

    acc = jnp.dot(xb_ref[...], w_ref[...], preferred_element_type=F32)
    if epi is not None:
        acc = epi(acc, *epi_refs)
    o_ref[...] = acc.astype(o_ref.dtype)


def _mm(x, w, *, tm, tn, pro=None, pro_ops=(), epi=None, epi_ops=(), out_dtype=F32, name="mm"):
    m, k = x.shape
    n = w.shape[1]
    assert m % tm == 0 and n % tn == 0, (m, tm, n, tn)
    ops = [x, w]
    in_specs = [pl.BlockSpec((tm, k), lambda i, j: (i, 0)), pl.BlockSpec((k, tn), lambda i, j: (0, j))]
    for a, s in (*pro_ops, *epi_ops):
        ops.append(a)
        in_specs.append(s)
    body = functools.partial(_mm_body, n_pro=len(pro_ops), n_epi=len(epi_ops), pro=pro, epi=epi)
    return pl.pallas_call(
        body,
        grid=(m // tm, n // tn),
        in_specs=in_specs,
        out_specs=pl.BlockSpec((tm, tn), lambda i, j: (i, j)),
        out_shape=jax.ShapeDtypeStruct((m, n), out_dtype),
        scratch_shapes=[pltpu.VMEM((tm, k), BF16)],
        compiler_params=pltpu.CompilerParams(dimension_semantics=("parallel", "arbitrary"),
                                             vmem_limit_bytes=VMEM_LIMIT),
        name=name,
    )(*ops)


def _modulate(x, mod_ref):
    d = x.shape[-1]
    return x * (1.0 + mod_ref[:, d:2 * d]) + mod_ref[:, 0:d]


def _deepnorm(y, xres_ref, mod_ref, g_ref, b_ref, *, alpha):
    d = y.shape[-1]
    h = alpha * xres_ref[...] + (1.0 + mod_ref[:, 2 * d:3 * d]) * y
    hc = h - jnp.mean(h, -1, keepdims=True)
    var = jnp.mean(hc * hc, -1, keepdims=True)
    return hc * lax.rsqrt(var + LN_EPS) * g_ref[...] + b_ref[...]


def _add_bias(y, b_ref):
    return y + b_ref[...]


def _dn_gates(y, alog_ref, dtb_ref):
    lane = lax.broadcasted_iota(jnp.int32, y.shape, 1)
    t = y + dtb_ref[...]
    softplus = jnp.maximum(t, 0.0) + jnp.log1p(jnp.exp(-jnp.abs(t)))
    g = -jnp.exp(alog_ref[...]) * softplus
    return jnp.where(lane < DN_HV, g, jax.nn.sigmoid(y))


def _mod_spec(mod, tm, rows_per_mod):
    w = mod.shape[-1]
    if mod.ndim == 3:
        assert rows_per_mod % tm == 0, (rows_per_mod, tm)
        return pl.BlockSpec((None, 1, w), lambda i, j: (i * tm // rows_per_mod, 0, 0))
    return pl.BlockSpec((tm, w), lambda i, j: (i, 0))


def _in_proj(x, mod, rows_per_mod, w, *, tm, tn, epi=None, epi_ops=(), name="in_proj"):
    return _mm(x, w, tm=tm, tn=tn, pro=_modulate, pro_ops=[(mod, _mod_spec(mod, tm, rows_per_mod))],
               epi=epi, epi_ops=epi_ops, name=name)


def _out_proj_norm(y, w, xres, mod, rows_per_mod, ln_g, ln_b, *, tm, alpha, name="out_proj"):
    d = w.shape[1]
    row = pl.BlockSpec((1, d), lambda i, j: (0, 0))
    return _mm(y, w, tm=tm, tn=d, epi=functools.partial(_deepnorm, alpha=alpha),
               epi_ops=[(xres, pl.BlockSpec((tm, d), lambda i, j: (i, 0))),
                        (mod, _mod_spec(mod, tm, rows_per_mod)),
                        (ln_g.reshape(1, d), row), (ln_b.reshape(1, d), row)], name=name)


def _l2norm_heads(y, o_ref, scale):
    for h in range(y.shape[-1] // DN_DK):
        yh = y[:, h * DN_DK:(h + 1) * DN_DK]
        inv = lax.rsqrt(jnp.sum(yh * yh, -1, keepdims=True) + RMS_EPS)
        o_ref[:, h * DN_DK:(h + 1) * DN_DK] = yh * inv * scale


def _dn_prep_body(x_ref, prev_ref, w_ref, o_ref, *, n_q_blk, n_k_blk):
    i = pl.program_id(1)
    c = pl.program_id(2)
    x = x_ref[...]
    tt = x.shape[0]
    prev = jnp.where(i > 0, prev_ref[...], 0.0)
    xx = jnp.concatenate([prev, x], axis=0)
    w = w_ref[...]
    off = SUBLANES - (DN_CONV_W - 1)
    conv = xx[off:off + tt] * w[0:1]
    for s in range(1, DN_CONV_W):
        conv = conv + xx[off + s:off + s + tt] * w[s:s + 1]
    y = _silu(conv)

    @pl.when(c >= n_q_blk + n_k_blk)
    def _():
        o_ref[...] = y

    @pl.when(c < n_q_blk + n_k_blk)
    def _():
        _l2norm_heads(y, o_ref, jnp.where(c < n_q_blk, DN_DK ** -0.5, 1.0))


def _dn_prep(qkv, conv_w, *, tt, cb):
    b, t, c = qkv.shape
    k_w = DN_HK * DN_DK
    body = functools.partial(_dn_prep_body, n_q_blk=k_w // cb, n_k_blk=k_w // cb)
    return pl.pallas_call(
        body,
        grid=(b, t // tt, c // cb),
        in_specs=[pl.BlockSpec((None, tt, cb), lambda bi, i, ci: (bi, i, ci)),
                  pl.BlockSpec((None, SUBLANES, cb),
                               lambda bi, i, ci: (bi, jnp.maximum(i * (tt // SUBLANES) - 1, 0), ci)),
                  pl.BlockSpec((DN_CONV_W, cb), lambda bi, i, ci: (0, ci))],
        out_specs=pl.BlockSpec((None, tt, cb), lambda bi, i, ci: (bi, i, ci)),
        out_shape=jax.ShapeDtypeStruct(qkv.shape, F32),
        compiler_params=pltpu.CompilerParams(dimension_semantics=("parallel", "parallel", "parallel"),
                                             vmem_limit_bytes=VMEM_LIMIT),
        name="dn_prep",
    )(qkv, qkv, conv_w)


def _gated_rms(o, ng, z):
    return o * lax.rsqrt(jnp.mean(o * o, -1, keepdims=True) + RMS_EPS) * ng * _silu(z)


def _dn_chunk_body(q_ref, k_ref, v_ref, z_ref, gb_ref, ng_ref, o_ref, s_ref, *, hb):
    ci = pl.program_id(2)

    @pl.when(ci == 0)
    def _():
        s_ref[...] = jnp.zeros_like(s_ref)

    c = q_ref.shape[0]
    gb = gb_ref[...]
    row = lax.broadcasted_iota(jnp.int32, gb.shape, 0)
    gc = gb
    sh = 1
    while sh < c:
        gc = gc + jnp.where(row >= sh, pltpu.roll(gc, sh, axis=0), 0.0)
        sh *= 2
    gct = gc.T
    ri = lax.broadcasted_iota(jnp.int32, (c, c), 0)
    cj = lax.broadcasted_iota(jnp.int32, (c, c), 1)
    tril = ri >= cj
    strict = ri > cj
    merge_masks = []
    k = 0
    while (1 << k) < c:
        merge_masks.append(((ri >> (k + 1)) == (cj >> (k + 1))) & ((ri >> k) != (cj >> k)))
        k += 1
    ng = ng_ref[...]
    for kj in range(hb // DN_REP):
        kh = k_ref[:, kj * DN_DK:(kj + 1) * DN_DK]
        qh = q_ref[:, kj * DN_DK:(kj + 1) * DN_DK]
        kk = lax.dot_general(kh, kh, _NT, preferred_element_type=F32)
        qk = lax.dot_general(qh, kh, _NT, preferred_element_type=F32)
        for r in range(DN_REP):
            j = kj * DN_REP + r
            gcol = gc[:, j:j + 1]
            bcol = gb[:, hb + j:hb + j + 1]
            decay = jnp.where(tril, jnp.exp(jnp.where(tril, gcol - gct[j:j + 1, :], 0.0)), 0.0)
            lmat = jnp.where(strict, bcol * kk * decay, 0.0)
            minv = jnp.where(merge_masks[0], -lmat, 0.0)
            for mask in merge_masks[1:]:
                loff = jnp.where(mask, lmat, 0.0)
                y = loff + jnp.dot(loff, minv, preferred_element_type=F32)
                minv = minv - y - jnp.dot(minv, y, preferred_element_type=F32)
            eg = jnp.exp(gcol)
            vh = v_ref[:, j * DN_DV:(j + 1) * DN_DV]
            rhs = jnp.concatenate([vh * bcol, kh * (bcol * eg)], axis=1)
            sol = rhs + jnp.dot(minv, rhs, preferred_element_type=F32)
            u = sol[:, :DN_DV]
            w = sol[:, DN_DV:]
            s = s_ref[j]
            ws = jnp.dot(jnp.concatenate([w, qh * eg], axis=0), s, preferred_element_type=F32)
            v_new = u - ws[:c]
            o = ws[c:] + jnp.dot(qk * decay, v_new, preferred_element_type=F32)
            glast = gc[c - 1:c, j:j + 1]
            kd = kh * jnp.exp(glast - gcol)
            s_ref[j] = s * jnp.exp(glast) + lax.dot_general(kd, v_new, _TN, preferred_element_type=F32)
            o_ref[:, j * DN_DV:(j + 1) * DN_DV] = _gated_rms(o, ng, z_ref[:, j * DN_DV:(j + 1) * DN_DV])


def _dn_chunks(qkvc, z, gates, norm_g, *, hb):
    b, t, _ = qkvc.shape
    c = DN_CHUNK
    assert t % c == 0
    kb = hb // DN_REP
    k_off = DN_HK // kb
    v_off = 2 * DN_HK * DN_DK // (hb * DN_DV)
    o, s = pl.pallas_call(
        functools.partial(_dn_chunk_body, hb=hb),
        grid=(b, DN_HV // hb, t // c),
        in_specs=[pl.BlockSpec((None, c, kb * DN_DK), lambda bi, g, ci: (bi, ci, g)),
                  pl.BlockSpec((None, c, kb * DN_DK), lambda bi, g, ci: (bi, ci, k_off + g)),
                  pl.BlockSpec((None, c, hb * DN_DV), lambda bi, g, ci: (bi, ci, v_off + g)),
                  pl.BlockSpec((None, c, hb * DN_DV), lambda bi, g, ci: (bi, ci, g)),
                  pl.BlockSpec((None, None, c, LANES), lambda bi, g, ci: (bi, g, ci, 0)),
                  pl.BlockSpec((1, DN_DV), lambda bi, g, ci: (0, 0))],
        out_specs=[pl.BlockSpec((None, c, hb * DN_DV), lambda bi, g, ci: (bi, ci, g)),
                   pl.BlockSpec((None, hb, DN_DK, DN_DV), lambda bi, g, ci: (bi, g, 0, 0))],
        out_shape=[jax.ShapeDtypeStruct((b, t, DN_HV * DN_DV), F32),
                   jax.ShapeDtypeStruct((b, DN_HV, DN_DK, DN_DV), F32)],
        compiler_params=pltpu.CompilerParams(dimension_semantics=("parallel", "parallel", "arbitrary"),
                                             vmem_limit_bytes=VMEM_LIMIT),
        name="dn_chunks",
    )(qkvc, qkvc, qkvc, z, gates, norm_g.reshape(1, DN_DV))
    return o, s


def _group_gates(gates, lead_shape, hb):
    g = gates[..., :DN_HV].reshape(lead_shape + (DN_HV // hb, hb))
    beta = gates[..., DN_HV:2 * DN_HV].reshape(lead_shape + (DN_HV // hb, hb))
    out = jnp.concatenate([g, beta], axis=-1)
    return jnp.pad(out, [(0, 0)] * (out.ndim - 1) + [(0, LANES - 2 * hb)])


def _dn_step_prep_body(x_ref, prev_ref, w_ref, o_ref, cv_ref, *, n_q_blk, n_k_blk):
    c = pl.program_id(0)
    x = x_ref[...]
    w = w_ref[...]
    conv = prev_ref[0] * w[0:1]
    for s in range(1, DN_CONV_W - 1):
        conv = conv + prev_ref[s] * w[s:s + 1]
    conv = conv + x * w[DN_CONV_W - 1:DN_CONV_W]
    y = _silu(conv)
    for s in range(DN_CONV_W - 2):
        cv_ref[s] = prev_ref[s + 1]
    cv_ref[DN_CONV_W - 2] = x

    @pl.when(c >= n_q_blk + n_k_blk)
    def _():
        o_ref[...] = y

    @pl.when(c < n_q_blk + n_k_blk)
    def _():
        _l2norm_heads(y, o_ref, jnp.where(c < n_q_blk, DN_DK ** -0.5, 1.0))


def _dn_step_prep(qkv, conv_prev_t, conv_w, *, cb):
    db, c = qkv.shape
    k_w = DN_HK * DN_DK
    nw = DN_CONV_W - 1
    return pl.pallas_call(
        functools.partial(_dn_step_prep_body, n_q_blk=k_w // cb, n_k_blk=k_w // cb),
        grid=(c // cb,),
        in_specs=[pl.BlockSpec((db, cb), lambda ci: (0, ci)),
                  pl.BlockSpec((nw, db, cb), lambda ci: (0, 0, ci)),
                  pl.BlockSpec((DN_CONV_W, cb), lambda ci: (0, ci))],
        out_specs=[pl.BlockSpec((db, cb), lambda ci: (0, ci)),
                   pl.BlockSpec((nw, db, cb), lambda ci: (0, 0, ci))],
        out_shape=[jax.ShapeDtypeStruct((db, c), F32), jax.ShapeDtypeStruct((nw, db, c), F32)],
        compiler_params=pltpu.CompilerParams(dimension_semantics=("parallel",), vmem_limit_bytes=VMEM_LIMIT),
        name="dn_step_prep",
    )(qkv, conv_prev_t, conv_w)


def _dn_step_body(q_ref, k_ref, v_ref, z_ref, gb_ref, ng_ref, s_ref, o_ref, so_ref):
    gb = gb_ref[...]
    ng = ng_ref[...]
    rows = lax.broadcasted_iota(jnp.int32, (SUBLANES, DN_DK), 0)
    for kj in range(DN_HK):
        kh = k_ref[kj:kj + 1, :]
        qh = q_ref[kj:kj + 1, :]
        qk = jnp.sum(qh * kh, -1, keepdims=True)
        lhs = jnp.where(rows == 0, kh, jnp.where(rows == 1, qh, 0.0))
        k8 = jnp.where(rows == 0, kh, 0.0)
        for r in range(DN_REP):
            h = kj * DN_REP + r
            s = s_ref[h]
            ks_qs = jnp.dot(lhs, s, preferred_element_type=F32)
            eg = jnp.exp(gb[:, h:h + 1])
            beta = gb[:, DN_HV + h:DN_HV + h + 1]
            v_new = beta * (v_ref[h:h + 1, :] - eg * ks_qs[0:1])
            o = eg * ks_qs[1:2] + qk * v_new
            vn8 = jnp.where(rows == 0, v_new, 0.0)
            so_ref[h] = s * eg + lax.dot_general(k8, vn8, _TN, preferred_element_type=F32)
            o_ref[h:h + 1, :] = _gated_rms(o, ng, z_ref[h:h + 1, :])


def _dn_step(q, k, v, z, gates, norm_g, s0):
    db = q.shape[0]
    qk_spec = pl.BlockSpec((None, DN_HK, DN_DK), lambda b: (b, 0, 0))
    vz_spec = pl.BlockSpec((None, DN_HV, DN_DV), lambda b: (b, 0, 0))
    s_spec = pl.BlockSpec((None, DN_HV, DN_DK, DN_DV), lambda b: (b, 0, 0, 0))
    return pl.pallas_call(
        _dn_step_body,
        grid=(db,),
        in_specs=[qk_spec, qk_spec, vz_spec, vz_spec,
                  pl.BlockSpec((None, 1, LANES), lambda b: (b, 0, 0)),
                  pl.BlockSpec((1, DN_DV), lambda b: (0, 0)), s_spec],
        out_specs=[vz_spec, s_spec],
        out_shape=[jax.ShapeDtypeStruct((db, DN_HV, DN_DV), F32), jax.ShapeDtypeStruct(s0.shape, F32)],
        compiler_params=pltpu.CompilerParams(dimension_semantics=("parallel",), vmem_limit_bytes=VMEM_LIMIT),
        name="dn_step",
    )(q, k, v, z, gates, norm_g.reshape(1, DN_DV), s0)


def _dn_layer(xp, xs, mod_p, mod_s, seq, conv_prev, s0, w_in, conv_w, a_log, dt_bias, norm_g, w_out,
              ln_g, ln_b, alpha):
    mp, d = xp.shape
    bp = mp // seq
    db = xs.shape[0]
    conv_dim = 2 * DN_HK * DN_DK + DN_HV * DN_DV
    v_w = DN_HV * DN_DV
    w_qkv = w_in[:, :conv_dim].astype(BF16)
    w_z = w_in[:, conv_dim:conv_dim + v_w].astype(BF16)
    w_ab = jnp.pad(w_in[:, conv_dim + v_w:], ((0, 0), (0, LANES - 2 * DN_HV))).astype(BF16)
    w_o = w_out.astype(BF16)
    alog = jnp.pad(a_log, (0, LANES - DN_HV)).reshape(1, LANES)
    dtb = jnp.pad(dt_bias, (0, LANES - DN_HV)).reshape(1, LANES)
    lane_row = pl.BlockSpec((1, LANES), lambda i, j: (0, 0))
    gate_ops = [(alog, lane_row), (dtb, lane_row)]

    qkv = _in_proj(xp, mod_p, seq, w_qkv, tm=1024, tn=512, name="dn_in_qkv_p")
    z = _in_proj(xp, mod_p, seq, w_z, tm=1024, tn=512, name="dn_in_z_p")
    gates = _in_proj(xp, mod_p, seq, w_ab, tm=1024, tn=LANES, epi=_dn_gates, epi_ops=gate_ops,
                     name="dn_in_gates_p")
    qkv3 = qkv.reshape(bp, seq, conv_dim)
    cv_p = qkv3[:, seq - (DN_CONV_W - 1):]
    qkvc = _dn_prep(qkv3, conv_w, tt=512, cb=512)
    gates_g = jnp.swapaxes(_group_gates(gates.reshape(bp, seq, LANES), (bp, seq), DN_HB), 1, 2)
    o, s_p = _dn_chunks(qkvc, z.reshape(bp, seq, v_w), gates_g, norm_g, hb=DN_HB)
    xp_new = _out_proj_norm(o.reshape(mp, v_w), w_o, xp, mod_p, seq, ln_g, ln_b, tm=512, alpha=alpha,
                            name="dn_out_p")

    qkv_s = _in_proj(xs, mod_s, 1, w_qkv, tm=db, tn=512, name="dn_in_qkv_s")
    z_s = _in_proj(xs, mod_s, 1, w_z, tm=db, tn=512, name="dn_in_z_s")
    gates_s = _in_proj(xs, mod_s, 1, w_ab, tm=db, tn=LANES, epi=_dn_gates, epi_ops=gate_ops,
                       name="dn_in_gates_s")
    qkvc_s, cv_t = _dn_step_prep(qkv_s, jnp.swapaxes(conv_prev, 0, 1), conv_w, cb=512)
    k_w = DN_HK * DN_DK
    o_s, s_s = _dn_step(qkvc_s[:, :k_w].reshape(db, DN_HK, DN_DK),
                        qkvc_s[:, k_w:2 * k_w].reshape(db, DN_HK, DN_DK),
                        qkvc_s[:, 2 * k_w:].reshape(db, DN_HV, DN_DV),
                        z_s.reshape(db, DN_HV, DN_DV), gates_s.reshape(db, 1, LANES), norm_g, s0)
    xs_new = _out_proj_norm(o_s.reshape(db, v_w), w_o, xs, mod_s, 1, ln_g, ln_b, tm=db, alpha=alpha,
                            name="dn_out_s")
    return xp_new, xs_new, s_p, cv_p, s_s, jnp.swapaxes(cv_t, 0, 1)


def _diff_lambda(lq1_ref, lk1_ref, lq2_ref, lk2_ref, lam_init):
    e1 = jnp.exp(jnp.sum(lq1_ref[...] * lk1_ref[...], -1, keepdims=True))
    e2 = jnp.exp(jnp.sum(lq2_ref[...] * lk2_ref[...], -1, keepdims=True))
    return e1 - e2 + lam_init


def _diff_head_out(o, sg, z, lam_init):
    on = o * lax.rsqrt(jnp.mean(o * o, -1, keepdims=True) + RMS_EPS) * sg * (1.0 - lam_init)
    return on * _silu(z)


def _diff_flash_body(q_ref, k_ref, v_ref, z_ref, lq1_ref, lk1_ref, lq2_ref, lk2_ref, sg_ref, o_ref,
                     m_ref, l_ref, acc_ref, *, lam_init):
    i = pl.program_id(2)
    j = pl.program_id(3)
    n_sub = 2 * DIFF_G

    @pl.when(j == 0)
    def _():
        m_ref[...] = jnp.full_like(m_ref, -jnp.inf)
        l_ref[...] = jnp.zeros_like(l_ref)
        acc_ref[...] = jnp.zeros_like(acc_ref)

    def update(masked):
        v = v_ref[...].astype(BF16)
        tq, tk = q_ref.shape[0], k_ref.shape[0]
        if masked:
            keep = (lax.broadcasted_iota(jnp.int32, (tq, tk), 1)
                    <= lax.broadcasted_iota(jnp.int32, (tq, tk), 0))
        for r in range(n_sub):
            half = r % 2
            s = lax.dot_general(q_ref[:, r * DIFF_HD:(r + 1) * DIFF_HD],
                                k_ref[:, half * DIFF_HD:(half + 1) * DIFF_HD], _NT,
                                preferred_element_type=F32) * (DIFF_HD ** -0.5)
            if masked:
                s = jnp.where(keep, s, -jnp.inf)
            m_prev = m_ref[r]
            m_new = jnp.maximum(m_prev, jnp.max(s, -1, keepdims=True))
            alpha = jnp.exp(m_prev - m_new)
            p = jnp.exp(s - m_new)
            l_ref[r] = alpha * l_ref[r] + jnp.sum(p, -1, keepdims=True)
            acc_ref[r] = alpha * acc_ref[r] + jnp.dot(p.astype(BF16), v, preferred_element_type=F32)
            m_ref[r] = m_new

    @pl.when(j < i)
    def _():
        update(False)

    @pl.when(j == i)
    def _():
        update(True)
        lam = _diff_lambda(lq1_ref, lk1_ref, lq2_ref, lk2_ref, lam_init)
        sg = sg_ref[...]
        for g in range(DIFF_G):
            o = acc_ref[2 * g] / l_ref[2 * g] - lam * (acc_ref[2 * g + 1] / l_ref[2 * g + 1])
            o_ref[:, g * DIFF_VD:(g + 1) * DIFF_VD] = _diff_head_out(
                o, sg, z_ref[:, g * DIFF_VD:(g + 1) * DIFF_VD], lam_init)


def _diff_flash(q, k, v, z, lams, subln_g, lam_init, *, tq):
    b, t, _ = q.shape
    nq = t // tq
    qw = DIFF_G * 2 * DIFF_HD
    ow = DIFF_G * DIFF_VD
    n_sub = 2 * DIFF_G
    vec = pl.BlockSpec((1, DIFF_HD), lambda bi, h, i, j: (0, 0))
    return pl.pallas_call(
        functools.partial(_diff_flash_body, lam_init=lam_init),
        grid=(b, DIFF_KVH, nq, nq),
        in_specs=[pl.BlockSpec((None, tq, qw), lambda bi, h, i, j: (bi, i, h)),
                  pl.BlockSpec((None, tq, 2 * DIFF_HD), lambda bi, h, i, j: (bi, jnp.minimum(j, i), h)),
                  pl.BlockSpec((None, tq, DIFF_VD), lambda bi, h, i, j: (bi, jnp.minimum(j, i), h)),
                  pl.BlockSpec((None, tq, ow), lambda bi, h, i, j: (bi, i, h)),
                  vec, vec, vec, vec,
                  pl.BlockSpec((1, DIFF_VD), lambda bi, h, i, j: (0, 0))],
        out_specs=pl.BlockSpec((None, tq, ow), lambda bi, h, i, j: (bi, i, h)),
        out_shape=jax.ShapeDtypeStruct((b, t, DIFF_H * DIFF_VD), F32),
        scratch_shapes=[pltpu.VMEM((n_sub, tq, 1), F32), pltpu.VMEM((n_sub, tq, 1), F32),
                        pltpu.VMEM((n_sub, tq, DIFF_VD), F32)],
        compiler_params=pltpu.CompilerParams(
            dimension_semantics=("parallel", "parallel", "parallel", "arbitrary"),
            vmem_limit_bytes=VMEM_LIMIT),
        name="diff_flash",
    )(q, k, v, z, *lams, subln_g.reshape(1, DIFF_VD))


def _diff_decode_body(pt_ref, q_ref, ks_ref, vs_ref, z_ref, lq1_ref, lk1_ref, lq2_ref, lk2_ref, sg_ref,
                      *rest, pp, lam_init):
    del pt_ref
    k_pages = rest[:pp]
    v_pages = rest[pp:2 * pp]
    o_ref = rest[2 * pp]
    qb_ref, m_ref, l_ref, acc_ref = rest[2 * pp + 1:]
    j = pl.program_id(1)
    n_rows = DIFF_KVH * DIFF_G * 2
    head_shift = (DIFF_G * 2).bit_length() - 1
    kw = 2 * DIFF_HD
    page_rows = k_pages[0].shape[0]

    @pl.when(j == 0)
    def _():
        d_i = lax.broadcasted_iota(jnp.int32, (DIFF_HD, kw), 0)
        c_i = lax.broadcasted_iota(jnp.int32, (DIFF_HD, kw), 1)
        spread = jnp.where((c_i & (DIFF_HD - 1)) == d_i, 1.0, 0.0)
        qt = jnp.dot(q_ref[...], spread, preferred_element_type=F32)
        r_i = lax.broadcasted_iota(jnp.int32, (n_rows, kw), 0)
        half = lax.broadcasted_iota(jnp.int32, (n_rows, kw), 1) >> (DIFF_HD.bit_length() - 1)
        qb_ref[...] = jnp.where(half == (r_i & 1), qt, 0.0) * (DIFF_HD ** -0.5)
        m_ref[...] = jnp.full_like(m_ref, -jnp.inf)
        l_ref[...] = jnp.zeros_like(l_ref)
        acc_ref[...] = jnp.zeros_like(acc_ref)

    qb = qb_ref[...]
    s = jnp.concatenate([lax.dot_general(qb, kp[...], _NT, preferred_element_type=F32) for kp in k_pages],
                        axis=1)
    row_head = lax.broadcasted_iota(jnp.int32, s.shape, 0) >> head_shift
    col_head = lax.broadcasted_iota(jnp.int32, s.shape, 1) & (DIFF_KVH - 1)
    s = jnp.where(row_head == col_head, s, -jnp.inf)
    m_prev = m_ref[...]
    m_new = jnp.maximum(m_prev, jnp.max(s, -1, keepdims=True))
    alpha = jnp.exp(m_prev - m_new)
    p = jnp.exp(s - m_new)
    l_new = alpha * l_ref[...] + jnp.sum(p, -1, keepdims=True)
    pv = jnp.dot(p[:, 0:page_rows], v_pages[0][...], preferred_element_type=F32)
    for t in range(1, pp):
        pv = pv + jnp.dot(p[:, t * page_rows:(t + 1) * page_rows], v_pages[t][...],
                          preferred_element_type=F32)
    acc_new = alpha * acc_ref[...] + pv
    m_ref[...] = m_new
    l_ref[...] = l_new
    acc_ref[...] = acc_new

    @pl.when(j == pl.num_programs(1) - 1)
    def _():
        rh = lax.broadcasted_iota(jnp.int32, (n_rows, kw), 0) >> head_shift
        ks = jnp.zeros((n_rows, kw), F32)
        vs = jnp.zeros((n_rows, kw), F32)
        for h in range(DIFF_KVH):
            ks = jnp.where(rh == h, ks_ref[h:h + 1, :], ks)
            vs = jnp.where(rh == h, vs_ref[h:h + 1, :], vs)
        s_self = jnp.sum(qb * ks, -1, keepdims=True)
        m_fin = jnp.maximum(m_new, s_self)
        a = jnp.exp(m_new - m_fin)
        p_self = jnp.exp(s_self - m_fin)
        l_fin = a * l_new + p_self
        acc = (a * acc_new + p_self * vs) / l_fin
        lam = _diff_lambda(lq1_ref, lk1_ref, lq2_ref, lk2_ref, lam_init)
        sg = sg_ref[...]
        for hh in range(DIFF_H):
            o = acc[2 * hh:2 * hh + 1] - lam * acc[2 * hh + 1:2 * hh + 2]
            o_ref[hh:hh + 1, :] = _diff_head_out(o, sg, z_ref[hh:hh + 1, :], lam_init)


def _diff_decode(q, k_self, v_self, z, cache_k, cache_v, layer, page_table, lams, subln_g, lam_init, *, pp):
    db = q.shape[0]
    n_pages = page_table.shape[1]
    assert n_pages % pp == 0
    page_rows, kw = cache_k.shape[2:]
    n_rows = DIFF_KVH * DIFF_G * 2

    def page_spec(t):
        return pl.BlockSpec((None, None, page_rows, kw), lambda b, j, pt: (layer, pt[b, j * pp + t], 0, 0))

    def seq_spec(shape):
        return pl.BlockSpec((None,) + shape, lambda b, j, pt: (b, 0, 0))

    vec = pl.BlockSpec((1, DIFF_HD), lambda b, j, pt: (0, 0))
    grid_spec = pltpu.PrefetchScalarGridSpec(
        num_scalar_prefetch=1,
        grid=(db, n_pages // pp),
        in_specs=[seq_spec((n_rows, DIFF_HD)), seq_spec((DIFF_KVH, kw)), seq_spec((DIFF_KVH, kw)),
                  seq_spec((DIFF_H, DIFF_VD)), vec, vec, vec, vec,
                  pl.BlockSpec((1, DIFF_VD), lambda b, j, pt: (0, 0))]
                 + [page_spec(t) for t in range(pp)] + [page_spec(t) for t in range(pp)],
        out_specs=seq_spec((DIFF_H, DIFF_VD)),
        scratch_shapes=[pltpu.VMEM((n_rows, kw), F32), pltpu.VMEM((n_rows, 1), F32),
                        pltpu.VMEM((n_rows, 1), F32), pltpu.VMEM((n_rows, DIFF_VD), F32)],
    )
    return pl.pallas_call(
        functools.partial(_diff_decode_body, pp=pp, lam_init=lam_init),
        grid_spec=grid_spec,
        out_shape=jax.ShapeDtypeStruct((db, DIFF_H, DIFF_VD), F32),
        compiler_params=pltpu.CompilerParams(dimension_semantics=("parallel", "arbitrary"),
                                             vmem_limit_bytes=VMEM_LIMIT),
        name="diff_decode",
    )(page_table, q, k_self, v_self, z, *lams, subln_g.reshape(1, DIFF_VD),
      *([cache_k] * pp), *([cache_v] * pp))


def _diff_layer(xp, xs, mod_p, mod_s, seq, layer_idx, j, cache_k, cache_v, page_table, w_in, lams, subln_g,
                w_out, ln_g, ln_b, alpha):
    mp, d = xp.shape
    bp = mp // seq
    db = xs.shape[0]
    lam_init = 0.8 - 0.6 * math.exp(-0.3 * layer_idx)
    q_w = DIFF_H * 2 * DIFF_HD
    k_w = DIFF_KVH * 2 * DIFF_HD
    v_w = DIFF_KVH * DIFF_VD
    w_q = w_in[:, :q_w].astype(BF16)
    w_k = w_in[:, q_w:q_w + k_w].astype(BF16)
    w_v = w_in[:, q_w + k_w:q_w + k_w + v_w].astype(BF16)
    w_z = w_in[:, q_w + k_w + v_w:].astype(BF16)
    w_o = w_out.astype(BF16)
    lams = [a.reshape(1, DIFF_HD) for a in lams]

    q = _in_proj(xp, mod_p, seq, w_q, tm=1024, tn=512, name="diff_in_q_p")
    k = _in_proj(xp, mod_p, seq, w_k, tm=1024, tn=512, name="diff_in_k_p")
    v = _in_proj(xp, mod_p, seq, w_v, tm=1024, tn=512, name="diff_in_v_p")
    z = _in_proj(xp, mod_p, seq, w_z, tm=1024, tn=512, name="diff_in_z_p")
    o = _diff_flash(q.reshape(bp, seq, q_w), k.reshape(bp, seq, k_w), v.reshape(bp, seq, v_w),
                    z.reshape(bp, seq, -1), lams, subln_g, lam_init, tq=min(DIFF_TQ, seq))
    xp_new = _out_proj_norm(o.reshape(mp, -1), w_o, xp, mod_p, seq, ln_g, ln_b, tm=512, alpha=alpha,
                            name="diff_out_p")

    q_s = _in_proj(xs, mod_s, 1, w_q, tm=db, tn=512, name="diff_in_q_s")
    k_s = _in_proj(xs, mod_s, 1, w_k, tm=db, tn=512, name="diff_in_k_s")
    v_s = _in_proj(xs, mod_s, 1, w_v, tm=db, tn=512, name="diff_in_v_s")
    z_s = _in_proj(xs, mod_s, 1, w_z, tm=db, tn=512, name="diff_in_z_s")
    n_l, n_pool, page = cache_k.shape[:3]
    o_s = _diff_decode(q_s.reshape(db, DIFF_KVH * DIFF_G * 2, DIFF_HD), k_s.reshape(db, DIFF_KVH, 2 * DIFF_HD),
                       v_s.reshape(db, DIFF_KVH, DIFF_VD), z_s.reshape(db, DIFF_H, DIFF_VD),
                       cache_k.reshape(n_l, n_pool, page * DIFF_KVH, 2 * DIFF_HD),
                       cache_v.reshape(n_l, n_pool, page * DIFF_KVH, DIFF_VD),
                       j, page_table, lams, subln_g, lam_init, pp=DIFF_PAGES_PER_STEP)
    xs_new = _out_proj_norm(o_s.reshape(db, -1), w_o, xs, mod_s, 1, ln_g, ln_b, tm=db, alpha=alpha,
                            name="diff_out_s")
    return (xp_new, xs_new, k.reshape(bp, seq, DIFF_KVH, 2 * DIFF_HD), v.reshape(bp, seq, DIFF_KVH, DIFF_VD),
            k_s.reshape(db, 1, DIFF_KVH, 2 * DIFF_HD), v_s.reshape(db, 1, DIFF_KVH, DIFF_VD))


def _swa_band_body(q_ref, kc_ref, kp_ref, vc_ref, vp_ref, z_ref, sink_ref, o_ref):
    i = pl.program_id(1)
    blk = q_ref.shape[0]
    rows = SWA_G * blk
    qi = lax.broadcasted_iota(jnp.int32, (rows, 2 * blk), 0) & (blk - 1)
    kj = lax.broadcasted_iota(jnp.int32, (rows, 2 * blk), 1)
    dist = qi + blk - kj
    keep = (dist >= 0) & (dist <= WINDOW) & ((i > 0) | (kj >= blk))
    for h in range(SWA_KVH):
        hs = slice(h * SWA_HD, (h + 1) * SWA_HD)
        k2 = jnp.concatenate([kp_ref[:, hs], kc_ref[:, hs]], axis=0)
        v2 = jnp.concatenate([vp_ref[:, hs], vc_ref[:, hs]], axis=0)
        qs = jnp.concatenate([q_ref[:, (h * SWA_G + g) * SWA_HD:(h * SWA_G + g + 1) * SWA_HD]
                              for g in range(SWA_G)], axis=0)
        sink = jnp.concatenate([jnp.broadcast_to(sink_ref[:, h * SWA_G + g:h * SWA_G + g + 1], (blk, 1))
                                for g in range(SWA_G)], axis=0)
        s = lax.dot_general(qs, k2, _NT, preferred_element_type=F32) * (SWA_HD ** -0.5)
        s = jnp.where(keep, s, -jnp.inf)
        m = jnp.maximum(jnp.max(s, -1, keepdims=True), sink)
        p = jnp.exp(s - m)
        den = jnp.sum(p, -1, keepdims=True) + jnp.exp(sink - m)
        o = jnp.dot(p / den, v2, preferred_element_type=F32)
        for g in range(SWA_G):
            cs = slice((h * SWA_G + g) * SWA_HD, (h * SWA_G + g + 1) * SWA_HD)
            o_ref[:, cs] = o[g * blk:(g + 1) * blk] * _silu(z_ref[:, cs])


def _swa_band(q, k, v, z, sinks):
    b, t, qw = q.shape
    blk = WINDOW
    kw = SWA_KVH * SWA_HD
    cur = pl.BlockSpec((None, blk, kw), lambda bi, i: (bi, i, 0))
    prev = pl.BlockSpec((None, blk, kw), lambda bi, i: (bi, jnp.maximum(i - 1, 0), 0))
    wide = pl.BlockSpec((None, blk, qw), lambda bi, i: (bi, i, 0))
    return pl.pallas_call(
        _swa_band_body,
        grid=(b, t // blk),
        in_specs=[wide, cur, prev, cur, prev, wide, pl.BlockSpec((1, SWA_H), lambda bi, i: (0, 0))],
        out_specs=wide,
        out_shape=jax.ShapeDtypeStruct((b, t, qw), F32),
        compiler_params=pltpu.CompilerParams(dimension_semantics=("parallel", "parallel"),
                                             vmem_limit_bytes=VMEM_LIMIT),
        name="swa_band",
    )(q, k, k, v, v, z, sinks.reshape(1, SWA_H))


def _swa_decode_body(q_ref, kb_ref, vb_ref, kn_ref, vn_ref, z_ref, sink_ref, o_ref, ko_ref, vo_ref):
    win = kb_ref.shape[1]
    for h in range(SWA_KVH):
        hs = slice(h * SWA_HD, (h + 1) * SWA_HD)
        gs = slice(h * SWA_G, (h + 1) * SWA_G)
        qh = q_ref[:, gs, :]
        s = jnp.einsum("bqd,bkd->bqk", qh, kb_ref[:, :, hs], preferred_element_type=F32) * (SWA_HD ** -0.5)
        s_new = jnp.sum(qh * kn_ref[:, :, hs], -1, keepdims=True) * (SWA_HD ** -0.5)
        sink = sink_ref[gs, :][None]
        m = jnp.maximum(jnp.maximum(jnp.max(s, -1, keepdims=True), s_new), sink)
        p = jnp.exp(s - m)
        p_new = jnp.exp(s_new - m)
        den = jnp.sum(p, -1, keepdims=True) + p_new + jnp.exp(sink - m)
        o = jnp.einsum("bqk,bkd->bqd", p, vb_ref[:, :, hs], preferred_element_type=F32)
        o = (o + p_new * vn_ref[:, :, hs]) / den
        o_ref[:, gs, :] = o * _silu(z_ref[:, gs, :])
    ko_ref[:, 0:win - 1, :] = kb_ref[:, 1:win, :]
    ko_ref[:, win - 1:win, :] = kn_ref[...]
    vo_ref[:, 0:win - 1, :] = vb_ref[:, 1:win, :]
    vo_ref[:, win - 1:win, :] = vn_ref[...]


def _swa_decode(q, k_new, v_new, z, buf_k, buf_v, sinks, *, bb):
    db, win, kw = buf_k.shape
    qz = pl.BlockSpec((bb, SWA_H, SWA_HD), lambda b: (b, 0, 0))
    buf = pl.BlockSpec((bb, win, kw), lambda b: (b, 0, 0))
    new = pl.BlockSpec((bb, 1, kw), lambda b: (b, 0, 0))
    return pl.pallas_call(
        _swa_decode_body,
        grid=(db // bb,),
        in_specs=[qz, buf, buf, new, new, qz, pl.BlockSpec((SWA_H, 1), lambda b: (0, 0))],
        out_specs=[qz, buf, buf],
        out_shape=[jax.ShapeDtypeStruct(q.shape, F32), jax.ShapeDtypeStruct(buf_k.shape, F32),
                   jax.ShapeDtypeStruct(buf_v.shape, F32)],
        compiler_params=pltpu.CompilerParams(dimension_semantics=("parallel",), vmem_limit_bytes=VMEM_LIMIT),
        name="swa_decode",
    )(q, buf_k, buf_v, k_new, v_new, z, sinks.reshape(SWA_H, 1))


def _swa_layer(xp, xs, mod_p, mod_s, seq, buf_k, buf_v, sinks, w_in, w_out, ln_g, ln_b, alpha):
    mp, d = xp.shape
    bp = mp // seq
    db = xs.shape[0]
    q_w = SWA_H * SWA_HD
    kv_w = SWA_KVH * SWA_HD
    w_q = w_in[:, :q_w].astype(BF16)
    w_kv = w_in[:, q_w:q_w + 2 * kv_w].astype(BF16)
    w_z = w_in[:, q_w + 2 * kv_w:].astype(BF16)
    w_o = w_out.astype(BF16)

    q = _in_proj(xp, mod_p, seq, w_q, tm=1024, tn=512, name="swa_in_q_p")
    kv = _in_proj(xp, mod_p, seq, w_kv, tm=1024, tn=2 * kv_w, name="swa_in_kv_p")
    z = _in_proj(xp, mod_p, seq, w_z, tm=1024, tn=512, name="swa_in_z_p")
    k = kv[:, :kv_w].reshape(bp, seq, kv_w)
    v = kv[:, kv_w:].reshape(bp, seq, kv_w)
    o = _swa_band(q.reshape(bp, seq, q_w), k, v, z.reshape(bp, seq, q_w), sinks)
    xp_new = _out_proj_norm(o.reshape(mp, q_w), w_o, xp, mod_p, seq, ln_g, ln_b, tm=512, alpha=alpha,
                            name="swa_out_p")

    q_s = _in_proj(xs, mod_s, 1, w_q, tm=db, tn=512, name="swa_in_q_s")
    kv_s = _in_proj(xs, mod_s, 1, w_kv, tm=db, tn=2 * kv_w, name="swa_in_kv_s")
    z_s = _in_proj(xs, mod_s, 1, w_z, tm=db, tn=512, name="swa_in_z_s")
    win = buf_k.shape[1]
    o_s, k_buf, v_buf = _swa_decode(q_s.reshape(db, SWA_H, SWA_HD), kv_s[:, :kv_w].reshape(db, 1, kv_w),
                                    kv_s[:, kv_w:].reshape(db, 1, kv_w), z_s.reshape(db, SWA_H, SWA_HD),
                                    buf_k.reshape(db, win, kv_w), buf_v.reshape(db, win, kv_w), sinks,
                                    bb=SWA_DEC_BB)
    xs_new = _out_proj_norm(o_s.reshape(db, q_w), w_o, xs, mod_s, 1, ln_g, ln_b, tm=db, alpha=alpha,
                            name="swa_out_s")
    kv_shape = (SWA_KVH, SWA_HD)
    return (xp_new, xs_new, k[:, seq - WINDOW:].reshape((bp, WINDOW) + kv_shape),
            v[:, seq - WINDOW:].reshape((bp, WINDOW) + kv_shape),
            k_buf.reshape((db, win) + kv_shape), v_buf.reshape((db, win) + kv_shape))


def kernel(x_prompt, x_sample, cache_diff_k, cache_diff_v, page_table, cache_swa_k, cache_swa_v, state_dn_S, state_dn_conv, c_prompt, c_sample, ada_w, ada_b, ln_g, ln_b, dn_w_in, dn_conv_w, dn_a_log, dn_dt_bias, dn_norm_g, dn_w_out, diff_w_in, diff_lq1, diff_lk1, diff_lq2, diff_lk2, diff_subln_g, diff_w_out, swa_w_in, swa_sinks, swa_w_out):
    bp, seq, d = x_prompt.shape
    db, dec_seq, _ = x_sample.shape
    assert dec_seq == 1 and bp <= SUBLANES
    depth = ada_w.shape[0]
    alpha = (2 * depth) ** 0.25
    xp = x_prompt.reshape(bp * seq, d)
    xs = x_sample.reshape(db, d)

    c_all = jnp.concatenate([c_prompt, jnp.zeros((SUBLANES - bp, d), F32), c_sample], axis=0)
    st = {n: [] for n in ("dk_p", "dv_p", "dk_s", "dv_s", "sk_p", "sv_p", "sk_s", "sv_s",
                          "S_p", "cv_p", "S_s", "cv_s")}
    for l in range(depth):
        j = l // N_MIXERS
        mod = _mm(c_all, ada_w[l].astype(BF16), tm=c_all.shape[0], tn=1024, pro=_silu, epi=_add_bias,
                  epi_ops=[(ada_b[l].reshape(1, 3 * d), pl.BlockSpec((1, 1024), lambda i, jj: (0, jj)))],
                  name="adaln")
        mod_p = mod[:bp].reshape(bp, 1, 3 * d)
        mod_s = mod[SUBLANES:]
        if l % N_MIXERS == 0:
            xp, xs, s_p, cv_p, s_s, cv_s = _dn_layer(
                xp, xs, mod_p, mod_s, seq, state_dn_conv[j], state_dn_S[j], dn_w_in[j], dn_conv_w[j],
                dn_a_log[j], dn_dt_bias[j], dn_norm_g[j], dn_w_out[j], ln_g[l], ln_b[l], alpha)
            st["S_p"].append(s_p)
            st["cv_p"].append(cv_p)
            st["S_s"].append(s_s)
            st["cv_s"].append(cv_s)
        elif l % N_MIXERS == 1:
            xp, xs, k_p, v_p, k_s, v_s = _diff_layer(
                xp, xs, mod_p, mod_s, seq, l, j, cache_diff_k, cache_diff_v, page_table, diff_w_in[j],
                (diff_lq1[j], diff_lk1[j], diff_lq2[j], diff_lk2[j]), diff_subln_g[j], diff_w_out[j],
                ln_g[l], ln_b[l], alpha)
            st["dk_p"].append(k_p)
            st["dv_p"].append(v_p)
            st["dk_s"].append(k_s)
            st["dv_s"].append(v_s)
        else:
            xp, xs, k_p, v_p, k_s, v_s = _swa_layer(
                xp, xs, mod_p, mod_s, seq, cache_swa_k[j], cache_swa_v[j], swa_sinks[j], swa_w_in[j],
                swa_w_out[j], ln_g[l], ln_b[l], alpha)
            st["sk_p"].append(k_p)
            st["sv_p"].append(v_p)
            st["sk_s"].append(k_s)
            st["sv_s"].append(v_s)
    return (xp.reshape(bp, seq, d), xs.reshape(db, 1, d),
            jnp.stack(st["dk_p"]), jnp.stack(st["dv_p"]), jnp.stack(st["dk_s"]), jnp.stack(st["dv_s"]),
            jnp.stack(st["sk_p"]), jnp.stack(st["sv_p"]), jnp.stack(st["sk_s"]), jnp.stack(st["sv_s"]),
            jnp.stack(st["S_p"]), jnp.stack(st["cv_p"]), jnp.stack(st["S_s"]), jnp.stack(st["cv_s"]))
```

```python
import functools
import math

import jax
import jax.numpy as jnp
from jax import lax
from jax.experimental import pallas as pl
from jax.experimental.pallas import tpu as pltpu

F32 = jnp.float32
BF16 = jnp.bfloat16

LN_EPS = 1e-5
RMS_EPS = 1e-6
N_MIXERS = 3

DN_HK = 8
DN_HV = 16
DN_DK = 128
DN_DV = 128
DN_REP = DN_HV // DN_HK
DN_CONV_W = 4
DN_CHUNK = 128
DN_HB = 16

DIFF_H = 8
DIFF_KVH = 4
DIFF_G = DIFF_H // DIFF_KVH
DIFF_HD = 64
DIFF_VD = 2 * DIFF_HD
DIFF_TQ = 512
DIFF_PAGES_PER_STEP = 16

SWA_H = 16
SWA_KVH = 2
SWA_G = SWA_H // SWA_KVH
SWA_HD = 64
WINDOW = 128
SWA_DEC_BB = 8

LANES = 128
SUBLANES = 8
VMEM_LIMIT = 48 * 1024 * 1024

_NT = (((1,), (1,)), ((), ()))
_TN = (((0,), (0,)), ((), ()))


def _silu(x):
    return x * jax.nn.sigmoid(x)


def _mm_body(*refs, n_pro, n_epi, pro, epi):
    x_ref, w_ref = refs[0], refs[1]
    pro_refs = refs[2:2 + n_pro]
    epi_refs = refs[2 + n_pro:2 + n_pro + n_epi]
    o_ref = refs[2 + n_pro + n_epi]
    xb_ref = refs[3 + n_pro + n_epi]

    @pl.when(pl.program_id(1) == 0)
    def _():
        x = x_ref[...]
        if pro is not None:
            x = pro(x, *pro_refs)
        xb_ref[...] = x.astype(xb_ref.dtype)

    acc = jnp.dot(xb_ref[...], w_ref[...], preferred_element_type=F32)
    if epi is not None:
        acc = epi(acc, *epi_refs)
    o_ref[...] = acc.astype(o_ref.dtype)


def _mm(x, w, *, tm, tn, pro=None, pro_ops=(), epi=None, epi_ops=(), out_dtype=F32, name="mm"):
    m, k = x.shape
    n = w.shape[1]
    assert m % tm == 0 and n % tn == 0, (m, tm, n, tn)
    ops = [x, w]
    in_specs = [pl.BlockSpec((tm, k), lambda i, j: (i, 0)), pl.BlockSpec((k, tn), lambda i, j: (0, j))]
    for a, s in (*pro_ops, *epi_ops):
        ops.append(a)
        in_specs.append(s)
    body = functools.partial(_mm_body, n_pro=len(pro_ops), n_epi=len(epi_ops), pro=pro, epi=epi)
    return pl.pallas_call(
        body,
        grid=(m // tm, n // tn),
        in_specs=in_specs,
        out_specs=pl.BlockSpec((tm, tn), lambda i, j: (i, j)),
        out_shape=jax.ShapeDtypeStruct((m, n), out_dtype),
        scratch_shapes=[pltpu.VMEM((tm, k), BF16)],
        compiler_params=pltpu.CompilerParams(dimension_semantics=("parallel", "arbitrary"),
                                             vmem_limit_bytes=VMEM_LIMIT),
        name=name,
    )(*ops)


def _modulate(x, mod_ref):
    d = x.shape[-1]
    return x * (1.0 + mod_ref[:, d:2 * d]) + mod_ref[:, 0:d]


def _deepnorm(y, xres_ref, mod_ref, g_ref, b_ref, *, alpha):
    d = y.shape[-1]
    h = alpha * xres_ref[...] + (1.0 + mod_ref[:, 2 * d:3 * d]) * y
    hc = h - jnp.mean(h, -1, keepdims=True)
    var = jnp.mean(hc * hc, -1, keepdims=True)
    return hc * lax.rsqrt(var + LN_EPS) * g_ref[...] + b_ref[...]


def _add_bias(y, b_ref):
    return y + b_ref[...]


def _dn_gates(y, alog_ref, dtb_ref):
    lane = lax.broadcasted_iota(jnp.int32, y.shape, 1)
    t = y + dtb_ref[...]
    softplus = jnp.maximum(t, 0.0) + jnp.log1p(jnp.exp(-jnp.abs(t)))
    g = -jnp.exp(alog_ref[...]) * softplus
    return jnp.where(lane < DN_HV, g, jax.nn.sigmoid(y))


def _mod_spec(mod, tm, rows_per_mod):
    w = mod.shape[-1]
    if mod.ndim == 3:
        assert rows_per_mod % tm == 0, (rows_per_mod, tm)
        return pl.BlockSpec((None, 1, w), lambda i, j: (i * tm // rows_per_mod, 0, 0))
    return pl.BlockSpec((tm, w), lambda i, j: (i, 0))


def _in_proj(x, mod, rows_per_mod, w, *, tm, tn, epi=None, epi_ops=(), name="in_proj"):
    return _mm(x, w, tm=tm, tn=tn, pro=_modulate, pro_ops=[(mod, _mod_spec(mod, tm, rows_per_mod))],
               epi=epi, epi_ops=epi_ops, name=name)


def _out_proj_norm(y, w, xres, mod, rows_per_mod, ln_g, ln_b, *, tm, alpha, name="out_proj"):
    d = w.shape[1]
    row = pl.BlockSpec((1, d), lambda i, j: (0, 0))
    return _mm(y, w, tm=tm, tn=d, epi=functools.partial(_deepnorm, alpha=alpha),
               epi_ops=[(xres, pl.BlockSpec((tm, d), lambda i, j: (i, 0))),
                        (mod, _mod_spec(mod, tm, rows_per_mod)),
                        (ln_g.reshape(1, d), row), (ln_b.reshape(1, d), row)], name=name)


def _l2norm_heads(y, o_ref, scale):
    for h in range(y.shape[-1] // DN_DK):
        yh = y[:, h * DN_DK:(h + 1) * DN_DK]
        inv = lax.rsqrt(jnp.sum(yh * yh, -1, keepdims=True) + RMS_EPS)
        o_ref[:, h * DN_DK:(h + 1) * DN_DK] = yh * inv * scale


def _dn_prep_body(x_ref, prev_ref, w_ref, o_ref, *, n_q_blk, n_k_blk):
    i = pl.program_id(1)
    c = pl.program_id(2)
    x = x_ref[...]
    tt = x.shape[0]
    prev = jnp.where(i > 0, prev_ref[...], 0.0)
    xx = jnp.concatenate([prev, x], axis=0)
    w = w_ref[...]
    off = SUBLANES - (DN_CONV_W - 1)
    conv = xx[off:off + tt] * w[0:1]
    for s in range(1, DN_CONV_W):
        conv = conv + xx[off + s:off + s + tt] * w[s:s + 1]
    y = _silu(conv)

    @pl.when(c >= n_q_blk + n_k_blk)
    def _():
        o_ref[...] = y

    @pl.when(c < n_q_blk + n_k_blk)
    def _():
        _l2norm_heads(y, o_ref, jnp.where(c < n_q_blk, DN_DK ** -0.5, 1.0))


def _dn_prep(qkv, conv_w, *, tt, cb):
    b, t, c = qkv.shape
    k_w = DN_HK * DN_DK
    body = functools.partial(_dn_prep_body, n_q_blk=k_w // cb, n_k_blk=k_w // cb)
    return pl.pallas_call(
        body,
        grid=(b, t // tt, c // cb),
        in_specs=[pl.BlockSpec((None, tt, cb), lambda bi, i, ci: (bi, i, ci)),
                  pl.BlockSpec((None, SUBLANES, cb),
                               lambda bi, i, ci: (bi, jnp.maximum(i * (tt // SUBLANES) - 1, 0), ci)),
                  pl.BlockSpec((DN_CONV_W, cb), lambda bi, i, ci: (0, ci))],
        out_specs=pl.BlockSpec((None, tt, cb), lambda bi, i, ci: (bi, i, ci)),
        out_shape=jax.ShapeDtypeStruct(qkv.shape, F32),
        compiler_params=pltpu.CompilerParams(dimension_semantics=("parallel", "parallel", "parallel"),
                                             vmem_limit_bytes=VMEM_LIMIT),
        name="dn_prep",
    )(qkv, qkv, conv_w)


def _gated_rms(o, ng, z):
    return o * lax.rsqrt(jnp.mean(o * o, -1, keepdims=True) + RMS_EPS) * ng * _silu(z)


def _dn_chunk_body(q_ref, k_ref, v_ref, z_ref, gb_ref, ng_ref, o_ref, s_ref, *, hb):
    ci = pl.program_id(2)

    @pl.when(ci == 0)
    def _():
        s_ref[...] = jnp.zeros_like(s_ref)

    c = q_ref.shape[0]
    gb = gb_ref[...]
    row = lax.broadcasted_iota(jnp.int32, gb.shape, 0)
    gc = gb
    sh = 1
    while sh < c:
        gc = gc + jnp.where(row >= sh, pltpu.roll(gc, sh, axis=0), 0.0)
        sh *= 2
    gct = gc.T
    ri = lax.broadcasted_iota(jnp.int32, (c, c), 0)
    cj = lax.broadcasted_iota(jnp.int32, (c, c), 1)
    tril = ri >= cj
    strict = ri > cj
    merge_masks = []
    k = 0
    while (1 << k) < c:
        merge_masks.append(((ri >> (k + 1)) == (cj >> (k + 1))) & ((ri >> k) != (cj >> k)))
        k += 1
    ng = ng_ref[...]
    heads = range(hb)
    dot = functools.partial(jnp.dot, preferred_element_type=F32)
    kh = [k_ref[:, i * DN_DK:(i + 1) * DN_DK] for i in range(hb // DN_REP)]
    qh = [q_ref[:, i * DN_DK:(i + 1) * DN_DK] for i in range(hb // DN_REP)]
    kk = [lax.dot_general(a, a, _NT, preferred_element_type=F32) for a in kh]
    qk = [lax.dot_general(a, b, _NT, preferred_element_type=F32) for a, b in zip(qh, kh)]
    gcol = [gc[:, j:j + 1] for j in heads]
    bcol = [gb[:, hb + j:hb + j + 1] for j in heads]
    decay = [jnp.where(tril, jnp.exp(jnp.where(tril, gcol[j] - gct[j:j + 1, :], 0.0)), 0.0) for j in heads]
    lmat = [jnp.where(strict, bcol[j] * kk[j // DN_REP] * decay[j], 0.0) for j in heads]
    minv = [jnp.where(merge_masks[0], -lmat[j], 0.0) for j in heads]
    for mask in merge_masks[1:]:
        loff = [jnp.where(mask, lmat[j], 0.0) for j in heads]
        y = [loff[j] + dot(loff[j], minv[j]) for j in heads]
        minv = [minv[j] - y[j] - dot(minv[j], y[j]) for j in heads]
    eg = [jnp.exp(gcol[j]) for j in heads]
    rhs = [jnp.concatenate([v_ref[:, j * DN_DV:(j + 1) * DN_DV] * bcol[j],
                            kh[j // DN_REP] * (bcol[j] * eg[j])], axis=1) for j in heads]
    sol = [rhs[j] + dot(minv[j], rhs[j]) for j in heads]
    s_old = [s_ref[j] for j in heads]
    ws = [dot(jnp.concatenate([sol[j][:, DN_DV:], qh[j // DN_REP] * eg[j]], axis=0), s_old[j]) for j in heads]
    v_new = [sol[j][:, :DN_DV] - ws[j][:c] for j in heads]
    o = [ws[j][c:] + dot(qk[j // DN_REP] * decay[j], v_new[j]) for j in heads]
    for j in heads:
        glast = gc[c - 1:c, j:j + 1]
        kd = kh[j // DN_REP] * jnp.exp(glast - gcol[j])
        s_ref[j] = s_old[j] * jnp.exp(glast) + lax.dot_general(kd, v_new[j], _TN, preferred_element_type=F32)
        o_ref[:, j * DN_DV:(j + 1) * DN_DV] = _gated_rms(o[j], ng, z_ref[:, j * DN_DV:(j + 1) * DN_DV])


def _dn_chunks(qkvc, z, gates, norm_g, *, hb):
    b, t, _ = qkvc.shape
    c = DN_CHUNK
    assert t % c == 0
    kb = hb // DN_REP
    k_off = DN_HK // kb
    v_off = 2 * DN_HK * DN_DK // (hb * DN_DV)
    o, s = pl.pallas_call(
        functools.partial(_dn_chunk_body, hb=hb),
        grid=(b, DN_HV // hb, t // c),
        in_specs=[pl.BlockSpec((None, c, kb * DN_DK), lambda bi, g, ci: (bi, ci, g)),
                  pl.BlockSpec((None, c, kb * DN_DK), lambda bi, g, ci: (bi, ci, k_off + g)),
                  pl.BlockSpec((None, c, hb * DN_DV), lambda bi, g, ci: (bi, ci, v_off + g)),
                  pl.BlockSpec((None, c, hb * DN_DV), lambda bi, g, ci: (bi, ci, g)),
                  pl.BlockSpec((None, None, c, LANES), lambda bi, g, ci: (bi, g, ci, 0)),
                  pl.BlockSpec((1, DN_DV), lambda bi, g, ci: (0, 0))],
        out_specs=[pl.BlockSpec((None, c, hb * DN_DV), lambda bi, g, ci: (bi, ci, g)),
                   pl.BlockSpec((None, hb, DN_DK, DN_DV), lambda bi, g, ci: (bi, g, 0, 0))],
        out_shape=[jax.ShapeDtypeStruct((b, t, DN_HV * DN_DV), F32),
                   jax.ShapeDtypeStruct((b, DN_HV, DN_DK, DN_DV), F32)],
        compiler_params=pltpu.CompilerParams(dimension_semantics=("parallel", "parallel", "arbitrary"),
                                             vmem_limit_bytes=VMEM_LIMIT),
        name="dn_chunks",
    )(qkvc, qkvc, qkvc, z, gates, norm_g.reshape(1, DN_DV))
    return o, s


def _group_gates(gates, lead_shape, hb):
    g = gates[..., :DN_HV].reshape(lead_shape + (DN_HV // hb, hb))
    beta = gates[..., DN_HV:2 * DN_HV].reshape(lead_shape + (DN_HV // hb, hb))
    out = jnp.concatenate([g, beta], axis=-1)
    return jnp.pad(out, [(0, 0)] * (out.ndim - 1) + [(0, LANES - 2 * hb)])


def _dn_step_prep_body(x_ref, prev_ref, w_ref, o_ref, cv_ref, *, n_q_blk, n_k_blk):
    c = pl.program_id(0)
    x = x_ref[...]
    w = w_ref[...]
    conv = prev_ref[0] * w[0:1]
    for s in range(1, DN_CONV_W - 1):
        conv = conv + prev_ref[s] * w[s:s + 1]
    conv = conv + x * w[DN_CONV_W - 1:DN_CONV_W]
    y = _silu(conv)
    for s in range(DN_CONV_W - 2):
        cv_ref[s] = prev_ref[s + 1]
    cv_ref[DN_CONV_W - 2] = x

    @pl.when(c >= n_q_blk + n_k_blk)
    def _():
        o_ref[...] = y

    @pl.when(c < n_q_blk + n_k_blk)
    def _():
        _l2norm_heads(y, o_ref, jnp.where(c < n_q_blk, DN_DK ** -0.5, 1.0))


def _dn_step_prep(qkv, conv_prev_t, conv_w, *, cb):
    db, c = qkv.shape
    k_w = DN_HK * DN_DK
    nw = DN_CONV_W - 1
    return pl.pallas_call(
        functools.partial(_dn_step_prep_body, n_q_blk=k_w // cb, n_k_blk=k_w // cb),
        grid=(c // cb,),
        in_specs=[pl.BlockSpec((db, cb), lambda ci: (0, ci)),
                  pl.BlockSpec((nw, db, cb), lambda ci: (0, 0, ci)),
                  pl.BlockSpec((DN_CONV_W, cb), lambda ci: (0, ci))],
        out_specs=[pl.BlockSpec((db, cb), lambda ci: (0, ci)),
                   pl.BlockSpec((nw, db, cb), lambda ci: (0, 0, ci))],
        out_shape=[jax.ShapeDtypeStruct((db, c), F32), jax.ShapeDtypeStruct((nw, db, c), F32)],
        compiler_params=pltpu.CompilerParams(dimension_semantics=("parallel",), vmem_limit_bytes=VMEM_LIMIT),
        name="dn_step_prep",
    )(qkv, conv_prev_t, conv_w)


def _dn_step_body(q_ref, k_ref, v_ref, z_ref, gb_ref, ng_ref, s_ref, o_ref, so_ref):
    gb = gb_ref[...]
    ng = ng_ref[...]
    rows = lax.broadcasted_iota(jnp.int32, (SUBLANES, DN_DK), 0)
    heads = range(DN_HV)
    kh = [k_ref[i:i + 1, :] for i in range(DN_HK)]
    qh = [q_ref[i:i + 1, :] for i in range(DN_HK)]
    qk = [jnp.sum(a * b, -1, keepdims=True) for a, b in zip(qh, kh)]
    lhs = [jnp.where(rows == 0, b, jnp.where(rows == 1, a, 0.0)) for a, b in zip(qh, kh)]
    k8 = [jnp.where(rows == 0, b, 0.0) for b in kh]
    s_old = [s_ref[h] for h in heads]
    ks_qs = [jnp.dot(lhs[h // DN_REP], s_old[h], preferred_element_type=F32) for h in heads]
    eg = [jnp.exp(gb[:, h:h + 1]) for h in heads]
    v_new = [gb[:, DN_HV + h:DN_HV + h + 1] * (v_ref[h:h + 1, :] - eg[h] * ks_qs[h][0:1]) for h in heads]
    outer = [lax.dot_general(k8[h // DN_REP], jnp.where(rows == 0, v_new[h], 0.0), _TN,
                             preferred_element_type=F32) for h in heads]
    for h in heads:
        so_ref[h] = s_old[h] * eg[h] + outer[h]
        o = eg[h] * ks_qs[h][1:2] + qk[h // DN_REP] * v_new[h]
        o_ref[h:h + 1, :] = _gated_rms(o, ng, z_ref[h:h + 1, :])


def _dn_step(q, k, v, z, gates, norm_g, s_all, layer):
    db = q.shape[0]
    qk_spec = pl.BlockSpec((None, DN_HK, DN_DK), lambda b: (b, 0, 0))
    vz_spec = pl.BlockSpec((None, DN_HV, DN_DV), lambda b: (b, 0, 0))
    s_spec = pl.BlockSpec((None, None, DN_HV, DN_DK, DN_DV), lambda b: (layer, b, 0, 0, 0))
    return pl.pallas_call(
        _dn_step_body,
        grid=(db,),
        in_specs=[qk_spec, qk_spec, vz_spec, vz_spec,
                  pl.BlockSpec((None, 1, LANES), lambda b: (b, 0, 0)),
                  pl.BlockSpec((1, DN_DV), lambda b: (0, 0)), s_spec],
        out_specs=[vz_spec, s_spec],
        out_shape=[jax.ShapeDtypeStruct((db, DN_HV, DN_DV), F32), jax.ShapeDtypeStruct(s_all.shape, F32)],
        input_output_aliases={6: 1},
        compiler_params=pltpu.CompilerParams(dimension_semantics=("parallel",), vmem_limit_bytes=VMEM_LIMIT),
        name="dn_step",
    )(q, k, v, z, gates, norm_g.reshape(1, DN_DV), s_all)


def _dn_layer(xp, xs, mod_p, mod_s, seq, conv_prev, s_all, layer, w_in, conv_w, a_log, dt_bias, norm_g, w_out,
              ln_g, ln_b, alpha):
    mp, d = xp.shape
    bp = mp // seq
    db = xs.shape[0]
    conv_dim = 2 * DN_HK * DN_DK + DN_HV * DN_DV
    v_w = DN_HV * DN_DV
    w_qkv = w_in[:, :conv_dim].astype(BF16)
    w_z = w_in[:, conv_dim:conv_dim + v_w].astype(BF16)
    w_ab = jnp.pad(w_in[:, conv_dim + v_w:], ((0, 0), (0, LANES - 2 * DN_HV))).astype(BF16)
    w_o = w_out.astype(BF16)
    alog = jnp.pad(a_log, (0, LANES - DN_HV)).reshape(1, LANES)
    dtb = jnp.pad(dt_bias, (0, LANES - DN_HV)).reshape(1, LANES)
    lane_row = pl.BlockSpec((1, LANES), lambda i, j: (0, 0))
    gate_ops = [(alog, lane_row), (dtb, lane_row)]

    qkv = _in_proj(xp, mod_p, seq, w_qkv, tm=1024, tn=512, name="dn_in_qkv_p")
    z = _in_proj(xp, mod_p, seq, w_z, tm=1024, tn=512, name="dn_in_z_p")
    gates = _in_proj(xp, mod_p, seq, w_ab, tm=1024, tn=LANES, epi=_dn_gates, epi_ops=gate_ops,
                     name="dn_in_gates_p")
    qkv3 = qkv.reshape(bp, seq, conv_dim)
    cv_p = qkv3[:, seq - (DN_CONV_W - 1):]
    qkvc = _dn_prep(qkv3, conv_w, tt=512, cb=512)
    gates_g = jnp.swapaxes(_group_gates(gates.reshape(bp, seq, LANES), (bp, seq), DN_HB), 1, 2)
    o, s_p = _dn_chunks(qkvc, z.reshape(bp, seq, v_w), gates_g, norm_g, hb=DN_HB)
    xp_new = _out_proj_norm(o.reshape(mp, v_w), w_o, xp, mod_p, seq, ln_g, ln_b, tm=512, alpha=alpha,
                            name="dn_out_p")

    qkv_s = _in_proj(xs, mod_s, 1, w_qkv, tm=db, tn=512, name="dn_in_qkv_s")
    z_s = _in_proj(xs, mod_s, 1, w_z, tm=db, tn=512, name="dn_in_z_s")
    gates_s = _in_proj(xs, mod_s, 1, w_ab, tm=db, tn=LANES, epi=_dn_gates, epi_ops=gate_ops,
                       name="dn_in_gates_s")
    qkvc_s, cv_t = _dn_step_prep(qkv_s, jnp.swapaxes(conv_prev, 0, 1), conv_w, cb=512)
    k_w = DN_HK * DN_DK
    o_s, s_s = _dn_step(qkvc_s[:, :k_w].reshape(db, DN_HK, DN_DK),
                        qkvc_s[:, k_w:2 * k_w].reshape(db, DN_HK, DN_DK),
                        qkvc_s[:, 2 * k_w:].reshape(db, DN_HV, DN_DV),
                        z_s.reshape(db, DN_HV, DN_DV), gates_s.reshape(db, 1, LANES), norm_g, s_all, layer)
    xs_new = _out_proj_norm(o_s.reshape(db, v_w), w_o, xs, mod_s, 1, ln_g, ln_b, tm=db, alpha=alpha,
                            name="dn_out_s")
    return xp_new, xs_new, s_p, cv_p, s_s, jnp.swapaxes(cv_t, 0, 1)


def _diff_lambda(lq1_ref, lk1_ref, lq2_ref, lk2_ref, lam_init):
    e1 = jnp.exp(jnp.sum(lq1_ref[...] * lk1_ref[...], -1, keepdims=True))
    e2 = jnp.exp(jnp.sum(lq2_ref[...] * lk2_ref[...], -1, keepdims=True))
    return e1 - e2 + lam_init


def _diff_head_out(o, sg, z, lam_init):
    on = o * lax.rsqrt(jnp.mean(o * o, -1, keepdims=True) + RMS_EPS) * sg * (1.0 - lam_init)
    return on * _silu(z)


def _diff_flash_body(q_ref, k_ref, v_ref, z_ref, lq1_ref, lk1_ref, lq2_ref, lk2_ref, sg_ref, o_ref,
                     m_ref, l_ref, acc_ref, *, lam_init):
    i = pl.program_id(2)
    j = pl.program_id(3)
    tq, tk = q_ref.shape[0], k_ref.shape[0]
    gw = DIFF_G * tq

    @pl.when(j == 0)
    def _():
        m_ref[...] = jnp.full_like(m_ref, -jnp.inf)
        l_ref[...] = jnp.zeros_like(l_ref)
        acc_ref[...] = jnp.zeros_like(acc_ref)

    def update(masked):
        v = v_ref[...].astype(BF16)
        if masked:
            keep = (lax.broadcasted_iota(jnp.int32, (tk, gw), 0)
                    <= (lax.broadcasted_iota(jnp.int32, (tk, gw), 1) & (tq - 1)))
        for half in range(2):
            qm = jnp.concatenate([q_ref[:, (g * 2 + half) * DIFF_HD:(g * 2 + half + 1) * DIFF_HD]
                                  for g in range(DIFF_G)], axis=0) * (DIFF_HD ** -0.5)
            st = lax.dot_general(k_ref[:, half * DIFF_HD:(half + 1) * DIFF_HD], qm, _NT,
                                 preferred_element_type=F32)
            if masked:
                st = jnp.where(keep, st, -jnp.inf)
            cols = slice(half * gw, (half + 1) * gw)
            m_prev = m_ref[:, cols]
            m_new = jnp.maximum(m_prev, jnp.max(st, 0, keepdims=True))
            alpha = jnp.exp(m_prev - m_new)
            p = jnp.exp(st - m_new)
            l_ref[:, cols] = alpha * l_ref[:, cols] + jnp.sum(p, 0, keepdims=True)
            acc_ref[:, cols] = alpha * acc_ref[:, cols] + lax.dot_general(
                v, p.astype(BF16), _TN, preferred_element_type=F32)
            m_ref[:, cols] = m_new

    @pl.when(j < i)
    def _():
        update(False)

    @pl.when(j == i)
    def _():
        update(True)
        lam = _diff_lambda(lq1_ref, lk1_ref, lq2_ref, lk2_ref, lam_init)
        sg = sg_ref[...]
        for g in range(DIFF_G):
            c0 = slice(g * tq, (g + 1) * tq)
            c1 = slice(gw + g * tq, gw + (g + 1) * tq)
            ot = acc_ref[:, c0] * (1.0 / l_ref[:, c0]) - lam * (acc_ref[:, c1] * (1.0 / l_ref[:, c1]))
            o_ref[:, g * DIFF_VD:(g + 1) * DIFF_VD] = _diff_head_out(
                ot.T, sg, z_ref[:, g * DIFF_VD:(g + 1) * DIFF_VD], lam_init)


def _diff_flash(q, k, v, z, lams, subln_g, lam_init, *, tq):
    b, t, _ = q.shape
    assert t % tq == 0 and tq & (tq - 1) == 0, (t, tq)
    nq = t // tq
    qw = DIFF_G * 2 * DIFF_HD
    ow = DIFF_G * DIFF_VD
    n_sub = 2 * DIFF_G
    vec = pl.BlockSpec((1, DIFF_HD), lambda bi, h, i, j: (0, 0))
    return pl.pallas_call(
        functools.partial(_diff_flash_body, lam_init=lam_init),
        grid=(b, DIFF_KVH, nq, nq),
        in_specs=[pl.BlockSpec((None, tq, qw), lambda bi, h, i, j: (bi, i, h)),
                  pl.BlockSpec((None, tq, 2 * DIFF_HD), lambda bi, h, i, j: (bi, jnp.minimum(j, i), h)),
                  pl.BlockSpec((None, tq, DIFF_VD), lambda bi, h, i, j: (bi, jnp.minimum(j, i), h)),
                  pl.BlockSpec((None, tq, ow), lambda bi, h, i, j: (bi, i, h)),
                  vec, vec, vec, vec,
                  pl.BlockSpec((1, DIFF_VD), lambda bi, h, i, j: (0, 0))],
        out_specs=pl.BlockSpec((None, tq, ow), lambda bi, h, i, j: (bi, i, h)),
        out_shape=jax.ShapeDtypeStruct((b, t, DIFF_H * DIFF_VD), F32),
        scratch_shapes=[pltpu.VMEM((1, n_sub * tq), F32), pltpu.VMEM((1, n_sub * tq), F32),
                        pltpu.VMEM((DIFF_VD, n_sub * tq), F32)],
        compiler_params=pltpu.CompilerParams(
            dimension_semantics=("parallel", "parallel", "parallel", "arbitrary"),
            vmem_limit_bytes=VMEM_LIMIT),
        name="diff_flash",
    )(q, k, v, z, *lams, subln_g.reshape(1, DIFF_VD))


def _diff_decode_body(pt_ref, q_ref, ks_ref, vs_ref, z_ref, lq1_ref, lk1_ref, lq2_ref, lk2_ref, sg_ref,
                      *rest, pp, lam_init):
    del pt_ref
    k_pages = rest[:pp]
    v_pages = rest[pp:2 * pp]
    o_ref = rest[2 * pp]
    qb_ref, m_ref, l_ref, acc_ref = rest[2 * pp + 1:]
    j = pl.program_id(1)
    n_rows = DIFF_KVH * DIFF_G * 2
    head_shift = (DIFF_G * 2).bit_length() - 1
    kw = 2 * DIFF_HD
    page_rows = k_pages[0].shape[0]

    @pl.when(j == 0)
    def _():
        d_i = lax.broadcasted_iota(jnp.int32, (DIFF_HD, kw), 0)
        c_i = lax.broadcasted_iota(jnp.int32, (DIFF_HD, kw), 1)
        spread = jnp.where((c_i & (DIFF_HD - 1)) == d_i, 1.0, 0.0)
        qt = jnp.dot(q_ref[...], spread, preferred_element_type=F32)
        r_i = lax.broadcasted_iota(jnp.int32, (n_rows, kw), 0)
        half = lax.broadcasted_iota(jnp.int32, (n_rows, kw), 1) >> (DIFF_HD.bit_length() - 1)
        qb_ref[...] = jnp.where(half == (r_i & 1), qt, 0.0) * (DIFF_HD ** -0.5)
        m_ref[...] = jnp.full_like(m_ref, -jnp.inf)
        l_ref[...] = jnp.zeros_like(l_ref)
        acc_ref[...] = jnp.zeros_like(acc_ref)

    qb = qb_ref[...]
    s = jnp.concatenate([lax.dot_general(qb, kp[...], _NT, preferred_element_type=F32) for kp in k_pages],
                        axis=1)
    row_head = lax.broadcasted_iota(jnp.int32, s.shape, 0) >> head_shift
    col_head = lax.broadcasted_iota(jnp.int32, s.shape, 1) & (DIFF_KVH - 1)
    s = jnp.where(row_head == col_head, s, -jnp.inf)
    m_prev = m_ref[...]
    m_new = jnp.maximum(m_prev, jnp.max(s, -1, keepdims=True))
    alpha = jnp.exp(m_prev - m_new)
    p = jnp.exp(s - m_new)
    l_new = alpha * l_ref[...] + jnp.sum(p, -1, keepdims=True)
    pv = jnp.dot(p[:, 0:page_rows], v_pages[0][...], preferred_element_type=F32)
    for t in range(1, pp):
        pv = pv + jnp.dot(p[:, t * page_rows:(t + 1) * page_rows], v_pages[t][...],
                          preferred_element_type=F32)
    acc_new = alpha * acc_ref[...] + pv
    m_ref[...] = m_new
    l_ref[...] = l_new
    acc_ref[...] = acc_new

    @pl.when(j == pl.num_programs(1) - 1)
    def _():
        rh = lax.broadcasted_iota(jnp.int32, (n_rows, kw), 0) >> head_shift
        ks = jnp.zeros((n_rows, kw), F32)
        vs = jnp.zeros((n_rows, kw), F32)
        for h in range(DIFF_KVH):
            ks = jnp.where(rh == h, ks_ref[h:h + 1, :], ks)
            vs = jnp.where(rh == h, vs_ref[h:h + 1, :], vs)
        s_self = jnp.sum(qb * ks, -1, keepdims=True)
        m_fin = jnp.maximum(m_new, s_self)
        a = jnp.exp(m_new - m_fin)
        p_self = jnp.exp(s_self - m_fin)
        l_fin = a * l_new + p_self
        acc = (a * acc_new + p_self * vs) / l_fin
        lam = _diff_lambda(lq1_ref, lk1_ref, lq2_ref, lk2_ref, lam_init)
        sg = sg_ref[...]
        for hh in range(DIFF_H):
            o = acc[2 * hh:2 * hh + 1] - lam * acc[2 * hh + 1:2 * hh + 2]
            o_ref[hh:hh + 1, :] = _diff_head_out(o, sg, z_ref[hh:hh + 1, :], lam_init)


def _diff_decode(q, k_self, v_self, z, cache_k, cache_v, layer, page_table, lams, subln_g, lam_init, *, pp):
    db = q.shape[0]
    n_pages = page_table.shape[1]
    assert n_pages % pp == 0
    page_rows, kw = cache_k.shape[2:]
    n_rows = DIFF_KVH * DIFF_G * 2

    def page_spec(t):
        return pl.BlockSpec((None, None, page_rows, kw), lambda b, j, pt: (layer, pt[b, j * pp + t], 0, 0))

    def seq_spec(shape):
        return pl.BlockSpec((None,) + shape, lambda b, j, pt: (b, 0, 0))

    vec = pl.BlockSpec((1, DIFF_HD), lambda b, j, pt: (0, 0))
    grid_spec = pltpu.PrefetchScalarGridSpec(
        num_scalar_prefetch=1,
        grid=(db, n_pages // pp),
        in_specs=[seq_spec((n_rows, DIFF_HD)), seq_spec((DIFF_KVH, kw)), seq_spec((DIFF_KVH, kw)),
                  seq_spec((DIFF_H, DIFF_VD)), vec, vec, vec, vec,
                  pl.BlockSpec((1, DIFF_VD), lambda b, j, pt: (0, 0))]
                 + [page_spec(t) for t in range(pp)] + [page_spec(t) for t in range(pp)],
        out_specs=seq_spec((DIFF_H, DIFF_VD)),
        scratch_shapes=[pltpu.VMEM((n_rows, kw), F32), pltpu.VMEM((n_rows, 1), F32),
                        pltpu.VMEM((n_rows, 1), F32), pltpu.VMEM((n_rows, DIFF_VD), F32)],
    )
    return pl.pallas_call(
        functools.partial(_diff_decode_body, pp=pp, lam_init=lam_init),
        grid_spec=grid_spec,
        out_shape=jax.ShapeDtypeStruct((db, DIFF_H, DIFF_VD), F32),
        compiler_params=pltpu.CompilerParams(dimension_semantics=("parallel", "arbitrary"),
                                             vmem_limit_bytes=VMEM_LIMIT),
        name="diff_decode",
    )(page_table, q, k_self, v_self, z, *lams, subln_g.reshape(1, DIFF_VD),
      *([cache_k] * pp), *([cache_v] * pp))


def _diff_layer(xp, xs, mod_p, mod_s, seq, layer_idx, j, cache_k, cache_v, page_table, w_in, lams, subln_g,
                w_out, ln_g, ln_b, alpha):
    mp, d = xp.shape
    bp = mp // seq
    db = xs.shape[0]
    lam_init = 0.8 - 0.6 * math.exp(-0.3 * layer_idx)
    q_w = DIFF_H * 2 * DIFF_HD
    k_w = DIFF_KVH * 2 * DIFF_HD
    v_w = DIFF_KVH * DIFF_VD
    w_q = w_in[:, :q_w].astype(BF16)
    w_k = w_in[:, q_w:q_w + k_w].astype(BF16)
    w_v = w_in[:, q_w + k_w:q_w + k_w + v_w].astype(BF16)
    w_z = w_in[:, q_w + k_w + v_w:].astype(BF16)
    w_o = w_out.astype(BF16)
    lams = [a.reshape(1, DIFF_HD) for a in lams]

    q = _in_proj(xp, mod_p, seq, w_q, tm=1024, tn=512, name="diff_in_q_p")
    k = _in_proj(xp, mod_p, seq, w_k, tm=1024, tn=512, name="diff_in_k_p")
    v = _in_proj(xp, mod_p, seq, w_v, tm=1024, tn=512, name="diff_in_v_p")
    z = _in_proj(xp, mod_p, seq, w_z, tm=1024, tn=512, name="diff_in_z_p")
    o = _diff_flash(q.reshape(bp, seq, q_w), k.reshape(bp, seq, k_w), v.reshape(bp, seq, v_w),
                    z.reshape(bp, seq, -1), lams, subln_g, lam_init, tq=min(DIFF_TQ, seq))
    xp_new = _out_proj_norm(o.reshape(mp, -1), w_o, xp, mod_p, seq, ln_g, ln_b, tm=512, alpha=alpha,
                            name="diff_out_p")

    q_s = _in_proj(xs, mod_s, 1, w_q, tm=db, tn=512, name="diff_in_q_s")
    k_s = _in_proj(xs, mod_s, 1, w_k, tm=db, tn=512, name="diff_in_k_s")
    v_s = _in_proj(xs, mod_s, 1, w_v, tm=db, tn=512, name="diff_in_v_s")
    z_s = _in_proj(xs, mod_s, 1, w_z, tm=db, tn=512, name="diff_in_z_s")
    n_l, n_pool, page = cache_k.shape[:3]
    o_s = _diff_decode(q_s.reshape(db, DIFF_KVH * DIFF_G * 2, DIFF_HD), k_s.reshape(db, DIFF_KVH, 2 * DIFF_HD),
                       v_s.reshape(db, DIFF_KVH, DIFF_VD), z_s.reshape(db, DIFF_H, DIFF_VD),
                       cache_k.reshape(n_l, n_pool, page * DIFF_KVH, 2 * DIFF_HD),
                       cache_v.reshape(n_l, n_pool, page * DIFF_KVH, DIFF_VD),
                       j, page_table, lams, subln_g, lam_init, pp=DIFF_PAGES_PER_STEP)
    xs_new = _out_proj_norm(o_s.reshape(db, -1), w_o, xs, mod_s, 1, ln_g, ln_b, tm=db, alpha=alpha,
                            name="diff_out_s")
    return (xp_new, xs_new, k.reshape(bp, seq, DIFF_KVH, 2 * DIFF_HD), v.reshape(bp, seq, DIFF_KVH, DIFF_VD),
            k_s.reshape(db, 1, DIFF_KVH, 2 * DIFF_HD), v_s.reshape(db, 1, DIFF_KVH, DIFF_VD))


def _swa_band_body(q_ref, kc_ref, kp_ref, vc_ref, vp_ref, z_ref, sink_ref, o_ref):
    i = pl.program_id(1)
    blk = q_ref.shape[0]
    cols = SWA_G * blk
    kj = lax.broadcasted_iota(jnp.int32, (2 * blk, cols), 0)
    qi = lax.broadcasted_iota(jnp.int32, (2 * blk, cols), 1) & (blk - 1)
    dist = qi + blk - kj
    keep = (dist >= 0) & (dist <= WINDOW) & ((i > 0) | (kj >= blk))
    for h in range(SWA_KVH):
        hs = slice(h * SWA_HD, (h + 1) * SWA_HD)
        k2 = jnp.concatenate([kp_ref[:, hs], kc_ref[:, hs]], axis=0)
        v2 = jnp.concatenate([vp_ref[:, hs], vc_ref[:, hs]], axis=0)
        qs = jnp.concatenate([q_ref[:, (h * SWA_G + g) * SWA_HD:(h * SWA_G + g + 1) * SWA_HD]
                              for g in range(SWA_G)], axis=0) * (SWA_HD ** -0.5)
        sink = jnp.concatenate([jnp.broadcast_to(sink_ref[:, h * SWA_G + g:h * SWA_G + g + 1], (1, blk))
                                for g in range(SWA_G)], axis=1)
        st = lax.dot_general(k2, qs, _NT, preferred_element_type=F32)
        st = jnp.where(keep, st, -jnp.inf)
        m = jnp.maximum(jnp.max(st, 0, keepdims=True), sink)
        p = jnp.exp(st - m)
        den = jnp.sum(p, 0, keepdims=True) + jnp.exp(sink - m)
        ot = lax.dot_general(v2, p, _TN, preferred_element_type=F32) * (1.0 / den)
        for g in range(0, SWA_G, 2):
            cs = slice((h * SWA_G + g) * SWA_HD, (h * SWA_G + g + 2) * SWA_HD)
            pair = jnp.concatenate([ot[:, g * blk:(g + 1) * blk], ot[:, (g + 1) * blk:(g + 2) * blk]], axis=0)
            o_ref[:, cs] = pair.T * _silu(z_ref[:, cs])


def _swa_band(q, k, v, z, sinks):
    b, t, qw = q.shape
    blk = WINDOW
    kw = SWA_KVH * SWA_HD
    cur = pl.BlockSpec((None, blk, kw), lambda bi, i: (bi, i, 0))
    prev = pl.BlockSpec((None, blk, kw), lambda bi, i: (bi, jnp.maximum(i - 1, 0), 0))
    wide = pl.BlockSpec((None, blk, qw), lambda bi, i: (bi, i, 0))
    return pl.pallas_call(
        _swa_band_body,
        grid=(b, t // blk),
        in_specs=[wide, cur, prev, cur, prev, wide, pl.BlockSpec((1, SWA_H), lambda bi, i: (0, 0))],
        out_specs=wide,
        out_shape=jax.ShapeDtypeStruct((b, t, qw), F32),
        compiler_params=pltpu.CompilerParams(dimension_semantics=("parallel", "parallel"),
                                             vmem_limit_bytes=VMEM_LIMIT),
        name="swa_band",
    )(q, k, k, v, v, z, sinks.reshape(1, SWA_H))


def _swa_decode_body(q_ref, kb_ref, vb_ref, kn_ref, vn_ref, z_ref, sink_ref, o_ref, ko_ref, vo_ref):
    win = kb_ref.shape[1]
    for h in range(SWA_KVH):
        hs = slice(h * SWA_HD, (h + 1) * SWA_HD)
        gs = slice(h * SWA_G, (h + 1) * SWA_G)
        qh = q_ref[:, gs, :]
        s = jnp.einsum("bqd,bkd->bqk", qh, kb_ref[:, :, hs], preferred_element_type=F32) * (SWA_HD ** -0.5)
        s_new = jnp.sum(qh * kn_ref[:, :, hs], -1, keepdims=True) * (SWA_HD ** -0.5)
        sink = sink_ref[gs, :][None]
        m = jnp.maximum(jnp.maximum(jnp.max(s, -1, keepdims=True), s_new), sink)
        p = jnp.exp(s - m)
        p_new = jnp.exp(s_new - m)
        den = jnp.sum(p, -1, keepdims=True) + p_new + jnp.exp(sink - m)
        o = jnp.einsum("bqk,bkd->bqd", p, vb_ref[:, :, hs], preferred_element_type=F32)
        o = (o + p_new * vn_ref[:, :, hs]) / den
        o_ref[:, gs, :] = o * _silu(z_ref[:, gs, :])
    ko_ref[:, 0:win - 1, :] = kb_ref[:, 1:win, :]
    ko_ref[:, win - 1:win, :] = kn_ref[...]
    vo_ref[:, 0:win - 1, :] = vb_ref[:, 1:win, :]
    vo_ref[:, win - 1:win, :] = vn_ref[...]


def _swa_decode(q, k_new, v_new, z, buf_k, buf_v, sinks, *, bb):
    db, win, kw = buf_k.shape
    qz = pl.BlockSpec((bb, SWA_H, SWA_HD), lambda b: (b, 0, 0))
    buf = pl.BlockSpec((bb, win, kw), lambda b: (b, 0, 0))
    new = pl.BlockSpec((bb, 1, kw), lambda b: (b, 0, 0))
    return pl.pallas_call(
        _swa_decode_body,
        grid=(db // bb,),
        in_specs=[qz, buf, buf, new, new, qz, pl.BlockSpec((SWA_H, 1), lambda b: (0, 0))],
        out_specs=[qz, buf, buf],
        out_shape=[jax.ShapeDtypeStruct(q.shape, F32), jax.ShapeDtypeStruct(buf_k.shape, F32),
                   jax.ShapeDtypeStruct(buf_v.shape, F32)],
        compiler_params=pltpu.CompilerParams(dimension_semantics=("parallel",), vmem_limit_bytes=VMEM_LIMIT),
        name="swa_decode",
    )(q, buf_k, buf_v, k_new, v_new, z, sinks.reshape(SWA_H, 1))


def _swa_layer(xp, xs, mod_p, mod_s, seq, buf_k, buf_v, sinks, w_in, w_out, ln_g, ln_b, alpha):
    mp, d = xp.shape
    bp = mp // seq
    db = xs.shape[0]
    q_w = SWA_H * SWA_HD
    kv_w = SWA_KVH * SWA_HD
    w_q = w_in[:, :q_w].astype(BF16)
    w_kv = w_in[:, q_w:q_w + 2 * kv_w].astype(BF16)
    w_z = w_in[:, q_w + 2 * kv_w:].astype(BF16)
    w_o = w_out.astype(BF16)

    q = _in_proj(xp, mod_p, seq, w_q, tm=1024, tn=512, name="swa_in_q_p")
    kv = _in_proj(xp, mod_p, seq, w_kv, tm=1024, tn=2 * kv_w, name="swa_in_kv_p")
    z = _in_proj(xp, mod_p, seq, w_z, tm=1024, tn=512, name="swa_in_z_p")
    k = kv[:, :kv_w].reshape(bp, seq, kv_w)
    v = kv[:, kv_w:].reshape(bp, seq, kv_w)
    o = _swa_band(q.reshape(bp, seq, q_w), k, v, z.reshape(bp, seq, q_w), sinks)
    xp_new = _out_proj_norm(o.reshape(mp, q_w), w_o, xp, mod_p, seq, ln_g, ln_b, tm=512, alpha=alpha,
                            name="swa_out_p")

    q_s = _in_proj(xs, mod_s, 1, w_q, tm=db, tn=512, name="swa_in_q_s")
    kv_s = _in_proj(xs, mod_s, 1, w_kv, tm=db, tn=2 * kv_w, name="swa_in_kv_s")
    z_s = _in_proj(xs, mod_s, 1, w_z, tm=db, tn=512, name="swa_in_z_s")
    win = buf_k.shape[1]
    o_s, k_buf, v_buf = _swa_decode(q_s.reshape(db, SWA_H, SWA_HD), kv_s[:, :kv_w].reshape(db, 1, kv_w),
                                    kv_s[:, kv_w:].reshape(db, 1, kv_w), z_s.reshape(db, SWA_H, SWA_HD),
                                    buf_k.reshape(db, win, kv_w), buf_v.reshape(db, win, kv_w), sinks,
                                    bb=SWA_DEC_BB)
    xs_new = _out_proj_norm(o_s.reshape(db, q_w), w_o, xs, mod_s, 1, ln_g, ln_b, tm=db, alpha=alpha,
                            name="swa_out_s")
    kv_shape = (SWA_KVH, SWA_HD)
    return (xp_new, xs_new, k[:, seq - WINDOW:].reshape((bp, WINDOW) + kv_shape),
            v[:, seq - WINDOW:].reshape((bp, WINDOW) + kv_shape),
            k_buf.reshape((db, win) + kv_shape), v_buf.reshape((db, win) + kv_shape))


def kernel(x_prompt, x_sample, cache_diff_k, cache_diff_v, page_table, cache_swa_k, cache_swa_v, state_dn_S, state_dn_conv, c_prompt, c_sample, ada_w, ada_b, ln_g, ln_b, dn_w_in, dn_conv_w, dn_a_log, dn_dt_bias, dn_norm_g, dn_w_out, diff_w_in, diff_lq1, diff_lk1, diff_lq2, diff_lk2, diff_subln_g, diff_w_out, swa_w_in, swa_sinks, swa_w_out):
    bp, seq, d = x_prompt.shape
    db, dec_seq, _ = x_sample.shape
    assert dec_seq == 1 and bp <= SUBLANES
    depth = ada_w.shape[0]
    alpha = (2 * depth) ** 0.25
    xp = x_prompt.reshape(bp * seq, d)
    xs = x_sample.reshape(db, d)

    c_all = jnp.concatenate([c_prompt, jnp.zeros((SUBLANES - bp, d), F32), c_sample], axis=0)
    st = {n: [] for n in ("dk_p", "dv_p", "dk_s", "dv_s", "sk_p", "sv_p", "sk_s", "sv_s",
                          "S_p", "cv_p", "cv_s")}
    s_s = state_dn_S
    for l in range(depth):
        j = l // N_MIXERS
        mod = _mm(c_all, ada_w[l].astype(BF16), tm=c_all.shape[0], tn=1024, pro=_silu, epi=_add_bias,
                  epi_ops=[(ada_b[l].reshape(1, 3 * d), pl.BlockSpec((1, 1024), lambda i, jj: (0, jj)))],
                  name="adaln")
        mod_p = mod[:bp].reshape(bp, 1, 3 * d)
        mod_s = mod[SUBLANES:]
        if l % N_MIXERS == 0:
            xp, xs, s_p, cv_p, s_s, cv_s = _dn_layer(
                xp, xs, mod_p, mod_s, seq, state_dn_conv[j], s_s, j, dn_w_in[j], dn_conv_w[j],
                dn_a_log[j], dn_dt_bias[j], dn_norm_g[j], dn_w_out[j], ln_g[l], ln_b[l], alpha)
            st["S_p"].append(s_p)
            st["cv_p"].append(cv_p)
            st["cv_s"].append(cv_s)
        elif l % N_MIXERS == 1:
            xp, xs, k_p, v_p, k_s, v_s = _diff_layer(
                xp, xs, mod_p, mod_s, seq, l, j, cache_diff_k, cache_diff_v, page_table, diff_w_in[j],
                (diff_lq1[j], diff_lk1[j], diff_lq2[j], diff_lk2[j]), diff_subln_g[j], diff_w_out[j],
                ln_g[l], ln_b[l], alpha)
            st["dk_p"].append(k_p)
            st["dv_p"].append(v_p)
            st["dk_s"].append(k_s)
            st["dv_s"].append(v_s)
        else:
            xp, xs, k_p, v_p, k_s, v_s = _swa_layer(
                xp, xs, mod_p, mod_s, seq, cache_swa_k[j], cache_swa_v[j], swa_sinks[j], swa_w_in[j],
                swa_w_out[j], ln_g[l], ln_b[l], alpha)
            st["sk_p"].append(k_p)
            st["sv_p"].append(v_p)
            st["sk_s"].append(k_s)
            st["sv_s"].append(v_s)
    return (xp.reshape(bp, seq, d), xs.reshape(db, 1, d),
            jnp.stack(st["dk_p"]), jnp.stack(st["dv_p"]), jnp.stack(st["dk_s"]), jnp.stack(st["dv_s"]),
            jnp.stack(st["sk_p"]), jnp.stack(st["sv_p"]), jnp.stack(st["sk_s"]), jnp.stack(st["sv_s"]),
            jnp.stack(st["S_p"]), jnp.stack(st["cv_p"]), s_s, jnp.stack(st["cv_s"]))
```

```python
import functools
import math

import jax
import jax.numpy as jnp
from jax import lax
from jax.experimental import pallas as pl
from jax.experimental.pallas import tpu as pltpu

F32 = jnp.float32
BF16 = jnp.bfloat16

LN_EPS = 1e-5
RMS_EPS = 1e-6
N_MIXERS = 3

DN_HK = 8
DN_HV = 16
DN_DK = 128
DN_DV = 128
DN_REP = DN_HV // DN_HK
DN_CONV_W = 4
DN_CHUNK = 128

DIFF_H = 8
DIFF_KVH = 4
DIFF_G = DIFF_H // DIFF_KVH
DIFF_HD = 64
DIFF_VD = 2 * DIFF_HD
DIFF_TQ = 512
LOG2E = math.log2(math.e)
DIFF_PAGES_PER_STEP = 16

SWA_H = 16
SWA_KVH = 2
SWA_G = SWA_H // SWA_KVH
SWA_HD = 64
WINDOW = 128
SWA_DEC_BB = 8

LANES = 128
SUBLANES = 8
VMEM_LIMIT = 48 * 1024 * 1024

_NT = (((1,), (1,)), ((), ()))
_TN = (((0,), (0,)), ((), ()))


def _silu(x):
    return x * jax.nn.sigmoid(x)


def _mm_body(*refs, n_pro, n_epi, pro, epi):
    x_ref, w_ref = refs[0], refs[1]
    pro_refs = refs[2:2 + n_pro]
    epi_refs = refs[2 + n_pro:2 + n_pro + n_epi]
    o_ref = refs[2 + n_pro + n_epi]
    xb_ref = refs[3 + n_pro + n_epi]

    @pl.when(pl.program_id(1) == 0)
    def _():
        x = x_ref[...]
        if pro is not None:
            x = pro(x, *pro_refs)
        xb_ref[...] = x.astype(xb_ref.dtype)

    acc = jnp.dot(xb_ref[...], w_ref[...], preferred_element_type=F32)
    if epi is not None:
        acc = epi(acc, *epi_refs)
    o_ref[...] = acc.astype(o_ref.dtype)


def _mm(x, w, *, tm, tn, pro=None, pro_ops=(), epi=None, epi_ops=(), out_dtype=F32, name="mm"):
    m, k = x.shape
    n = w.shape[1]
    assert m % tm == 0 and n % tn == 0, (m, tm, n, tn)
    ops = [x, w]
    in_specs = [pl.BlockSpec((tm, k), lambda i, j: (i, 0)), pl.BlockSpec((k, tn), lambda i, j: (0, j))]
    for a, s in (*pro_ops, *epi_ops):
        ops.append(a)
        in_specs.append(s)
    body = functools.partial(_mm_body, n_pro=len(pro_ops), n_epi=len(epi_ops), pro=pro, epi=epi)
    return pl.pallas_call(
        body,
        grid=(m // tm, n // tn),
        in_specs=in_specs,
        out_specs=pl.BlockSpec((tm, tn), lambda i, j: (i, j)),
        out_shape=jax.ShapeDtypeStruct((m, n), out_dtype),
        scratch_shapes=[pltpu.VMEM((tm, k), BF16)],
        compiler_params=pltpu.CompilerParams(dimension_semantics=("parallel", "arbitrary"),
                                             vmem_limit_bytes=VMEM_LIMIT),
        name=name,
    )(*ops)


def _modulate(x, mod_ref):
    d = x.shape[-1]
    return x * (1.0 + mod_ref[:, d:2 * d]) + mod_ref[:, 0:d]


def _deepnorm(y, xres_ref, mod_ref, g_ref, b_ref, *, alpha):
    d = y.shape[-1]
    h = alpha * xres_ref[...] + (1.0 + mod_ref[:, 2 * d:3 * d]) * y
    hc = h - jnp.mean(h, -1, keepdims=True)
    var = jnp.mean(hc * hc, -1, keepdims=True)
    return hc * lax.rsqrt(var + LN_EPS) * g_ref[...] + b_ref[...]


def _add_bias(y, b_ref):
    return y + b_ref[...]


def _dn_gates(y, alog_ref, dtb_ref):
    lane = lax.broadcasted_iota(jnp.int32, y.shape, 1)
    t = y + dtb_ref[...]
    softplus = jnp.maximum(t, 0.0) + jnp.log1p(jnp.exp(-jnp.abs(t)))
    g = -jnp.exp(alog_ref[...]) * softplus
    return jnp.where(lane < DN_HV, g, jax.nn.sigmoid(y))


def _mod_spec(mod, tm, rows_per_mod):
    w = mod.shape[-1]
    if mod.ndim == 3:
        assert rows_per_mod % tm == 0, (rows_per_mod, tm)
        return pl.BlockSpec((None, 1, w), lambda i, j: (i * tm // rows_per_mod, 0, 0))
    return pl.BlockSpec((tm, w), lambda i, j: (i, 0))


def _in_proj(x, mod, rows_per_mod, w, *, tm, tn, epi=None, epi_ops=(), name="in_proj"):
    return _mm(x, w, tm=tm, tn=tn, pro=_modulate, pro_ops=[(mod, _mod_spec(mod, tm, rows_per_mod))],
               epi=epi, epi_ops=epi_ops, name=name)


def _out_proj_norm(y, w, xres, mod, rows_per_mod, ln_g, ln_b, *, tm, alpha, name="out_proj"):
    d = w.shape[1]
    row = pl.BlockSpec((1, d), lambda i, j: (0, 0))
    return _mm(y, w, tm=tm, tn=d, epi=functools.partial(_deepnorm, alpha=alpha),
               epi_ops=[(xres, pl.BlockSpec((tm, d), lambda i, j: (i, 0))),
                        (mod, _mod_spec(mod, tm, rows_per_mod)),
                        (ln_g.reshape(1, d), row), (ln_b.reshape(1, d), row)], name=name)


def _l2norm_heads(y, o_ref, scale):
    for h in range(y.shape[-1] // DN_DK):
        yh = y[:, h * DN_DK:(h + 1) * DN_DK]
        inv = lax.rsqrt(jnp.sum(yh * yh, -1, keepdims=True) + RMS_EPS)
        o_ref[:, h * DN_DK:(h + 1) * DN_DK] = yh * inv * scale


def _dn_prep_body(x_ref, prev_ref, w_ref, o_ref, *, n_q_blk, n_k_blk):
    i = pl.program_id(1)
    c = pl.program_id(2)
    tt, cb = x_ref.shape
    taps = [w_ref[s:s + 1, :] for s in range(DN_CONV_W)]
    backs = range(1, DN_CONV_W)
    row = lax.broadcasted_iota(jnp.int32, (SUBLANES, cb), 0)

    def rolled(v):
        return tuple(pltpu.roll(v, d, axis=0) for d in backs)

    def run(norm_scale):
        def body(si, prev_rolled):
            r0 = pl.multiple_of(si * SUBLANES, SUBLANES)
            cur = x_ref[pl.ds(r0, SUBLANES), :]
            cur_rolled = rolled(cur)
            back = [jnp.where(row < d, p, q) for d, p, q in zip(backs, prev_rolled, cur_rolled)]
            conv = back[DN_CONV_W - 2] * taps[0]
            for s in range(1, DN_CONV_W - 1):
                conv = conv + back[DN_CONV_W - 2 - s] * taps[s]
            y = _silu(conv + cur * taps[DN_CONV_W - 1])
            out = o_ref.at[pl.ds(r0, SUBLANES), :]
            if norm_scale is None:
                out[...] = y
            else:
                _l2norm_heads(y, out, norm_scale)
            return cur_rolled

        first = jnp.where(i > 0, prev_ref[...], 0.0)
        lax.fori_loop(0, tt // SUBLANES, body, rolled(first), unroll=8)

    @pl.when(c >= n_q_blk + n_k_blk)
    def _():
        run(None)

    @pl.when(c < n_q_blk + n_k_blk)
    def _():
        run(jnp.where(c < n_q_blk, DN_DK ** -0.5, 1.0))


def _dn_prep(qkv, conv_w, *, tt, cb):
    b, t, c = qkv.shape
    k_w = DN_HK * DN_DK
    body = functools.partial(_dn_prep_body, n_q_blk=k_w // cb, n_k_blk=k_w // cb)
    return pl.pallas_call(
        body,
        grid=(b, t // tt, c // cb),
        in_specs=[pl.BlockSpec((None, tt, cb), lambda bi, i, ci: (bi, i, ci)),
                  pl.BlockSpec((None, SUBLANES, cb),
                               lambda bi, i, ci: (bi, jnp.maximum(i * (tt // SUBLANES) - 1, 0), ci)),
                  pl.BlockSpec((DN_CONV_W, cb), lambda bi, i, ci: (0, ci))],
        out_specs=pl.BlockSpec((None, tt, cb), lambda bi, i, ci: (bi, i, ci)),
        out_shape=jax.ShapeDtypeStruct(qkv.shape, F32),
        compiler_params=pltpu.CompilerParams(dimension_semantics=("parallel", "parallel", "parallel"),
                                             vmem_limit_bytes=VMEM_LIMIT),
        name="dn_prep",
    )(qkv, qkv, conv_w)


def _gated_rms(o, ng, z):
    return o * lax.rsqrt(jnp.mean(o * o, -1, keepdims=True) + RMS_EPS) * ng * _silu(z)


def _dn_chunk_body(x_ref, z_ref, gb_ref, ng_ref, o_ref, s_ref):
    ci = pl.program_id(1)
    c = x_ref.shape[0]
    hb = DN_HV

    @pl.when(ci == 0)
    def _():
        s_ref[...] = jnp.zeros_like(s_ref)

    k_w = DN_HK * DN_DK
    gb = gb_ref[...]
    row = lax.broadcasted_iota(jnp.int32, gb.shape, 0)
    gc = gb
    sh = 1
    while sh < c:
        gc = gc + jnp.where(row >= sh, pltpu.roll(gc, sh, axis=0), 0.0)
        sh *= 2
    gct = gc.T
    ri = lax.broadcasted_iota(jnp.int32, (c, c), 0)
    cj = lax.broadcasted_iota(jnp.int32, (c, c), 1)
    tril = ri >= cj
    strict = ri > cj
    merge_masks = []
    k = 0
    while (1 << k) < c:
        merge_masks.append(((ri >> (k + 1)) == (cj >> (k + 1))) & ((ri >> k) != (cj >> k)))
        k += 1
    ng = ng_ref[...]
    heads = range(hb)
    dot = functools.partial(jnp.dot, preferred_element_type=F32)
    qh = [x_ref[:, i * DN_DK:(i + 1) * DN_DK] for i in range(DN_HK)]
    kh = [x_ref[:, k_w + i * DN_DK:k_w + (i + 1) * DN_DK] for i in range(DN_HK)]
    kk = [lax.dot_general(a, a, _NT, preferred_element_type=F32) for a in kh]
    qk = [lax.dot_general(a, b, _NT, preferred_element_type=F32) for a, b in zip(qh, kh)]
    gcol = [gc[:, j:j + 1] for j in heads]
    bcol = [gb[:, hb + j:hb + j + 1] for j in heads]
    decay = [jnp.where(tril, jnp.exp(jnp.where(tril, gcol[j] - gct[j:j + 1, :], 0.0)), 0.0) for j in heads]
    lmat = [jnp.where(strict, bcol[j] * kk[j // DN_REP] * decay[j], 0.0) for j in heads]
    minv = [jnp.where(merge_masks[0], -lmat[j], 0.0) for j in heads]
    for mask in merge_masks[1:]:
        loff = [jnp.where(mask, lmat[j], 0.0) for j in heads]
        y = [loff[j] + dot(loff[j], minv[j]) for j in heads]
        minv = [minv[j] - y[j] - dot(minv[j], y[j]) for j in heads]
    eg = [jnp.exp(gcol[j]) for j in heads]
    rhs = [jnp.concatenate([x_ref[:, 2 * k_w + j * DN_DV:2 * k_w + (j + 1) * DN_DV] * bcol[j],
                            kh[j // DN_REP] * (bcol[j] * eg[j])], axis=1) for j in heads]
    sol = [rhs[j] + dot(minv[j], rhs[j]) for j in heads]
    s_old = [s_ref[j] for j in heads]
    ws = [dot(jnp.concatenate([sol[j][:, DN_DV:], qh[j // DN_REP] * eg[j]], axis=0), s_old[j]) for j in heads]
    v_new = [sol[j][:, :DN_DV] - ws[j][:c] for j in heads]
    o = [ws[j][c:] + dot(qk[j // DN_REP] * decay[j], v_new[j]) for j in heads]
    for j in heads:
        glast = gc[c - 1:c, j:j + 1]
        kd = kh[j // DN_REP] * jnp.exp(glast - gcol[j])
        s_ref[j] = s_old[j] * jnp.exp(glast) + lax.dot_general(kd, v_new[j], _TN, preferred_element_type=F32)
        o_ref[:, j * DN_DV:(j + 1) * DN_DV] = _gated_rms(o[j], ng, z_ref[:, j * DN_DV:(j + 1) * DN_DV])


def _dn_chunks(qkv, z, gates, norm_g):
    b, t, cd = qkv.shape
    c = DN_CHUNK
    assert t % c == 0
    v_w = DN_HV * DN_DV
    o, s = pl.pallas_call(
        _dn_chunk_body,
        grid=(b, t // c),
        in_specs=[pl.BlockSpec((None, c, cd), lambda bi, ci: (bi, ci, 0)),
                  pl.BlockSpec((None, c, v_w), lambda bi, ci: (bi, ci, 0)),
                  pl.BlockSpec((None, c, LANES), lambda bi, ci: (bi, ci, 0)),
                  pl.BlockSpec((1, DN_DV), lambda bi, ci: (0, 0))],
        out_specs=[pl.BlockSpec((None, c, v_w), lambda bi, ci: (bi, ci, 0)),
                   pl.BlockSpec((None, DN_HV, DN_DK, DN_DV), lambda bi, ci: (bi, 0, 0, 0))],
        out_shape=[jax.ShapeDtypeStruct((b, t, v_w), F32),
                   jax.ShapeDtypeStruct((b, DN_HV, DN_DK, DN_DV), F32)],
        compiler_params=pltpu.CompilerParams(dimension_semantics=("parallel", "arbitrary"),
                                             vmem_limit_bytes=VMEM_LIMIT),
        name="dn_chunks",
    )(qkv, z, gates, norm_g.reshape(1, DN_DV))
    return o, s


def _dn_step_prep_body(x_ref, prev_ref, w_ref, o_ref, cv_ref, *, n_q_blk, n_k_blk):
    c = pl.program_id(0)
    x = x_ref[...]
    w = w_ref[...]
    conv = prev_ref[0] * w[0:1]
    for s in range(1, DN_CONV_W - 1):
        conv = conv + prev_ref[s] * w[s:s + 1]
    conv = conv + x * w[DN_CONV_W - 1:DN_CONV_W]
    y = _silu(conv)
    for s in range(DN_CONV_W - 2):
        cv_ref[s] = prev_ref[s + 1]
    cv_ref[DN_CONV_W - 2] = x

    @pl.when(c >= n_q_blk + n_k_blk)
    def _():
        o_ref[...] = y

    @pl.when(c < n_q_blk + n_k_blk)
    def _():
        _l2norm_heads(y, o_ref, jnp.where(c < n_q_blk, DN_DK ** -0.5, 1.0))


def _dn_step_prep(qkv, conv_prev_t, conv_w, *, cb):
    db, c = qkv.shape
    k_w = DN_HK * DN_DK
    nw = DN_CONV_W - 1
    return pl.pallas_call(
        functools.partial(_dn_step_prep_body, n_q_blk=k_w // cb, n_k_blk=k_w // cb),
        grid=(c // cb,),
        in_specs=[pl.BlockSpec((db, cb), lambda ci: (0, ci)),
                  pl.BlockSpec((nw, db, cb), lambda ci: (0, 0, ci)),
                  pl.BlockSpec((DN_CONV_W, cb), lambda ci: (0, ci))],
        out_specs=[pl.BlockSpec((db, cb), lambda ci: (0, ci)),
                   pl.BlockSpec((nw, db, cb), lambda ci: (0, 0, ci))],
        out_shape=[jax.ShapeDtypeStruct((db, c), F32), jax.ShapeDtypeStruct((nw, db, c), F32)],
        compiler_params=pltpu.CompilerParams(dimension_semantics=("parallel",), vmem_limit_bytes=VMEM_LIMIT),
        name="dn_step_prep",
    )(qkv, conv_prev_t, conv_w)


def _dn_step_body(q_ref, k_ref, v_ref, z_ref, gb_ref, ng_ref, s_ref, o_ref, so_ref):
    gb = gb_ref[...]
    ng = ng_ref[...]
    rows = lax.broadcasted_iota(jnp.int32, (SUBLANES, DN_DK), 0)
    heads = range(DN_HV)
    kh = [k_ref[i:i + 1, :] for i in range(DN_HK)]
    qh = [q_ref[i:i + 1, :] for i in range(DN_HK)]
    qk = [jnp.sum(a * b, -1, keepdims=True) for a, b in zip(qh, kh)]
    lhs = [jnp.where(rows == 0, b, jnp.where(rows == 1, a, 0.0)) for a, b in zip(qh, kh)]
    k8 = [jnp.where(rows == 0, b, 0.0) for b in kh]
    s_old = [s_ref[h] for h in heads]
    ks_qs = [jnp.dot(lhs[h // DN_REP], s_old[h], preferred_element_type=F32) for h in heads]
    eg = [jnp.exp(gb[:, h:h + 1]) for h in heads]
    v_new = [gb[:, DN_HV + h:DN_HV + h + 1] * (v_ref[h:h + 1, :] - eg[h] * ks_qs[h][0:1]) for h in heads]
    outer = [lax.dot_general(k8[h // DN_REP], jnp.where(rows == 0, v_new[h], 0.0), _TN,
                             preferred_element_type=F32) for h in heads]
    for h in heads:
        so_ref[h] = s_old[h] * eg[h] + outer[h]
        o = eg[h] * ks_qs[h][1:2] + qk[h // DN_REP] * v_new[h]
        o_ref[h:h + 1, :] = _gated_rms(o, ng, z_ref[h:h + 1, :])


def _dn_step(q, k, v, z, gates, norm_g, s_all, layer):
    db = q.shape[0]
    qk_spec = pl.BlockSpec((None, DN_HK, DN_DK), lambda b: (b, 0, 0))
    vz_spec = pl.BlockSpec((None, DN_HV, DN_DV), lambda b: (b, 0, 0))
    s_spec = pl.BlockSpec((None, None, DN_HV, DN_DK, DN_DV), lambda b: (layer, b, 0, 0, 0))
    return pl.pallas_call(
        _dn_step_body,
        grid=(db,),
        in_specs=[qk_spec, qk_spec, vz_spec, vz_spec,
                  pl.BlockSpec((None, 1, LANES), lambda b: (b, 0, 0)),
                  pl.BlockSpec((1, DN_DV), lambda b: (0, 0)), s_spec],
        out_specs=[vz_spec, s_spec],
        out_shape=[jax.ShapeDtypeStruct((db, DN_HV, DN_DV), F32), jax.ShapeDtypeStruct(s_all.shape, F32)],
        input_output_aliases={6: 1},
        compiler_params=pltpu.CompilerParams(dimension_semantics=("parallel",), vmem_limit_bytes=VMEM_LIMIT),
        name="dn_step",
    )(q, k, v, z, gates, norm_g.reshape(1, DN_DV), s_all)


def _dn_layer(xp, xs, mod_p, mod_s, seq, conv_prev, s_all, layer, w_in, conv_w, a_log, dt_bias, norm_g, w_out,
              ln_g, ln_b, alpha):
    mp, d = xp.shape
    bp = mp // seq
    db = xs.shape[0]
    conv_dim = 2 * DN_HK * DN_DK + DN_HV * DN_DV
    v_w = DN_HV * DN_DV
    w_qkv = w_in[:, :conv_dim].astype(BF16)
    w_z = w_in[:, conv_dim:conv_dim + v_w].astype(BF16)
    w_ab = jnp.pad(w_in[:, conv_dim + v_w:], ((0, 0), (0, LANES - 2 * DN_HV))).astype(BF16)
    w_o = w_out.astype(BF16)
    alog = jnp.pad(a_log, (0, LANES - DN_HV)).reshape(1, LANES)
    dtb = jnp.pad(dt_bias, (0, LANES - DN_HV)).reshape(1, LANES)
    lane_row = pl.BlockSpec((1, LANES), lambda i, j: (0, 0))
    gate_ops = [(alog, lane_row), (dtb, lane_row)]

    qkv = _in_proj(xp, mod_p, seq, w_qkv, tm=1024, tn=512, name="dn_in_qkv_p")
    z = _in_proj(xp, mod_p, seq, w_z, tm=1024, tn=512, name="dn_in_z_p")
    gates = _in_proj(xp, mod_p, seq, w_ab, tm=1024, tn=LANES, epi=_dn_gates, epi_ops=gate_ops,
                     name="dn_in_gates_p")
    qkv3 = qkv.reshape(bp, seq, conv_dim)
    cv_p = qkv3[:, seq - (DN_CONV_W - 1):]
    qkvc = _dn_prep(qkv3, conv_w, tt=512, cb=512)
    o, s_p = _dn_chunks(qkvc, z.reshape(bp, seq, v_w), gates.reshape(bp, seq, LANES), norm_g)
    xp_new = _out_proj_norm(o.reshape(mp, v_w), w_o, xp, mod_p, seq, ln_g, ln_b, tm=512, alpha=alpha,
                            name="dn_out_p")

    qkv_s = _in_proj(xs, mod_s, 1, w_qkv, tm=db, tn=512, name="dn_in_qkv_s")
    z_s = _in_proj(xs, mod_s, 1, w_z, tm=db, tn=512, name="dn_in_z_s")
    gates_s = _in_proj(xs, mod_s, 1, w_ab, tm=db, tn=LANES, epi=_dn_gates, epi_ops=gate_ops,
                       name="dn_in_gates_s")
    qkvc_s, cv_t = _dn_step_prep(qkv_s, jnp.swapaxes(conv_prev, 0, 1), conv_w, cb=512)
    k_w = DN_HK * DN_DK
    o_s, s_s = _dn_step(qkvc_s[:, :k_w].reshape(db, DN_HK, DN_DK),
                        qkvc_s[:, k_w:2 * k_w].reshape(db, DN_HK, DN_DK),
                        qkvc_s[:, 2 * k_w:].reshape(db, DN_HV, DN_DV),
                        z_s.reshape(db, DN_HV, DN_DV), gates_s.reshape(db, 1, LANES), norm_g, s_all, layer)
    xs_new = _out_proj_norm(o_s.reshape(db, v_w), w_o, xs, mod_s, 1, ln_g, ln_b, tm=db, alpha=alpha,
                            name="dn_out_s")
    return xp_new, xs_new, s_p, cv_p, s_s, jnp.swapaxes(cv_t, 0, 1)


def _diff_lambda(lq1_ref, lk1_ref, lq2_ref, lk2_ref, lam_init):
    e1 = jnp.exp(jnp.sum(lq1_ref[...] * lk1_ref[...], -1, keepdims=True))
    e2 = jnp.exp(jnp.sum(lq2_ref[...] * lk2_ref[...], -1, keepdims=True))
    return e1 - e2 + lam_init


def _diff_head_out(o, sg, z, lam_init):
    on = o * lax.rsqrt(jnp.mean(o * o, -1, keepdims=True) + RMS_EPS) * sg * (1.0 - lam_init)
    return on * _silu(z)


def _diff_flash_body(q_ref, k_ref, v_ref, z_ref, lq1_ref, lk1_ref, lq2_ref, lk2_ref, sg_ref, o_ref,
                     m_ref, l_ref, acc_ref, *, lam_init):
    i = pl.program_id(2)
    j = pl.program_id(3)
    tq, tk = q_ref.shape[0], k_ref.shape[0]
    gw = DIFF_G * tq

    @pl.when(j == 0)
    def _():
        m_ref[...] = jnp.full_like(m_ref, -jnp.inf)
        l_ref[...] = jnp.zeros_like(l_ref)
        acc_ref[...] = jnp.zeros_like(acc_ref)

    def update(masked):
        v = v_ref[...].astype(BF16)
        if masked:
            keep = (lax.broadcasted_iota(jnp.int32, (tk, gw), 0)
                    <= (lax.broadcasted_iota(jnp.int32, (tk, gw), 1) & (tq - 1)))
        sts = []
        for half in range(2):
            qm = jnp.concatenate([q_ref[:, (g * 2 + half) * DIFF_HD:(g * 2 + half + 1) * DIFF_HD]
                                  for g in range(DIFF_G)], axis=0) * (DIFF_HD ** -0.5 * LOG2E)
            st = lax.dot_general(k_ref[:, half * DIFF_HD:(half + 1) * DIFF_HD], qm, _NT,
                                 preferred_element_type=F32)
            sts.append(jnp.where(keep, st, -jnp.inf) if masked else st)
        for half in range(2):
            st = sts[half]
            cols = slice(half * gw, (half + 1) * gw)
            m_prev = m_ref[:, cols]
            m_new = jnp.maximum(m_prev, jnp.max(st, 0, keepdims=True))
            alpha = jnp.exp2(m_prev - m_new)
            p = jnp.exp2(st - m_new)
            l_ref[:, cols] = alpha * l_ref[:, cols] + jnp.sum(p, 0, keepdims=True)
            acc_ref[:, cols] = alpha * acc_ref[:, cols] + lax.dot_general(
                v, p.astype(BF16), _TN, preferred_element_type=F32)
            m_ref[:, cols] = m_new

    @pl.when(j < i)
    def _():
        update(False)

    @pl.when(j == i)
    def _():
        update(True)
        lam = _diff_lambda(lq1_ref, lk1_ref, lq2_ref, lk2_ref, lam_init)
        sg = sg_ref[...]
        for g in range(DIFF_G):
            c0 = slice(g * tq, (g + 1) * tq)
            c1 = slice(gw + g * tq, gw + (g + 1) * tq)
            ot = acc_ref[:, c0] * (1.0 / l_ref[:, c0]) - lam * (acc_ref[:, c1] * (1.0 / l_ref[:, c1]))
            o_ref[:, g * DIFF_VD:(g + 1) * DIFF_VD] = _diff_head_out(
                ot.T, sg, z_ref[:, g * DIFF_VD:(g + 1) * DIFF_VD], lam_init)


def _diff_flash(q, k, v, z, lams, subln_g, lam_init, *, tq):
    b, t, _ = q.shape
    assert t % tq == 0 and tq & (tq - 1) == 0, (t, tq)
    nq = t // tq
    qw = DIFF_G * 2 * DIFF_HD
    ow = DIFF_G * DIFF_VD
    n_sub = 2 * DIFF_G
    vec = pl.BlockSpec((1, DIFF_HD), lambda bi, h, i, j: (0, 0))
    return pl.pallas_call(
        functools.partial(_diff_flash_body, lam_init=lam_init),
        grid=(b, DIFF_KVH, nq, nq),
        in_specs=[pl.BlockSpec((None, tq, qw), lambda bi, h, i, j: (bi, i, h)),
                  pl.BlockSpec((None, tq, 2 * DIFF_HD), lambda bi, h, i, j: (bi, jnp.minimum(j, i), h)),
                  pl.BlockSpec((None, tq, DIFF_VD), lambda bi, h, i, j: (bi, jnp.minimum(j, i), h)),
                  pl.BlockSpec((None, tq, ow), lambda bi, h, i, j: (bi, i, h)),
                  vec, vec, vec, vec,
                  pl.BlockSpec((1, DIFF_VD), lambda bi, h, i, j: (0, 0))],
        out_specs=pl.BlockSpec((None, tq, ow), lambda bi, h, i, j: (bi, i, h)),
        out_shape=jax.ShapeDtypeStruct((b, t, DIFF_H * DIFF_VD), F32),
        scratch_shapes=[pltpu.VMEM((1, n_sub * tq), F32), pltpu.VMEM((1, n_sub * tq), F32),
                        pltpu.VMEM((DIFF_VD, n_sub * tq), F32)],
        compiler_params=pltpu.CompilerParams(
            dimension_semantics=("parallel", "parallel", "parallel", "arbitrary"),
            vmem_limit_bytes=VMEM_LIMIT),
        name="diff_flash",
    )(q, k, v, z, *lams, subln_g.reshape(1, DIFF_VD))


def _diff_decode_body(pt_ref, q_ref, ks_ref, vs_ref, z_ref, lq1_ref, lk1_ref, lq2_ref, lk2_ref, sg_ref,
                      *rest, pp, lam_init):
    del pt_ref
    k_pages = rest[:pp]
    v_pages = rest[pp:2 * pp]
    o_ref = rest[2 * pp]
    qb_ref, m_ref, l_ref, acc_ref = rest[2 * pp + 1:]
    j = pl.program_id(1)
    n_rows = DIFF_KVH * DIFF_G * 2
    head_shift = (DIFF_G * 2).bit_length() - 1
    kw = 2 * DIFF_HD
    page_rows = k_pages[0].shape[0]

    @pl.when(j == 0)
    def _():
        d_i = lax.broadcasted_iota(jnp.int32, (DIFF_HD, kw), 0)
        c_i = lax.broadcasted_iota(jnp.int32, (DIFF_HD, kw), 1)
        spread = jnp.where((c_i & (DIFF_HD - 1)) == d_i, 1.0, 0.0)
        qt = jnp.dot(q_ref[...], spread, preferred_element_type=F32)
        r_i = lax.broadcasted_iota(jnp.int32, (n_rows, kw), 0)
        half = lax.broadcasted_iota(jnp.int32, (n_rows, kw), 1) >> (DIFF_HD.bit_length() - 1)
        qb_ref[...] = jnp.where(half == (r_i & 1), qt, 0.0) * (DIFF_HD ** -0.5)
        m_ref[...] = jnp.full_like(m_ref, -jnp.inf)
        l_ref[...] = jnp.zeros_like(l_ref)
        acc_ref[...] = jnp.zeros_like(acc_ref)

    qb = qb_ref[...]
    s = jnp.concatenate([lax.dot_general(qb, kp[...], _NT, preferred_element_type=F32) for kp in k_pages],
                        axis=1)
    row_head = lax.broadcasted_iota(jnp.int32, s.shape, 0) >> head_shift
    col_head = lax.broadcasted_iota(jnp.int32, s.shape, 1) & (DIFF_KVH - 1)
    s = jnp.where(row_head == col_head, s, -jnp.inf)
    m_prev = m_ref[...]
    m_new = jnp.maximum(m_prev, jnp.max(s, -1, keepdims=True))
    alpha = jnp.exp(m_prev - m_new)
    p = jnp.exp(s - m_new)
    l_new = alpha * l_ref[...] + jnp.sum(p, -1, keepdims=True)
    pv = jnp.dot(p[:, 0:page_rows], v_pages[0][...], preferred_element_type=F32)
    for t in range(1, pp):
        pv = pv + jnp.dot(p[:, t * page_rows:(t + 1) * page_rows], v_pages[t][...],
                          preferred_element_type=F32)
    acc_new = alpha * acc_ref[...] + pv
    m_ref[...] = m_new
    l_ref[...] = l_new
    acc_ref[...] = acc_new

    @pl.when(j == pl.num_programs(1) - 1)
    def _():
        rh = lax.broadcasted_iota(jnp.int32, (n_rows, kw), 0) >> head_shift
        ks = jnp.zeros((n_rows, kw), F32)
        vs = jnp.zeros((n_rows, kw), F32)
        for h in range(DIFF_KVH):
            ks = jnp.where(rh == h, ks_ref[h:h + 1, :], ks)
            vs = jnp.where(rh == h, vs_ref[h:h + 1, :], vs)
        s_self = jnp.sum(qb * ks, -1, keepdims=True)
        m_fin = jnp.maximum(m_new, s_self)
        a = jnp.exp(m_new - m_fin)
        p_self = jnp.exp(s_self - m_fin)
        l_fin = a * l_new + p_self
        acc = (a * acc_new + p_self * vs) / l_fin
        lam = _diff_lambda(lq1_ref, lk1_ref, lq2_ref, lk2_ref, lam_init)
        sg = sg_ref[...]
        for hh in range(DIFF_H):
            o = acc[2 * hh:2 * hh + 1] - lam * acc[2 * hh + 1:2 * hh + 2]
            o_ref[hh:hh + 1, :] = _diff_head_out(o, sg, z_ref[hh:hh + 1, :], lam_init)


def _diff_decode(q, k_self, v_self, z, cache_k, cache_v, layer, page_table, lams, subln_g, lam_init, *, pp):
    db = q.shape[0]
    n_pages = page_table.shape[1]
    assert n_pages % pp == 0
    page_rows, kw = cache_k.shape[2:]
    n_rows = DIFF_KVH * DIFF_G * 2

    def page_spec(t):
        return pl.BlockSpec((None, None, page_rows, kw), lambda b, j, pt: (layer, pt[b, j * pp + t], 0, 0))

    def seq_spec(shape):
        return pl.BlockSpec((None,) + shape, lambda b, j, pt: (b, 0, 0))

    vec = pl.BlockSpec((1, DIFF_HD), lambda b, j, pt: (0, 0))
    grid_spec = pltpu.PrefetchScalarGridSpec(
        num_scalar_prefetch=1,
        grid=(db, n_pages // pp),
        in_specs=[seq_spec((n_rows, DIFF_HD)), seq_spec((DIFF_KVH, kw)), seq_spec((DIFF_KVH, kw)),
                  seq_spec((DIFF_H, DIFF_VD)), vec, vec, vec, vec,
                  pl.BlockSpec((1, DIFF_VD), lambda b, j, pt: (0, 0))]
                 + [page_spec(t) for t in range(pp)] + [page_spec(t) for t in range(pp)],
        out_specs=seq_spec((DIFF_H, DIFF_VD)),
        scratch_shapes=[pltpu.VMEM((n_rows, kw), F32), pltpu.VMEM((n_rows, 1), F32),
                        pltpu.VMEM((n_rows, 1), F32), pltpu.VMEM((n_rows, DIFF_VD), F32)],
    )
    return pl.pallas_call(
        functools.partial(_diff_decode_body, pp=pp, lam_init=lam_init),
        grid_spec=grid_spec,
        out_shape=jax.ShapeDtypeStruct((db, DIFF_H, DIFF_VD), F32),
        compiler_params=pltpu.CompilerParams(dimension_semantics=("parallel", "arbitrary"),
                                             vmem_limit_bytes=VMEM_LIMIT),
        name="diff_decode",
    )(page_table, q, k_self, v_self, z, *lams, subln_g.reshape(1, DIFF_VD),
      *([cache_k] * pp), *([cache_v] * pp))


def _diff_layer(xp, xs, mod_p, mod_s, seq, layer_idx, j, cache_k, cache_v, page_table, w_in, lams, subln_g,
                w_out, ln_g, ln_b, alpha):
    mp, d = xp.shape
    bp = mp // seq
    db = xs.shape[0]
    lam_init = 0.8 - 0.6 * math.exp(-0.3 * layer_idx)
    q_w = DIFF_H * 2 * DIFF_HD
    k_w = DIFF_KVH * 2 * DIFF_HD
    v_w = DIFF_KVH * DIFF_VD
    w_q = w_in[:, :q_w].astype(BF16)
    w_k = w_in[:, q_w:q_w + k_w].astype(BF16)
    w_v = w_in[:, q_w + k_w:q_w + k_w + v_w].astype(BF16)
    w_z = w_in[:, q_w + k_w + v_w:].astype(BF16)
    w_o = w_out.astype(BF16)
    lams = [a.reshape(1, DIFF_HD) for a in lams]

    q = _in_proj(xp, mod_p, seq, w_q, tm=1024, tn=512, name="diff_in_q_p")
    k = _in_proj(xp, mod_p, seq, w_k, tm=1024, tn=512, name="diff_in_k_p")
    v = _in_proj(xp, mod_p, seq, w_v, tm=1024, tn=512, name="diff_in_v_p")
    z = _in_proj(xp, mod_p, seq, w_z, tm=1024, tn=512, name="diff_in_z_p")
    o = _diff_flash(q.reshape(bp, seq, q_w), k.reshape(bp, seq, k_w), v.reshape(bp, seq, v_w),
                    z.reshape(bp, seq, -1), lams, subln_g, lam_init, tq=min(DIFF_TQ, seq))
    xp_new = _out_proj_norm(o.reshape(mp, -1), w_o, xp, mod_p, seq, ln_g, ln_b, tm=512, alpha=alpha,
                            name="diff_out_p")

    q_s = _in_proj(xs, mod_s, 1, w_q, tm=db, tn=512, name="diff_in_q_s")
    k_s = _in_proj(xs, mod_s, 1, w_k, tm=db, tn=512, name="diff_in_k_s")
    v_s = _in_proj(xs, mod_s, 1, w_v, tm=db, tn=512, name="diff_in_v_s")
    z_s = _in_proj(xs, mod_s, 1, w_z, tm=db, tn=512, name="diff_in_z_s")
    n_l, n_pool, page = cache_k.shape[:3]
    o_s = _diff_decode(q_s.reshape(db, DIFF_KVH * DIFF_G * 2, DIFF_HD), k_s.reshape(db, DIFF_KVH, 2 * DIFF_HD),
                       v_s.reshape(db, DIFF_KVH, DIFF_VD), z_s.reshape(db, DIFF_H, DIFF_VD),
                       cache_k.reshape(n_l, n_pool, page * DIFF_KVH, 2 * DIFF_HD),
                       cache_v.reshape(n_l, n_pool, page * DIFF_KVH, DIFF_VD),
                       j, page_table, lams, subln_g, lam_init, pp=DIFF_PAGES_PER_STEP)
    xs_new = _out_proj_norm(o_s.reshape(db, -1), w_o, xs, mod_s, 1, ln_g, ln_b, tm=db, alpha=alpha,
                            name="diff_out_s")
    return (xp_new, xs_new, k.reshape(bp, seq, DIFF_KVH, 2 * DIFF_HD), v.reshape(bp, seq, DIFF_KVH, DIFF_VD),
            k_s.reshape(db, 1, DIFF_KVH, 2 * DIFF_HD), v_s.reshape(db, 1, DIFF_KVH, DIFF_VD))


def _swa_band_body(q_ref, kc_ref, kp_ref, vc_ref, vp_ref, z_ref, sink_ref, o_ref):
    i = pl.program_id(1)
    blk = q_ref.shape[0]
    cols = SWA_G * blk
    kj = lax.broadcasted_iota(jnp.int32, (2 * blk, cols), 0)
    qi = lax.broadcasted_iota(jnp.int32, (2 * blk, cols), 1) & (blk - 1)
    dist = qi + blk - kj
    keep = (dist >= 0) & (dist <= WINDOW) & ((i > 0) | (kj >= blk))
    for h in range(SWA_KVH):
        hs = slice(h * SWA_HD, (h + 1) * SWA_HD)
        k2 = jnp.concatenate([kp_ref[:, hs], kc_ref[:, hs]], axis=0)
        v2 = jnp.concatenate([vp_ref[:, hs], vc_ref[:, hs]], axis=0)
        qs = jnp.concatenate([q_ref[:, (h * SWA_G + g) * SWA_HD:(h * SWA_G + g + 1) * SWA_HD]
                              for g in range(SWA_G)], axis=0) * (SWA_HD ** -0.5)
        sink = jnp.concatenate([jnp.broadcast_to(sink_ref[:, h * SWA_G + g:h * SWA_G + g + 1], (1, blk))
                                for g in range(SWA_G)], axis=1)
        st = lax.dot_general(k2, qs, _NT, preferred_element_type=F32)
        st = jnp.where(keep, st, -jnp.inf)
        m = jnp.maximum(jnp.max(st, 0, keepdims=True), sink)
        p = jnp.exp(st - m)
        den = jnp.sum(p, 0, keepdims=True) + jnp.exp(sink - m)
        ot = lax.dot_general(v2, p, _TN, preferred_element_type=F32) * (1.0 / den)
        for g in range(0, SWA_G, 2):
            cs = slice((h * SWA_G + g) * SWA_HD, (h * SWA_G + g + 2) * SWA_HD)
            pair = jnp.concatenate([ot[:, g * blk:(g + 1) * blk], ot[:, (g + 1) * blk:(g + 2) * blk]], axis=0)
            o_ref[:, cs] = pair.T * _silu(z_ref[:, cs])


def _swa_band(q, k, v, z, sinks):
    b, t, qw = q.shape
    blk = WINDOW
    kw = SWA_KVH * SWA_HD
    cur = pl.BlockSpec((None, blk, kw), lambda bi, i: (bi, i, 0))
    prev = pl.BlockSpec((None, blk, kw), lambda bi, i: (bi, jnp.maximum(i - 1, 0), 0))
    wide = pl.BlockSpec((None, blk, qw), lambda bi, i: (bi, i, 0))
    return pl.pallas_call(
        _swa_band_body,
        grid=(b, t // blk),
        in_specs=[wide, cur, prev, cur, prev, wide, pl.BlockSpec((1, SWA_H), lambda bi, i: (0, 0))],
        out_specs=wide,
        out_shape=jax.ShapeDtypeStruct((b, t, qw), F32),
        compiler_params=pltpu.CompilerParams(dimension_semantics=("parallel", "parallel"),
                                             vmem_limit_bytes=VMEM_LIMIT),
        name="swa_band",
    )(q, k, k, v, v, z, sinks.reshape(1, SWA_H))


def _swa_decode_body(q_ref, kb_ref, vb_ref, kn_ref, vn_ref, z_ref, sink_ref, o_ref, ko_ref, vo_ref):
    win = kb_ref.shape[1]
    for h in range(SWA_KVH):
        hs = slice(h * SWA_HD, (h + 1) * SWA_HD)
        gs = slice(h * SWA_G, (h + 1) * SWA_G)
        qh = q_ref[:, gs, :]
        s = jnp.einsum("bqd,bkd->bqk", qh, kb_ref[:, :, hs], preferred_element_type=F32) * (SWA_HD ** -0.5)
        s_new = jnp.sum(qh * kn_ref[:, :, hs], -1, keepdims=True) * (SWA_HD ** -0.5)
        sink = sink_ref[gs, :][None]
        m = jnp.maximum(jnp.maximum(jnp.max(s, -1, keepdims=True), s_new), sink)
        p = jnp.exp(s - m)
        p_new = jnp.exp(s_new - m)
        den = jnp.sum(p, -1, keepdims=True) + p_new + jnp.exp(sink - m)
        o = jnp.einsum("bqk,bkd->bqd", p, vb_ref[:, :, hs], preferred_element_type=F32)
        o = (o + p_new * vn_ref[:, :, hs]) / den
        o_ref[:, gs, :] = o * _silu(z_ref[:, gs, :])
    ko_ref[:, 0:win - 1, :] = kb_ref[:, 1:win, :]
    ko_ref[:, win - 1:win, :] = kn_ref[...]
    vo_ref[:, 0:win - 1, :] = vb_ref[:, 1:win, :]
    vo_ref[:, win - 1:win, :] = vn_ref[...]


def _swa_decode(q, k_new, v_new, z, buf_k, buf_v, sinks, *, bb):
    db, win, kw = buf_k.shape
    qz = pl.BlockSpec((bb, SWA_H, SWA_HD), lambda b: (b, 0, 0))
    buf = pl.BlockSpec((bb, win, kw), lambda b: (b, 0, 0))
    new = pl.BlockSpec((bb, 1, kw), lambda b: (b, 0, 0))
    return pl.pallas_call(
        _swa_decode_body,
        grid=(db // bb,),
        in_specs=[qz, buf, buf, new, new, qz, pl.BlockSpec((SWA_H, 1), lambda b: (0, 0))],
        out_specs=[qz, buf, buf],
        out_shape=[jax.ShapeDtypeStruct(q.shape, F32), jax.ShapeDtypeStruct(buf_k.shape, F32),
                   jax.ShapeDtypeStruct(buf_v.shape, F32)],
        compiler_params=pltpu.CompilerParams(dimension_semantics=("parallel",), vmem_limit_bytes=VMEM_LIMIT),
        name="swa_decode",
    )(q, buf_k, buf_v, k_new, v_new, z, sinks.reshape(SWA_H, 1))


def _swa_layer(xp, xs, mod_p, mod_s, seq, buf_k, buf_v, sinks, w_in, w_out, ln_g, ln_b, alpha):
    mp, d = xp.shape
    bp = mp // seq
    db = xs.shape[0]
    q_w = SWA_H * SWA_HD
    kv_w = SWA_KVH * SWA_HD
    w_q = w_in[:, :q_w].astype(BF16)
    w_kv = w_in[:, q_w:q_w + 2 * kv_w].astype(BF16)
    w_z = w_in[:, q_w + 2 * kv_w:].astype(BF16)
    w_o = w_out.astype(BF16)

    q = _in_proj(xp, mod_p, seq, w_q, tm=1024, tn=512, name="swa_in_q_p")
    kv = _in_proj(xp, mod_p, seq, w_kv, tm=1024, tn=2 * kv_w, name="swa_in_kv_p")
    z = _in_proj(xp, mod_p, seq, w_z, tm=1024, tn=512, name="swa_in_z_p")
    k = kv[:, :kv_w].reshape(bp, seq, kv_w)
    v = kv[:, kv_w:].reshape(bp, seq, kv_w)
    o = _swa_band(q.reshape(bp, seq, q_w), k, v, z.reshape(bp, seq, q_w), sinks)
    xp_new = _out_proj_norm(o.reshape(mp, q_w), w_o, xp, mod_p, seq, ln_g, ln_b, tm=512, alpha=alpha,
                            name="swa_out_p")

    q_s = _in_proj(xs, mod_s, 1, w_q, tm=db, tn=512, name="swa_in_q_s")
    kv_s = _in_proj(xs, mod_s, 1, w_kv, tm=db, tn=2 * kv_w, name="swa_in_kv_s")
    z_s = _in_proj(xs, mod_s, 1, w_z, tm=db, tn=512, name="swa_in_z_s")
    win = buf_k.shape[1]
    o_s, k_buf, v_buf = _swa_decode(q_s.reshape(db, SWA_H, SWA_HD), kv_s[:, :kv_w].reshape(db, 1, kv_w),
                                    kv_s[:, kv_w:].reshape(db, 1, kv_w), z_s.reshape(db, SWA_H, SWA_HD),
                                    buf_k.reshape(db, win, kv_w), buf_v.reshape(db, win, kv_w), sinks,
                                    bb=SWA_DEC_BB)
    xs_new = _out_proj_norm(o_s.reshape(db, q_w), w_o, xs, mod_s, 1, ln_g, ln_b, tm=db, alpha=alpha,
                            name="swa_out_s")
    kv_shape = (SWA_KVH, SWA_HD)
    return (xp_new, xs_new, k[:, seq - WINDOW:].reshape((bp, WINDOW) + kv_shape),
            v[:, seq - WINDOW:].reshape((bp, WINDOW) + kv_shape),
            k_buf.reshape((db, win) + kv_shape), v_buf.reshape((db, win) + kv_shape))


def kernel(x_prompt, x_sample, cache_diff_k, cache_diff_v, page_table, cache_swa_k, cache_swa_v, state_dn_S, state_dn_conv, c_prompt, c_sample, ada_w, ada_b, ln_g, ln_b, dn_w_in, dn_conv_w, dn_a_log, dn_dt_bias, dn_norm_g, dn_w_out, diff_w_in, diff_lq1, diff_lk1, diff_lq2, diff_lk2, diff_subln_g, diff_w_out, swa_w_in, swa_sinks, swa_w_out):
    bp, seq, d = x_prompt.shape
    db, dec_seq, _ = x_sample.shape
    assert dec_seq == 1 and bp <= SUBLANES
    depth = ada_w.shape[0]
    alpha = (2 * depth) ** 0.25
    xp = x_prompt.reshape(bp * seq, d)
    xs = x_sample.reshape(db, d)

    c_all = jnp.concatenate([c_prompt, jnp.zeros((SUBLANES - bp, d), F32), c_sample], axis=0)
    st = {n: [] for n in ("dk_p", "dv_p", "dk_s", "dv_s", "sk_p", "sv_p", "sk_s", "sv_s",
                          "S_p", "cv_p", "cv_s")}
    s_s = state_dn_S
    for l in range(depth):
        j = l // N_MIXERS
        mod = _mm(c_all, ada_w[l].astype(BF16), tm=c_all.shape[0], tn=1024, pro=_silu, epi=_add_bias,
                  epi_ops=[(ada_b[l].reshape(1, 3 * d), pl.BlockSpec((1, 1024), lambda i, jj: (0, jj)))],
                  name="adaln")
        mod_p = mod[:bp].reshape(bp, 1, 3 * d)
        mod_s = mod[SUBLANES:]
        if l % N_MIXERS == 0:
            xp, xs, s_p, cv_p, s_s, cv_s = _dn_layer(
                xp, xs, mod_p, mod_s, seq, state_dn_conv[j], s_s, j, dn_w_in[j], dn_conv_w[j],
                dn_a_log[j], dn_dt_bias[j], dn_norm_g[j], dn_w_out[j], ln_g[l], ln_b[l], alpha)
            st["S_p"].append(s_p)
            st["cv_p"].append(cv_p)
            st["cv_s"].append(cv_s)
        elif l % N_MIXERS == 1:
            xp, xs, k_p, v_p, k_s, v_s = _diff_layer(
                xp, xs, mod_p, mod_s, seq, l, j, cache_diff_k, cache_diff_v, page_table, diff_w_in[j],
                (diff_lq1[j], diff_lk1[j], diff_lq2[j], diff_lk2[j]), diff_subln_g[j], diff_w_out[j],
                ln_g[l], ln_b[l], alpha)
            st["dk_p"].append(k_p)
            st["dv_p"].append(v_p)
            st["dk_s"].append(k_s)
            st["dv_s"].append(v_s)
        else:
            xp, xs, k_p, v_p, k_s, v_s = _swa_layer(
                xp, xs, mod_p, mod_s, seq, cache_swa_k[j], cache_swa_v[j], swa_sinks[j], swa_w_in[j],
                swa_w_out[j], ln_g[l], ln_b[l], alpha)
            st["sk_p"].append(k_p)
            st["sv_p"].append(v_p)
            st["sk_s"].append(k_s)
            st["sv_s"].append(v_s)
    return (xp.reshape(bp, seq, d), xs.reshape(db, 1, d),
            jnp.stack(st["dk_p"]), jnp.stack(st["dv_p"]), jnp.stack(st["dk_s"]), jnp.stack(st["dv_s"]),
            jnp.stack(st["sk_p"]), jnp.stack(st["sv_p"]), jnp.stack(st["sk_s"]), jnp.stack(st["sv_s"]),
            jnp.stack(st["S_p"]), jnp.stack(st["cv_p"]), s_s, jnp.stack(st["cv_s"]))
```

```python
import functools
import math

import jax
import jax.numpy as jnp
from jax import lax
from jax.experimental import pallas as pl
from jax.experimental.pallas import tpu as pltpu

F32 = jnp.float32
BF16 = jnp.bfloat16

LN_EPS = 1e-5
RMS_EPS = 1e-6
N_MIXERS = 3

DN_HK = 8
DN_HV = 16
DN_DK = 128
DN_DV = 128
DN_REP = DN_HV // DN_HK
DN_CONV_W = 4
DN_CHUNK = 128
DN_PREP_ROWS = 256
PROJ_TM = 512

DIFF_H = 8
DIFF_KVH = 4
DIFF_G = DIFF_H // DIFF_KVH
DIFF_HD = 64
DIFF_VD = 2 * DIFF_HD
DIFF_TQ = 512
LOG2E = math.log2(math.e)
DIFF_PAGES_PER_STEP = 16

SWA_H = 16
SWA_KVH = 2
SWA_G = SWA_H // SWA_KVH
SWA_HD = 64
WINDOW = 128
SWA_DEC_BB = 8

LANES = 128
SUBLANES = 8
VMEM_LIMIT = 48 * 1024 * 1024

_NT = (((1,), (1,)), ((), ()))
_TN = (((0,), (0,)), ((), ()))


def _silu(x):
    return x * jax.nn.sigmoid(x)


def _proj_body(x_ref, w_ref, *refs, n_pro, segs, pro):
    pro_refs = refs[:n_pro]
    n_epi = sum(s[3] for s in segs)
    epi_refs = refs[n_pro:n_pro + n_epi]
    o_refs = refs[n_pro + n_epi:]
    x = x_ref[...]
    if pro is not None:
        x = pro(x, *pro_refs)
    xb = x.astype(BF16)
    e = 0
    for (start, width, epi, n_e), o_ref in zip(segs, o_refs):
        acc = jnp.dot(xb, w_ref[:, start:start + width], preferred_element_type=F32)
        if epi is not None:
            acc = epi(acc, *epi_refs[e:e + n_e])
        e += n_e
        o_ref[...] = acc


def _proj(x, w, segments, *, tm, pro=None, pro_ops=(), name="proj"):
    m, k = x.shape
    n = w.shape[1]
    assert m % tm == 0, (m, tm)
    ops = [x, w]
    in_specs = [pl.BlockSpec((tm, k), lambda i: (i, 0)),
                pl.BlockSpec((k, n), lambda i: (0, 0), pipeline_mode=pl.Buffered(1))]
    for a, s in pro_ops:
        ops.append(a)
        in_specs.append(s)
    segs = []
    start = 0
    for width, epi, epi_ops in segments:
        segs.append((start, width, epi, len(epi_ops)))
        start += width
        for a, s in epi_ops:
            ops.append(a)
            in_specs.append(s)
    assert start <= n, (start, n)
    return pl.pallas_call(
        functools.partial(_proj_body, n_pro=len(pro_ops), segs=tuple(segs), pro=pro),
        grid=(m // tm,),
        in_specs=in_specs,
        out_specs=[pl.BlockSpec((tm, s[1]), lambda i: (i, 0)) for s in segs],
        out_shape=[jax.ShapeDtypeStruct((m, s[1]), F32) for s in segs],
        compiler_params=pltpu.CompilerParams(dimension_semantics=("parallel",), vmem_limit_bytes=VMEM_LIMIT),
        name=name,
    )(*ops)


def _modulate(x, mod_ref):
    d = x.shape[-1]
    return x * (1.0 + mod_ref[:, d:2 * d]) + mod_ref[:, 0:d]


def _deepnorm(y, xres_ref, mod_ref, g_ref, b_ref, *, alpha):
    d = y.shape[-1]
    h = alpha * xres_ref[...] + (1.0 + mod_ref[:, 2 * d:3 * d]) * y
    hc = h - jnp.mean(h, -1, keepdims=True)
    var = jnp.mean(hc * hc, -1, keepdims=True)
    return hc * lax.rsqrt(var + LN_EPS) * g_ref[...] + b_ref[...]


def _add_bias(y, b_ref):
    return y + b_ref[...]


def _dn_gates(y, alog_ref, dtb_ref):
    lane = lax.broadcasted_iota(jnp.int32, y.shape, 1)
    t = y + dtb_ref[...]
    softplus = jnp.maximum(t, 0.0) + jnp.log1p(jnp.exp(-jnp.abs(t)))
    g = -jnp.exp(alog_ref[...]) * softplus
    return jnp.where(lane < DN_HV, g, jax.nn.sigmoid(y))


def _mod_spec(mod, tm, rows_per_mod):
    w = mod.shape[-1]
    if mod.ndim == 3:
        assert rows_per_mod % tm == 0, (rows_per_mod, tm)
        return pl.BlockSpec((None, 1, w), lambda i: (i * tm // rows_per_mod, 0, 0))
    return pl.BlockSpec((tm, w), lambda i: (i, 0))


def _in_proj(x, mod, rows_per_mod, w, segments, *, tm, name="in_proj"):
    segments = [(s, None, ()) if isinstance(s, int) else s for s in segments]
    return _proj(x, w, segments, tm=tm, pro=_modulate, pro_ops=[(mod, _mod_spec(mod, tm, rows_per_mod))],
                 name=name)


def _out_proj_norm(y, w, xres, mod, rows_per_mod, ln_g, ln_b, *, tm, alpha, name="out_proj"):
    d = w.shape[1]
    row = pl.BlockSpec((1, d), lambda i: (0, 0))
    epi_ops = [(xres, pl.BlockSpec((tm, d), lambda i: (i, 0))), (mod, _mod_spec(mod, tm, rows_per_mod)),
               (ln_g.reshape(1, d), row), (ln_b.reshape(1, d), row)]
    return _proj(y, w, [(d, functools.partial(_deepnorm, alpha=alpha), epi_ops)], tm=tm, name=name)[0]


def _l2norm_heads(y, o_ref, scale):
    for h in range(y.shape[-1] // DN_DK):
        yh = y[:, h * DN_DK:(h + 1) * DN_DK]
        inv = lax.rsqrt(jnp.sum(yh * yh, -1, keepdims=True) + RMS_EPS)
        o_ref[:, h * DN_DK:(h + 1) * DN_DK] = yh * inv * scale


def _dn_prep_body(x_ref, prev_ref, w_ref, o_ref, *, cb):
    i = pl.program_id(1)
    tt = x_ref.shape[0]
    k_w = DN_HK * DN_DK
    backs = range(1, DN_CONV_W)
    row = lax.broadcasted_iota(jnp.int32, (SUBLANES, cb), 0)

    def rolled(v):
        return tuple(pltpu.roll(v, d, axis=0) for d in backs)

    for c0 in range(0, x_ref.shape[1], cb):
        cols = slice(c0, c0 + cb)
        taps = [w_ref[s:s + 1, cols] for s in range(DN_CONV_W)]
        norm_scale = DN_DK ** -0.5 if c0 < k_w else (1.0 if c0 < 2 * k_w else None)

        def body(si, prev_rolled, cols=cols, taps=taps, norm_scale=norm_scale):
            r0 = pl.multiple_of(si * SUBLANES, SUBLANES)
            cur = x_ref[pl.ds(r0, SUBLANES), cols]
            cur_rolled = rolled(cur)
            back = [jnp.where(row < d, p, q) for d, p, q in zip(backs, prev_rolled, cur_rolled)]
            conv = back[DN_CONV_W - 2] * taps[0]
            for s in range(1, DN_CONV_W - 1):
                conv = conv + back[DN_CONV_W - 2 - s] * taps[s]
            y = _silu(conv + cur * taps[DN_CONV_W - 1])
            out = o_ref.at[pl.ds(r0, SUBLANES), cols]
            if norm_scale is None:
                out[...] = y
            else:
                _l2norm_heads(y, out, norm_scale)
            return cur_rolled

        first = jnp.where(i > 0, prev_ref[:, cols], 0.0)
        lax.fori_loop(0, tt // SUBLANES, body, rolled(first), unroll=8)


def _dn_prep(qkv, conv_w, *, tt, cb):
    b, t, c = qkv.shape
    return pl.pallas_call(
        functools.partial(_dn_prep_body, cb=cb),
        grid=(b, t // tt),
        in_specs=[pl.BlockSpec((None, tt, c), lambda bi, i: (bi, i, 0)),
                  pl.BlockSpec((None, SUBLANES, c),
                               lambda bi, i: (bi, jnp.maximum(i * (tt // SUBLANES) - 1, 0), 0)),
                  pl.BlockSpec((DN_CONV_W, c), lambda bi, i: (0, 0))],
        out_specs=pl.BlockSpec((None, tt, c), lambda bi, i: (bi, i, 0)),
        out_shape=jax.ShapeDtypeStruct(qkv.shape, F32),
        compiler_params=pltpu.CompilerParams(dimension_semantics=("parallel", "parallel"),
                                             vmem_limit_bytes=VMEM_LIMIT),
        name="dn_prep",
    )(qkv, qkv, conv_w)


def _gated_rms(o, ng, z):
    return o * lax.rsqrt(jnp.mean(o * o, -1, keepdims=True) + RMS_EPS) * ng * _silu(z)


def _dn_chunk_body(x_ref, z_ref, gb_ref, ng_ref, o_ref, s_ref):
    ci = pl.program_id(1)
    c = x_ref.shape[0]
    hb = DN_HV

    @pl.when(ci == 0)
    def _():
        s_ref[...] = jnp.zeros_like(s_ref)

    k_w = DN_HK * DN_DK
    gb = gb_ref[...]
    row = lax.broadcasted_iota(jnp.int32, gb.shape, 0)
    gc = gb
    sh = 1
    while sh < c:
        gc = gc + jnp.where(row >= sh, pltpu.roll(gc, sh, axis=0), 0.0)
        sh *= 2
    gct = gc.T
    ri = lax.broadcasted_iota(jnp.int32, (c, c), 0)
    cj = lax.broadcasted_iota(jnp.int32, (c, c), 1)
    tril = ri >= cj
    strict = ri > cj
    merge_masks = []
    k = 0
    while (1 << k) < c:
        merge_masks.append(((ri >> (k + 1)) == (cj >> (k + 1))) & ((ri >> k) != (cj >> k)))
        k += 1
    ng = ng_ref[...]
    heads = range(hb)
    dot = functools.partial(jnp.dot, preferred_element_type=F32)
    qh = [x_ref[:, i * DN_DK:(i + 1) * DN_DK] for i in range(DN_HK)]
    kh = [x_ref[:, k_w + i * DN_DK:k_w + (i + 1) * DN_DK] for i in range(DN_HK)]
    kk = [lax.dot_general(a, a, _NT, preferred_element_type=F32) for a in kh]
    qk = [lax.dot_general(a, b, _NT, preferred_element_type=F32) for a, b in zip(qh, kh)]
    gcol = [gc[:, j:j + 1] for j in heads]
    bcol = [gb[:, hb + j:hb + j + 1] for j in heads]
    decay = [jnp.where(tril, jnp.exp(jnp.where(tril, gcol[j] - gct[j:j + 1, :], 0.0)), 0.0) for j in heads]
    lmat = [jnp.where(strict, bcol[j] * kk[j // DN_REP] * decay[j], 0.0) for j in heads]
    minv = [jnp.where(merge_masks[0], -lmat[j], 0.0) for j in heads]
    for mask in merge_masks[1:]:
        loff = [jnp.where(mask, lmat[j], 0.0) for j in heads]
        y = [loff[j] + dot(loff[j], minv[j]) for j in heads]
        minv = [minv[j] - y[j] - dot(minv[j], y[j]) for j in heads]
    eg = [jnp.exp(gcol[j]) for j in heads]
    rhs = [jnp.concatenate([x_ref[:, 2 * k_w + j * DN_DV:2 * k_w + (j + 1) * DN_DV] * bcol[j],
                            kh[j // DN_REP] * (bcol[j] * eg[j])], axis=1) for j in heads]
    sol = [rhs[j] + dot(minv[j], rhs[j]) for j in heads]
    s_old = [s_ref[j] for j in heads]
    ws = [dot(jnp.concatenate([sol[j][:, DN_DV:], qh[j // DN_REP] * eg[j]], axis=0), s_old[j]) for j in heads]
    v_new = [sol[j][:, :DN_DV] - ws[j][:c] for j in heads]
    o = [ws[j][c:] + dot(qk[j // DN_REP] * decay[j], v_new[j]) for j in heads]
    for j in heads:
        glast = gc[c - 1:c, j:j + 1]
        kd = kh[j // DN_REP] * jnp.exp(glast - gcol[j])
        s_ref[j] = s_old[j] * jnp.exp(glast) + lax.dot_general(kd, v_new[j], _TN, preferred_element_type=F32)
        o_ref[:, j * DN_DV:(j + 1) * DN_DV] = _gated_rms(o[j], ng, z_ref[:, j * DN_DV:(j + 1) * DN_DV])


def _dn_chunks(qkv, z, gates, norm_g):
    b, t, cd = qkv.shape
    c = DN_CHUNK
    assert t % c == 0
    v_w = DN_HV * DN_DV
    o, s = pl.pallas_call(
        _dn_chunk_body,
        grid=(b, t // c),
        in_specs=[pl.BlockSpec((None, c, cd), lambda bi, ci: (bi, ci, 0)),
                  pl.BlockSpec((None, c, v_w), lambda bi, ci: (bi, ci, 0)),
                  pl.BlockSpec((None, c, LANES), lambda bi, ci: (bi, ci, 0)),
                  pl.BlockSpec((1, DN_DV), lambda bi, ci: (0, 0))],
        out_specs=[pl.BlockSpec((None, c, v_w), lambda bi, ci: (bi, ci, 0)),
                   pl.BlockSpec((None, DN_HV, DN_DK, DN_DV), lambda bi, ci: (bi, 0, 0, 0))],
        out_shape=[jax.ShapeDtypeStruct((b, t, v_w), F32),
                   jax.ShapeDtypeStruct((b, DN_HV, DN_DK, DN_DV), F32)],
        compiler_params=pltpu.CompilerParams(dimension_semantics=("parallel", "arbitrary"),
                                             vmem_limit_bytes=VMEM_LIMIT),
        name="dn_chunks",
    )(qkv, z, gates, norm_g.reshape(1, DN_DV))
    return o, s


def _dn_step_prep_body(x_ref, prev_ref, w_ref, o_ref, cv_ref, *, n_q_blk, n_k_blk):
    c = pl.program_id(0)
    x = x_ref[...]
    w = w_ref[...]
    conv = prev_ref[0] * w[0:1]
    for s in range(1, DN_CONV_W - 1):
        conv = conv + prev_ref[s] * w[s:s + 1]
    conv = conv + x * w[DN_CONV_W - 1:DN_CONV_W]
    y = _silu(conv)
    for s in range(DN_CONV_W - 2):
        cv_ref[s] = prev_ref[s + 1]
    cv_ref[DN_CONV_W - 2] = x

    @pl.when(c >= n_q_blk + n_k_blk)
    def _():
        o_ref[...] = y

    @pl.when(c < n_q_blk + n_k_blk)
    def _():
        _l2norm_heads(y, o_ref, jnp.where(c < n_q_blk, DN_DK ** -0.5, 1.0))


def _dn_step_prep(qkv, conv_prev_t, conv_w, *, cb):
    db, c = qkv.shape
    k_w = DN_HK * DN_DK
    nw = DN_CONV_W - 1
    return pl.pallas_call(
        functools.partial(_dn_step_prep_body, n_q_blk=k_w // cb, n_k_blk=k_w // cb),
        grid=(c // cb,),
        in_specs=[pl.BlockSpec((db, cb), lambda ci: (0, ci)),
                  pl.BlockSpec((nw, db, cb), lambda ci: (0, 0, ci)),
                  pl.BlockSpec((DN_CONV_W, cb), lambda ci: (0, ci))],
        out_specs=[pl.BlockSpec((db, cb), lambda ci: (0, ci)),
                   pl.BlockSpec((nw, db, cb), lambda ci: (0, 0, ci))],
        out_shape=[jax.ShapeDtypeStruct((db, c), F32), jax.ShapeDtypeStruct((nw, db, c), F32)],
        compiler_params=pltpu.CompilerParams(dimension_semantics=("parallel",), vmem_limit_bytes=VMEM_LIMIT),
        name="dn_step_prep",
    )(qkv, conv_prev_t, conv_w)


def _dn_step_body(q_ref, k_ref, v_ref, z_ref, gb_ref, ng_ref, s_ref, o_ref, so_ref):
    gb = gb_ref[...]
    ng = ng_ref[...]
    rows = lax.broadcasted_iota(jnp.int32, (SUBLANES, DN_DK), 0)
    heads = range(DN_HV)
    kh = [k_ref[i:i + 1, :] for i in range(DN_HK)]
    qh = [q_ref[i:i + 1, :] for i in range(DN_HK)]
    qk = [jnp.sum(a * b, -1, keepdims=True) for a, b in zip(qh, kh)]
    lhs = [jnp.where(rows == 0, b, jnp.where(rows == 1, a, 0.0)) for a, b in zip(qh, kh)]
    k8 = [jnp.where(rows == 0, b, 0.0) for b in kh]
    s_old = [s_ref[h] for h in heads]
    ks_qs = [jnp.dot(lhs[h // DN_REP], s_old[h], preferred_element_type=F32) for h in heads]
    eg = [jnp.exp(gb[:, h:h + 1]) for h in heads]
    v_new = [gb[:, DN_HV + h:DN_HV + h + 1] * (v_ref[h:h + 1, :] - eg[h] * ks_qs[h][0:1]) for h in heads]
    outer = [lax.dot_general(k8[h // DN_REP], jnp.where(rows == 0, v_new[h], 0.0), _TN,
                             preferred_element_type=F32) for h in heads]
    for h in heads:
        so_ref[h] = s_old[h] * eg[h] + outer[h]
        o = eg[h] * ks_qs[h][1:2] + qk[h // DN_REP] * v_new[h]
        o_ref[h:h + 1, :] = _gated_rms(o, ng, z_ref[h:h + 1, :])


def _dn_step(q, k, v, z, gates, norm_g, s_all, layer):
    db = q.shape[0]
    qk_spec = pl.BlockSpec((None, DN_HK, DN_DK), lambda b: (b, 0, 0))
    vz_spec = pl.BlockSpec((None, DN_HV, DN_DV), lambda b: (b, 0, 0))
    s_spec = pl.BlockSpec((None, None, DN_HV, DN_DK, DN_DV), lambda b: (layer, b, 0, 0, 0))
    return pl.pallas_call(
        _dn_step_body,
        grid=(db,),
        in_specs=[qk_spec, qk_spec, vz_spec, vz_spec,
                  pl.BlockSpec((None, 1, LANES), lambda b: (b, 0, 0)),
                  pl.BlockSpec((1, DN_DV), lambda b: (0, 0)), s_spec],
        out_specs=[vz_spec, s_spec],
        out_shape=[jax.ShapeDtypeStruct((db, DN_HV, DN_DV), F32), jax.ShapeDtypeStruct(s_all.shape, F32)],
        input_output_aliases={6: 1},
        compiler_params=pltpu.CompilerParams(dimension_semantics=("parallel",), vmem_limit_bytes=VMEM_LIMIT),
        name="dn_step",
    )(q, k, v, z, gates, norm_g.reshape(1, DN_DV), s_all)


def _dn_layer(xp, xs, mod_p, mod_s, seq, conv_prev, s_all, layer, w_in, conv_w, a_log, dt_bias, norm_g, w_out,
              ln_g, ln_b, alpha):
    mp, d = xp.shape
    bp = mp // seq
    db = xs.shape[0]
    conv_dim = 2 * DN_HK * DN_DK + DN_HV * DN_DV
    v_w = DN_HV * DN_DV
    w_all = jnp.pad(w_in, ((0, 0), (0, LANES - 2 * DN_HV))).astype(BF16)
    w_o = w_out.astype(BF16)
    alog = jnp.pad(a_log, (0, LANES - DN_HV)).reshape(1, LANES)
    dtb = jnp.pad(dt_bias, (0, LANES - DN_HV)).reshape(1, LANES)
    lane_row = pl.BlockSpec((1, LANES), lambda i: (0, 0))
    segments = [conv_dim, v_w, (LANES, _dn_gates, [(alog, lane_row), (dtb, lane_row)])]

    qkv, z, gates = _in_proj(xp, mod_p, seq, w_all, segments, tm=PROJ_TM, name="dn_in_p")
    qkv3 = qkv.reshape(bp, seq, conv_dim)
    cv_p = qkv3[:, seq - (DN_CONV_W - 1):]
    qkvc = _dn_prep(qkv3, conv_w, tt=DN_PREP_ROWS, cb=512)
    o, s_p = _dn_chunks(qkvc, z.reshape(bp, seq, v_w), gates.reshape(bp, seq, LANES), norm_g)
    xp_new = _out_proj_norm(o.reshape(mp, v_w), w_o, xp, mod_p, seq, ln_g, ln_b, tm=PROJ_TM, alpha=alpha,
                            name="dn_out_p")

    qkv_s, z_s, gates_s = _in_proj(xs, mod_s, 1, w_all, segments, tm=db, name="dn_in_s")
    qkvc_s, cv_t = _dn_step_prep(qkv_s, jnp.swapaxes(conv_prev, 0, 1), conv_w, cb=512)
    k_w = DN_HK * DN_DK
    o_s, s_s = _dn_step(qkvc_s[:, :k_w].reshape(db, DN_HK, DN_DK),
                        qkvc_s[:, k_w:2 * k_w].reshape(db, DN_HK, DN_DK),
                        qkvc_s[:, 2 * k_w:].reshape(db, DN_HV, DN_DV),
                        z_s.reshape(db, DN_HV, DN_DV), gates_s.reshape(db, 1, LANES), norm_g, s_all, layer)
    xs_new = _out_proj_norm(o_s.reshape(db, v_w), w_o, xs, mod_s, 1, ln_g, ln_b, tm=db, alpha=alpha,
                            name="dn_out_s")
    return xp_new, xs_new, s_p, cv_p, s_s, jnp.swapaxes(cv_t, 0, 1)


def _diff_lambda(lq1_ref, lk1_ref, lq2_ref, lk2_ref, lam_init):
    e1 = jnp.exp(jnp.sum(lq1_ref[...] * lk1_ref[...], -1, keepdims=True))
    e2 = jnp.exp(jnp.sum(lq2_ref[...] * lk2_ref[...], -1, keepdims=True))
    return e1 - e2 + lam_init


def _diff_head_out(o, sg, z, lam_init):
    on = o * lax.rsqrt(jnp.mean(o * o, -1, keepdims=True) + RMS_EPS) * sg * (1.0 - lam_init)
    return on * _silu(z)


def _diff_flash_body(q_ref, k_ref, v_ref, z_ref, lq1_ref, lk1_ref, lq2_ref, lk2_ref, sg_ref, o_ref,
                     m_ref, l_ref, acc_ref, *, lam_init):
    i = pl.program_id(2)
    j = pl.program_id(3)
    tq, tk = q_ref.shape[0], k_ref.shape[0]
    gw = DIFF_G * tq

    @pl.when(j == 0)
    def _():
        m_ref[...] = jnp.full_like(m_ref, -jnp.inf)
        l_ref[...] = jnp.zeros_like(l_ref)
        acc_ref[...] = jnp.zeros_like(acc_ref)

    def update(masked):
        v = v_ref[...].astype(BF16)
        if masked:
            keep = (lax.broadcasted_iota(jnp.int32, (tk, gw), 0)
                    <= (lax.broadcasted_iota(jnp.int32, (tk, gw), 1) & (tq - 1)))
        sts = []
        for half in range(2):
            qm = jnp.concatenate([q_ref[:, (g * 2 + half) * DIFF_HD:(g * 2 + half + 1) * DIFF_HD]
                                  for g in range(DIFF_G)], axis=0) * (DIFF_HD ** -0.5 * LOG2E)
            st = lax.dot_general(k_ref[:, half * DIFF_HD:(half + 1) * DIFF_HD], qm, _NT,
                                 preferred_element_type=F32)
            sts.append(jnp.where(keep, st, -jnp.inf) if masked else st)
        for half in range(2):
            st = sts[half]
            cols = slice(half * gw, (half + 1) * gw)
            m_prev = m_ref[:, cols]
            m_new = jnp.maximum(m_prev, jnp.max(st, 0, keepdims=True))
            alpha = jnp.exp2(m_prev - m_new)
            p = jnp.exp2(st - m_new)
            l_ref[:, cols] = alpha * l_ref[:, cols] + jnp.sum(p, 0, keepdims=True)
            acc_ref[:, cols] = alpha * acc_ref[:, cols] + lax.dot_general(
                v, p.astype(BF16), _TN, preferred_element_type=F32)
            m_ref[:, cols] = m_new

    @pl.when(j < i)
    def _():
        update(False)

    @pl.when(j == i)
    def _():
        update(True)
        lam = _diff_lambda(lq1_ref, lk1_ref, lq2_ref, lk2_ref, lam_init)
        sg = sg_ref[...]
        for g in range(DIFF_G):
            c0 = slice(g * tq, (g + 1) * tq)
            c1 = slice(gw + g * tq, gw + (g + 1) * tq)
            ot = acc_ref[:, c0] * (1.0 / l_ref[:, c0]) - lam * (acc_ref[:, c1] * (1.0 / l_ref[:, c1]))
            o_ref[:, g * DIFF_VD:(g + 1) * DIFF_VD] = _diff_head_out(
                ot.T, sg, z_ref[:, g * DIFF_VD:(g + 1) * DIFF_VD], lam_init)


def _diff_flash(q, k, v, z, lams, subln_g, lam_init, *, tq):
    b, t, _ = q.shape
    assert t % tq == 0 and tq & (tq - 1) == 0, (t, tq)
    nq = t // tq
    qw = DIFF_G * 2 * DIFF_HD
    ow = DIFF_G * DIFF_VD
    n_sub = 2 * DIFF_G
    vec = pl.BlockSpec((1, DIFF_HD), lambda bi, h, i, j: (0, 0))
    return pl.pallas_call(
        functools.partial(_diff_flash_body, lam_init=lam_init),
        grid=(b, DIFF_KVH, nq, nq),
        in_specs=[pl.BlockSpec((None, tq, qw), lambda bi, h, i, j: (bi, i, h)),
                  pl.BlockSpec((None, tq, 2 * DIFF_HD), lambda bi, h, i, j: (bi, jnp.minimum(j, i), h)),
                  pl.BlockSpec((None, tq, DIFF_VD), lambda bi, h, i, j: (bi, jnp.minimum(j, i), h)),
                  pl.BlockSpec((None, tq, ow), lambda bi, h, i, j: (bi, i, h)),
                  vec, vec, vec, vec,
                  pl.BlockSpec((1, DIFF_VD), lambda bi, h, i, j: (0, 0))],
        out_specs=pl.BlockSpec((None, tq, ow), lambda bi, h, i, j: (bi, i, h)),
        out_shape=jax.ShapeDtypeStruct((b, t, DIFF_H * DIFF_VD), F32),
        scratch_shapes=[pltpu.VMEM((1, n_sub * tq), F32), pltpu.VMEM((1, n_sub * tq), F32),
                        pltpu.VMEM((DIFF_VD, n_sub * tq), F32)],
        compiler_params=pltpu.CompilerParams(
            dimension_semantics=("parallel", "parallel", "parallel", "arbitrary"),
            vmem_limit_bytes=VMEM_LIMIT),
        name="diff_flash",
    )(q, k, v, z, *lams, subln_g.reshape(1, DIFF_VD))


def _diff_decode_body(pt_ref, q_ref, ks_ref, vs_ref, z_ref, lq1_ref, lk1_ref, lq2_ref, lk2_ref, sg_ref,
                      *rest, pp, lam_init):
    del pt_ref
    k_pages = rest[:pp]
    v_pages = rest[pp:2 * pp]
    o_ref = rest[2 * pp]
    qb_ref, m_ref, l_ref, acc_ref = rest[2 * pp + 1:]
    j = pl.program_id(1)
    n_rows = DIFF_KVH * DIFF_G * 2
    head_shift = (DIFF_G * 2).bit_length() - 1
    kw = 2 * DIFF_HD
    page_rows = k_pages[0].shape[0]

    @pl.when(j == 0)
    def _():
        d_i = lax.broadcasted_iota(jnp.int32, (DIFF_HD, kw), 0)
        c_i = lax.broadcasted_iota(jnp.int32, (DIFF_HD, kw), 1)
        spread = jnp.where((c_i & (DIFF_HD - 1)) == d_i, 1.0, 0.0)
        qt = jnp.dot(q_ref[...], spread, preferred_element_type=F32)
        r_i = lax.broadcasted_iota(jnp.int32, (n_rows, kw), 0)
        half = lax.broadcasted_iota(jnp.int32, (n_rows, kw), 1) >> (DIFF_HD.bit_length() - 1)
        qb_ref[...] = jnp.where(half == (r_i & 1), qt, 0.0) * (DIFF_HD ** -0.5)
        m_ref[...] = jnp.full_like(m_ref, -jnp.inf)
        l_ref[...] = jnp.zeros_like(l_ref)
        acc_ref[...] = jnp.zeros_like(acc_ref)

    qb = qb_ref[...]
    s = jnp.concatenate([lax.dot_general(qb, kp[...], _NT, preferred_element_type=F32) for kp in k_pages],
                        axis=1)
    row_head = lax.broadcasted_iota(jnp.int32, s.shape, 0) >> head_shift
    col_head = lax.broadcasted_iota(jnp.int32, s.shape, 1) & (DIFF_KVH - 1)
    s = jnp.where(row_head == col_head, s, -jnp.inf)
    m_prev = m_ref[...]
    m_new = jnp.maximum(m_prev, jnp.max(s, -1, keepdims=True))
    alpha = jnp.exp(m_prev - m_new)
    p = jnp.exp(s - m_new)
    l_new = alpha * l_ref[...] + jnp.sum(p, -1, keepdims=True)
    pv = jnp.dot(p[:, 0:page_rows], v_pages[0][...], preferred_element_type=F32)
    for t in range(1, pp):
        pv = pv + jnp.dot(p[:, t * page_rows:(t + 1) * page_rows], v_pages[t][...],
                          preferred_element_type=F32)
    acc_new = alpha * acc_ref[...] + pv
    m_ref[...] = m_new
    l_ref[...] = l_new
    acc_ref[...] = acc_new

    @pl.when(j == pl.num_programs(1) - 1)
    def _():
        rh = lax.broadcasted_iota(jnp.int32, (n_rows, kw), 0) >> head_shift
        ks = jnp.zeros((n_rows, kw), F32)
        vs = jnp.zeros((n_rows, kw), F32)
        for h in range(DIFF_KVH):
            ks = jnp.where(rh == h, ks_ref[h:h + 1, :], ks)
            vs = jnp.where(rh == h, vs_ref[h:h + 1, :], vs)
        s_self = jnp.sum(qb * ks, -1, keepdims=True)
        m_fin = jnp.maximum(m_new, s_self)
        a = jnp.exp(m_new - m_fin)
        p_self = jnp.exp(s_self - m_fin)
        l_fin = a * l_new + p_self
        acc = (a * acc_new + p_self * vs) / l_fin
        lam = _diff_lambda(lq1_ref, lk1_ref, lq2_ref, lk2_ref, lam_init)
        sg = sg_ref[...]
        for hh in range(DIFF_H):
            o = acc[2 * hh:2 * hh + 1] - lam * acc[2 * hh + 1:2 * hh + 2]
            o_ref[hh:hh + 1, :] = _diff_head_out(o, sg, z_ref[hh:hh + 1, :], lam_init)


def _diff_decode(q, k_self, v_self, z, cache_k, cache_v, layer, page_table, lams, subln_g, lam_init, *, pp):
    db = q.shape[0]
    n_pages = page_table.shape[1]
    assert n_pages % pp == 0
    page_rows, kw = cache_k.shape[2:]
    n_rows = DIFF_KVH * DIFF_G * 2

    def page_spec(t):
        return pl.BlockSpec((None, None, page_rows, kw), lambda b, j, pt: (layer, pt[b, j * pp + t], 0, 0))

    def seq_spec(shape):
        return pl.BlockSpec((None,) + shape, lambda b, j, pt: (b, 0, 0))

    vec = pl.BlockSpec((1, DIFF_HD), lambda b, j, pt: (0, 0))
    grid_spec = pltpu.PrefetchScalarGridSpec(
        num_scalar_prefetch=1,
        grid=(db, n_pages // pp),
        in_specs=[seq_spec((n_rows, DIFF_HD)), seq_spec((DIFF_KVH, kw)), seq_spec((DIFF_KVH, kw)),
                  seq_spec((DIFF_H, DIFF_VD)), vec, vec, vec, vec,
                  pl.BlockSpec((1, DIFF_VD), lambda b, j, pt: (0, 0))]
                 + [page_spec(t) for t in range(pp)] + [page_spec(t) for t in range(pp)],
        out_specs=seq_spec((DIFF_H, DIFF_VD)),
        scratch_shapes=[pltpu.VMEM((n_rows, kw), F32), pltpu.VMEM((n_rows, 1), F32),
                        pltpu.VMEM((n_rows, 1), F32), pltpu.VMEM((n_rows, DIFF_VD), F32)],
    )
    return pl.pallas_call(
        functools.partial(_diff_decode_body, pp=pp, lam_init=lam_init),
        grid_spec=grid_spec,
        out_shape=jax.ShapeDtypeStruct((db, DIFF_H, DIFF_VD), F32),
        compiler_params=pltpu.CompilerParams(dimension_semantics=("parallel", "arbitrary"),
                                             vmem_limit_bytes=VMEM_LIMIT),
        name="diff_decode",
    )(page_table, q, k_self, v_self, z, *lams, subln_g.reshape(1, DIFF_VD),
      *([cache_k] * pp), *([cache_v] * pp))


def _diff_layer(xp, xs, mod_p, mod_s, seq, layer_idx, j, cache_k, cache_v, page_table, w_in, lams, subln_g,
                w_out, ln_g, ln_b, alpha):
    mp, d = xp.shape
    bp = mp // seq
    db = xs.shape[0]
    lam_init = 0.8 - 0.6 * math.exp(-0.3 * layer_idx)
    q_w = DIFF_H * 2 * DIFF_HD
    k_w = DIFF_KVH * 2 * DIFF_HD
    v_w = DIFF_KVH * DIFF_VD
    w_all = w_in.astype(BF16)
    w_o = w_out.astype(BF16)
    lams = [a.reshape(1, DIFF_HD) for a in lams]
    segments = [q_w, k_w, v_w, w_in.shape[1] - q_w - k_w - v_w]

    q, k, v, z = _in_proj(xp, mod_p, seq, w_all, segments, tm=PROJ_TM, name="diff_in_p")
    o = _diff_flash(q.reshape(bp, seq, q_w), k.reshape(bp, seq, k_w), v.reshape(bp, seq, v_w),
                    z.reshape(bp, seq, -1), lams, subln_g, lam_init, tq=min(DIFF_TQ, seq))
    xp_new = _out_proj_norm(o.reshape(mp, -1), w_o, xp, mod_p, seq, ln_g, ln_b, tm=PROJ_TM, alpha=alpha,
                            name="diff_out_p")

    q_s, k_s, v_s, z_s = _in_proj(xs, mod_s, 1, w_all, segments, tm=db, name="diff_in_s")
    n_l, n_pool, page = cache_k.shape[:3]
    o_s = _diff_decode(q_s.reshape(db, DIFF_KVH * DIFF_G * 2, DIFF_HD), k_s.reshape(db, DIFF_KVH, 2 * DIFF_HD),
                       v_s.reshape(db, DIFF_KVH, DIFF_VD), z_s.reshape(db, DIFF_H, DIFF_VD),
                       cache_k.reshape(n_l, n_pool, page * DIFF_KVH, 2 * DIFF_HD),
                       cache_v.reshape(n_l, n_pool, page * DIFF_KVH, DIFF_VD),
                       j, page_table, lams, subln_g, lam_init, pp=DIFF_PAGES_PER_STEP)
    xs_new = _out_proj_norm(o_s.reshape(db, -1), w_o, xs, mod_s, 1, ln_g, ln_b, tm=db, alpha=alpha,
                            name="diff_out_s")
    return (xp_new, xs_new, k.reshape(bp, seq, DIFF_KVH, 2 * DIFF_HD), v.reshape(bp, seq, DIFF_KVH, DIFF_VD),
            k_s.reshape(db, 1, DIFF_KVH, 2 * DIFF_HD), v_s.reshape(db, 1, DIFF_KVH, DIFF_VD))


def _swa_band_body(q_ref, kc_ref, kp_ref, vc_ref, vp_ref, z_ref, sink_ref, o_ref):
    i = pl.program_id(1)
    blk = q_ref.shape[0]
    cols = SWA_G * blk
    kj = lax.broadcasted_iota(jnp.int32, (2 * blk, cols), 0)
    qi = lax.broadcasted_iota(jnp.int32, (2 * blk, cols), 1) & (blk - 1)
    dist = qi + blk - kj
    keep = (dist >= 0) & (dist <= WINDOW) & ((i > 0) | (kj >= blk))
    for h in range(SWA_KVH):
        hs = slice(h * SWA_HD, (h + 1) * SWA_HD)
        k2 = jnp.concatenate([kp_ref[:, hs], kc_ref[:, hs]], axis=0)
        v2 = jnp.concatenate([vp_ref[:, hs], vc_ref[:, hs]], axis=0)
        qs = jnp.concatenate([q_ref[:, (h * SWA_G + g) * SWA_HD:(h * SWA_G + g + 1) * SWA_HD]
                              for g in range(SWA_G)], axis=0) * (SWA_HD ** -0.5)
        sink = jnp.concatenate([jnp.broadcast_to(sink_ref[:, h * SWA_G + g:h * SWA_G + g + 1], (1, blk))
                                for g in range(SWA_G)], axis=1)
        st = lax.dot_general(k2, qs, _NT, preferred_element_type=F32)
        st = jnp.where(keep, st, -jnp.inf)
        m = jnp.maximum(jnp.max(st, 0, keepdims=True), sink)
        p = jnp.exp(st - m)
        den = jnp.sum(p, 0, keepdims=True) + jnp.exp(sink - m)
        ot = lax.dot_general(v2, p, _TN, preferred_element_type=F32) * (1.0 / den)
        for g in range(0, SWA_G, 2):
            cs = slice((h * SWA_G + g) * SWA_HD, (h * SWA_G + g + 2) * SWA_HD)
            pair = jnp.concatenate([ot[:, g * blk:(g + 1) * blk], ot[:, (g + 1) * blk:(g + 2) * blk]], axis=0)
            o_ref[:, cs] = pair.T * _silu(z_ref[:, cs])


def _swa_band(q, k, v, z, sinks):
    b, t, qw = q.shape
    blk = WINDOW
    kw = SWA_KVH * SWA_HD
    cur = pl.BlockSpec((None, blk, kw), lambda bi, i: (bi, i, 0))
    prev = pl.BlockSpec((None, blk, kw), lambda bi, i: (bi, jnp.maximum(i - 1, 0), 0))
    wide = pl.BlockSpec((None, blk, qw), lambda bi, i: (bi, i, 0))
    return pl.pallas_call(
        _swa_band_body,
        grid=(b, t // blk),
        in_specs=[wide, cur, prev, cur, prev, wide, pl.BlockSpec((1, SWA_H), lambda bi, i: (0, 0))],
        out_specs=wide,
        out_shape=jax.ShapeDtypeStruct((b, t, qw), F32),
        compiler_params=pltpu.CompilerParams(dimension_semantics=("parallel", "parallel"),
                                             vmem_limit_bytes=VMEM_LIMIT),
        name="swa_band",
    )(q, k, k, v, v, z, sinks.reshape(1, SWA_H))


def _swa_decode_body(q_ref, kb_ref, vb_ref, kn_ref, vn_ref, z_ref, sink_ref, o_ref, ko_ref, vo_ref):
    win = kb_ref.shape[1]
    for h in range(SWA_KVH):
        hs = slice(h * SWA_HD, (h + 1) * SWA_HD)
        gs = slice(h * SWA_G, (h + 1) * SWA_G)
        qh = q_ref[:, gs, :]
        s = jnp.einsum("bqd,bkd->bqk", qh, kb_ref[:, :, hs], preferred_element_type=F32) * (SWA_HD ** -0.5)
        s_new = jnp.sum(qh * kn_ref[:, :, hs], -1, keepdims=True) * (SWA_HD ** -0.5)
        sink = sink_ref[gs, :][None]
        m = jnp.maximum(jnp.maximum(jnp.max(s, -1, keepdims=True), s_new), sink)
        p = jnp.exp(s - m)
        p_new = jnp.exp(s_new - m)
        den = jnp.sum(p, -1, keepdims=True) + p_new + jnp.exp(sink - m)
        o = jnp.einsum("bqk,bkd->bqd", p, vb_ref[:, :, hs], preferred_element_type=F32)
        o = (o + p_new * vn_ref[:, :, hs]) / den
        o_ref[:, gs, :] = o * _silu(z_ref[:, gs, :])
    ko_ref[:, 0:win - 1, :] = kb_ref[:, 1:win, :]
    ko_ref[:, win - 1:win, :] = kn_ref[...]
    vo_ref[:, 0:win - 1, :] = vb_ref[:, 1:win, :]
    vo_ref[:, win - 1:win, :] = vn_ref[...]


def _swa_decode(q, k_new, v_new, z, buf_k, buf_v, sinks, *, bb):
    db, win, kw = buf_k.shape
    qz = pl.BlockSpec((bb, SWA_H, SWA_HD), lambda b: (b, 0, 0))
    buf = pl.BlockSpec((bb, win, kw), lambda b: (b, 0, 0))
    new = pl.BlockSpec((bb, 1, kw), lambda b: (b, 0, 0))
    return pl.pallas_call(
        _swa_decode_body,
        grid=(db // bb,),
        in_specs=[qz, buf, buf, new, new, qz, pl.BlockSpec((SWA_H, 1), lambda b: (0, 0))],
        out_specs=[qz, buf, buf],
        out_shape=[jax.ShapeDtypeStruct(q.shape, F32), jax.ShapeDtypeStruct(buf_k.shape, F32),
                   jax.ShapeDtypeStruct(buf_v.shape, F32)],
        compiler_params=pltpu.CompilerParams(dimension_semantics=("parallel",), vmem_limit_bytes=VMEM_LIMIT),
        name="swa_decode",
    )(q, buf_k, buf_v, k_new, v_new, z, sinks.reshape(SWA_H, 1))


def _swa_layer(xp, xs, mod_p, mod_s, seq, buf_k, buf_v, sinks, w_in, w_out, ln_g, ln_b, alpha):
    mp, d = xp.shape
    bp = mp // seq
    db = xs.shape[0]
    q_w = SWA_H * SWA_HD
    kv_w = SWA_KVH * SWA_HD
    w_all = w_in.astype(BF16)
    w_o = w_out.astype(BF16)
    segments = [q_w, kv_w, kv_w, w_in.shape[1] - q_w - 2 * kv_w]

    q, k, v, z = _in_proj(xp, mod_p, seq, w_all, segments, tm=PROJ_TM, name="swa_in_p")
    k = k.reshape(bp, seq, kv_w)
    v = v.reshape(bp, seq, kv_w)
    o = _swa_band(q.reshape(bp, seq, q_w), k, v, z.reshape(bp, seq, q_w), sinks)
    xp_new = _out_proj_norm(o.reshape(mp, q_w), w_o, xp, mod_p, seq, ln_g, ln_b, tm=PROJ_TM, alpha=alpha,
                            name="swa_out_p")

    q_s, k_s, v_s, z_s = _in_proj(xs, mod_s, 1, w_all, segments, tm=db, name="swa_in_s")
    win = buf_k.shape[1]
    o_s, k_buf, v_buf = _swa_decode(q_s.reshape(db, SWA_H, SWA_HD), k_s.reshape(db, 1, kv_w),
                                    v_s.reshape(db, 1, kv_w), z_s.reshape(db, SWA_H, SWA_HD),
                                    buf_k.reshape(db, win, kv_w), buf_v.reshape(db, win, kv_w), sinks,
                                    bb=SWA_DEC_BB)
    xs_new = _out_proj_norm(o_s.reshape(db, q_w), w_o, xs, mod_s, 1, ln_g, ln_b, tm=db, alpha=alpha,
                            name="swa_out_s")
    kv_shape = (SWA_KVH, SWA_HD)
    return (xp_new, xs_new, k[:, seq - WINDOW:].reshape((bp, WINDOW) + kv_shape),
            v[:, seq - WINDOW:].reshape((bp, WINDOW) + kv_shape),
            k_buf.reshape((db, win) + kv_shape), v_buf.reshape((db, win) + kv_shape))


def kernel(x_prompt, x_sample, cache_diff_k, cache_diff_v, page_table, cache_swa_k, cache_swa_v, state_dn_S, state_dn_conv, c_prompt, c_sample, ada_w, ada_b, ln_g, ln_b, dn_w_in, dn_conv_w, dn_a_log, dn_dt_bias, dn_norm_g, dn_w_out, diff_w_in, diff_lq1, diff_lk1, diff_lq2, diff_lk2, diff_subln_g, diff_w_out, swa_w_in, swa_sinks, swa_w_out):
    bp, seq, d = x_prompt.shape
    db, dec_seq, _ = x_sample.shape
    assert dec_seq == 1 and bp <= SUBLANES
    depth = ada_w.shape[0]
    alpha = (2 * depth) ** 0.25
    xp = x_prompt.reshape(bp * seq, d)
    xs = x_sample.reshape(db, d)

    c_all = jnp.concatenate([c_prompt, jnp.zeros((SUBLANES - bp, d), F32), c_sample], axis=0)
    st = {n: [] for n in ("dk_p", "dv_p", "dk_s", "dv_s", "sk_p", "sv_p", "sk_s", "sv_s",
                          "S_p", "cv_p", "cv_s")}
    s_s = state_dn_S
    for l in range(depth):
        j = l // N_MIXERS
        bias_op = (ada_b[l].reshape(1, 3 * d), pl.BlockSpec((1, 3 * d), lambda i: (0, 0)))
        mod, = _proj(c_all, ada_w[l].astype(BF16), [(3 * d, _add_bias, [bias_op])], tm=c_all.shape[0],
                     pro=_silu, name="adaln")
        mod_p = mod[:bp].reshape(bp, 1, 3 * d)
        mod_s = mod[SUBLANES:]
        if l % N_MIXERS == 0:
            xp, xs, s_p, cv_p, s_s, cv_s = _dn_layer(
                xp, xs, mod_p, mod_s, seq, state_dn_conv[j], s_s, j, dn_w_in[j], dn_conv_w[j],
                dn_a_log[j], dn_dt_bias[j], dn_norm_g[j], dn_w_out[j], ln_g[l], ln_b[l], alpha)
            st["S_p"].append(s_p)
            st["cv_p"].append(cv_p)
            st["cv_s"].append(cv_s)
        elif l % N_MIXERS == 1:
            xp, xs, k_p, v_p, k_s, v_s = _diff_layer(
                xp, xs, mod_p, mod_s, seq, l, j, cache_diff_k, cache_diff_v, page_table, diff_w_in[j],
                (diff_lq1[j], diff_lk1[j], diff_lq2[j], diff_lk2[j]), diff_subln_g[j], diff_w_out[j],
                ln_g[l], ln_b[l], alpha)
            st["dk_p"].append(k_p)
            st["dv_p"].append(v_p)
            st["dk_s"].append(k_s)
            st["dv_s"].append(v_s)
        else:
            xp, xs, k_p, v_p, k_s, v_s = _swa_layer(
                xp, xs, mod_p, mod_s, seq, cache_swa_k[j], cache_swa_v[j], swa_sinks[j], swa_w_in[j],
                swa_w_out[j], ln_g[l], ln_b[l], alpha)
            st["sk_p"].append(k_p)
            st["sv_p"].append(v_p)
            st["sk_s"].append(k_s)
            st["sv_s"].append(v_s)
    return (xp.reshape(bp, seq, d), xs.reshape(db, 1, d),
            jnp.stack(st["dk_p"]), jnp.stack(st["dv_p"]), jnp.stack(st["dk_s"]), jnp.stack(st["dv_s"]),
            jnp.stack(st["sk_p"]), jnp.stack(st["sv_p"]), jnp.stack(st["sk_s"]), jnp.stack(st["sv_s"]),
            jnp.stack(st["S_p"]), jnp.stack(st["cv_p"]), s_s, jnp.stack(st["cv_s"]))
```

```python
import functools
import math

import jax
import jax.numpy as jnp
from jax import lax
from jax.experimental import pallas as pl
from jax.experimental.pallas import tpu as pltpu

F32 = jnp.float32
BF16 = jnp.bfloat16

LN_EPS = 1e-5
RMS_EPS = 1e-6
N_MIXERS = 3

DN_HK = 8
DN_HV = 16
DN_DK = 128
DN_DV = 128
DN_REP = DN_HV // DN_HK
DN_CONV_W = 4
DN_CHUNK = 128
DN_PREP_ROWS = 256
PROJ_TM = 512

DIFF_H = 8
DIFF_KVH = 4
DIFF_G = DIFF_H // DIFF_KVH
DIFF_HD = 64
DIFF_VD = 2 * DIFF_HD
DIFF_TQ = 512
LOG2E = math.log2(math.e)
DIFF_PAGES_PER_STEP = 16

SWA_H = 16
SWA_KVH = 2
SWA_G = SWA_H // SWA_KVH
SWA_HD = 64
WINDOW = 128
SWA_DEC_BB = 8

LANES = 128
SUBLANES = 8
VMEM_LIMIT = 48 * 1024 * 1024

_NT = (((1,), (1,)), ((), ()))
_TN = (((0,), (0,)), ((), ()))


def _silu(x):
    return (0.5 * x) * (1.0 + jnp.tanh(0.5 * x))


def _proj_body(x_ref, w_ref, *refs, n_pro, segs, pro):
    pro_refs = refs[:n_pro]
    n_epi = sum(s[3] for s in segs)
    epi_refs = refs[n_pro:n_pro + n_epi]
    o_refs = refs[n_pro + n_epi:]
    x = x_ref[...]
    if pro is not None:
        x = pro(x, *pro_refs)
    xb = x.astype(BF16)
    e = 0
    for (start, width, epi, n_e), o_ref in zip(segs, o_refs):
        acc = jnp.dot(xb, w_ref[:, start:start + width], preferred_element_type=F32)
        if epi is not None:
            acc = epi(acc, *epi_refs[e:e + n_e])
        e += n_e
        o_ref[...] = acc


def _proj(x, w, segments, *, tm, pro=None, pro_ops=(), name="proj"):
    m, k = x.shape
    n = w.shape[1]
    assert m % tm == 0, (m, tm)
    ops = [x, w]
    in_specs = [pl.BlockSpec((tm, k), lambda i: (i, 0)),
                pl.BlockSpec((k, n), lambda i: (0, 0), pipeline_mode=pl.Buffered(1))]
    for a, s in pro_ops:
        ops.append(a)
        in_specs.append(s)
    segs = []
    start = 0
    for width, epi, epi_ops in segments:
        segs.append((start, width, epi, len(epi_ops)))
        start += width
        for a, s in epi_ops:
            ops.append(a)
            in_specs.append(s)
    assert start <= n, (start, n)
    return pl.pallas_call(
        functools.partial(_proj_body, n_pro=len(pro_ops), segs=tuple(segs), pro=pro),
        grid=(m // tm,),
        in_specs=in_specs,
        out_specs=[pl.BlockSpec((tm, s[1]), lambda i: (i, 0)) for s in segs],
        out_shape=[jax.ShapeDtypeStruct((m, s[1]), F32) for s in segs],
        compiler_params=pltpu.CompilerParams(dimension_semantics=("parallel",), vmem_limit_bytes=VMEM_LIMIT),
        name=name,
    )(*ops)


def _modulate(x, mod_ref):
    d = x.shape[-1]
    return x * (1.0 + mod_ref[:, d:2 * d]) + mod_ref[:, 0:d]


def _deepnorm(y, xres_ref, mod_ref, g_ref, b_ref, *, alpha):
    d = y.shape[-1]
    h = alpha * xres_ref[...] + (1.0 + mod_ref[:, 2 * d:3 * d]) * y
    hc = h - jnp.mean(h, -1, keepdims=True)
    var = jnp.mean(hc * hc, -1, keepdims=True)
    return hc * lax.rsqrt(var + LN_EPS) * g_ref[...] + b_ref[...]


def _add_bias(y, b_ref):
    return y + b_ref[...]


def _dn_gates(y, alog_ref, dtb_ref):
    lane = lax.broadcasted_iota(jnp.int32, y.shape, 1)
    t = y + dtb_ref[...]
    softplus = jnp.maximum(t, 0.0) + jnp.log1p(jnp.exp(-jnp.abs(t)))
    g = -jnp.exp(alog_ref[...]) * softplus
    return jnp.where(lane < DN_HV, g, jax.nn.sigmoid(y))


def _mod_spec(mod, tm, rows_per_mod):
    w = mod.shape[-1]
    if mod.ndim == 3:
        assert rows_per_mod % tm == 0, (rows_per_mod, tm)
        return pl.BlockSpec((None, 1, w), lambda i: (i * tm // rows_per_mod, 0, 0))
    return pl.BlockSpec((tm, w), lambda i: (i, 0))


def _in_proj(x, mod, rows_per_mod, w, segments, *, tm, name="in_proj"):
    segments = [(s, None, ()) if isinstance(s, int) else s for s in segments]
    return _proj(x, w, segments, tm=tm, pro=_modulate, pro_ops=[(mod, _mod_spec(mod, tm, rows_per_mod))],
                 name=name)


def _out_proj_norm(y, w, xres, mod, rows_per_mod, ln_g, ln_b, *, tm, alpha, name="out_proj"):
    d = w.shape[1]
    row = pl.BlockSpec((1, d), lambda i: (0, 0))
    epi_ops = [(xres, pl.BlockSpec((tm, d), lambda i: (i, 0))), (mod, _mod_spec(mod, tm, rows_per_mod)),
               (ln_g.reshape(1, d), row), (ln_b.reshape(1, d), row)]
    return _proj(y, w, [(d, functools.partial(_deepnorm, alpha=alpha), epi_ops)], tm=tm, name=name)[0]


def _l2norm_heads(y, o_ref, scale):
    for h in range(y.shape[-1] // DN_DK):
        yh = y[:, h * DN_DK:(h + 1) * DN_DK]
        inv = lax.rsqrt(jnp.sum(yh * yh, -1, keepdims=True) + RMS_EPS)
        o_ref[:, h * DN_DK:(h + 1) * DN_DK] = yh * inv * scale


def _dn_prep_body(x_ref, prev_ref, w_ref, o_ref, *, cb):
    i = pl.program_id(1)
    tt = x_ref.shape[0]
    backs = range(1, DN_CONV_W)
    row = lax.broadcasted_iota(jnp.int32, (SUBLANES, cb), 0)

    def rolled(v):
        return tuple(pltpu.roll(v, d, axis=0) for d in backs)

    for c0 in range(0, x_ref.shape[1], cb):
        cols = slice(c0, c0 + cb)
        taps = [w_ref[s:s + 1, cols] for s in range(DN_CONV_W)]

        def body(si, prev_rolled, cols=cols, taps=taps):
            r0 = pl.multiple_of(si * SUBLANES, SUBLANES)
            cur = x_ref[pl.ds(r0, SUBLANES), cols]
            cur_rolled = rolled(cur)
            back = [jnp.where(row < d, p, q) for d, p, q in zip(backs, prev_rolled, cur_rolled)]
            conv = back[DN_CONV_W - 2] * taps[0]
            for s in range(1, DN_CONV_W - 1):
                conv = conv + back[DN_CONV_W - 2 - s] * taps[s]
            o_ref[pl.ds(r0, SUBLANES), cols] = _silu(conv + cur * taps[DN_CONV_W - 1])
            return cur_rolled

        first = jnp.where(i > 0, prev_ref[:, cols], 0.0)
        lax.fori_loop(0, tt // SUBLANES, body, rolled(first), unroll=8)


def _dn_prep(qkv, conv_w, *, tt, cb):
    b, t, c = qkv.shape
    return pl.pallas_call(
        functools.partial(_dn_prep_body, cb=cb),
        grid=(b, t // tt),
        in_specs=[pl.BlockSpec((None, tt, c), lambda bi, i: (bi, i, 0)),
                  pl.BlockSpec((None, SUBLANES, c),
                               lambda bi, i: (bi, jnp.maximum(i * (tt // SUBLANES) - 1, 0), 0)),
                  pl.BlockSpec((DN_CONV_W, c), lambda bi, i: (0, 0))],
        out_specs=pl.BlockSpec((None, tt, c), lambda bi, i: (bi, i, 0)),
        out_shape=jax.ShapeDtypeStruct(qkv.shape, F32),
        compiler_params=pltpu.CompilerParams(dimension_semantics=("parallel", "parallel"),
                                             vmem_limit_bytes=VMEM_LIMIT),
        name="dn_prep",
    )(qkv, qkv, conv_w)


def _gated_rms(o, ng, z):
    return o * lax.rsqrt(jnp.mean(o * o, -1, keepdims=True) + RMS_EPS) * ng * _silu(z)


def _dn_chunk_body(x_ref, z_ref, gb_ref, ng_ref, o_ref, s_ref):
    ci = pl.program_id(1)
    c = x_ref.shape[0]
    hb = DN_HV

    @pl.when(ci == 0)
    def _():
        s_ref[...] = jnp.zeros_like(s_ref)

    k_w = DN_HK * DN_DK
    gb = gb_ref[...]
    row = lax.broadcasted_iota(jnp.int32, gb.shape, 0)
    gc = gb
    sh = 1
    while sh < c:
        gc = gc + jnp.where(row >= sh, pltpu.roll(gc, sh, axis=0), 0.0)
        sh *= 2
    gct = gc.T
    ri = lax.broadcasted_iota(jnp.int32, (c, c), 0)
    cj = lax.broadcasted_iota(jnp.int32, (c, c), 1)
    tril = ri >= cj
    strict = ri > cj
    merge_masks = []
    k = 0
    while (1 << k) < c:
        merge_masks.append(((ri >> (k + 1)) == (cj >> (k + 1))) & ((ri >> k) != (cj >> k)))
        k += 1
    ng = ng_ref[...]
    heads = range(hb)
    dot = functools.partial(jnp.dot, preferred_element_type=F32)
    def l2n(y, scale):
        return y * lax.rsqrt(jnp.sum(y * y, -1, keepdims=True) + RMS_EPS) * scale

    qh = [l2n(x_ref[:, i * DN_DK:(i + 1) * DN_DK], DN_DK ** -0.5) for i in range(DN_HK)]
    kh = [l2n(x_ref[:, k_w + i * DN_DK:k_w + (i + 1) * DN_DK], 1.0) for i in range(DN_HK)]
    kk = [lax.dot_general(a, a, _NT, preferred_element_type=F32) for a in kh]
    qk = [lax.dot_general(a, b, _NT, preferred_element_type=F32) for a, b in zip(qh, kh)]
    gcol = [gc[:, j:j + 1] for j in heads]
    bcol = [gb[:, hb + j:hb + j + 1] for j in heads]
    decay = [jnp.where(tril, jnp.exp(jnp.where(tril, gcol[j] - gct[j:j + 1, :], 0.0)), 0.0) for j in heads]
    lmat = [jnp.where(strict, bcol[j] * kk[j // DN_REP] * decay[j], 0.0) for j in heads]
    minv = [jnp.where(merge_masks[0], -lmat[j], 0.0) for j in heads]
    for mask in merge_masks[1:]:
        loff = [jnp.where(mask, lmat[j], 0.0) for j in heads]
        y = [loff[j] + dot(loff[j], minv[j]) for j in heads]
        minv = [minv[j] - y[j] - dot(minv[j], y[j]) for j in heads]
    eg = [jnp.exp(gcol[j]) for j in heads]
    rhs = [jnp.concatenate([x_ref[:, 2 * k_w + j * DN_DV:2 * k_w + (j + 1) * DN_DV] * bcol[j],
                            kh[j // DN_REP] * (bcol[j] * eg[j])], axis=1) for j in heads]
    sol = [rhs[j] + dot(minv[j], rhs[j]) for j in heads]
    s_old = [s_ref[j] for j in heads]
    ws = [dot(jnp.concatenate([sol[j][:, DN_DV:], qh[j // DN_REP] * eg[j]], axis=0), s_old[j]) for j in heads]
    v_new = [sol[j][:, :DN_DV] - ws[j][:c] for j in heads]
    o = [ws[j][c:] + dot(qk[j // DN_REP] * decay[j], v_new[j]) for j in heads]
    for j in heads:
        glast = gc[c - 1:c, j:j + 1]
        kd = kh[j // DN_REP] * jnp.exp(glast - gcol[j])
        s_ref[j] = s_old[j] * jnp.exp(glast) + lax.dot_general(kd, v_new[j], _TN, preferred_element_type=F32)
        o_ref[:, j * DN_DV:(j + 1) * DN_DV] = _gated_rms(
            o[j], ng, z_ref[:, j * DN_DV:(j + 1) * DN_DV]).astype(o_ref.dtype)


def _dn_chunks(qkv, z, gates, norm_g):
    b, t, cd = qkv.shape
    c = DN_CHUNK
    assert t % c == 0
    v_w = DN_HV * DN_DV
    o, s = pl.pallas_call(
        _dn_chunk_body,
        grid=(b, t // c),
        in_specs=[pl.BlockSpec((None, c, cd), lambda bi, ci: (bi, ci, 0)),
                  pl.BlockSpec((None, c, v_w), lambda bi, ci: (bi, ci, 0)),
                  pl.BlockSpec((None, c, LANES), lambda bi, ci: (bi, ci, 0)),
                  pl.BlockSpec((1, DN_DV), lambda bi, ci: (0, 0))],
        out_specs=[pl.BlockSpec((None, c, v_w), lambda bi, ci: (bi, ci, 0)),
                   pl.BlockSpec((None, DN_HV, DN_DK, DN_DV), lambda bi, ci: (bi, 0, 0, 0))],
        out_shape=[jax.ShapeDtypeStruct((b, t, v_w), BF16),
                   jax.ShapeDtypeStruct((b, DN_HV, DN_DK, DN_DV), F32)],
        compiler_params=pltpu.CompilerParams(dimension_semantics=("parallel", "arbitrary"),
                                             vmem_limit_bytes=VMEM_LIMIT),
        name="dn_chunks",
    )(qkv, z, gates, norm_g.reshape(1, DN_DV))
    return o, s


def _dn_step_prep_body(x_ref, prev_ref, w_ref, o_ref, cv_ref, *, n_q_blk, n_k_blk):
    c = pl.program_id(0)
    x = x_ref[...]
    w = w_ref[...]
    conv = prev_ref[0] * w[0:1]
    for s in range(1, DN_CONV_W - 1):
        conv = conv + prev_ref[s] * w[s:s + 1]
    conv = conv + x * w[DN_CONV_W - 1:DN_CONV_W]
    y = _silu(conv)
    for s in range(DN_CONV_W - 2):
        cv_ref[s] = prev_ref[s + 1]
    cv_ref[DN_CONV_W - 2] = x

    @pl.when(c >= n_q_blk + n_k_blk)
    def _():
        o_ref[...] = y

    @pl.when(c < n_q_blk + n_k_blk)
    def _():
        _l2norm_heads(y, o_ref, jnp.where(c < n_q_blk, DN_DK ** -0.5, 1.0))


def _dn_step_prep(qkv, conv_prev_t, conv_w, *, cb):
    db, c = qkv.shape
    k_w = DN_HK * DN_DK
    nw = DN_CONV_W - 1
    return pl.pallas_call(
        functools.partial(_dn_step_prep_body, n_q_blk=k_w // cb, n_k_blk=k_w // cb),
        grid=(c // cb,),
        in_specs=[pl.BlockSpec((db, cb), lambda ci: (0, ci)),
                  pl.BlockSpec((nw, db, cb), lambda ci: (0, 0, ci)),
                  pl.BlockSpec((DN_CONV_W, cb), lambda ci: (0, ci))],
        out_specs=[pl.BlockSpec((db, cb), lambda ci: (0, ci)),
                   pl.BlockSpec((nw, db, cb), lambda ci: (0, 0, ci))],
        out_shape=[jax.ShapeDtypeStruct((db, c), F32), jax.ShapeDtypeStruct((nw, db, c), F32)],
        compiler_params=pltpu.CompilerParams(dimension_semantics=("parallel",), vmem_limit_bytes=VMEM_LIMIT),
        name="dn_step_prep",
    )(qkv, conv_prev_t, conv_w)


def _dn_step_body(q_ref, k_ref, v_ref, z_ref, gb_ref, ng_ref, s_ref, o_ref, so_ref):
    gb = gb_ref[...]
    ng = ng_ref[...]
    rows = lax.broadcasted_iota(jnp.int32, (SUBLANES, DN_DK), 0)
    heads = range(DN_HV)
    kh = [k_ref[i:i + 1, :] for i in range(DN_HK)]
    qh = [q_ref[i:i + 1, :] for i in range(DN_HK)]
    qk = [jnp.sum(a * b, -1, keepdims=True) for a, b in zip(qh, kh)]
    lhs = [jnp.where(rows == 0, b, jnp.where(rows == 1, a, 0.0)) for a, b in zip(qh, kh)]
    k8 = [jnp.where(rows == 0, b, 0.0) for b in kh]
    s_old = [s_ref[h] for h in heads]
    ks_qs = [jnp.dot(lhs[h // DN_REP], s_old[h], preferred_element_type=F32) for h in heads]
    eg = [jnp.exp(gb[:, h:h + 1]) for h in heads]
    v_new = [gb[:, DN_HV + h:DN_HV + h + 1] * (v_ref[h:h + 1, :] - eg[h] * ks_qs[h][0:1]) for h in heads]
    outer = [lax.dot_general(k8[h // DN_REP], jnp.where(rows == 0, v_new[h], 0.0), _TN,
                             preferred_element_type=F32) for h in heads]
    for h in heads:
        so_ref[h] = s_old[h] * eg[h] + outer[h]
        o = eg[h] * ks_qs[h][1:2] + qk[h // DN_REP] * v_new[h]
        o_ref[h:h + 1, :] = _gated_rms(o, ng, z_ref[h:h + 1, :])


def _dn_step(q, k, v, z, gates, norm_g, s_all, layer):
    db = q.shape[0]
    qk_spec = pl.BlockSpec((None, DN_HK, DN_DK), lambda b: (b, 0, 0))
    vz_spec = pl.BlockSpec((None, DN_HV, DN_DV), lambda b: (b, 0, 0))
    s_spec = pl.BlockSpec((None, None, DN_HV, DN_DK, DN_DV), lambda b: (layer, b, 0, 0, 0))
    return pl.pallas_call(
        _dn_step_body,
        grid=(db,),
        in_specs=[qk_spec, qk_spec, vz_spec, vz_spec,
                  pl.BlockSpec((None, 1, LANES), lambda b: (b, 0, 0)),
                  pl.BlockSpec((1, DN_DV), lambda b: (0, 0)), s_spec],
        out_specs=[vz_spec, s_spec],
        out_shape=[jax.ShapeDtypeStruct((db, DN_HV, DN_DV), F32), jax.ShapeDtypeStruct(s_all.shape, F32)],
        input_output_aliases={6: 1},
        compiler_params=pltpu.CompilerParams(dimension_semantics=("parallel",), vmem_limit_bytes=VMEM_LIMIT),
        name="dn_step",
    )(q, k, v, z, gates, norm_g.reshape(1, DN_DV), s_all)


def _dn_layer(xp, xs, mod_p, mod_s, seq, conv_prev, s_all, layer, w_in, conv_w, a_log, dt_bias, norm_g, w_out,
              ln_g, ln_b, alpha):
    mp, d = xp.shape
    bp = mp // seq
    db = xs.shape[0]
    conv_dim = 2 * DN_HK * DN_DK + DN_HV * DN_DV
    v_w = DN_HV * DN_DV
    w_all = jnp.pad(w_in, ((0, 0), (0, LANES - 2 * DN_HV))).astype(BF16)
    w_o = w_out.astype(BF16)
    alog = jnp.pad(a_log, (0, LANES - DN_HV)).reshape(1, LANES)
    dtb = jnp.pad(dt_bias, (0, LANES - DN_HV)).reshape(1, LANES)
    lane_row = pl.BlockSpec((1, LANES), lambda i: (0, 0))
    segments = [conv_dim, v_w, (LANES, _dn_gates, [(alog, lane_row), (dtb, lane_row)])]

    qkv, z, gates = _in_proj(xp, mod_p, seq, w_all, segments, tm=PROJ_TM, name="dn_in_p")
    qkv3 = qkv.reshape(bp, seq, conv_dim)
    cv_p = qkv3[:, seq - (DN_CONV_W - 1):]
    qkvc = _dn_prep(qkv3, conv_w, tt=DN_PREP_ROWS, cb=512)
    o, s_p = _dn_chunks(qkvc, z.reshape(bp, seq, v_w), gates.reshape(bp, seq, LANES), norm_g)
    xp_new = _out_proj_norm(o.reshape(mp, v_w), w_o, xp, mod_p, seq, ln_g, ln_b, tm=PROJ_TM, alpha=alpha,
                            name="dn_out_p")

    qkv_s, z_s, gates_s = _in_proj(xs, mod_s, 1, w_all, segments, tm=db, name="dn_in_s")
    qkvc_s, cv_t = _dn_step_prep(qkv_s, jnp.swapaxes(conv_prev, 0, 1), conv_w, cb=512)
    k_w = DN_HK * DN_DK
    o_s, s_s = _dn_step(qkvc_s[:, :k_w].reshape(db, DN_HK, DN_DK),
                        qkvc_s[:, k_w:2 * k_w].reshape(db, DN_HK, DN_DK),
                        qkvc_s[:, 2 * k_w:].reshape(db, DN_HV, DN_DV),
                        z_s.reshape(db, DN_HV, DN_DV), gates_s.reshape(db, 1, LANES), norm_g, s_all, layer)
    xs_new = _out_proj_norm(o_s.reshape(db, v_w), w_o, xs, mod_s, 1, ln_g, ln_b, tm=db, alpha=alpha,
                            name="dn_out_s")
    return xp_new, xs_new, s_p, cv_p, s_s, jnp.swapaxes(cv_t, 0, 1)


def _diff_lambda(lq1_ref, lk1_ref, lq2_ref, lk2_ref, lam_init):
    e1 = jnp.exp(jnp.sum(lq1_ref[...] * lk1_ref[...], -1, keepdims=True))
    e2 = jnp.exp(jnp.sum(lq2_ref[...] * lk2_ref[...], -1, keepdims=True))
    return e1 - e2 + lam_init


def _diff_head_out(o, sg, z, lam_init):
    on = o * lax.rsqrt(jnp.mean(o * o, -1, keepdims=True) + RMS_EPS) * sg * (1.0 - lam_init)
    return on * _silu(z)


def _diff_flash_body(q_ref, k_ref, v_ref, z_ref, lq1_ref, lk1_ref, lq2_ref, lk2_ref, sg_ref, o_ref,
                     m_ref, l_ref, acc_ref, *, lam_init):
    i = pl.program_id(2)
    j = pl.program_id(3)
    tq, tk = q_ref.shape[0], k_ref.shape[0]
    gw = DIFF_G * tq

    @pl.when(j == 0)
    def _():
        m_ref[...] = jnp.full_like(m_ref, -jnp.inf)
        l_ref[...] = jnp.zeros_like(l_ref)
        acc_ref[...] = jnp.zeros_like(acc_ref)

    def update(masked):
        v = v_ref[...].astype(BF16)
        if masked:
            keep = (lax.broadcasted_iota(jnp.int32, (tk, gw), 0)
                    <= (lax.broadcasted_iota(jnp.int32, (tk, gw), 1) & (tq - 1)))
        sts = []
        for half in range(2):
            qm = jnp.concatenate([q_ref[:, (g * 2 + half) * DIFF_HD:(g * 2 + half + 1) * DIFF_HD]
                                  for g in range(DIFF_G)], axis=0) * (DIFF_HD ** -0.5 * LOG2E)
            st = lax.dot_general(k_ref[:, half * DIFF_HD:(half + 1) * DIFF_HD], qm, _NT,
                                 preferred_element_type=F32)
            sts.append(jnp.where(keep, st, -jnp.inf) if masked else st)
        for half in range(2):
            st = sts[half]
            cols = slice(half * gw, (half + 1) * gw)
            m_prev = m_ref[:, cols]
            m_new = jnp.maximum(m_prev, jnp.max(st, 0, keepdims=True))
            alpha = jnp.exp2(m_prev - m_new)
            p = jnp.exp2(st - m_new)
            l_ref[:, cols] = alpha * l_ref[:, cols] + jnp.sum(p, 0, keepdims=True)
            acc_ref[:, cols] = alpha * acc_ref[:, cols] + lax.dot_general(
                v, p.astype(BF16), _TN, preferred_element_type=F32)
            m_ref[:, cols] = m_new

    @pl.when(j < i)
    def _():
        update(False)

    @pl.when(j == i)
    def _():
        update(True)
        lam = _diff_lambda(lq1_ref, lk1_ref, lq2_ref, lk2_ref, lam_init)
        sg = sg_ref[...]
        for g in range(DIFF_G):
            c0 = slice(g * tq, (g + 1) * tq)
            c1 = slice(gw + g * tq, gw + (g + 1) * tq)
            ot = acc_ref[:, c0] * (1.0 / l_ref[:, c0]) - lam * (acc_ref[:, c1] * (1.0 / l_ref[:, c1]))
            o_ref[:, g * DIFF_VD:(g + 1) * DIFF_VD] = _diff_head_out(
                ot.T, sg, z_ref[:, g * DIFF_VD:(g + 1) * DIFF_VD], lam_init).astype(o_ref.dtype)


def _diff_flash(q, k, v, z, lams, subln_g, lam_init, *, tq):
    b, t, _ = q.shape
    assert t % tq == 0 and tq & (tq - 1) == 0, (t, tq)
    nq = t // tq
    qw = DIFF_G * 2 * DIFF_HD
    ow = DIFF_G * DIFF_VD
    n_sub = 2 * DIFF_G
    vec = pl.BlockSpec((1, DIFF_HD), lambda bi, h, i, j: (0, 0))
    return pl.pallas_call(
        functools.partial(_diff_flash_body, lam_init=lam_init),
        grid=(b, DIFF_KVH, nq, nq),
        in_specs=[pl.BlockSpec((None, tq, qw), lambda bi, h, i, j: (bi, i, h)),
                  pl.BlockSpec((None, tq, 2 * DIFF_HD), lambda bi, h, i, j: (bi, jnp.minimum(j, i), h)),
                  pl.BlockSpec((None, tq, DIFF_VD), lambda bi, h, i, j: (bi, jnp.minimum(j, i), h)),
                  pl.BlockSpec((None, tq, ow), lambda bi, h, i, j: (bi, i, h)),
                  vec, vec, vec, vec,
                  pl.BlockSpec((1, DIFF_VD), lambda bi, h, i, j: (0, 0))],
        out_specs=pl.BlockSpec((None, tq, ow), lambda bi, h, i, j: (bi, i, h)),
        out_shape=jax.ShapeDtypeStruct((b, t, DIFF_H * DIFF_VD), BF16),
        scratch_shapes=[pltpu.VMEM((1, n_sub * tq), F32), pltpu.VMEM((1, n_sub * tq), F32),
                        pltpu.VMEM((DIFF_VD, n_sub * tq), F32)],
        compiler_params=pltpu.CompilerParams(
            dimension_semantics=("parallel", "parallel", "parallel", "arbitrary"),
            vmem_limit_bytes=VMEM_LIMIT),
        name="diff_flash",
    )(q, k, v, z, *lams, subln_g.reshape(1, DIFF_VD))


def _diff_decode_body(pt_ref, q_ref, ks_ref, vs_ref, z_ref, lq1_ref, lk1_ref, lq2_ref, lk2_ref, sg_ref,
                      *rest, pp, lam_init):
    del pt_ref
    k_pages = rest[:pp]
    v_pages = rest[pp:2 * pp]
    o_ref = rest[2 * pp]
    qb_ref, m_ref, l_ref, acc_ref = rest[2 * pp + 1:]
    j = pl.program_id(1)
    n_rows = DIFF_KVH * DIFF_G * 2
    head_shift = (DIFF_G * 2).bit_length() - 1
    kw = 2 * DIFF_HD
    page_rows = k_pages[0].shape[0]

    @pl.when(j == 0)
    def _():
        d_i = lax.broadcasted_iota(jnp.int32, (DIFF_HD, kw), 0)
        c_i = lax.broadcasted_iota(jnp.int32, (DIFF_HD, kw), 1)
        spread = jnp.where((c_i & (DIFF_HD - 1)) == d_i, 1.0, 0.0)
        qt = jnp.dot(q_ref[...], spread, preferred_element_type=F32)
        r_i = lax.broadcasted_iota(jnp.int32, (n_rows, kw), 0)
        half = lax.broadcasted_iota(jnp.int32, (n_rows, kw), 1) >> (DIFF_HD.bit_length() - 1)
        qb_ref[...] = jnp.where(half == (r_i & 1), qt, 0.0) * (DIFF_HD ** -0.5)
        m_ref[...] = jnp.full_like(m_ref, -jnp.inf)
        l_ref[...] = jnp.zeros_like(l_ref)
        acc_ref[...] = jnp.zeros_like(acc_ref)

    qb = qb_ref[...]
    s = jnp.concatenate([lax.dot_general(qb, kp[...], _NT, preferred_element_type=F32) for kp in k_pages],
                        axis=1)
    row_head = lax.broadcasted_iota(jnp.int32, s.shape, 0) >> head_shift
    col_head = lax.broadcasted_iota(jnp.int32, s.shape, 1) & (DIFF_KVH - 1)
    s = jnp.where(row_head == col_head, s, -jnp.inf)
    m_prev = m_ref[...]
    m_new = jnp.maximum(m_prev, jnp.max(s, -1, keepdims=True))
    alpha = jnp.exp(m_prev - m_new)
    p = jnp.exp(s - m_new)
    l_new = alpha * l_ref[...] + jnp.sum(p, -1, keepdims=True)
    pv = jnp.dot(p[:, 0:page_rows], v_pages[0][...], preferred_element_type=F32)
    for t in range(1, pp):
        pv = pv + jnp.dot(p[:, t * page_rows:(t + 1) * page_rows], v_pages[t][...],
                          preferred_element_type=F32)
    acc_new = alpha * acc_ref[...] + pv
    m_ref[...] = m_new
    l_ref[...] = l_new
    acc_ref[...] = acc_new

    @pl.when(j == pl.num_programs(1) - 1)
    def _():
        rh = lax.broadcasted_iota(jnp.int32, (n_rows, kw), 0) >> head_shift
        ks = jnp.zeros((n_rows, kw), F32)
        vs = jnp.zeros((n_rows, kw), F32)
        for h in range(DIFF_KVH):
            ks = jnp.where(rh == h, ks_ref[h:h + 1, :], ks)
            vs = jnp.where(rh == h, vs_ref[h:h + 1, :], vs)
        s_self = jnp.sum(qb * ks, -1, keepdims=True)
        m_fin = jnp.maximum(m_new, s_self)
        a = jnp.exp(m_new - m_fin)
        p_self = jnp.exp(s_self - m_fin)
        l_fin = a * l_new + p_self
        acc = (a * acc_new + p_self * vs) / l_fin
        lam = _diff_lambda(lq1_ref, lk1_ref, lq2_ref, lk2_ref, lam_init)
        sg = sg_ref[...]
        for hh in range(DIFF_H):
            o = acc[2 * hh:2 * hh + 1] - lam * acc[2 * hh + 1:2 * hh + 2]
            o_ref[hh:hh + 1, :] = _diff_head_out(o, sg, z_ref[hh:hh + 1, :], lam_init)


def _diff_decode(q, k_self, v_self, z, cache_k, cache_v, layer, page_table, lams, subln_g, lam_init, *, pp):
    db = q.shape[0]
    n_pages = page_table.shape[1]
    assert n_pages % pp == 0
    page_rows, kw = cache_k.shape[2:]
    n_rows = DIFF_KVH * DIFF_G * 2

    def page_spec(t):
        return pl.BlockSpec((None, None, page_rows, kw), lambda b, j, pt: (layer, pt[b, j * pp + t], 0, 0))

    def seq_spec(shape):
        return pl.BlockSpec((None,) + shape, lambda b, j, pt: (b, 0, 0))

    vec = pl.BlockSpec((1, DIFF_HD), lambda b, j, pt: (0, 0))
    grid_spec = pltpu.PrefetchScalarGridSpec(
        num_scalar_prefetch=1,
        grid=(db, n_pages // pp),
        in_specs=[seq_spec((n_rows, DIFF_HD)), seq_spec((DIFF_KVH, kw)), seq_spec((DIFF_KVH, kw)),
                  seq_spec((DIFF_H, DIFF_VD)), vec, vec, vec, vec,
                  pl.BlockSpec((1, DIFF_VD), lambda b, j, pt: (0, 0))]
                 + [page_spec(t) for t in range(pp)] + [page_spec(t) for t in range(pp)],
        out_specs=seq_spec((DIFF_H, DIFF_VD)),
        scratch_shapes=[pltpu.VMEM((n_rows, kw), F32), pltpu.VMEM((n_rows, 1), F32),
                        pltpu.VMEM((n_rows, 1), F32), pltpu.VMEM((n_rows, DIFF_VD), F32)],
    )
    return pl.pallas_call(
        functools.partial(_diff_decode_body, pp=pp, lam_init=lam_init),
        grid_spec=grid_spec,
        out_shape=jax.ShapeDtypeStruct((db, DIFF_H, DIFF_VD), F32),
        compiler_params=pltpu.CompilerParams(dimension_semantics=("parallel", "arbitrary"),
                                             vmem_limit_bytes=VMEM_LIMIT),
        name="diff_decode",
    )(page_table, q, k_self, v_self, z, *lams, subln_g.reshape(1, DIFF_VD),
      *([cache_k] * pp), *([cache_v] * pp))


def _diff_layer(xp, xs, mod_p, mod_s, seq, layer_idx, j, cache_k, cache_v, page_table, w_in, lams, subln_g,
                w_out, ln_g, ln_b, alpha):
    mp, d = xp.shape
    bp = mp // seq
    db = xs.shape[0]
    lam_init = 0.8 - 0.6 * math.exp(-0.3 * layer_idx)
    q_w = DIFF_H * 2 * DIFF_HD
    k_w = DIFF_KVH * 2 * DIFF_HD
    v_w = DIFF_KVH * DIFF_VD
    w_all = w_in.astype(BF16)
    w_o = w_out.astype(BF16)
    lams = [a.reshape(1, DIFF_HD) for a in lams]
    segments = [q_w, k_w, v_w, w_in.shape[1] - q_w - k_w - v_w]

    q, k, v, z = _in_proj(xp, mod_p, seq, w_all, segments, tm=PROJ_TM, name="diff_in_p")
    o = _diff_flash(q.reshape(bp, seq, q_w), k.reshape(bp, seq, k_w), v.reshape(bp, seq, v_w),
                    z.reshape(bp, seq, -1), lams, subln_g, lam_init, tq=min(DIFF_TQ, seq))
    xp_new = _out_proj_norm(o.reshape(mp, -1), w_o, xp, mod_p, seq, ln_g, ln_b, tm=PROJ_TM, alpha=alpha,
                            name="diff_out_p")

    q_s, k_s, v_s, z_s = _in_proj(xs, mod_s, 1, w_all, segments, tm=db, name="diff_in_s")
    n_l, n_pool, page = cache_k.shape[:3]
    o_s = _diff_decode(q_s.reshape(db, DIFF_KVH * DIFF_G * 2, DIFF_HD), k_s.reshape(db, DIFF_KVH, 2 * DIFF_HD),
                       v_s.reshape(db, DIFF_KVH, DIFF_VD), z_s.reshape(db, DIFF_H, DIFF_VD),
                       cache_k.reshape(n_l, n_pool, page * DIFF_KVH, 2 * DIFF_HD),
                       cache_v.reshape(n_l, n_pool, page * DIFF_KVH, DIFF_VD),
                       j, page_table, lams, subln_g, lam_init, pp=DIFF_PAGES_PER_STEP)
    xs_new = _out_proj_norm(o_s.reshape(db, -1), w_o, xs, mod_s, 1, ln_g, ln_b, tm=db, alpha=alpha,
                            name="diff_out_s")
    return (xp_new, xs_new, k.reshape(bp, seq, DIFF_KVH, 2 * DIFF_HD), v.reshape(bp, seq, DIFF_KVH, DIFF_VD),
            k_s.reshape(db, 1, DIFF_KVH, 2 * DIFF_HD), v_s.reshape(db, 1, DIFF_KVH, DIFF_VD))


def _swa_band_body(q_ref, kc_ref, kp_ref, vc_ref, vp_ref, z_ref, sink_ref, o_ref):
    i = pl.program_id(1)
    blk = q_ref.shape[0]
    cols = SWA_G * blk
    kj = lax.broadcasted_iota(jnp.int32, (2 * blk, cols), 0)
    qi = lax.broadcasted_iota(jnp.int32, (2 * blk, cols), 1) & (blk - 1)
    dist = qi + blk - kj
    keep = (dist >= 0) & (dist <= WINDOW) & ((i > 0) | (kj >= blk))
    for h in range(SWA_KVH):
        hs = slice(h * SWA_HD, (h + 1) * SWA_HD)
        k2 = jnp.concatenate([kp_ref[:, hs], kc_ref[:, hs]], axis=0)
        v2 = jnp.concatenate([vp_ref[:, hs], vc_ref[:, hs]], axis=0)
        qs = jnp.concatenate([q_ref[:, (h * SWA_G + g) * SWA_HD:(h * SWA_G + g + 1) * SWA_HD]
                              for g in range(SWA_G)], axis=0) * (SWA_HD ** -0.5)
        sink = jnp.concatenate([jnp.broadcast_to(sink_ref[:, h * SWA_G + g:h * SWA_G + g + 1], (1, blk))
                                for g in range(SWA_G)], axis=1)
        st = lax.dot_general(k2, qs, _NT, preferred_element_type=F32)
        st = jnp.where(keep, st, -jnp.inf)
        m = jnp.maximum(jnp.max(st, 0, keepdims=True), sink)
        p = jnp.exp(st - m)
        den = jnp.sum(p, 0, keepdims=True) + jnp.exp(sink - m)
        ot = lax.dot_general(v2, p, _TN, preferred_element_type=F32) * (1.0 / den)
        for g in range(0, SWA_G, 2):
            cs = slice((h * SWA_G + g) * SWA_HD, (h * SWA_G + g + 2) * SWA_HD)
            pair = jnp.concatenate([ot[:, g * blk:(g + 1) * blk], ot[:, (g + 1) * blk:(g + 2) * blk]], axis=0)
            o_ref[:, cs] = (pair.T * _silu(z_ref[:, cs])).astype(o_ref.dtype)


def _swa_band(q, k, v, z, sinks):
    b, t, qw = q.shape
    blk = WINDOW
    kw = SWA_KVH * SWA_HD
    cur = pl.BlockSpec((None, blk, kw), lambda bi, i: (bi, i, 0))
    prev = pl.BlockSpec((None, blk, kw), lambda bi, i: (bi, jnp.maximum(i - 1, 0), 0))
    wide = pl.BlockSpec((None, blk, qw), lambda bi, i: (bi, i, 0))
    return pl.pallas_call(
        _swa_band_body,
        grid=(b, t // blk),
        in_specs=[wide, cur, prev, cur, prev, wide, pl.BlockSpec((1, SWA_H), lambda bi, i: (0, 0))],
        out_specs=wide,
        out_shape=jax.ShapeDtypeStruct((b, t, qw), BF16),
        compiler_params=pltpu.CompilerParams(dimension_semantics=("parallel", "parallel"),
                                             vmem_limit_bytes=VMEM_LIMIT),
        name="swa_band",
    )(q, k, k, v, v, z, sinks.reshape(1, SWA_H))


def _swa_decode_body(q_ref, kb_ref, vb_ref, kn_ref, vn_ref, z_ref, sink_ref, o_ref, ko_ref, vo_ref):
    win = kb_ref.shape[1]
    for h in range(SWA_KVH):
        hs = slice(h * SWA_HD, (h + 1) * SWA_HD)
        gs = slice(h * SWA_G, (h + 1) * SWA_G)
        qh = q_ref[:, gs, :]
        s = jnp.einsum("bqd,bkd->bqk", qh, kb_ref[:, :, hs], preferred_element_type=F32) * (SWA_HD ** -0.5)
        s_new = jnp.sum(qh * kn_ref[:, :, hs], -1, keepdims=True) * (SWA_HD ** -0.5)
        sink = sink_ref[gs, :][None]
        m = jnp.maximum(jnp.maximum(jnp.max(s, -1, keepdims=True), s_new), sink)
        p = jnp.exp(s - m)
        p_new = jnp.exp(s_new - m)
        den = jnp.sum(p, -1, keepdims=True) + p_new + jnp.exp(sink - m)
        o = jnp.einsum("bqk,bkd->bqd", p, vb_ref[:, :, hs], preferred_element_type=F32)
        o = (o + p_new * vn_ref[:, :, hs]) / den
        o_ref[:, gs, :] = o * _silu(z_ref[:, gs, :])
    ko_ref[:, 0:win - 1, :] = kb_ref[:, 1:win, :]
    ko_ref[:, win - 1:win, :] = kn_ref[...]
    vo_ref[:, 0:win - 1, :] = vb_ref[:, 1:win, :]
    vo_ref[:, win - 1:win, :] = vn_ref[...]


def _swa_decode(q, k_new, v_new, z, buf_k, buf_v, sinks, *, bb):
    db, win, kw = buf_k.shape
    qz = pl.BlockSpec((bb, SWA_H, SWA_HD), lambda b: (b, 0, 0))
    buf = pl.BlockSpec((bb, win, kw), lambda b: (b, 0, 0))
    new = pl.BlockSpec((bb, 1, kw), lambda b: (b, 0, 0))
    return pl.pallas_call(
        _swa_decode_body,
        grid=(db // bb,),
        in_specs=[qz, buf, buf, new, new, qz, pl.BlockSpec((SWA_H, 1), lambda b: (0, 0))],
        out_specs=[qz, buf, buf],
        out_shape=[jax.ShapeDtypeStruct(q.shape, F32), jax.ShapeDtypeStruct(buf_k.shape, F32),
                   jax.ShapeDtypeStruct(buf_v.shape, F32)],
        compiler_params=pltpu.CompilerParams(dimension_semantics=("parallel",), vmem_limit_bytes=VMEM_LIMIT),
        name="swa_decode",
    )(q, buf_k, buf_v, k_new, v_new, z, sinks.reshape(SWA_H, 1))


def _swa_layer(xp, xs, mod_p, mod_s, seq, buf_k, buf_v, sinks, w_in, w_out, ln_g, ln_b, alpha):
    mp, d = xp.shape
    bp = mp // seq
    db = xs.shape[0]
    q_w = SWA_H * SWA_HD
    kv_w = SWA_KVH * SWA_HD
    w_all = w_in.astype(BF16)
    w_o = w_out.astype(BF16)
    segments = [q_w, kv_w, kv_w, w_in.shape[1] - q_w - 2 * kv_w]

    q, k, v, z = _in_proj(xp, mod_p, seq, w_all, segments, tm=PROJ_TM, name="swa_in_p")
    k = k.reshape(bp, seq, kv_w)
    v = v.reshape(bp, seq, kv_w)
    o = _swa_band(q.reshape(bp, seq, q_w), k, v, z.reshape(bp, seq, q_w), sinks)
    xp_new = _out_proj_norm(o.reshape(mp, q_w), w_o, xp, mod_p, seq, ln_g, ln_b, tm=PROJ_TM, alpha=alpha,
                            name="swa_out_p")

    q_s, k_s, v_s, z_s = _in_proj(xs, mod_s, 1, w_all, segments, tm=db, name="swa_in_s")
    win = buf_k.shape[1]
    o_s, k_buf, v_buf = _swa_decode(q_s.reshape(db, SWA_H, SWA_HD), k_s.reshape(db, 1, kv_w),
                                    v_s.reshape(db, 1, kv_w), z_s.reshape(db, SWA_H, SWA_HD),
                                    buf_k.reshape(db, win, kv_w), buf_v.reshape(db, win, kv_w), sinks,
                                    bb=SWA_DEC_BB)
    xs_new = _out_proj_norm(o_s.reshape(db, q_w), w_o, xs, mod_s, 1, ln_g, ln_b, tm=db, alpha=alpha,
                            name="swa_out_s")
    kv_shape = (SWA_KVH, SWA_HD)
    return (xp_new, xs_new, k[:, seq - WINDOW:].reshape((bp, WINDOW) + kv_shape),
            v[:, seq - WINDOW:].reshape((bp, WINDOW) + kv_shape),
            k_buf.reshape((db, win) + kv_shape), v_buf.reshape((db, win) + kv_shape))


def kernel(x_prompt, x_sample, cache_diff_k, cache_diff_v, page_table, cache_swa_k, cache_swa_v, state_dn_S, state_dn_conv, c_prompt, c_sample, ada_w, ada_b, ln_g, ln_b, dn_w_in, dn_conv_w, dn_a_log, dn_dt_bias, dn_norm_g, dn_w_out, diff_w_in, diff_lq1, diff_lk1, diff_lq2, diff_lk2, diff_subln_g, diff_w_out, swa_w_in, swa_sinks, swa_w_out):
    bp, seq, d = x_prompt.shape
    db, dec_seq, _ = x_sample.shape
    assert dec_seq == 1 and bp <= SUBLANES
    depth = ada_w.shape[0]
    alpha = (2 * depth) ** 0.25
    xp = x_prompt.reshape(bp * seq, d)
    xs = x_sample.reshape(db, d)

    c_all = jnp.concatenate([c_prompt, jnp.zeros((SUBLANES - bp, d), F32), c_sample], axis=0)
    st = {n: [] for n in ("dk_p", "dv_p", "dk_s", "dv_s", "sk_p", "sv_p", "sk_s", "sv_s",
                          "S_p", "cv_p", "cv_s")}
    s_s = state_dn_S
    for l in range(depth):
        j = l // N_MIXERS
        bias_op = (ada_b[l].reshape(1, 3 * d), pl.BlockSpec((1, 3 * d), lambda i: (0, 0)))
        mod, = _proj(c_all, ada_w[l].astype(BF16), [(3 * d, _add_bias, [bias_op])], tm=c_all.shape[0],
                     pro=_silu, name="adaln")
        mod_p = mod[:bp].reshape(bp, 1, 3 * d)
        mod_s = mod[SUBLANES:]
        if l % N_MIXERS == 0:
            xp, xs, s_p, cv_p, s_s, cv_s = _dn_layer(
                xp, xs, mod_p, mod_s, seq, state_dn_conv[j], s_s, j, dn_w_in[j], dn_conv_w[j],
                dn_a_log[j], dn_dt_bias[j], dn_norm_g[j], dn_w_out[j], ln_g[l], ln_b[l], alpha)
            st["S_p"].append(s_p)
            st["cv_p"].append(cv_p)
            st["cv_s"].append(cv_s)
        elif l % N_MIXERS == 1:
            xp, xs, k_p, v_p, k_s, v_s = _diff_layer(
                xp, xs, mod_p, mod_s, seq, l, j, cache_diff_k, cache_diff_v, page_table, diff_w_in[j],
                (diff_lq1[j], diff_lk1[j], diff_lq2[j], diff_lk2[j]), diff_subln_g[j], diff_w_out[j],
                ln_g[l], ln_b[l], alpha)
            st["dk_p"].append(k_p)
            st["dv_p"].append(v_p)
            st["dk_s"].append(k_s)
            st["dv_s"].append(v_s)
        else:
            xp, xs, k_p, v_p, k_s, v_s = _swa_layer(
                xp, xs, mod_p, mod_s, seq, cache_swa_k[j], cache_swa_v[j], swa_sinks[j], swa_w_in[j],
                swa_w_out[j], ln_g[l], ln_b[l], alpha)
            st["sk_p"].append(k_p)
            st["sv_p"].append(v_p)
            st["sk_s"].append(k_s)
            st["sv_s"].append(v_s)
    return (xp.reshape(bp, seq, d), xs.reshape(db, 1, d),
            jnp.stack(st["dk_p"]), jnp.stack(st["dv_p"]), jnp.stack(st["dk_s"]), jnp.stack(st["dv_s"]),
            jnp.stack(st["sk_p"]), jnp.stack(st["sv_p"]), jnp.stack(st["sk_s"]), jnp.stack(st["sv_s"]),
            jnp.stack(st["S_p"]), jnp.stack(st["cv_p"]), s_s, jnp.stack(st["cv_s"]))
```

```python
import functools
import math

import jax
import jax.numpy as jnp
from jax import lax
from jax.experimental import pallas as pl
from jax.experimental.pallas import tpu as pltpu

F32 = jnp.float32
BF16 = jnp.bfloat16

LN_EPS = 1e-5
RMS_EPS = 1e-6
N_MIXERS = 3

DN_HK = 8
DN_HV = 16
DN_DK = 128
DN_DV = 128
DN_REP = DN_HV // DN_HK
DN_CONV_W = 4
DN_CHUNK = 128
DN_PREP_ROWS = 256
PROJ_TM = 512

DIFF_H = 8
DIFF_KVH = 4
DIFF_G = DIFF_H // DIFF_KVH
DIFF_HD = 64
DIFF_VD = 2 * DIFF_HD
DIFF_TQ = 512
LOG2E = math.log2(math.e)
DIFF_PAGES_PER_STEP = 16

SWA_H = 16
SWA_KVH = 2
SWA_G = SWA_H // SWA_KVH
SWA_HD = 64
WINDOW = 128
SWA_DEC_BB = 8

LANES = 128
SUBLANES = 8
VMEM_LIMIT = 48 * 1024 * 1024

_NT = (((1,), (1,)), ((), ()))
_TN = (((0,), (0,)), ((), ()))


def _silu(x):
    return (0.5 * x) * (1.0 + jnp.tanh(0.5 * x))


def _proj_body(x_ref, w_ref, *refs, n_pro, segs, pro):
    pro_refs = refs[:n_pro]
    n_epi = sum(s[3] for s in segs)
    epi_refs = refs[n_pro:n_pro + n_epi]
    o_refs = refs[n_pro + n_epi:]
    x = x_ref[...]
    if pro is not None:
        x = pro(x, *pro_refs)
    xb = x.astype(BF16)
    e = 0
    for (start, width, epi, n_e), o_ref in zip(segs, o_refs):
        acc = jnp.dot(xb, w_ref[:, start:start + width], preferred_element_type=F32)
        if epi is not None:
            acc = epi(acc, *epi_refs[e:e + n_e])
        e += n_e
        o_ref[...] = acc


def _proj(x, w, segments, *, tm, pro=None, pro_ops=(), name="proj"):
    m, k = x.shape
    n = w.shape[1]
    assert m % tm == 0, (m, tm)
    ops = [x, w]
    in_specs = [pl.BlockSpec((tm, k), lambda i: (i, 0)),
                pl.BlockSpec((k, n), lambda i: (0, 0), pipeline_mode=pl.Buffered(1))]
    for a, s in pro_ops:
        ops.append(a)
        in_specs.append(s)
    segs = []
    start = 0
    for width, epi, epi_ops in segments:
        segs.append((start, width, epi, len(epi_ops)))
        start += width
        for a, s in epi_ops:
            ops.append(a)
            in_specs.append(s)
    assert start <= n, (start, n)
    return pl.pallas_call(
        functools.partial(_proj_body, n_pro=len(pro_ops), segs=tuple(segs), pro=pro),
        grid=(m // tm,),
        in_specs=in_specs,
        out_specs=[pl.BlockSpec((tm, s[1]), lambda i: (i, 0)) for s in segs],
        out_shape=[jax.ShapeDtypeStruct((m, s[1]), F32) for s in segs],
        compiler_params=pltpu.CompilerParams(dimension_semantics=("parallel",), vmem_limit_bytes=VMEM_LIMIT),
        name=name,
    )(*ops)


def _modulate(x, mod_ref):
    d = x.shape[-1]
    return x * (1.0 + mod_ref[:, d:2 * d]) + mod_ref[:, 0:d]


def _deepnorm(y, xres_ref, mod_ref, g_ref, b_ref, *, alpha):
    d = y.shape[-1]
    h = alpha * xres_ref[...] + (1.0 + mod_ref[:, 2 * d:3 * d]) * y
    hc = h - jnp.mean(h, -1, keepdims=True)
    var = jnp.mean(hc * hc, -1, keepdims=True)
    return hc * lax.rsqrt(var + LN_EPS) * g_ref[...] + b_ref[...]


def _add_bias(y, b_ref):
    return y + b_ref[...]


def _dn_gates(y, alog_ref, dtb_ref):
    lane = lax.broadcasted_iota(jnp.int32, y.shape, 1)
    t = y + dtb_ref[...]
    softplus = jnp.maximum(t, 0.0) + jnp.log1p(jnp.exp(-jnp.abs(t)))
    g = -jnp.exp(alog_ref[...]) * softplus
    return jnp.where(lane < DN_HV, g, jax.nn.sigmoid(y))


def _mod_spec(mod, tm, rows_per_mod):
    w = mod.shape[-1]
    if mod.ndim == 3:
        assert rows_per_mod % tm == 0, (rows_per_mod, tm)
        return pl.BlockSpec((None, 1, w), lambda i: (i * tm // rows_per_mod, 0, 0))
    return pl.BlockSpec((tm, w), lambda i: (i, 0))


def _in_proj(x, mod, rows_per_mod, w, segments, *, tm, name="in_proj"):
    segments = [(s, None, ()) if isinstance(s, int) else s for s in segments]
    return _proj(x, w, segments, tm=tm, pro=_modulate, pro_ops=[(mod, _mod_spec(mod, tm, rows_per_mod))],
                 name=name)


def _out_proj_norm(y, w, xres, mod, rows_per_mod, ln_g, ln_b, *, tm, alpha, name="out_proj"):
    d = w.shape[1]
    row = pl.BlockSpec((1, d), lambda i: (0, 0))
    epi_ops = [(xres, pl.BlockSpec((tm, d), lambda i: (i, 0))), (mod, _mod_spec(mod, tm, rows_per_mod)),
               (ln_g.reshape(1, d), row), (ln_b.reshape(1, d), row)]
    return _proj(y, w, [(d, functools.partial(_deepnorm, alpha=alpha), epi_ops)], tm=tm, name=name)[0]


def _l2norm_heads(y, o_ref, scale):
    for h in range(y.shape[-1] // DN_DK):
        yh = y[:, h * DN_DK:(h + 1) * DN_DK]
        inv = lax.rsqrt(jnp.sum(yh * yh, -1, keepdims=True) + RMS_EPS)
        o_ref[:, h * DN_DK:(h + 1) * DN_DK] = yh * inv * scale


def _dn_prep_body(x_ref, prev_ref, w_ref, o_ref, *, cb):
    i = pl.program_id(1)
    tt = x_ref.shape[0]
    backs = range(1, DN_CONV_W)
    row = lax.broadcasted_iota(jnp.int32, (SUBLANES, cb), 0)

    def rolled(v):
        return tuple(pltpu.roll(v, d, axis=0) for d in backs)

    for c0 in range(0, x_ref.shape[1], cb):
        cols = slice(c0, c0 + cb)
        taps = [w_ref[s:s + 1, cols] for s in range(DN_CONV_W)]

        def body(si, prev_rolled, cols=cols, taps=taps):
            r0 = pl.multiple_of(si * SUBLANES, SUBLANES)
            cur = x_ref[pl.ds(r0, SUBLANES), cols]
            cur_rolled = rolled(cur)
            back = [jnp.where(row < d, p, q) for d, p, q in zip(backs, prev_rolled, cur_rolled)]
            conv = back[DN_CONV_W - 2] * taps[0]
            for s in range(1, DN_CONV_W - 1):
                conv = conv + back[DN_CONV_W - 2 - s] * taps[s]
            o_ref[pl.ds(r0, SUBLANES), cols] = _silu(conv + cur * taps[DN_CONV_W - 1])
            return cur_rolled

        first = jnp.where(i > 0, prev_ref[:, cols], 0.0)
        lax.fori_loop(0, tt // SUBLANES, body, rolled(first), unroll=8)


def _dn_prep(qkv, conv_w, *, tt, cb):
    b, t, c = qkv.shape
    return pl.pallas_call(
        functools.partial(_dn_prep_body, cb=cb),
        grid=(b, t // tt),
        in_specs=[pl.BlockSpec((None, tt, c), lambda bi, i: (bi, i, 0)),
                  pl.BlockSpec((None, SUBLANES, c),
                               lambda bi, i: (bi, jnp.maximum(i * (tt // SUBLANES) - 1, 0), 0)),
                  pl.BlockSpec((DN_CONV_W, c), lambda bi, i: (0, 0))],
        out_specs=pl.BlockSpec((None, tt, c), lambda bi, i: (bi, i, 0)),
        out_shape=jax.ShapeDtypeStruct(qkv.shape, F32),
        compiler_params=pltpu.CompilerParams(dimension_semantics=("parallel", "parallel"),
                                             vmem_limit_bytes=VMEM_LIMIT),
        name="dn_prep",
    )(qkv, qkv, conv_w)


def _gated_rms(o, ng, z):
    return o * lax.rsqrt(jnp.mean(o * o, -1, keepdims=True) + RMS_EPS) * ng * _silu(z)


def _dn_chunk_body(x_ref, z_ref, gb_ref, ng_ref, o_ref, s_ref):
    ci = pl.program_id(1)
    c = x_ref.shape[0]
    hb = DN_HV

    @pl.when(ci == 0)
    def _():
        s_ref[...] = jnp.zeros_like(s_ref)

    k_w = DN_HK * DN_DK
    gb = gb_ref[...]
    row = lax.broadcasted_iota(jnp.int32, gb.shape, 0)
    gc = gb
    sh = 1
    while sh < c:
        gc = gc + jnp.where(row >= sh, pltpu.roll(gc, sh, axis=0), 0.0)
        sh *= 2
    gct = gc.T
    ri = lax.broadcasted_iota(jnp.int32, (c, c), 0)
    cj = lax.broadcasted_iota(jnp.int32, (c, c), 1)
    tril = ri >= cj
    strict = ri > cj
    merge_masks = []
    k = 0
    while (1 << k) < c:
        merge_masks.append(((ri >> (k + 1)) == (cj >> (k + 1))) & ((ri >> k) != (cj >> k)))
        k += 1
    ng = ng_ref[...]
    heads = range(hb)
    dot = functools.partial(jnp.dot, preferred_element_type=F32)
    def l2n(y, scale):
        return y * lax.rsqrt(jnp.sum(y * y, -1, keepdims=True) + RMS_EPS) * scale

    qh = [l2n(x_ref[:, i * DN_DK:(i + 1) * DN_DK], DN_DK ** -0.5) for i in range(DN_HK)]
    kh = [l2n(x_ref[:, k_w + i * DN_DK:k_w + (i + 1) * DN_DK], 1.0) for i in range(DN_HK)]
    kk = [lax.dot_general(a, a, _NT, preferred_element_type=F32) for a in kh]
    qk = [lax.dot_general(a, b, _NT, preferred_element_type=F32) for a, b in zip(qh, kh)]
    gcol = [gc[:, j:j + 1] for j in heads]
    bcol = [gb[:, hb + j:hb + j + 1] for j in heads]
    decay = [jnp.where(tril, jnp.exp(jnp.where(tril, gcol[j] - gct[j:j + 1, :], 0.0)), 0.0) for j in heads]
    lmat = [jnp.where(strict, bcol[j] * kk[j // DN_REP] * decay[j], 0.0) for j in heads]
    def odd_rows(a, blk):
        return jnp.concatenate([a[r:r + blk] for r in range(blk, c, 2 * blk)], axis=0)

    def spread_odd(a, blk):
        zero = jnp.zeros((blk, a.shape[1]), a.dtype)
        return jnp.concatenate([piece for t in range(c // (2 * blk))
                                for piece in (zero, a[t * blk:(t + 1) * blk])], axis=0)

    minv = [jnp.where(merge_masks[0], -lmat[j], 0.0) for j in heads]
    for lvl in range(1, len(merge_masks)):
        blk = 1 << lvl
        if blk < SUBLANES:
            loff = [jnp.where(merge_masks[lvl], lmat[j], 0.0) for j in heads]
            y = [loff[j] + dot(loff[j], minv[j]) for j in heads]
            minv = [minv[j] - y[j] - dot(minv[j], y[j]) for j in heads]
        else:
            rg = lax.broadcasted_iota(jnp.int32, (c // 2, c), 0)
            ro = ((rg >> lvl) << (lvl + 1)) + blk + (rg & (blk - 1))
            co = lax.broadcasted_iota(jnp.int32, (c // 2, c), 1)
            mask = ((ro >> (lvl + 1)) == (co >> (lvl + 1))) & ((ro >> lvl) != (co >> lvl))
            loff = [jnp.where(mask, odd_rows(lmat[j], blk), 0.0) for j in heads]
            y = [loff[j] + dot(loff[j], minv[j]) for j in heads]
            upd = [y[j] + dot(odd_rows(minv[j], blk), spread_odd(y[j], blk)) for j in heads]
            minv = [minv[j] - spread_odd(upd[j], blk) for j in heads]
    eg = [jnp.exp(gcol[j]) for j in heads]
    rhs = [jnp.concatenate([x_ref[:, 2 * k_w + j * DN_DV:2 * k_w + (j + 1) * DN_DV] * bcol[j],
                            kh[j // DN_REP] * (bcol[j] * eg[j])], axis=1) for j in heads]
    sol = [rhs[j] + dot(minv[j], rhs[j]) for j in heads]
    s_old = [s_ref[j] for j in heads]
    ws = [dot(jnp.concatenate([sol[j][:, DN_DV:], qh[j // DN_REP] * eg[j]], axis=0), s_old[j]) for j in heads]
    v_new = [sol[j][:, :DN_DV] - ws[j][:c] for j in heads]
    o = [ws[j][c:] + dot(qk[j // DN_REP] * decay[j], v_new[j]) for j in heads]
    for j in heads:
        glast = gc[c - 1:c, j:j + 1]
        kd = kh[j // DN_REP] * jnp.exp(glast - gcol[j])
        s_ref[j] = s_old[j] * jnp.exp(glast) + lax.dot_general(kd, v_new[j], _TN, preferred_element_type=F32)
        o_ref[:, j * DN_DV:(j + 1) * DN_DV] = _gated_rms(
            o[j], ng, z_ref[:, j * DN_DV:(j + 1) * DN_DV]).astype(o_ref.dtype)


def _dn_chunks(qkv, z, gates, norm_g):
    b, t, cd = qkv.shape
    c = DN_CHUNK
    assert t % c == 0
    v_w = DN_HV * DN_DV
    o, s = pl.pallas_call(
        _dn_chunk_body,
        grid=(b, t // c),
        in_specs=[pl.BlockSpec((None, c, cd), lambda bi, ci: (bi, ci, 0)),
                  pl.BlockSpec((None, c, v_w), lambda bi, ci: (bi, ci, 0)),
                  pl.BlockSpec((None, c, LANES), lambda bi, ci: (bi, ci, 0)),
                  pl.BlockSpec((1, DN_DV), lambda bi, ci: (0, 0))],
        out_specs=[pl.BlockSpec((None, c, v_w), lambda bi, ci: (bi, ci, 0)),
                   pl.BlockSpec((None, DN_HV, DN_DK, DN_DV), lambda bi, ci: (bi, 0, 0, 0))],
        out_shape=[jax.ShapeDtypeStruct((b, t, v_w), BF16),
                   jax.ShapeDtypeStruct((b, DN_HV, DN_DK, DN_DV), F32)],
        compiler_params=pltpu.CompilerParams(dimension_semantics=("parallel", "arbitrary"),
                                             vmem_limit_bytes=VMEM_LIMIT),
        name="dn_chunks",
    )(qkv, z, gates, norm_g.reshape(1, DN_DV))
    return o, s


def _dn_step_prep_body(x_ref, prev_ref, w_ref, o_ref, cv_ref, *, n_q_blk, n_k_blk):
    c = pl.program_id(0)
    x = x_ref[...]
    w = w_ref[...]
    conv = prev_ref[0] * w[0:1]
    for s in range(1, DN_CONV_W - 1):
        conv = conv + prev_ref[s] * w[s:s + 1]
    conv = conv + x * w[DN_CONV_W - 1:DN_CONV_W]
    y = _silu(conv)
    for s in range(DN_CONV_W - 2):
        cv_ref[s] = prev_ref[s + 1]
    cv_ref[DN_CONV_W - 2] = x

    @pl.when(c >= n_q_blk + n_k_blk)
    def _():
        o_ref[...] = y

    @pl.when(c < n_q_blk + n_k_blk)
    def _():
        _l2norm_heads(y, o_ref, jnp.where(c < n_q_blk, DN_DK ** -0.5, 1.0))


def _dn_step_prep(qkv, conv_prev_t, conv_w, *, cb):
    db, c = qkv.shape
    k_w = DN_HK * DN_DK
    nw = DN_CONV_W - 1
    return pl.pallas_call(
        functools.partial(_dn_step_prep_body, n_q_blk=k_w // cb, n_k_blk=k_w // cb),
        grid=(c // cb,),
        in_specs=[pl.BlockSpec((db, cb), lambda ci: (0, ci)),
                  pl.BlockSpec((nw, db, cb), lambda ci: (0, 0, ci)),
                  pl.BlockSpec((DN_CONV_W, cb), lambda ci: (0, ci))],
        out_specs=[pl.BlockSpec((db, cb), lambda ci: (0, ci)),
                   pl.BlockSpec((nw, db, cb), lambda ci: (0, 0, ci))],
        out_shape=[jax.ShapeDtypeStruct((db, c), F32), jax.ShapeDtypeStruct((nw, db, c), F32)],
        compiler_params=pltpu.CompilerParams(dimension_semantics=("parallel",), vmem_limit_bytes=VMEM_LIMIT),
        name="dn_step_prep",
    )(qkv, conv_prev_t, conv_w)


def _dn_step_body(q_ref, k_ref, v_ref, z_ref, gb_ref, ng_ref, s_ref, o_ref, so_ref):
    gb = gb_ref[...]
    ng = ng_ref[...]
    rows = lax.broadcasted_iota(jnp.int32, (SUBLANES, DN_DK), 0)
    heads = range(DN_HV)
    kh = [k_ref[i:i + 1, :] for i in range(DN_HK)]
    qh = [q_ref[i:i + 1, :] for i in range(DN_HK)]
    qk = [jnp.sum(a * b, -1, keepdims=True) for a, b in zip(qh, kh)]
    lhs = [jnp.where(rows == 0, b, jnp.where(rows == 1, a, 0.0)) for a, b in zip(qh, kh)]
    k8 = [jnp.where(rows == 0, b, 0.0) for b in kh]
    s_old = [s_ref[h] for h in heads]
    ks_qs = [jnp.dot(lhs[h // DN_REP], s_old[h], preferred_element_type=F32) for h in heads]
    eg = [jnp.exp(gb[:, h:h + 1]) for h in heads]
    v_new = [gb[:, DN_HV + h:DN_HV + h + 1] * (v_ref[h:h + 1, :] - eg[h] * ks_qs[h][0:1]) for h in heads]
    outer = [lax.dot_general(k8[h // DN_REP], jnp.where(rows == 0, v_new[h], 0.0), _TN,
                             preferred_element_type=F32) for h in heads]
    for h in heads:
        so_ref[h] = s_old[h] * eg[h] + outer[h]
        o = eg[h] * ks_qs[h][1:2] + qk[h // DN_REP] * v_new[h]
        o_ref[h:h + 1, :] = _gated_rms(o, ng, z_ref[h:h + 1, :])


def _dn_step(q, k, v, z, gates, norm_g, s_all, layer):
    db = q.shape[0]
    qk_spec = pl.BlockSpec((None, DN_HK, DN_DK), lambda b: (b, 0, 0))
    vz_spec = pl.BlockSpec((None, DN_HV, DN_DV), lambda b: (b, 0, 0))
    s_spec = pl.BlockSpec((None, None, DN_HV, DN_DK, DN_DV), lambda b: (layer, b, 0, 0, 0))
    return pl.pallas_call(
        _dn_step_body,
        grid=(db,),
        in_specs=[qk_spec, qk_spec, vz_spec, vz_spec,
                  pl.BlockSpec((None, 1, LANES), lambda b: (b, 0, 0)),
                  pl.BlockSpec((1, DN_DV), lambda b: (0, 0)), s_spec],
        out_specs=[vz_spec, s_spec],
        out_shape=[jax.ShapeDtypeStruct((db, DN_HV, DN_DV), F32), jax.ShapeDtypeStruct(s_all.shape, F32)],
        input_output_aliases={6: 1},
        compiler_params=pltpu.CompilerParams(dimension_semantics=("parallel",), vmem_limit_bytes=VMEM_LIMIT),
        name="dn_step",
    )(q, k, v, z, gates, norm_g.reshape(1, DN_DV), s_all)


def _dn_layer(xp, xs, mod_p, mod_s, seq, conv_prev, s_all, layer, w_in, conv_w, a_log, dt_bias, norm_g, w_out,
              ln_g, ln_b, alpha):
    mp, d = xp.shape
    bp = mp // seq
    db = xs.shape[0]
    conv_dim = 2 * DN_HK * DN_DK + DN_HV * DN_DV
    v_w = DN_HV * DN_DV
    w_all = jnp.pad(w_in, ((0, 0), (0, LANES - 2 * DN_HV))).astype(BF16)
    w_o = w_out.astype(BF16)
    alog = jnp.pad(a_log, (0, LANES - DN_HV)).reshape(1, LANES)
    dtb = jnp.pad(dt_bias, (0, LANES - DN_HV)).reshape(1, LANES)
    lane_row = pl.BlockSpec((1, LANES), lambda i: (0, 0))
    segments = [conv_dim, v_w, (LANES, _dn_gates, [(alog, lane_row), (dtb, lane_row)])]

    qkv, z, gates = _in_proj(xp, mod_p, seq, w_all, segments, tm=PROJ_TM, name="dn_in_p")
    qkv3 = qkv.reshape(bp, seq, conv_dim)
    cv_p = qkv3[:, seq - (DN_CONV_W - 1):]
    qkvc = _dn_prep(qkv3, conv_w, tt=DN_PREP_ROWS, cb=512)
    o, s_p = _dn_chunks(qkvc, z.reshape(bp, seq, v_w), gates.reshape(bp, seq, LANES), norm_g)
    xp_new = _out_proj_norm(o.reshape(mp, v_w), w_o, xp, mod_p, seq, ln_g, ln_b, tm=PROJ_TM, alpha=alpha,
                            name="dn_out_p")

    qkv_s, z_s, gates_s = _in_proj(xs, mod_s, 1, w_all, segments, tm=db, name="dn_in_s")
    qkvc_s, cv_t = _dn_step_prep(qkv_s, jnp.swapaxes(conv_prev, 0, 1), conv_w, cb=512)
    k_w = DN_HK * DN_DK
    o_s, s_s = _dn_step(qkvc_s[:, :k_w].reshape(db, DN_HK, DN_DK),
                        qkvc_s[:, k_w:2 * k_w].reshape(db, DN_HK, DN_DK),
                        qkvc_s[:, 2 * k_w:].reshape(db, DN_HV, DN_DV),
                        z_s.reshape(db, DN_HV, DN_DV), gates_s.reshape(db, 1, LANES), norm_g, s_all, layer)
    xs_new = _out_proj_norm(o_s.reshape(db, v_w), w_o, xs, mod_s, 1, ln_g, ln_b, tm=db, alpha=alpha,
                            name="dn_out_s")
    return xp_new, xs_new, s_p, cv_p, s_s, jnp.swapaxes(cv_t, 0, 1)


def _diff_lambda(lq1_ref, lk1_ref, lq2_ref, lk2_ref, lam_init):
    e1 = jnp.exp(jnp.sum(lq1_ref[...] * lk1_ref[...], -1, keepdims=True))
    e2 = jnp.exp(jnp.sum(lq2_ref[...] * lk2_ref[...], -1, keepdims=True))
    return e1 - e2 + lam_init


def _diff_head_out(o, sg, z, lam_init):
    on = o * lax.rsqrt(jnp.mean(o * o, -1, keepdims=True) + RMS_EPS) * sg * (1.0 - lam_init)
    return on * _silu(z)


def _diff_flash_body(qi_ref, kj_ref, q_ref, k_ref, v_ref, z_ref, lq1_ref, lk1_ref, lq2_ref, lk2_ref, sg_ref,
                     o_ref, m_ref, l_ref, acc_ref, *, lam_init):
    i = qi_ref[pl.program_id(2)]
    j = kj_ref[pl.program_id(2)]
    tq, tk = q_ref.shape[0], k_ref.shape[0]
    gw = DIFF_G * tq

    @pl.when(j == 0)
    def _():
        m_ref[...] = jnp.full_like(m_ref, -jnp.inf)
        l_ref[...] = jnp.zeros_like(l_ref)
        acc_ref[...] = jnp.zeros_like(acc_ref)

    def update(masked):
        v = v_ref[...].astype(BF16)
        if masked:
            keep = (lax.broadcasted_iota(jnp.int32, (tk, gw), 0)
                    <= (lax.broadcasted_iota(jnp.int32, (tk, gw), 1) & (tq - 1)))
        sts = []
        for half in range(2):
            qm = jnp.concatenate([q_ref[:, (g * 2 + half) * DIFF_HD:(g * 2 + half + 1) * DIFF_HD]
                                  for g in range(DIFF_G)], axis=0) * (DIFF_HD ** -0.5 * LOG2E)
            st = lax.dot_general(k_ref[:, half * DIFF_HD:(half + 1) * DIFF_HD], qm, _NT,
                                 preferred_element_type=F32)
            sts.append(jnp.where(keep, st, -jnp.inf) if masked else st)
        for half in range(2):
            st = sts[half]
            cols = slice(half * gw, (half + 1) * gw)
            m_prev = m_ref[:, cols]
            m_new = jnp.maximum(m_prev, jnp.max(st, 0, keepdims=True))
            alpha = jnp.exp2(m_prev - m_new)
            p = jnp.exp2(st - m_new)
            l_ref[:, cols] = alpha * l_ref[:, cols] + jnp.sum(p, 0, keepdims=True)
            acc_ref[:, cols] = alpha * acc_ref[:, cols] + lax.dot_general(
                v, p.astype(BF16), _TN, preferred_element_type=F32)
            m_ref[:, cols] = m_new

    @pl.when(j < i)
    def _():
        update(False)

    @pl.when(j == i)
    def _():
        update(True)
        lam = _diff_lambda(lq1_ref, lk1_ref, lq2_ref, lk2_ref, lam_init)
        sg = sg_ref[...]
        for g in range(DIFF_G):
            c0 = slice(g * tq, (g + 1) * tq)
            c1 = slice(gw + g * tq, gw + (g + 1) * tq)
            ot = acc_ref[:, c0] * (1.0 / l_ref[:, c0]) - lam * (acc_ref[:, c1] * (1.0 / l_ref[:, c1]))
            o_ref[:, g * DIFF_VD:(g + 1) * DIFF_VD] = _diff_head_out(
                ot.T, sg, z_ref[:, g * DIFF_VD:(g + 1) * DIFF_VD], lam_init).astype(o_ref.dtype)


def _diff_flash(q, k, v, z, lams, subln_g, lam_init, *, tq):
    b, t, _ = q.shape
    assert t % tq == 0 and tq & (tq - 1) == 0, (t, tq)
    nq = t // tq
    qw = DIFF_G * 2 * DIFF_HD
    ow = DIFF_G * DIFF_VD
    n_sub = 2 * DIFF_G
    pairs = [(i, j) for i in range(nq) for j in range(i + 1)]
    qi = jnp.asarray([p[0] for p in pairs], jnp.int32)
    kj = jnp.asarray([p[1] for p in pairs], jnp.int32)
    vec = pl.BlockSpec((1, DIFF_HD), lambda bi, h, s, qi, kj: (0, 0))
    grid_spec = pltpu.PrefetchScalarGridSpec(
        num_scalar_prefetch=2,
        grid=(b, DIFF_KVH, len(pairs)),
        in_specs=[pl.BlockSpec((None, tq, qw), lambda bi, h, s, qi, kj: (bi, qi[s], h)),
                  pl.BlockSpec((None, tq, 2 * DIFF_HD), lambda bi, h, s, qi, kj: (bi, kj[s], h)),
                  pl.BlockSpec((None, tq, DIFF_VD), lambda bi, h, s, qi, kj: (bi, kj[s], h)),
                  pl.BlockSpec((None, tq, ow), lambda bi, h, s, qi, kj: (bi, qi[s], h)),
                  vec, vec, vec, vec,
                  pl.BlockSpec((1, DIFF_VD), lambda bi, h, s, qi, kj: (0, 0))],
        out_specs=pl.BlockSpec((None, tq, ow), lambda bi, h, s, qi, kj: (bi, qi[s], h)),
        scratch_shapes=[pltpu.VMEM((1, n_sub * tq), F32), pltpu.VMEM((1, n_sub * tq), F32),
                        pltpu.VMEM((DIFF_VD, n_sub * tq), F32)],
    )
    return pl.pallas_call(
        functools.partial(_diff_flash_body, lam_init=lam_init),
        grid_spec=grid_spec,
        out_shape=jax.ShapeDtypeStruct((b, t, DIFF_H * DIFF_VD), BF16),
        compiler_params=pltpu.CompilerParams(
            dimension_semantics=("parallel", "parallel", "arbitrary"),
            vmem_limit_bytes=VMEM_LIMIT),
        name="diff_flash",
    )(qi, kj, q, k, v, z, *lams, subln_g.reshape(1, DIFF_VD))


def _diff_decode_body(pt_ref, q_ref, ks_ref, vs_ref, z_ref, lq1_ref, lk1_ref, lq2_ref, lk2_ref, sg_ref,
                      *rest, pp, lam_init):
    del pt_ref
    k_pages = rest[:pp]
    v_pages = rest[pp:2 * pp]
    o_ref = rest[2 * pp]
    qb_ref, m_ref, l_ref, acc_ref = rest[2 * pp + 1:]
    j = pl.program_id(1)
    n_rows = DIFF_KVH * DIFF_G * 2
    head_shift = (DIFF_G * 2).bit_length() - 1
    kw = 2 * DIFF_HD
    page_rows = k_pages[0].shape[0]

    @pl.when(j == 0)
    def _():
        d_i = lax.broadcasted_iota(jnp.int32, (DIFF_HD, kw), 0)
        c_i = lax.broadcasted_iota(jnp.int32, (DIFF_HD, kw), 1)
        spread = jnp.where((c_i & (DIFF_HD - 1)) == d_i, 1.0, 0.0)
        qt = jnp.dot(q_ref[...], spread, preferred_element_type=F32)
        r_i = lax.broadcasted_iota(jnp.int32, (n_rows, kw), 0)
        half = lax.broadcasted_iota(jnp.int32, (n_rows, kw), 1) >> (DIFF_HD.bit_length() - 1)
        qb_ref[...] = jnp.where(half == (r_i & 1), qt, 0.0) * (DIFF_HD ** -0.5)
        m_ref[...] = jnp.full_like(m_ref, -jnp.inf)
        l_ref[...] = jnp.zeros_like(l_ref)
        acc_ref[...] = jnp.zeros_like(acc_ref)

    qb = qb_ref[...]
    s = jnp.concatenate([lax.dot_general(qb, kp[...], _NT, preferred_element_type=F32) for kp in k_pages],
                        axis=1)
    row_head = lax.broadcasted_iota(jnp.int32, s.shape, 0) >> head_shift
    col_head = lax.broadcasted_iota(jnp.int32, s.shape, 1) & (DIFF_KVH - 1)
    s = jnp.where(row_head == col_head, s, -jnp.inf)
    m_prev = m_ref[...]
    m_new = jnp.maximum(m_prev, jnp.max(s, -1, keepdims=True))
    alpha = jnp.exp(m_prev - m_new)
    p = jnp.exp(s - m_new)
    l_new = alpha * l_ref[...] + jnp.sum(p, -1, keepdims=True)
    pv = jnp.dot(p[:, 0:page_rows], v_pages[0][...], preferred_element_type=F32)
    for t in range(1, pp):
        pv = pv + jnp.dot(p[:, t * page_rows:(t + 1) * page_rows], v_pages[t][...],
                          preferred_element_type=F32)
    acc_new = alpha * acc_ref[...] + pv
    m_ref[...] = m_new
    l_ref[...] = l_new
    acc_ref[...] = acc_new

    @pl.when(j == pl.num_programs(1) - 1)
    def _():
        rh = lax.broadcasted_iota(jnp.int32, (n_rows, kw), 0) >> head_shift
        ks = jnp.zeros((n_rows, kw), F32)
        vs = jnp.zeros((n_rows, kw), F32)
        for h in range(DIFF_KVH):
            ks = jnp.where(rh == h, ks_ref[h:h + 1, :], ks)
            vs = jnp.where(rh == h, vs_ref[h:h + 1, :], vs)
        s_self = jnp.sum(qb * ks, -1, keepdims=True)
        m_fin = jnp.maximum(m_new, s_self)
        a = jnp.exp(m_new - m_fin)
        p_self = jnp.exp(s_self - m_fin)
        l_fin = a * l_new + p_self
        acc = (a * acc_new + p_self * vs) / l_fin
        lam = _diff_lambda(lq1_ref, lk1_ref, lq2_ref, lk2_ref, lam_init)
        sg = sg_ref[...]
        for hh in range(DIFF_H):
            o = acc[2 * hh:2 * hh + 1] - lam * acc[2 * hh + 1:2 * hh + 2]
            o_ref[hh:hh + 1, :] = _diff_head_out(o, sg, z_ref[hh:hh + 1, :], lam_init)


def _diff_decode(q, k_self, v_self, z, cache_k, cache_v, layer, page_table, lams, subln_g, lam_init, *, pp):
    db = q.shape[0]
    n_pages = page_table.shape[1]
    assert n_pages % pp == 0
    page_rows, kw = cache_k.shape[2:]
    n_rows = DIFF_KVH * DIFF_G * 2

    def page_spec(t):
        return pl.BlockSpec((None, None, page_rows, kw), lambda b, j, pt: (layer, pt[b, j * pp + t], 0, 0))

    def seq_spec(shape):
        return pl.BlockSpec((None,) + shape, lambda b, j, pt: (b, 0, 0))

    vec = pl.BlockSpec((1, DIFF_HD), lambda b, j, pt: (0, 0))
    grid_spec = pltpu.PrefetchScalarGridSpec(
        num_scalar_prefetch=1,
        grid=(db, n_pages // pp),
        in_specs=[seq_spec((n_rows, DIFF_HD)), seq_spec((DIFF_KVH, kw)), seq_spec((DIFF_KVH, kw)),
                  seq_spec((DIFF_H, DIFF_VD)), vec, vec, vec, vec,
                  pl.BlockSpec((1, DIFF_VD), lambda b, j, pt: (0, 0))]
                 + [page_spec(t) for t in range(pp)] + [page_spec(t) for t in range(pp)],
        out_specs=seq_spec((DIFF_H, DIFF_VD)),
        scratch_shapes=[pltpu.VMEM((n_rows, kw), F32), pltpu.VMEM((n_rows, 1), F32),
                        pltpu.VMEM((n_rows, 1), F32), pltpu.VMEM((n_rows, DIFF_VD), F32)],
    )
    return pl.pallas_call(
        functools.partial(_diff_decode_body, pp=pp, lam_init=lam_init),
        grid_spec=grid_spec,
        out_shape=jax.ShapeDtypeStruct((db, DIFF_H, DIFF_VD), F32),
        compiler_params=pltpu.CompilerParams(dimension_semantics=("parallel", "arbitrary"),
                                             vmem_limit_bytes=VMEM_LIMIT),
        name="diff_decode",
    )(page_table, q, k_self, v_self, z, *lams, subln_g.reshape(1, DIFF_VD),
      *([cache_k] * pp), *([cache_v] * pp))


def _diff_layer(xp, xs, mod_p, mod_s, seq, layer_idx, j, cache_k, cache_v, page_table, w_in, lams, subln_g,
                w_out, ln_g, ln_b, alpha):
    mp, d = xp.shape
    bp = mp // seq
    db = xs.shape[0]
    lam_init = 0.8 - 0.6 * math.exp(-0.3 * layer_idx)
    q_w = DIFF_H * 2 * DIFF_HD
    k_w = DIFF_KVH * 2 * DIFF_HD
    v_w = DIFF_KVH * DIFF_VD
    w_all = w_in.astype(BF16)
    w_o = w_out.astype(BF16)
    lams = [a.reshape(1, DIFF_HD) for a in lams]
    segments = [q_w, k_w, v_w, w_in.shape[1] - q_w - k_w - v_w]

    q, k, v, z = _in_proj(xp, mod_p, seq, w_all, segments, tm=PROJ_TM, name="diff_in_p")
    o = _diff_flash(q.reshape(bp, seq, q_w), k.reshape(bp, seq, k_w), v.reshape(bp, seq, v_w),
                    z.reshape(bp, seq, -1), lams, subln_g, lam_init, tq=min(DIFF_TQ, seq))
    xp_new = _out_proj_norm(o.reshape(mp, -1), w_o, xp, mod_p, seq, ln_g, ln_b, tm=PROJ_TM, alpha=alpha,
                            name="diff_out_p")

    q_s, k_s, v_s, z_s = _in_proj(xs, mod_s, 1, w_all, segments, tm=db, name="diff_in_s")
    n_l, n_pool, page = cache_k.shape[:3]
    o_s = _diff_decode(q_s.reshape(db, DIFF_KVH * DIFF_G * 2, DIFF_HD), k_s.reshape(db, DIFF_KVH, 2 * DIFF_HD),
                       v_s.reshape(db, DIFF_KVH, DIFF_VD), z_s.reshape(db, DIFF_H, DIFF_VD),
                       cache_k.reshape(n_l, n_pool, page * DIFF_KVH, 2 * DIFF_HD),
                       cache_v.reshape(n_l, n_pool, page * DIFF_KVH, DIFF_VD),
                       j, page_table, lams, subln_g, lam_init, pp=DIFF_PAGES_PER_STEP)
    xs_new = _out_proj_norm(o_s.reshape(db, -1), w_o, xs, mod_s, 1, ln_g, ln_b, tm=db, alpha=alpha,
                            name="diff_out_s")
    return (xp_new, xs_new, k.reshape(bp, seq, DIFF_KVH, 2 * DIFF_HD), v.reshape(bp, seq, DIFF_KVH, DIFF_VD),
            k_s.reshape(db, 1, DIFF_KVH, 2 * DIFF_HD), v_s.reshape(db, 1, DIFF_KVH, DIFF_VD))


def _swa_band_body(q_ref, kc_ref, kp_ref, vc_ref, vp_ref, z_ref, sink_ref, o_ref):
    i = pl.program_id(1)
    blk = q_ref.shape[0]
    cols = SWA_G * blk
    kj = lax.broadcasted_iota(jnp.int32, (2 * blk, cols), 0)
    qi = lax.broadcasted_iota(jnp.int32, (2 * blk, cols), 1) & (blk - 1)
    dist = qi + blk - kj
    keep = (dist >= 0) & (dist <= WINDOW) & ((i > 0) | (kj >= blk))
    for h in range(SWA_KVH):
        hs = slice(h * SWA_HD, (h + 1) * SWA_HD)
        k2 = jnp.concatenate([kp_ref[:, hs], kc_ref[:, hs]], axis=0)
        v2 = jnp.concatenate([vp_ref[:, hs], vc_ref[:, hs]], axis=0)
        qs = jnp.concatenate([q_ref[:, (h * SWA_G + g) * SWA_HD:(h * SWA_G + g + 1) * SWA_HD]
                              for g in range(SWA_G)], axis=0) * (SWA_HD ** -0.5)
        sink = jnp.concatenate([jnp.broadcast_to(sink_ref[:, h * SWA_G + g:h * SWA_G + g + 1], (1, blk))
                                for g in range(SWA_G)], axis=1)
        st = lax.dot_general(k2, qs, _NT, preferred_element_type=F32)
        st = jnp.where(keep, st, -jnp.inf)
        m = jnp.maximum(jnp.max(st, 0, keepdims=True), sink)
        p = jnp.exp(st - m)
        den = jnp.sum(p, 0, keepdims=True) + jnp.exp(sink - m)
        ot = lax.dot_general(v2, p, _TN, preferred_element_type=F32) * (1.0 / den)
        for g in range(0, SWA_G, 2):
            cs = slice((h * SWA_G + g) * SWA_HD, (h * SWA_G + g + 2) * SWA_HD)
            pair = jnp.concatenate([ot[:, g * blk:(g + 1) * blk], ot[:, (g + 1) * blk:(g + 2) * blk]], axis=0)
            o_ref[:, cs] = (pair.T * _silu(z_ref[:, cs])).astype(o_ref.dtype)


def _swa_band(q, k, v, z, sinks):
    b, t, qw = q.shape
    blk = WINDOW
    kw = SWA_KVH * SWA_HD
    cur = pl.BlockSpec((None, blk, kw), lambda bi, i: (bi, i, 0))
    prev = pl.BlockSpec((None, blk, kw), lambda bi, i: (bi, jnp.maximum(i - 1, 0), 0))
    wide = pl.BlockSpec((None, blk, qw), lambda bi, i: (bi, i, 0))
    return pl.pallas_call(
        _swa_band_body,
        grid=(b, t // blk),
        in_specs=[wide, cur, prev, cur, prev, wide, pl.BlockSpec((1, SWA_H), lambda bi, i: (0, 0))],
        out_specs=wide,
        out_shape=jax.ShapeDtypeStruct((b, t, qw), BF16),
        compiler_params=pltpu.CompilerParams(dimension_semantics=("parallel", "parallel"),
                                             vmem_limit_bytes=VMEM_LIMIT),
        name="swa_band",
    )(q, k, k, v, v, z, sinks.reshape(1, SWA_H))


def _swa_decode_body(q_ref, kb_ref, vb_ref, kn_ref, vn_ref, z_ref, sink_ref, o_ref, ko_ref, vo_ref):
    win = kb_ref.shape[1]
    for h in range(SWA_KVH):
        hs = slice(h * SWA_HD, (h + 1) * SWA_HD)
        gs = slice(h * SWA_G, (h + 1) * SWA_G)
        qh = q_ref[:, gs, :]
        s = jnp.einsum("bqd,bkd->bqk", qh, kb_ref[:, :, hs], preferred_element_type=F32) * (SWA_HD ** -0.5)
        s_new = jnp.sum(qh * kn_ref[:, :, hs], -1, keepdims=True) * (SWA_HD ** -0.5)
        sink = sink_ref[gs, :][None]
        m = jnp.maximum(jnp.maximum(jnp.max(s, -1, keepdims=True), s_new), sink)
        p = jnp.exp(s - m)
        p_new = jnp.exp(s_new - m)
        den = jnp.sum(p, -1, keepdims=True) + p_new + jnp.exp(sink - m)
        o = jnp.einsum("bqk,bkd->bqd", p, vb_ref[:, :, hs], preferred_element_type=F32)
        o = (o + p_new * vn_ref[:, :, hs]) / den
        o_ref[:, gs, :] = o * _silu(z_ref[:, gs, :])
    ko_ref[:, 0:win - 1, :] = kb_ref[:, 1:win, :]
    ko_ref[:, win - 1:win, :] = kn_ref[...]
    vo_ref[:, 0:win - 1, :] = vb_ref[:, 1:win, :]
    vo_ref[:, win - 1:win, :] = vn_ref[...]


def _swa_decode(q, k_new, v_new, z, buf_k, buf_v, sinks, *, bb):
    db, win, kw = buf_k.shape
    qz = pl.BlockSpec((bb, SWA_H, SWA_HD), lambda b: (b, 0, 0))
    buf = pl.BlockSpec((bb, win, kw), lambda b: (b, 0, 0))
    new = pl.BlockSpec((bb, 1, kw), lambda b: (b, 0, 0))
    return pl.pallas_call(
        _swa_decode_body,
        grid=(db // bb,),
        in_specs=[qz, buf, buf, new, new, qz, pl.BlockSpec((SWA_H, 1), lambda b: (0, 0))],
        out_specs=[qz, buf, buf],
        out_shape=[jax.ShapeDtypeStruct(q.shape, F32), jax.ShapeDtypeStruct(buf_k.shape, F32),
                   jax.ShapeDtypeStruct(buf_v.shape, F32)],
        compiler_params=pltpu.CompilerParams(dimension_semantics=("parallel",), vmem_limit_bytes=VMEM_LIMIT),
        name="swa_decode",
    )(q, buf_k, buf_v, k_new, v_new, z, sinks.reshape(SWA_H, 1))


def _swa_layer(xp, xs, mod_p, mod_s, seq, buf_k, buf_v, sinks, w_in, w_out, ln_g, ln_b, alpha):
    mp, d = xp.shape
    bp = mp // seq
    db = xs.shape[0]
    q_w = SWA_H * SWA_HD
    kv_w = SWA_KVH * SWA_HD
    w_all = w_in.astype(BF16)
    w_o = w_out.astype(BF16)
    segments = [q_w, kv_w, kv_w, w_in.shape[1] - q_w - 2 * kv_w]

    q, k, v, z = _in_proj(xp, mod_p, seq, w_all, segments, tm=PROJ_TM, name="swa_in_p")
    k = k.reshape(bp, seq, kv_w)
    v = v.reshape(bp, seq, kv_w)
    o = _swa_band(q.reshape(bp, seq, q_w), k, v, z.reshape(bp, seq, q_w), sinks)
    xp_new = _out_proj_norm(o.reshape(mp, q_w), w_o, xp, mod_p, seq, ln_g, ln_b, tm=PROJ_TM, alpha=alpha,
                            name="swa_out_p")

    q_s, k_s, v_s, z_s = _in_proj(xs, mod_s, 1, w_all, segments, tm=db, name="swa_in_s")
    win = buf_k.shape[1]
    o_s, k_buf, v_buf = _swa_decode(q_s.reshape(db, SWA_H, SWA_HD), k_s.reshape(db, 1, kv_w),
                                    v_s.reshape(db, 1, kv_w), z_s.reshape(db, SWA_H, SWA_HD),
                                    buf_k.reshape(db, win, kv_w), buf_v.reshape(db, win, kv_w), sinks,
                                    bb=SWA_DEC_BB)
    xs_new = _out_proj_norm(o_s.reshape(db, q_w), w_o, xs, mod_s, 1, ln_g, ln_b, tm=db, alpha=alpha,
                            name="swa_out_s")
    kv_shape = (SWA_KVH, SWA_HD)
    return (xp_new, xs_new, k[:, seq - WINDOW:].reshape((bp, WINDOW) + kv_shape),
            v[:, seq - WINDOW:].reshape((bp, WINDOW) + kv_shape),
            k_buf.reshape((db, win) + kv_shape), v_buf.reshape((db, win) + kv_shape))


def kernel(x_prompt, x_sample, cache_diff_k, cache_diff_v, page_table, cache_swa_k, cache_swa_v, state_dn_S, state_dn_conv, c_prompt, c_sample, ada_w, ada_b, ln_g, ln_b, dn_w_in, dn_conv_w, dn_a_log, dn_dt_bias, dn_norm_g, dn_w_out, diff_w_in, diff_lq1, diff_lk1, diff_lq2, diff_lk2, diff_subln_g, diff_w_out, swa_w_in, swa_sinks, swa_w_out):
    bp, seq, d = x_prompt.shape
    db, dec_seq, _ = x_sample.shape
    assert dec_seq == 1 and bp <= SUBLANES
    depth = ada_w.shape[0]
    alpha = (2 * depth) ** 0.25
    xp = x_prompt.reshape(bp * seq, d)
    xs = x_sample.reshape(db, d)

    c_all = jnp.concatenate([c_prompt, jnp.zeros((SUBLANES - bp, d), F32), c_sample], axis=0)
    st = {n: [] for n in ("dk_p", "dv_p", "dk_s", "dv_s", "sk_p", "sv_p", "sk_s", "sv_s",
                          "S_p", "cv_p", "cv_s")}
    s_s = state_dn_S
    for l in range(depth):
        j = l // N_MIXERS
        bias_op = (ada_b[l].reshape(1, 3 * d), pl.BlockSpec((1, 3 * d), lambda i: (0, 0)))
        mod, = _proj(c_all, ada_w[l].astype(BF16), [(3 * d, _add_bias, [bias_op])], tm=c_all.shape[0],
                     pro=_silu, name="adaln")
        mod_p = mod[:bp].reshape(bp, 1, 3 * d)
        mod_s = mod[SUBLANES:]
        if l % N_MIXERS == 0:
            xp, xs, s_p, cv_p, s_s, cv_s = _dn_layer(
                xp, xs, mod_p, mod_s, seq, state_dn_conv[j], s_s, j, dn_w_in[j], dn_conv_w[j],
                dn_a_log[j], dn_dt_bias[j], dn_norm_g[j], dn_w_out[j], ln_g[l], ln_b[l], alpha)
            st["S_p"].append(s_p)
            st["cv_p"].append(cv_p)
            st["cv_s"].append(cv_s)
        elif l % N_MIXERS == 1:
            xp, xs, k_p, v_p, k_s, v_s = _diff_layer(
                xp, xs, mod_p, mod_s, seq, l, j, cache_diff_k, cache_diff_v, page_table, diff_w_in[j],
                (diff_lq1[j], diff_lk1[j], diff_lq2[j], diff_lk2[j]), diff_subln_g[j], diff_w_out[j],
                ln_g[l], ln_b[l], alpha)
            st["dk_p"].append(k_p)
            st["dv_p"].append(v_p)
            st["dk_s"].append(k_s)
            st["dv_s"].append(v_s)
        else:
            xp, xs, k_p, v_p, k_s, v_s = _swa_layer(
                xp, xs, mod_p, mod_s, seq, cache_swa_k[j], cache_swa_v[j], swa_sinks[j], swa_w_in[j],
                swa_w_out[j], ln_g[l], ln_b[l], alpha)
            st["sk_p"].append(k_p)
            st["sv_p"].append(v_p)
            st["sk_s"].append(k_s)
            st["sv_s"].append(v_s)
    return (xp.reshape(bp, seq, d), xs.reshape(db, 1, d),
            jnp.stack(st["dk_p"]), jnp.stack(st["dv_p"]), jnp.stack(st["dk_s"]), jnp.stack(st["dv_s"]),
            jnp.stack(st["sk_p"]), jnp.stack(st["sv_p"]), jnp.stack(st["sk_s"]), jnp.stack(st["sv_s"]),
            jnp.stack(st["S_p"]), jnp.stack(st["cv_p"]), s_s, jnp.stack(st["cv_s"]))
```

```python
import functools
import math

import jax
import jax.numpy as jnp
from jax import lax
from jax.experimental import pallas as pl
from jax.experimental.pallas import tpu as pltpu

F32 = jnp.float32
BF16 = jnp.bfloat16

LN_EPS = 1e-5
RMS_EPS = 1e-6
N_MIXERS = 3

DN_HK = 8
DN_HV = 16
DN_DK = 128
DN_DV = 128
DN_REP = DN_HV // DN_HK
DN_CONV_W = 4
DN_CHUNK = 128
DN_PREP_ROWS = 256
DN_STEP_SEQS = 2
PROJ_TM = 512

DIFF_H = 8
DIFF_KVH = 4
DIFF_G = DIFF_H // DIFF_KVH
DIFF_HD = 64
DIFF_VD = 2 * DIFF_HD
DIFF_TQ = 512
LOG2E = math.log2(math.e)
DIFF_PAGES_PER_STEP = 16

SWA_H = 16
SWA_KVH = 2
SWA_G = SWA_H // SWA_KVH
SWA_HD = 64
WINDOW = 128
SWA_DEC_BB = 8

LANES = 128
SUBLANES = 8
VMEM_LIMIT = 48 * 1024 * 1024

_NT = (((1,), (1,)), ((), ()))
_TN = (((0,), (0,)), ((), ()))


def _silu(x):
    return (0.5 * x) * (1.0 + jnp.tanh(0.5 * x))


def _proj_body(x_ref, w_ref, *refs, n_pro, segs, pro):
    pro_refs = refs[:n_pro]
    n_epi = sum(s[3] for s in segs)
    epi_refs = refs[n_pro:n_pro + n_epi]
    o_refs = refs[n_pro + n_epi:]
    x = x_ref[...]
    if pro is not None:
        x = pro(x, *pro_refs)
    xb = x.astype(BF16)
    e = 0
    for (start, width, epi, n_e), o_ref in zip(segs, o_refs):
        acc = jnp.dot(xb, w_ref[:, start:start + width], preferred_element_type=F32)
        if epi is not None:
            acc = epi(acc, *epi_refs[e:e + n_e])
        e += n_e
        o_ref[...] = acc


def _proj(x, w, segments, *, tm, pro=None, pro_ops=(), name="proj"):
    m, k = x.shape
    n = w.shape[1]
    assert m % tm == 0, (m, tm)
    ops = [x, w]
    in_specs = [pl.BlockSpec((tm, k), lambda i: (i, 0)),
                pl.BlockSpec((k, n), lambda i: (0, 0), pipeline_mode=pl.Buffered(1))]
    for a, s in pro_ops:
        ops.append(a)
        in_specs.append(s)
    segs = []
    start = 0
    for width, epi, epi_ops in segments:
        segs.append((start, width, epi, len(epi_ops)))
        start += width
        for a, s in epi_ops:
            ops.append(a)
            in_specs.append(s)
    assert start <= n, (start, n)
    return pl.pallas_call(
        functools.partial(_proj_body, n_pro=len(pro_ops), segs=tuple(segs), pro=pro),
        grid=(m // tm,),
        in_specs=in_specs,
        out_specs=[pl.BlockSpec((tm, s[1]), lambda i: (i, 0)) for s in segs],
        out_shape=[jax.ShapeDtypeStruct((m, s[1]), F32) for s in segs],
        compiler_params=pltpu.CompilerParams(dimension_semantics=("parallel",), vmem_limit_bytes=VMEM_LIMIT),
        name=name,
    )(*ops)


def _modulate(x, mod_ref):
    d = x.shape[-1]
    return x * (1.0 + mod_ref[:, d:2 * d]) + mod_ref[:, 0:d]


def _deepnorm(y, xres_ref, mod_ref, g_ref, b_ref, *, alpha):
    d = y.shape[-1]
    h = alpha * xres_ref[...] + (1.0 + mod_ref[:, 2 * d:3 * d]) * y
    hc = h - jnp.mean(h, -1, keepdims=True)
    var = jnp.mean(hc * hc, -1, keepdims=True)
    return hc * lax.rsqrt(var + LN_EPS) * g_ref[...] + b_ref[...]


def _add_bias(y, b_ref):
    return y + b_ref[...]


def _dn_gates(y, alog_ref, dtb_ref):
    lane = lax.broadcasted_iota(jnp.int32, y.shape, 1)
    t = y + dtb_ref[...]
    softplus = jnp.maximum(t, 0.0) + jnp.log1p(jnp.exp(-jnp.abs(t)))
    g = -jnp.exp(alog_ref[...]) * softplus
    return jnp.where(lane < DN_HV, g, jax.nn.sigmoid(y))


def _mod_spec(mod, tm, rows_per_mod):
    w = mod.shape[-1]
    if mod.ndim == 3:
        assert rows_per_mod % tm == 0, (rows_per_mod, tm)
        return pl.BlockSpec((None, 1, w), lambda i: (i * tm // rows_per_mod, 0, 0))
    return pl.BlockSpec((tm, w), lambda i: (i, 0))


def _in_proj(x, mod, rows_per_mod, w, segments, *, tm, name="in_proj"):
    segments = [(s, None, ()) if isinstance(s, int) else s for s in segments]
    return _proj(x, w, segments, tm=tm, pro=_modulate, pro_ops=[(mod, _mod_spec(mod, tm, rows_per_mod))],
                 name=name)


def _out_proj_norm(y, w, xres, mod, rows_per_mod, ln_g, ln_b, *, tm, alpha, name="out_proj"):
    d = w.shape[1]
    row = pl.BlockSpec((1, d), lambda i: (0, 0))
    epi_ops = [(xres, pl.BlockSpec((tm, d), lambda i: (i, 0))), (mod, _mod_spec(mod, tm, rows_per_mod)),
               (ln_g.reshape(1, d), row), (ln_b.reshape(1, d), row)]
    return _proj(y, w, [(d, functools.partial(_deepnorm, alpha=alpha), epi_ops)], tm=tm, name=name)[0]


def _l2norm_heads(y, o_ref, scale):
    for h in range(y.shape[-1] // DN_DK):
        yh = y[:, h * DN_DK:(h + 1) * DN_DK]
        inv = lax.rsqrt(jnp.sum(yh * yh, -1, keepdims=True) + RMS_EPS)
        o_ref[:, h * DN_DK:(h + 1) * DN_DK] = yh * inv * scale


def _dn_prep_body(x_ref, prev_ref, w_ref, o_ref, *, cb):
    i = pl.program_id(1)
    tt = x_ref.shape[0]
    backs = range(1, DN_CONV_W)
    row = lax.broadcasted_iota(jnp.int32, (SUBLANES, cb), 0)

    def rolled(v):
        return tuple(pltpu.roll(v, d, axis=0) for d in backs)

    for c0 in range(0, x_ref.shape[1], cb):
        cols = slice(c0, c0 + cb)
        taps = [w_ref[s:s + 1, cols] for s in range(DN_CONV_W)]

        def body(si, prev_rolled, cols=cols, taps=taps):
            r0 = pl.multiple_of(si * SUBLANES, SUBLANES)
            cur = x_ref[pl.ds(r0, SUBLANES), cols]
            cur_rolled = rolled(cur)
            back = [jnp.where(row < d, p, q) for d, p, q in zip(backs, prev_rolled, cur_rolled)]
            conv = back[DN_CONV_W - 2] * taps[0]
            for s in range(1, DN_CONV_W - 1):
                conv = conv + back[DN_CONV_W - 2 - s] * taps[s]
            o_ref[pl.ds(r0, SUBLANES), cols] = _silu(conv + cur * taps[DN_CONV_W - 1])
            return cur_rolled

        first = jnp.where(i > 0, prev_ref[:, cols], 0.0)
        lax.fori_loop(0, tt // SUBLANES, body, rolled(first), unroll=8)


def _dn_prep(qkv, conv_w, *, tt, cb):
    b, t, c = qkv.shape
    return pl.pallas_call(
        functools.partial(_dn_prep_body, cb=cb),
        grid=(b, t // tt),
        in_specs=[pl.BlockSpec((None, tt, c), lambda bi, i: (bi, i, 0)),
                  pl.BlockSpec((None, SUBLANES, c),
                               lambda bi, i: (bi, jnp.maximum(i * (tt // SUBLANES) - 1, 0), 0)),
                  pl.BlockSpec((DN_CONV_W, c), lambda bi, i: (0, 0))],
        out_specs=pl.BlockSpec((None, tt, c), lambda bi, i: (bi, i, 0)),
        out_shape=jax.ShapeDtypeStruct(qkv.shape, F32),
        compiler_params=pltpu.CompilerParams(dimension_semantics=("parallel", "parallel"),
                                             vmem_limit_bytes=VMEM_LIMIT),
        name="dn_prep",
    )(qkv, qkv, conv_w)


def _gated_rms(o, ng, z):
    return o * lax.rsqrt(jnp.mean(o * o, -1, keepdims=True) + RMS_EPS) * ng * _silu(z)


def _dn_chunk_body(x_ref, z_ref, gb_ref, ng_ref, o_ref, s_ref):
    ci = pl.program_id(1)
    c = x_ref.shape[0]
    hb = DN_HV

    @pl.when(ci == 0)
    def _():
        s_ref[...] = jnp.zeros_like(s_ref)

    k_w = DN_HK * DN_DK
    gb = gb_ref[...]
    row = lax.broadcasted_iota(jnp.int32, gb.shape, 0)
    gc = gb
    sh = 1
    while sh < c:
        gc = gc + jnp.where(row >= sh, pltpu.roll(gc, sh, axis=0), 0.0)
        sh *= 2
    gct = gc.T
    ri = lax.broadcasted_iota(jnp.int32, (c, c), 0)
    cj = lax.broadcasted_iota(jnp.int32, (c, c), 1)
    tril = ri >= cj
    strict = ri > cj
    merge_masks = []
    k = 0
    while (1 << k) < c:
        merge_masks.append(((ri >> (k + 1)) == (cj >> (k + 1))) & ((ri >> k) != (cj >> k)))
        k += 1
    ng = ng_ref[...]
    heads = range(hb)
    dot = functools.partial(jnp.dot, preferred_element_type=F32)
    def l2n(y, scale):
        return y * lax.rsqrt(jnp.sum(y * y, -1, keepdims=True) + RMS_EPS) * scale

    qh = [l2n(x_ref[:, i * DN_DK:(i + 1) * DN_DK], DN_DK ** -0.5) for i in range(DN_HK)]
    kh = [l2n(x_ref[:, k_w + i * DN_DK:k_w + (i + 1) * DN_DK], 1.0) for i in range(DN_HK)]
    kk = [lax.dot_general(a, a, _NT, preferred_element_type=F32) for a in kh]
    qk = [lax.dot_general(a, b, _NT, preferred_element_type=F32) for a, b in zip(qh, kh)]
    gcol = [gc[:, j:j + 1] for j in heads]
    bcol = [gb[:, hb + j:hb + j + 1] for j in heads]
    decay = [jnp.where(tril, jnp.exp(jnp.where(tril, gcol[j] - gct[j:j + 1, :], 0.0)), 0.0) for j in heads]
    lmat = [jnp.where(strict, bcol[j] * kk[j // DN_REP] * decay[j], 0.0) for j in heads]
    def odd_rows(a, blk):
        return jnp.concatenate([a[r:r + blk] for r in range(blk, c, 2 * blk)], axis=0)

    def spread_odd(a, blk):
        zero = jnp.zeros((blk, a.shape[1]), a.dtype)
        return jnp.concatenate([piece for t in range(c // (2 * blk))
                                for piece in (zero, a[t * blk:(t + 1) * blk])], axis=0)

    minv = [jnp.where(merge_masks[0], -lmat[j], 0.0) for j in heads]
    for lvl in range(1, len(merge_masks)):
        blk = 1 << lvl
        if blk < SUBLANES:
            loff = [jnp.where(merge_masks[lvl], lmat[j], 0.0) for j in heads]
            y = [loff[j] + dot(loff[j], minv[j]) for j in heads]
            minv = [minv[j] - y[j] - dot(minv[j], y[j]) for j in heads]
        else:
            rg = lax.broadcasted_iota(jnp.int32, (c // 2, c), 0)
            ro = ((rg >> lvl) << (lvl + 1)) + blk + (rg & (blk - 1))
            co = lax.broadcasted_iota(jnp.int32, (c // 2, c), 1)
            mask = ((ro >> (lvl + 1)) == (co >> (lvl + 1))) & ((ro >> lvl) != (co >> lvl))
            loff = [jnp.where(mask, odd_rows(lmat[j], blk), 0.0) for j in heads]
            y = [loff[j] + dot(loff[j], minv[j]) for j in heads]
            upd = [y[j] + dot(odd_rows(minv[j], blk), spread_odd(y[j], blk)) for j in heads]
            minv = [minv[j] - spread_odd(upd[j], blk) for j in heads]
    eg = [jnp.exp(gcol[j]) for j in heads]
    rhs = [jnp.concatenate([x_ref[:, 2 * k_w + j * DN_DV:2 * k_w + (j + 1) * DN_DV] * bcol[j],
                            kh[j // DN_REP] * (bcol[j] * eg[j])], axis=1) for j in heads]
    sol = [rhs[j] + dot(minv[j], rhs[j]) for j in heads]
    s_old = [s_ref[j] for j in heads]
    ws = [dot(jnp.concatenate([sol[j][:, DN_DV:], qh[j // DN_REP] * eg[j]], axis=0), s_old[j]) for j in heads]
    v_new = [sol[j][:, :DN_DV] - ws[j][:c] for j in heads]
    o = [ws[j][c:] + dot(qk[j // DN_REP] * decay[j], v_new[j]) for j in heads]
    for j in heads:
        glast = gc[c - 1:c, j:j + 1]
        kd = kh[j // DN_REP] * jnp.exp(glast - gcol[j])
        s_ref[j] = s_old[j] * jnp.exp(glast) + lax.dot_general(kd, v_new[j], _TN, preferred_element_type=F32)
        o_ref[:, j * DN_DV:(j + 1) * DN_DV] = _gated_rms(
            o[j], ng, z_ref[:, j * DN_DV:(j + 1) * DN_DV]).astype(o_ref.dtype)


def _dn_chunks(qkv, z, gates, norm_g):
    b, t, cd = qkv.shape
    c = DN_CHUNK
    assert t % c == 0
    v_w = DN_HV * DN_DV
    o, s = pl.pallas_call(
        _dn_chunk_body,
        grid=(b, t // c),
        in_specs=[pl.BlockSpec((None, c, cd), lambda bi, ci: (bi, ci, 0)),
                  pl.BlockSpec((None, c, v_w), lambda bi, ci: (bi, ci, 0)),
                  pl.BlockSpec((None, c, LANES), lambda bi, ci: (bi, ci, 0)),
                  pl.BlockSpec((1, DN_DV), lambda bi, ci: (0, 0))],
        out_specs=[pl.BlockSpec((None, c, v_w), lambda bi, ci: (bi, ci, 0)),
                   pl.BlockSpec((None, DN_HV, DN_DK, DN_DV), lambda bi, ci: (bi, 0, 0, 0))],
        out_shape=[jax.ShapeDtypeStruct((b, t, v_w), BF16),
                   jax.ShapeDtypeStruct((b, DN_HV, DN_DK, DN_DV), F32)],
        compiler_params=pltpu.CompilerParams(dimension_semantics=("parallel", "arbitrary"),
                                             vmem_limit_bytes=VMEM_LIMIT),
        name="dn_chunks",
    )(qkv, z, gates, norm_g.reshape(1, DN_DV))
    return o, s


def _dn_step_prep_body(x_ref, prev_ref, w_ref, o_ref, cv_ref, *, n_q_blk, n_k_blk):
    c = pl.program_id(0)
    x = x_ref[...]
    w = w_ref[...]
    conv = prev_ref[0] * w[0:1]
    for s in range(1, DN_CONV_W - 1):
        conv = conv + prev_ref[s] * w[s:s + 1]
    conv = conv + x * w[DN_CONV_W - 1:DN_CONV_W]
    y = _silu(conv)
    for s in range(DN_CONV_W - 2):
        cv_ref[s] = prev_ref[s + 1]
    cv_ref[DN_CONV_W - 2] = x

    @pl.when(c >= n_q_blk + n_k_blk)
    def _():
        o_ref[...] = y

    @pl.when(c < n_q_blk + n_k_blk)
    def _():
        _l2norm_heads(y, o_ref, jnp.where(c < n_q_blk, DN_DK ** -0.5, 1.0))


def _dn_step_prep(qkv, conv_prev_t, conv_w, *, cb):
    db, c = qkv.shape
    k_w = DN_HK * DN_DK
    nw = DN_CONV_W - 1
    return pl.pallas_call(
        functools.partial(_dn_step_prep_body, n_q_blk=k_w // cb, n_k_blk=k_w // cb),
        grid=(c // cb,),
        in_specs=[pl.BlockSpec((db, cb), lambda ci: (0, ci)),
                  pl.BlockSpec((nw, db, cb), lambda ci: (0, 0, ci)),
                  pl.BlockSpec((DN_CONV_W, cb), lambda ci: (0, ci))],
        out_specs=[pl.BlockSpec((db, cb), lambda ci: (0, ci)),
                   pl.BlockSpec((nw, db, cb), lambda ci: (0, 0, ci))],
        out_shape=[jax.ShapeDtypeStruct((db, c), F32), jax.ShapeDtypeStruct((nw, db, c), F32)],
        compiler_params=pltpu.CompilerParams(dimension_semantics=("parallel",), vmem_limit_bytes=VMEM_LIMIT),
        name="dn_step_prep",
    )(qkv, conv_prev_t, conv_w)


def _dn_step_body(q_ref, k_ref, v_ref, z_ref, gb_ref, ng_ref, s_ref, o_ref, so_ref, *, n_pass):
    g = pl.program_id(0)

    if n_pass:
        @pl.when(g < n_pass)
        def _():
            so_ref[...] = s_ref[...]

    @pl.when(g == n_pass)
    def _():
        ng = ng_ref[...]
        rows = lax.broadcasted_iota(jnp.int32, (SUBLANES, DN_DK), 0)
        heads = [(t, h) for t in range(q_ref.shape[0]) for h in range(DN_HV)]
        s_old = [s_ref[t, h] for t, h in heads]
        kh = [k_ref[t, h // DN_REP:h // DN_REP + 1, :] for t, h in heads]
        qh = [q_ref[t, h // DN_REP:h // DN_REP + 1, :] for t, h in heads]
        ks_qs = [jnp.dot(jnp.where(rows == 0, kh[n], jnp.where(rows == 1, qh[n], 0.0)), s_old[n],
                         preferred_element_type=F32) for n in range(len(heads))]
        eg = [jnp.exp(gb_ref[t, :, h:h + 1]) for t, h in heads]
        v_new = [gb_ref[t, :, DN_HV + h:DN_HV + h + 1] * (v_ref[t, h:h + 1, :] - eg[n] * ks_qs[n][0:1])
                 for n, (t, h) in enumerate(heads)]
        outer = [lax.dot_general(jnp.where(rows == 0, kh[n], 0.0), jnp.where(rows == 0, v_new[n], 0.0), _TN,
                                 preferred_element_type=F32) for n in range(len(heads))]
        for n, (t, h) in enumerate(heads):
            so_ref[t, h] = s_old[n] * eg[n] + outer[n]
            o = eg[n] * ks_qs[n][1:2] + jnp.sum(qh[n] * kh[n], -1, keepdims=True) * v_new[n]
            o_ref[t, h:h + 1, :] = _gated_rms(o, ng, z_ref[t, h:h + 1, :])


def _dn_step(q, k, v, z, gates, norm_g, s_all, layer, *, in_place):
    n, db = s_all.shape[:2]
    bs = DN_STEP_SEQS
    assert db % bs == 0
    n_pass = 0 if in_place else n - 1

    def s_layer(g):
        return layer if in_place else (layer + 1 + g) % n

    def row_spec(rows, width):
        return pl.BlockSpec((bs, rows, width), lambda g, b: (jnp.where(g == n_pass, b, 0), 0, 0))

    s_spec = pl.BlockSpec((None, bs, DN_HV, DN_DK, DN_DV), lambda g, b: (s_layer(g), b, 0, 0, 0))
    return pl.pallas_call(
        functools.partial(_dn_step_body, n_pass=n_pass),
        grid=(n_pass + 1, db // bs),
        in_specs=[row_spec(DN_HK, DN_DK), row_spec(DN_HK, DN_DK), row_spec(DN_HV, DN_DV),
                  row_spec(DN_HV, DN_DV), row_spec(1, LANES),
                  pl.BlockSpec((1, DN_DV), lambda g, b: (0, 0)), s_spec],
        out_specs=[row_spec(DN_HV, DN_DV), s_spec],
        out_shape=[jax.ShapeDtypeStruct((db, DN_HV, DN_DV), F32), jax.ShapeDtypeStruct(s_all.shape, F32)],
        input_output_aliases={6: 1} if in_place else {},
        compiler_params=pltpu.CompilerParams(dimension_semantics=("arbitrary", "arbitrary"),
                                             vmem_limit_bytes=VMEM_LIMIT),
        name="dn_step",
    )(q, k, v, z, gates, norm_g.reshape(1, DN_DV), s_all)


def _dn_layer(xp, xs, mod_p, mod_s, seq, conv_prev, s_all, layer, w_in, conv_w, a_log, dt_bias, norm_g, w_out,
              ln_g, ln_b, alpha):
    mp, d = xp.shape
    bp = mp // seq
    db = xs.shape[0]
    conv_dim = 2 * DN_HK * DN_DK + DN_HV * DN_DV
    v_w = DN_HV * DN_DV
    w_all = jnp.pad(w_in, ((0, 0), (0, LANES - 2 * DN_HV))).astype(BF16)
    w_o = w_out.astype(BF16)
    alog = jnp.pad(a_log, (0, LANES - DN_HV)).reshape(1, LANES)
    dtb = jnp.pad(dt_bias, (0, LANES - DN_HV)).reshape(1, LANES)
    lane_row = pl.BlockSpec((1, LANES), lambda i: (0, 0))
    segments = [conv_dim, v_w, (LANES, _dn_gates, [(alog, lane_row), (dtb, lane_row)])]

    qkv, z, gates = _in_proj(xp, mod_p, seq, w_all, segments, tm=PROJ_TM, name="dn_in_p")
    qkv3 = qkv.reshape(bp, seq, conv_dim)
    cv_p = qkv3[:, seq - (DN_CONV_W - 1):]
    qkvc = _dn_prep(qkv3, conv_w, tt=DN_PREP_ROWS, cb=512)
    o, s_p = _dn_chunks(qkvc, z.reshape(bp, seq, v_w), gates.reshape(bp, seq, LANES), norm_g)
    xp_new = _out_proj_norm(o.reshape(mp, v_w), w_o, xp, mod_p, seq, ln_g, ln_b, tm=PROJ_TM, alpha=alpha,
                            name="dn_out_p")

    qkv_s, z_s, gates_s = _in_proj(xs, mod_s, 1, w_all, segments, tm=db, name="dn_in_s")
    qkvc_s, cv_t = _dn_step_prep(qkv_s, jnp.swapaxes(conv_prev, 0, 1), conv_w, cb=512)
    k_w = DN_HK * DN_DK
    o_s, s_s = _dn_step(qkvc_s[:, :k_w].reshape(db, DN_HK, DN_DK),
                        qkvc_s[:, k_w:2 * k_w].reshape(db, DN_HK, DN_DK),
                        qkvc_s[:, 2 * k_w:].reshape(db, DN_HV, DN_DV),
                        z_s.reshape(db, DN_HV, DN_DV), gates_s.reshape(db, 1, LANES), norm_g, s_all, layer,
                        in_place=layer > 0)
    xs_new = _out_proj_norm(o_s.reshape(db, v_w), w_o, xs, mod_s, 1, ln_g, ln_b, tm=db, alpha=alpha,
                            name="dn_out_s")
    return xp_new, xs_new, s_p, cv_p, s_s, jnp.swapaxes(cv_t, 0, 1)


def _diff_lambda(lq1_ref, lk1_ref, lq2_ref, lk2_ref, lam_init):
    e1 = jnp.exp(jnp.sum(lq1_ref[...] * lk1_ref[...], -1, keepdims=True))
    e2 = jnp.exp(jnp.sum(lq2_ref[...] * lk2_ref[...], -1, keepdims=True))
    return e1 - e2 + lam_init


def _diff_head_out(o, sg, z, lam_init):
    on = o * lax.rsqrt(jnp.mean(o * o, -1, keepdims=True) + RMS_EPS) * sg * (1.0 - lam_init)
    return on * _silu(z)


def _diff_flash_body(qi_ref, kj_ref, q_ref, k_ref, v_ref, z_ref, lq1_ref, lk1_ref, lq2_ref, lk2_ref, sg_ref,
                     o_ref, m_ref, l_ref, acc_ref, *, lam_init):
    i = qi_ref[pl.program_id(2)]
    j = kj_ref[pl.program_id(2)]
    tq, tk = q_ref.shape[0], k_ref.shape[0]
    gw = DIFF_G * tq

    @pl.when(j == 0)
    def _():
        m_ref[...] = jnp.full_like(m_ref, -jnp.inf)
        l_ref[...] = jnp.zeros_like(l_ref)
        acc_ref[...] = jnp.zeros_like(acc_ref)

    def update(masked):
        v = v_ref[...].astype(BF16)
        if masked:
            keep = (lax.broadcasted_iota(jnp.int32, (tk, gw), 0)
                    <= (lax.broadcasted_iota(jnp.int32, (tk, gw), 1) & (tq - 1)))
        sts = []
        for half in range(2):
            qm = jnp.concatenate([q_ref[:, (g * 2 + half) * DIFF_HD:(g * 2 + half + 1) * DIFF_HD]
                                  for g in range(DIFF_G)], axis=0) * (DIFF_HD ** -0.5 * LOG2E)
            st = lax.dot_general(k_ref[:, half * DIFF_HD:(half + 1) * DIFF_HD], qm, _NT,
                                 preferred_element_type=F32)
            sts.append(jnp.where(keep, st, -jnp.inf) if masked else st)
        for half in range(2):
            st = sts[half]
            cols = slice(half * gw, (half + 1) * gw)
            m_prev = m_ref[:, cols]
            m_new = jnp.maximum(m_prev, jnp.max(st, 0, keepdims=True))
            alpha = jnp.exp2(m_prev - m_new)
            p = jnp.exp2(st - m_new)
            l_ref[:, cols] = alpha * l_ref[:, cols] + jnp.sum(p, 0, keepdims=True)
            acc_ref[:, cols] = alpha * acc_ref[:, cols] + lax.dot_general(
                v, p.astype(BF16), _TN, preferred_element_type=F32)
            m_ref[:, cols] = m_new

    @pl.when(j < i)
    def _():
        update(False)

    @pl.when(j == i)
    def _():
        update(True)
        lam = _diff_lambda(lq1_ref, lk1_ref, lq2_ref, lk2_ref, lam_init)
        sg = sg_ref[...]
        for g in range(DIFF_G):
            c0 = slice(g * tq, (g + 1) * tq)
            c1 = slice(gw + g * tq, gw + (g + 1) * tq)
            ot = acc_ref[:, c0] * (1.0 / l_ref[:, c0]) - lam * (acc_ref[:, c1] * (1.0 / l_ref[:, c1]))
            o_ref[:, g * DIFF_VD:(g + 1) * DIFF_VD] = _diff_head_out(
                ot.T, sg, z_ref[:, g * DIFF_VD:(g + 1) * DIFF_VD], lam_init).astype(o_ref.dtype)


def _diff_flash(q, k, v, z, lams, subln_g, lam_init, *, tq):
    b, t, _ = q.shape
    assert t % tq == 0 and tq & (tq - 1) == 0, (t, tq)
    nq = t // tq
    qw = DIFF_G * 2 * DIFF_HD
    ow = DIFF_G * DIFF_VD
    n_sub = 2 * DIFF_G
    pairs = [(i, j) for i in range(nq) for j in range(i + 1)]
    qi = jnp.asarray([p[0] for p in pairs], jnp.int32)
    kj = jnp.asarray([p[1] for p in pairs], jnp.int32)
    vec = pl.BlockSpec((1, DIFF_HD), lambda bi, h, s, qi, kj: (0, 0))
    grid_spec = pltpu.PrefetchScalarGridSpec(
        num_scalar_prefetch=2,
        grid=(b, DIFF_KVH, len(pairs)),
        in_specs=[pl.BlockSpec((None, tq, qw), lambda bi, h, s, qi, kj: (bi, qi[s], h)),
                  pl.BlockSpec((None, tq, 2 * DIFF_HD), lambda bi, h, s, qi, kj: (bi, kj[s], h)),
                  pl.BlockSpec((None, tq, DIFF_VD), lambda bi, h, s, qi, kj: (bi, kj[s], h)),
                  pl.BlockSpec((None, tq, ow), lambda bi, h, s, qi, kj: (bi, qi[s], h)),
                  vec, vec, vec, vec,
                  pl.BlockSpec((1, DIFF_VD), lambda bi, h, s, qi, kj: (0, 0))],
        out_specs=pl.BlockSpec((None, tq, ow), lambda bi, h, s, qi, kj: (bi, qi[s], h)),
        scratch_shapes=[pltpu.VMEM((1, n_sub * tq), F32), pltpu.VMEM((1, n_sub * tq), F32),
                        pltpu.VMEM((DIFF_VD, n_sub * tq), F32)],
    )
    return pl.pallas_call(
        functools.partial(_diff_flash_body, lam_init=lam_init),
        grid_spec=grid_spec,
        out_shape=jax.ShapeDtypeStruct((b, t, DIFF_H * DIFF_VD), BF16),
        compiler_params=pltpu.CompilerParams(
            dimension_semantics=("parallel", "parallel", "arbitrary"),
            vmem_limit_bytes=VMEM_LIMIT),
        name="diff_flash",
    )(qi, kj, q, k, v, z, *lams, subln_g.reshape(1, DIFF_VD))


def _diff_decode_body(pt_ref, q_ref, ks_ref, vs_ref, z_ref, lq1_ref, lk1_ref, lq2_ref, lk2_ref, sg_ref,
                      *rest, pp, lam_init):
    del pt_ref
    k_pages = rest[:pp]
    v_pages = rest[pp:2 * pp]
    o_ref = rest[2 * pp]
    qb_ref, m_ref, l_ref, acc_ref = rest[2 * pp + 1:]
    j = pl.program_id(1)
    n_rows = DIFF_KVH * DIFF_G * 2
    head_shift = (DIFF_G * 2).bit_length() - 1
    kw = 2 * DIFF_HD
    page_rows = k_pages[0].shape[0]

    @pl.when(j == 0)
    def _():
        d_i = lax.broadcasted_iota(jnp.int32, (DIFF_HD, kw), 0)
        c_i = lax.broadcasted_iota(jnp.int32, (DIFF_HD, kw), 1)
        spread = jnp.where((c_i & (DIFF_HD - 1)) == d_i, 1.0, 0.0)
        qt = jnp.dot(q_ref[...], spread, preferred_element_type=F32)
        r_i = lax.broadcasted_iota(jnp.int32, (n_rows, kw), 0)
        half = lax.broadcasted_iota(jnp.int32, (n_rows, kw), 1) >> (DIFF_HD.bit_length() - 1)
        qb_ref[...] = jnp.where(half == (r_i & 1), qt, 0.0) * (DIFF_HD ** -0.5)
        m_ref[...] = jnp.full_like(m_ref, -jnp.inf)
        l_ref[...] = jnp.zeros_like(l_ref)
        acc_ref[...] = jnp.zeros_like(acc_ref)

    qb = qb_ref[...]
    s = jnp.concatenate([lax.dot_general(qb, kp[...], _NT, preferred_element_type=F32) for kp in k_pages],
                        axis=1)
    row_head = lax.broadcasted_iota(jnp.int32, s.shape, 0) >> head_shift
    col_head = lax.broadcasted_iota(jnp.int32, s.shape, 1) & (DIFF_KVH - 1)
    s = jnp.where(row_head == col_head, s, -jnp.inf)
    m_prev = m_ref[...]
    m_new = jnp.maximum(m_prev, jnp.max(s, -1, keepdims=True))
    alpha = jnp.exp(m_prev - m_new)
    p = jnp.exp(s - m_new)
    l_new = alpha * l_ref[...] + jnp.sum(p, -1, keepdims=True)
    pv = jnp.dot(p[:, 0:page_rows], v_pages[0][...], preferred_element_type=F32)
    for t in range(1, pp):
        pv = pv + jnp.dot(p[:, t * page_rows:(t + 1) * page_rows], v_pages[t][...],
                          preferred_element_type=F32)
    acc_new = alpha * acc_ref[...] + pv
    m_ref[...] = m_new
    l_ref[...] = l_new
    acc_ref[...] = acc_new

    @pl.when(j == pl.num_programs(1) - 1)
    def _():
        rh = lax.broadcasted_iota(jnp.int32, (n_rows, kw), 0) >> head_shift
        ks = jnp.zeros((n_rows, kw), F32)
        vs = jnp.zeros((n_rows, kw), F32)
        for h in range(DIFF_KVH):
            ks = jnp.where(rh == h, ks_ref[h:h + 1, :], ks)
            vs = jnp.where(rh == h, vs_ref[h:h + 1, :], vs)
        s_self = jnp.sum(qb * ks, -1, keepdims=True)
        m_fin = jnp.maximum(m_new, s_self)
        a = jnp.exp(m_new - m_fin)
        p_self = jnp.exp(s_self - m_fin)
        l_fin = a * l_new + p_self
        acc = (a * acc_new + p_self * vs) / l_fin
        lam = _diff_lambda(lq1_ref, lk1_ref, lq2_ref, lk2_ref, lam_init)
        sg = sg_ref[...]
        for hh in range(DIFF_H):
            o = acc[2 * hh:2 * hh + 1] - lam * acc[2 * hh + 1:2 * hh + 2]
            o_ref[hh:hh + 1, :] = _diff_head_out(o, sg, z_ref[hh:hh + 1, :], lam_init)


def _diff_decode(q, k_self, v_self, z, cache_k, cache_v, layer, page_table, lams, subln_g, lam_init, *, pp):
    db = q.shape[0]
    n_pages = page_table.shape[1]
    assert n_pages % pp == 0
    page_rows, kw = cache_k.shape[2:]
    n_rows = DIFF_KVH * DIFF_G * 2

    def page_spec(t):
        return pl.BlockSpec((None, None, page_rows, kw), lambda b, j, pt: (layer, pt[b, j * pp + t], 0, 0))

    def seq_spec(shape):
        return pl.BlockSpec((None,) + shape, lambda b, j, pt: (b, 0, 0))

    vec = pl.BlockSpec((1, DIFF_HD), lambda b, j, pt: (0, 0))
    grid_spec = pltpu.PrefetchScalarGridSpec(
        num_scalar_prefetch=1,
        grid=(db, n_pages // pp),
        in_specs=[seq_spec((n_rows, DIFF_HD)), seq_spec((DIFF_KVH, kw)), seq_spec((DIFF_KVH, kw)),
                  seq_spec((DIFF_H, DIFF_VD)), vec, vec, vec, vec,
                  pl.BlockSpec((1, DIFF_VD), lambda b, j, pt: (0, 0))]
                 + [page_spec(t) for t in range(pp)] + [page_spec(t) for t in range(pp)],
        out_specs=seq_spec((DIFF_H, DIFF_VD)),
        scratch_shapes=[pltpu.VMEM((n_rows, kw), F32), pltpu.VMEM((n_rows, 1), F32),
                        pltpu.VMEM((n_rows, 1), F32), pltpu.VMEM((n_rows, DIFF_VD), F32)],
    )
    return pl.pallas_call(
        functools.partial(_diff_decode_body, pp=pp, lam_init=lam_init),
        grid_spec=grid_spec,
        out_shape=jax.ShapeDtypeStruct((db, DIFF_H, DIFF_VD), F32),
        compiler_params=pltpu.CompilerParams(dimension_semantics=("parallel", "arbitrary"),
                                             vmem_limit_bytes=VMEM_LIMIT),
        name="diff_decode",
    )(page_table, q, k_self, v_self, z, *lams, subln_g.reshape(1, DIFF_VD),
      *([cache_k] * pp), *([cache_v] * pp))


def _diff_layer(xp, xs, mod_p, mod_s, seq, layer_idx, j, cache_k, cache_v, page_table, w_in, lams, subln_g,
                w_out, ln_g, ln_b, alpha):
    mp, d = xp.shape
    bp = mp // seq
    db = xs.shape[0]
    lam_init = 0.8 - 0.6 * math.exp(-0.3 * layer_idx)
    q_w = DIFF_H * 2 * DIFF_HD
    k_w = DIFF_KVH * 2 * DIFF_HD
    v_w = DIFF_KVH * DIFF_VD
    w_all = w_in.astype(BF16)
    w_o = w_out.astype(BF16)
    lams = [a.reshape(1, DIFF_HD) for a in lams]
    segments = [q_w, k_w, v_w, w_in.shape[1] - q_w - k_w - v_w]

    q, k, v, z = _in_proj(xp, mod_p, seq, w_all, segments, tm=PROJ_TM, name="diff_in_p")
    o = _diff_flash(q.reshape(bp, seq, q_w), k.reshape(bp, seq, k_w), v.reshape(bp, seq, v_w),
                    z.reshape(bp, seq, -1), lams, subln_g, lam_init, tq=min(DIFF_TQ, seq))
    xp_new = _out_proj_norm(o.reshape(mp, -1), w_o, xp, mod_p, seq, ln_g, ln_b, tm=PROJ_TM, alpha=alpha,
                            name="diff_out_p")

    q_s, k_s, v_s, z_s = _in_proj(xs, mod_s, 1, w_all, segments, tm=db, name="diff_in_s")
    n_l, n_pool, page = cache_k.shape[:3]
    o_s = _diff_decode(q_s.reshape(db, DIFF_KVH * DIFF_G * 2, DIFF_HD), k_s.reshape(db, DIFF_KVH, 2 * DIFF_HD),
                       v_s.reshape(db, DIFF_KVH, DIFF_VD), z_s.reshape(db, DIFF_H, DIFF_VD),
                       cache_k.reshape(n_l, n_pool, page * DIFF_KVH, 2 * DIFF_HD),
                       cache_v.reshape(n_l, n_pool, page * DIFF_KVH, DIFF_VD),
                       j, page_table, lams, subln_g, lam_init, pp=DIFF_PAGES_PER_STEP)
    xs_new = _out_proj_norm(o_s.reshape(db, -1), w_o, xs, mod_s, 1, ln_g, ln_b, tm=db, alpha=alpha,
                            name="diff_out_s")
    return (xp_new, xs_new, k.reshape(bp, seq, DIFF_KVH, 2 * DIFF_HD), v.reshape(bp, seq, DIFF_KVH, DIFF_VD),
            k_s.reshape(db, 1, DIFF_KVH, 2 * DIFF_HD), v_s.reshape(db, 1, DIFF_KVH, DIFF_VD))


def _swa_band_body(q_ref, kc_ref, kp_ref, vc_ref, vp_ref, z_ref, sink_ref, o_ref):
    i = pl.program_id(1)
    blk = q_ref.shape[0]
    cols = SWA_G * blk
    kj = lax.broadcasted_iota(jnp.int32, (2 * blk, cols), 0)
    qi = lax.broadcasted_iota(jnp.int32, (2 * blk, cols), 1) & (blk - 1)
    dist = qi + blk - kj
    keep = (dist >= 0) & (dist <= WINDOW) & ((i > 0) | (kj >= blk))
    for h in range(SWA_KVH):
        hs = slice(h * SWA_HD, (h + 1) * SWA_HD)
        k2 = jnp.concatenate([kp_ref[:, hs], kc_ref[:, hs]], axis=0)
        v2 = jnp.concatenate([vp_ref[:, hs], vc_ref[:, hs]], axis=0)
        qs = jnp.concatenate([q_ref[:, (h * SWA_G + g) * SWA_HD:(h * SWA_G + g + 1) * SWA_HD]
                              for g in range(SWA_G)], axis=0) * (SWA_HD ** -0.5)
        sink = jnp.concatenate([jnp.broadcast_to(sink_ref[:, h * SWA_G + g:h * SWA_G + g + 1], (1, blk))
                                for g in range(SWA_G)], axis=1)
        st = lax.dot_general(k2, qs, _NT, preferred_element_type=F32)
        st = jnp.where(keep, st, -jnp.inf)
        m = jnp.maximum(jnp.max(st, 0, keepdims=True), sink)
        p = jnp.exp(st - m)
        den = jnp.sum(p, 0, keepdims=True) + jnp.exp(sink - m)
        ot = lax.dot_general(v2, p, _TN, preferred_element_type=F32) * (1.0 / den)
        for g in range(0, SWA_G, 2):
            cs = slice((h * SWA_G + g) * SWA_HD, (h * SWA_G + g + 2) * SWA_HD)
            pair = jnp.concatenate([ot[:, g * blk:(g + 1) * blk], ot[:, (g + 1) * blk:(g + 2) * blk]], axis=0)
            o_ref[:, cs] = (pair.T * _silu(z_ref[:, cs])).astype(o_ref.dtype)


def _swa_band(q, k, v, z, sinks):
    b, t, qw = q.shape
    blk = WINDOW
    kw = SWA_KVH * SWA_HD
    cur = pl.BlockSpec((None, blk, kw), lambda bi, i: (bi, i, 0))
    prev = pl.BlockSpec((None, blk, kw), lambda bi, i: (bi, jnp.maximum(i - 1, 0), 0))
    wide = pl.BlockSpec((None, blk, qw), lambda bi, i: (bi, i, 0))
    return pl.pallas_call(
        _swa_band_body,
        grid=(b, t // blk),
        in_specs=[wide, cur, prev, cur, prev, wide, pl.BlockSpec((1, SWA_H), lambda bi, i: (0, 0))],
        out_specs=wide,
        out_shape=jax.ShapeDtypeStruct((b, t, qw), BF16),
        compiler_params=pltpu.CompilerParams(dimension_semantics=("parallel", "parallel"),
                                             vmem_limit_bytes=VMEM_LIMIT),
        name="swa_band",
    )(q, k, k, v, v, z, sinks.reshape(1, SWA_H))


def _swa_decode_body(q_ref, kb_ref, vb_ref, kn_ref, vn_ref, z_ref, sink_ref, o_ref, ko_ref, vo_ref):
    win = kb_ref.shape[1]
    for h in range(SWA_KVH):
        hs = slice(h * SWA_HD, (h + 1) * SWA_HD)
        gs = slice(h * SWA_G, (h + 1) * SWA_G)
        qh = q_ref[:, gs, :]
        s = jnp.einsum("bqd,bkd->bqk", qh, kb_ref[:, :, hs], preferred_element_type=F32) * (SWA_HD ** -0.5)
        s_new = jnp.sum(qh * kn_ref[:, :, hs], -1, keepdims=True) * (SWA_HD ** -0.5)
        sink = sink_ref[gs, :][None]
        m = jnp.maximum(jnp.maximum(jnp.max(s, -1, keepdims=True), s_new), sink)
        p = jnp.exp(s - m)
        p_new = jnp.exp(s_new - m)
        den = jnp.sum(p, -1, keepdims=True) + p_new + jnp.exp(sink - m)
        o = jnp.einsum("bqk,bkd->bqd", p, vb_ref[:, :, hs], preferred_element_type=F32)
        o = (o + p_new * vn_ref[:, :, hs]) / den
        o_ref[:, gs, :] = o * _silu(z_ref[:, gs, :])
    ko_ref[:, 0:win - 1, :] = kb_ref[:, 1:win, :]
    ko_ref[:, win - 1:win, :] = kn_ref[...]
    vo_ref[:, 0:win - 1, :] = vb_ref[:, 1:win, :]
    vo_ref[:, win - 1:win, :] = vn_ref[...]


def _swa_decode(q, k_new, v_new, z, buf_k, buf_v, sinks, *, bb):
    db, win, kw = buf_k.shape
    qz = pl.BlockSpec((bb, SWA_H, SWA_HD), lambda b: (b, 0, 0))
    buf = pl.BlockSpec((bb, win, kw), lambda b: (b, 0, 0))
    new = pl.BlockSpec((bb, 1, kw), lambda b: (b, 0, 0))
    return pl.pallas_call(
        _swa_decode_body,
        grid=(db // bb,),
        in_specs=[qz, buf, buf, new, new, qz, pl.BlockSpec((SWA_H, 1), lambda b: (0, 0))],
        out_specs=[qz, buf, buf],
        out_shape=[jax.ShapeDtypeStruct(q.shape, F32), jax.ShapeDtypeStruct(buf_k.shape, F32),
                   jax.ShapeDtypeStruct(buf_v.shape, F32)],
        compiler_params=pltpu.CompilerParams(dimension_semantics=("parallel",), vmem_limit_bytes=VMEM_LIMIT),
        name="swa_decode",
    )(q, buf_k, buf_v, k_new, v_new, z, sinks.reshape(SWA_H, 1))


def _swa_layer(xp, xs, mod_p, mod_s, seq, buf_k, buf_v, sinks, w_in, w_out, ln_g, ln_b, alpha):
    mp, d = xp.shape
    bp = mp // seq
    db = xs.shape[0]
    q_w = SWA_H * SWA_HD
    kv_w = SWA_KVH * SWA_HD
    w_all = w_in.astype(BF16)
    w_o = w_out.astype(BF16)
    segments = [q_w, kv_w, kv_w, w_in.shape[1] - q_w - 2 * kv_w]

    q, k, v, z = _in_proj(xp, mod_p, seq, w_all, segments, tm=PROJ_TM, name="swa_in_p")
    k = k.reshape(bp, seq, kv_w)
    v = v.reshape(bp, seq, kv_w)
    o = _swa_band(q.reshape(bp, seq, q_w), k, v, z.reshape(bp, seq, q_w), sinks)
    xp_new = _out_proj_norm(o.reshape(mp, q_w), w_o, xp, mod_p, seq, ln_g, ln_b, tm=PROJ_TM, alpha=alpha,
                            name="swa_out_p")

    q_s, k_s, v_s, z_s = _in_proj(xs, mod_s, 1, w_all, segments, tm=db, name="swa_in_s")
    win = buf_k.shape[1]
    o_s, k_buf, v_buf = _swa_decode(q_s.reshape(db, SWA_H, SWA_HD), k_s.reshape(db, 1, kv_w),
                                    v_s.reshape(db, 1, kv_w), z_s.reshape(db, SWA_H, SWA_HD),
                                    buf_k.reshape(db, win, kv_w), buf_v.reshape(db, win, kv_w), sinks,
                                    bb=SWA_DEC_BB)
    xs_new = _out_proj_norm(o_s.reshape(db, q_w), w_o, xs, mod_s, 1, ln_g, ln_b, tm=db, alpha=alpha,
                            name="swa_out_s")
    kv_shape = (SWA_KVH, SWA_HD)
    return (xp_new, xs_new, k[:, seq - WINDOW:].reshape((bp, WINDOW) + kv_shape),
            v[:, seq - WINDOW:].reshape((bp, WINDOW) + kv_shape),
            k_buf.reshape((db, win) + kv_shape), v_buf.reshape((db, win) + kv_shape))


def kernel(x_prompt, x_sample, cache_diff_k, cache_diff_v, page_table, cache_swa_k, cache_swa_v, state_dn_S, state_dn_conv, c_prompt, c_sample, ada_w, ada_b, ln_g, ln_b, dn_w_in, dn_conv_w, dn_a_log, dn_dt_bias, dn_norm_g, dn_w_out, diff_w_in, diff_lq1, diff_lk1, diff_lq2, diff_lk2, diff_subln_g, diff_w_out, swa_w_in, swa_sinks, swa_w_out):
    bp, seq, d = x_prompt.shape
    db, dec_seq, _ = x_sample.shape
    assert dec_seq == 1 and bp <= SUBLANES
    depth = ada_w.shape[0]
    alpha = (2 * depth) ** 0.25
    xp = x_prompt.reshape(bp * seq, d)
    xs = x_sample.reshape(db, d)

    c_all = jnp.concatenate([c_prompt, jnp.zeros((SUBLANES - bp, d), F32), c_sample], axis=0)
    st = {n: [] for n in ("dk_p", "dv_p", "dk_s", "dv_s", "sk_p", "sv_p", "sk_s", "sv_s",
                          "S_p", "cv_p", "cv_s")}
    s_s = state_dn_S
    for l in range(depth):
        j = l // N_MIXERS
        bias_op = (ada_b[l].reshape(1, 3 * d), pl.BlockSpec((1, 3 * d), lambda i: (0, 0)))
        mod, = _proj(c_all, ada_w[l].astype(BF16), [(3 * d, _add_bias, [bias_op])], tm=c_all.shape[0],
                     pro=_silu, name="adaln")
        mod_p = mod[:bp].reshape(bp, 1, 3 * d)
        mod_s = mod[SUBLANES:]
        if l % N_MIXERS == 0:
            xp, xs, s_p, cv_p, s_s, cv_s = _dn_layer(
                xp, xs, mod_p, mod_s, seq, state_dn_conv[j], s_s, j, dn_w_in[j], dn_conv_w[j],
                dn_a_log[j], dn_dt_bias[j], dn_norm_g[j], dn_w_out[j], ln_g[l], ln_b[l], alpha)
            st["S_p"].append(s_p)
            st["cv_p"].append(cv_p)
            st["cv_s"].append(cv_s)
        elif l % N_MIXERS == 1:
            xp, xs, k_p, v_p, k_s, v_s = _diff_layer(
                xp, xs, mod_p, mod_s, seq, l, j, cache_diff_k, cache_diff_v, page_table, diff_w_in[j],
                (diff_lq1[j], diff_lk1[j], diff_lq2[j], diff_lk2[j]), diff_subln_g[j], diff_w_out[j],
                ln_g[l], ln_b[l], alpha)
            st["dk_p"].append(k_p)
            st["dv_p"].append(v_p)
            st["dk_s"].append(k_s)
            st["dv_s"].append(v_s)
        else:
            xp, xs, k_p, v_p, k_s, v_s = _swa_layer(
                xp, xs, mod_p, mod_s, seq, cache_swa_k[j], cache_swa_v[j], swa_sinks[j], swa_w_in[j],
                swa_w_out[j], ln_g[l], ln_b[l], alpha)
            st["sk_p"].append(k_p)
            st["sv_p"].append(v_p)
            st["sk_s"].append(k_s)
            st["sv_s"].append(v_s)
    return (xp.reshape(bp, seq, d), xs.reshape(db, 1, d),
            jnp.stack(st["dk_p"]), jnp.stack(st["dv_p"]), jnp.stack(st["dk_s"]), jnp.stack(st["dv_s"]),
            jnp.stack(st["sk_p"]), jnp.stack(st["sv_p"]), jnp.stack(st["sk_s"]), jnp.stack(st["sv_s"]),
            jnp.stack(st["S_p"]), jnp.stack(st["cv_p"]), s_s, jnp.stack(st["cv_s"]))
```

```python
import functools
import math

import jax
import jax.numpy as jnp
from jax import lax
from jax.experimental import pallas as pl
from jax.experimental.pallas import tpu as pltpu

F32 = jnp.float32
BF16 = jnp.bfloat16

LN_EPS = 1e-5
RMS_EPS = 1e-6
N_MIXERS = 3

DN_HK = 8
DN_HV = 16
DN_DK = 128
DN_DV = 128
DN_REP = DN_HV // DN_HK
DN_CONV_W = 4
DN_CHUNK = 128
DN_PREP_ROWS = 256
DN_STEP_SEQS = 4
PROJ_TM = 512

DIFF_H = 8
DIFF_KVH = 4
DIFF_G = DIFF_H // DIFF_KVH
DIFF_HD = 64
DIFF_VD = 2 * DIFF_HD
DIFF_TQ = 512
LOG2E = math.log2(math.e)
DIFF_PAGES_PER_STEP = 32

SWA_H = 16
SWA_KVH = 2
SWA_G = SWA_H // SWA_KVH
SWA_HD = 64
WINDOW = 128
SWA_DEC_BB = 8

LANES = 128
SUBLANES = 8
VMEM_LIMIT = 48 * 1024 * 1024

_NT = (((1,), (1,)), ((), ()))
_TN = (((0,), (0,)), ((), ()))


def _silu(x):
    return (0.5 * x) * (1.0 + jnp.tanh(0.5 * x))


def _proj_body(x_ref, w_ref, *refs, n_pro, segs, pro):
    pro_refs = refs[:n_pro]
    n_epi = sum(s[3] for s in segs)
    epi_refs = refs[n_pro:n_pro + n_epi]
    o_refs = refs[n_pro + n_epi:]
    x = x_ref[...]
    if pro is not None:
        x = pro(x, *pro_refs)
    xb = x.astype(BF16)
    e = 0
    for (start, width, epi, n_e), o_ref in zip(segs, o_refs):
        acc = jnp.dot(xb, w_ref[:, start:start + width], preferred_element_type=F32)
        if epi is not None:
            acc = epi(acc, *epi_refs[e:e + n_e])
        e += n_e
        o_ref[...] = acc


def _proj(x, w, segments, *, tm, pro=None, pro_ops=(), name="proj"):
    m, k = x.shape
    n = w.shape[1]
    assert m % tm == 0, (m, tm)
    ops = [x, w]
    in_specs = [pl.BlockSpec((tm, k), lambda i: (i, 0)),
                pl.BlockSpec((k, n), lambda i: (0, 0), pipeline_mode=pl.Buffered(1))]
    for a, s in pro_ops:
        ops.append(a)
        in_specs.append(s)
    segs = []
    start = 0
    for width, epi, epi_ops in segments:
        segs.append((start, width, epi, len(epi_ops)))
        start += width
        for a, s in epi_ops:
            ops.append(a)
            in_specs.append(s)
    assert start <= n, (start, n)
    return pl.pallas_call(
        functools.partial(_proj_body, n_pro=len(pro_ops), segs=tuple(segs), pro=pro),
        grid=(m // tm,),
        in_specs=in_specs,
        out_specs=[pl.BlockSpec((tm, s[1]), lambda i: (i, 0)) for s in segs],
        out_shape=[jax.ShapeDtypeStruct((m, s[1]), F32) for s in segs],
        compiler_params=pltpu.CompilerParams(dimension_semantics=("parallel",), vmem_limit_bytes=VMEM_LIMIT),
        name=name,
    )(*ops)


def _modulate(x, mod_ref):
    d = x.shape[-1]
    return x * (1.0 + mod_ref[:, d:2 * d]) + mod_ref[:, 0:d]


def _deepnorm(y, xres_ref, mod_ref, g_ref, b_ref, *, alpha):
    d = y.shape[-1]
    h = alpha * xres_ref[...] + (1.0 + mod_ref[:, 2 * d:3 * d]) * y
    hc = h - jnp.mean(h, -1, keepdims=True)
    var = jnp.mean(hc * hc, -1, keepdims=True)
    return hc * lax.rsqrt(var + LN_EPS) * g_ref[...] + b_ref[...]


def _add_bias(y, b_ref):
    return y + b_ref[...]


def _dn_gates(y, alog_ref, dtb_ref):
    lane = lax.broadcasted_iota(jnp.int32, y.shape, 1)
    t = y + dtb_ref[...]
    softplus = jnp.maximum(t, 0.0) + jnp.log1p(jnp.exp(-jnp.abs(t)))
    g = -jnp.exp(alog_ref[...]) * softplus
    return jnp.where(lane < DN_HV, g, jax.nn.sigmoid(y))


def _mod_spec(mod, tm, rows_per_mod):
    w = mod.shape[-1]
    if mod.ndim == 3:
        assert rows_per_mod % tm == 0, (rows_per_mod, tm)
        return pl.BlockSpec((None, 1, w), lambda i: (i * tm // rows_per_mod, 0, 0))
    return pl.BlockSpec((tm, w), lambda i: (i, 0))


def _in_proj(x, mod, rows_per_mod, w, segments, *, tm, name="in_proj"):
    segments = [(s, None, ()) if isinstance(s, int) else s for s in segments]
    return _proj(x, w, segments, tm=tm, pro=_modulate, pro_ops=[(mod, _mod_spec(mod, tm, rows_per_mod))],
                 name=name)


def _out_proj_norm(y, w, xres, mod, rows_per_mod, ln_g, ln_b, *, tm, alpha, name="out_proj"):
    d = w.shape[1]
    row = pl.BlockSpec((1, d), lambda i: (0, 0))
    epi_ops = [(xres, pl.BlockSpec((tm, d), lambda i: (i, 0))), (mod, _mod_spec(mod, tm, rows_per_mod)),
               (ln_g.reshape(1, d), row), (ln_b.reshape(1, d), row)]
    return _proj(y, w, [(d, functools.partial(_deepnorm, alpha=alpha), epi_ops)], tm=tm, name=name)[0]


def _l2norm_heads(y, o_ref, scale):
    for h in range(y.shape[-1] // DN_DK):
        yh = y[:, h * DN_DK:(h + 1) * DN_DK]
        inv = lax.rsqrt(jnp.sum(yh * yh, -1, keepdims=True) + RMS_EPS)
        o_ref[:, h * DN_DK:(h + 1) * DN_DK] = yh * inv * scale


def _dn_prep_body(x_ref, prev_ref, w_ref, o_ref, *, cb):
    i = pl.program_id(1)
    tt = x_ref.shape[0]
    backs = range(1, DN_CONV_W)
    row = lax.broadcasted_iota(jnp.int32, (SUBLANES, cb), 0)

    def rolled(v):
        return tuple(pltpu.roll(v, d, axis=0) for d in backs)

    for c0 in range(0, x_ref.shape[1], cb):
        cols = slice(c0, c0 + cb)
        taps = [w_ref[s:s + 1, cols] for s in range(DN_CONV_W)]

        def body(si, prev_rolled, cols=cols, taps=taps):
            r0 = pl.multiple_of(si * SUBLANES, SUBLANES)
            cur = x_ref[pl.ds(r0, SUBLANES), cols]
            cur_rolled = rolled(cur)
            back = [jnp.where(row < d, p, q) for d, p, q in zip(backs, prev_rolled, cur_rolled)]
            conv = back[DN_CONV_W - 2] * taps[0]
            for s in range(1, DN_CONV_W - 1):
                conv = conv + back[DN_CONV_W - 2 - s] * taps[s]
            o_ref[pl.ds(r0, SUBLANES), cols] = _silu(conv + cur * taps[DN_CONV_W - 1])
            return cur_rolled

        first = jnp.where(i > 0, prev_ref[:, cols], 0.0)
        lax.fori_loop(0, tt // SUBLANES, body, rolled(first), unroll=8)


def _dn_prep(qkv, conv_w, *, tt, cb):
    b, t, c = qkv.shape
    return pl.pallas_call(
        functools.partial(_dn_prep_body, cb=cb),
        grid=(b, t // tt),
        in_specs=[pl.BlockSpec((None, tt, c), lambda bi, i: (bi, i, 0)),
                  pl.BlockSpec((None, SUBLANES, c),
                               lambda bi, i: (bi, jnp.maximum(i * (tt // SUBLANES) - 1, 0), 0)),
                  pl.BlockSpec((DN_CONV_W, c), lambda bi, i: (0, 0))],
        out_specs=pl.BlockSpec((None, tt, c), lambda bi, i: (bi, i, 0)),
        out_shape=jax.ShapeDtypeStruct(qkv.shape, F32),
        compiler_params=pltpu.CompilerParams(dimension_semantics=("parallel", "parallel"),
                                             vmem_limit_bytes=VMEM_LIMIT),
        name="dn_prep",
    )(qkv, qkv, conv_w)


def _gated_rms(o, ng, z):
    return o * lax.rsqrt(jnp.mean(o * o, -1, keepdims=True) + RMS_EPS) * ng * _silu(z)


def _dn_chunk_body(x_ref, z_ref, gb_ref, ng_ref, o_ref, s_ref):
    ci = pl.program_id(1)
    c = x_ref.shape[0]
    hb = DN_HV

    @pl.when(ci == 0)
    def _():
        s_ref[...] = jnp.zeros_like(s_ref)

    k_w = DN_HK * DN_DK
    gb = gb_ref[...]
    row = lax.broadcasted_iota(jnp.int32, gb.shape, 0)
    gc = gb
    sh = 1
    while sh < c:
        gc = gc + jnp.where(row >= sh, pltpu.roll(gc, sh, axis=0), 0.0)
        sh *= 2
    gct = gc.T
    ri = lax.broadcasted_iota(jnp.int32, (c, c), 0)
    cj = lax.broadcasted_iota(jnp.int32, (c, c), 1)
    tril = ri >= cj
    strict = ri > cj
    merge_masks = []
    k = 0
    while (1 << k) < c:
        merge_masks.append(((ri >> (k + 1)) == (cj >> (k + 1))) & ((ri >> k) != (cj >> k)))
        k += 1
    ng = ng_ref[...]
    heads = range(hb)
    dot = functools.partial(jnp.dot, preferred_element_type=F32)
    def l2n(y, scale):
        return y * lax.rsqrt(jnp.sum(y * y, -1, keepdims=True) + RMS_EPS) * scale

    qh = [l2n(x_ref[:, i * DN_DK:(i + 1) * DN_DK], DN_DK ** -0.5) for i in range(DN_HK)]
    kh = [l2n(x_ref[:, k_w + i * DN_DK:k_w + (i + 1) * DN_DK], 1.0) for i in range(DN_HK)]
    kk = [lax.dot_general(a, a, _NT, preferred_element_type=F32) for a in kh]
    qk = [lax.dot_general(a, b, _NT, preferred_element_type=F32) for a, b in zip(qh, kh)]
    gcol = [gc[:, j:j + 1] for j in heads]
    bcol = [gb[:, hb + j:hb + j + 1] for j in heads]
    decay = [jnp.where(tril, jnp.exp(jnp.where(tril, gcol[j] - gct[j:j + 1, :], 0.0)), 0.0) for j in heads]
    lmat = [jnp.where(strict, bcol[j] * kk[j // DN_REP] * decay[j], 0.0) for j in heads]
    def odd_rows(a, blk):
        return jnp.concatenate([a[r:r + blk] for r in range(blk, c, 2 * blk)], axis=0)

    def spread_odd(a, blk):
        zero = jnp.zeros((blk, a.shape[1]), a.dtype)
        return jnp.concatenate([piece for t in range(c // (2 * blk))
                                for piece in (zero, a[t * blk:(t + 1) * blk])], axis=0)

    minv = [jnp.where(merge_masks[0], -lmat[j], 0.0) for j in heads]
    for lvl in range(1, len(merge_masks)):
        blk = 1 << lvl
        if blk < SUBLANES:
            loff = [jnp.where(merge_masks[lvl], lmat[j], 0.0) for j in heads]
            y = [loff[j] + dot(loff[j], minv[j]) for j in heads]
            minv = [minv[j] - y[j] - dot(minv[j], y[j]) for j in heads]
        else:
            rg = lax.broadcasted_iota(jnp.int32, (c // 2, c), 0)
            ro = ((rg >> lvl) << (lvl + 1)) + blk + (rg & (blk - 1))
            co = lax.broadcasted_iota(jnp.int32, (c // 2, c), 1)
            mask = ((ro >> (lvl + 1)) == (co >> (lvl + 1))) & ((ro >> lvl) != (co >> lvl))
            loff = [jnp.where(mask, odd_rows(lmat[j], blk), 0.0) for j in heads]
            y = [loff[j] + dot(loff[j], minv[j]) for j in heads]
            upd = [y[j] + dot(odd_rows(minv[j], blk), spread_odd(y[j], blk)) for j in heads]
            minv = [minv[j] - spread_odd(upd[j], blk) for j in heads]
    eg = [jnp.exp(gcol[j]) for j in heads]
    rhs = [jnp.concatenate([x_ref[:, 2 * k_w + j * DN_DV:2 * k_w + (j + 1) * DN_DV] * bcol[j],
                            kh[j // DN_REP] * (bcol[j] * eg[j])], axis=1) for j in heads]
    sol = [rhs[j] + dot(minv[j], rhs[j]) for j in heads]
    s_old = [s_ref[j] for j in heads]
    ws = [dot(jnp.concatenate([sol[j][:, DN_DV:], qh[j // DN_REP] * eg[j]], axis=0), s_old[j]) for j in heads]
    v_new = [sol[j][:, :DN_DV] - ws[j][:c] for j in heads]
    o = [ws[j][c:] + dot(qk[j // DN_REP] * decay[j], v_new[j]) for j in heads]
    for j in heads:
        glast = gc[c - 1:c, j:j + 1]
        kd = kh[j // DN_REP] * jnp.exp(glast - gcol[j])
        s_ref[j] = s_old[j] * jnp.exp(glast) + lax.dot_general(kd, v_new[j], _TN, preferred_element_type=F32)
        o_ref[:, j * DN_DV:(j + 1) * DN_DV] = _gated_rms(
            o[j], ng, z_ref[:, j * DN_DV:(j + 1) * DN_DV]).astype(o_ref.dtype)


def _dn_chunks(qkv, z, gates, norm_g):
    b, t, cd = qkv.shape
    c = DN_CHUNK
    assert t % c == 0
    v_w = DN_HV * DN_DV
    o, s = pl.pallas_call(
        _dn_chunk_body,
        grid=(b, t // c),
        in_specs=[pl.BlockSpec((None, c, cd), lambda bi, ci: (bi, ci, 0)),
                  pl.BlockSpec((None, c, v_w), lambda bi, ci: (bi, ci, 0)),
                  pl.BlockSpec((None, c, LANES), lambda bi, ci: (bi, ci, 0)),
                  pl.BlockSpec((1, DN_DV), lambda bi, ci: (0, 0))],
        out_specs=[pl.BlockSpec((None, c, v_w), lambda bi, ci: (bi, ci, 0)),
                   pl.BlockSpec((None, DN_HV, DN_DK, DN_DV), lambda bi, ci: (bi, 0, 0, 0))],
        out_shape=[jax.ShapeDtypeStruct((b, t, v_w), BF16),
                   jax.ShapeDtypeStruct((b, DN_HV, DN_DK, DN_DV), F32)],
        compiler_params=pltpu.CompilerParams(dimension_semantics=("parallel", "arbitrary"),
                                             vmem_limit_bytes=VMEM_LIMIT),
        name="dn_chunks",
    )(qkv, z, gates, norm_g.reshape(1, DN_DV))
    return o, s


def _dn_step_prep_body(x_ref, prev_ref, w_ref, o_ref, cv_ref, *, n_q_blk, n_k_blk):
    c = pl.program_id(0)
    x = x_ref[...]
    w = w_ref[...]
    conv = prev_ref[0] * w[0:1]
    for s in range(1, DN_CONV_W - 1):
        conv = conv + prev_ref[s] * w[s:s + 1]
    conv = conv + x * w[DN_CONV_W - 1:DN_CONV_W]
    y = _silu(conv)
    for s in range(DN_CONV_W - 2):
        cv_ref[s] = prev_ref[s + 1]
    cv_ref[DN_CONV_W - 2] = x

    @pl.when(c >= n_q_blk + n_k_blk)
    def _():
        o_ref[...] = y

    @pl.when(c < n_q_blk + n_k_blk)
    def _():
        _l2norm_heads(y, o_ref, jnp.where(c < n_q_blk, DN_DK ** -0.5, 1.0))


def _dn_step_prep(qkv, conv_prev_t, conv_w, *, cb):
    db, c = qkv.shape
    k_w = DN_HK * DN_DK
    nw = DN_CONV_W - 1
    return pl.pallas_call(
        functools.partial(_dn_step_prep_body, n_q_blk=k_w // cb, n_k_blk=k_w // cb),
        grid=(c // cb,),
        in_specs=[pl.BlockSpec((db, cb), lambda ci: (0, ci)),
                  pl.BlockSpec((nw, db, cb), lambda ci: (0, 0, ci)),
                  pl.BlockSpec((DN_CONV_W, cb), lambda ci: (0, ci))],
        out_specs=[pl.BlockSpec((db, cb), lambda ci: (0, ci)),
                   pl.BlockSpec((nw, db, cb), lambda ci: (0, 0, ci))],
        out_shape=[jax.ShapeDtypeStruct((db, c), F32), jax.ShapeDtypeStruct((nw, db, c), F32)],
        compiler_params=pltpu.CompilerParams(dimension_semantics=("parallel",), vmem_limit_bytes=VMEM_LIMIT),
        name="dn_step_prep",
    )(qkv, conv_prev_t, conv_w)


def _dn_step_body(q_ref, k_ref, v_ref, z_ref, gb_ref, ng_ref, s_ref, o_ref, so_ref, *, n_pass):
    g = pl.program_id(0)

    if n_pass:
        @pl.when(g < n_pass)
        def _():
            so_ref[...] = s_ref[...]

    @pl.when(g == n_pass)
    def _():
        ng = ng_ref[...]
        rows = lax.broadcasted_iota(jnp.int32, (SUBLANES, DN_DK), 0)
        heads = [(t, h) for t in range(q_ref.shape[0]) for h in range(DN_HV)]
        s_old = [s_ref[t, h] for t, h in heads]
        kh = [k_ref[t, h // DN_REP:h // DN_REP + 1, :] for t, h in heads]
        qh = [q_ref[t, h // DN_REP:h // DN_REP + 1, :] for t, h in heads]
        ks_qs = [jnp.dot(jnp.where(rows == 0, kh[n], jnp.where(rows == 1, qh[n], 0.0)), s_old[n],
                         preferred_element_type=F32) for n in range(len(heads))]
        eg = [jnp.exp(gb_ref[t, :, h:h + 1]) for t, h in heads]
        v_new = [gb_ref[t, :, DN_HV + h:DN_HV + h + 1] * (v_ref[t, h:h + 1, :] - eg[n] * ks_qs[n][0:1])
                 for n, (t, h) in enumerate(heads)]
        outer = [lax.dot_general(jnp.where(rows == 0, kh[n], 0.0), jnp.where(rows == 0, v_new[n], 0.0), _TN,
                                 preferred_element_type=F32) for n in range(len(heads))]
        for n, (t, h) in enumerate(heads):
            so_ref[t, h] = s_old[n] * eg[n] + outer[n]
            o = eg[n] * ks_qs[n][1:2] + jnp.sum(qh[n] * kh[n], -1, keepdims=True) * v_new[n]
            o_ref[t, h:h + 1, :] = _gated_rms(o, ng, z_ref[t, h:h + 1, :])


def _dn_step(q, k, v, z, gates, norm_g, s_all, layer, *, in_place):
    n, db = s_all.shape[:2]
    bs = DN_STEP_SEQS
    assert db % bs == 0
    n_pass = 0 if in_place else n - 1

    def s_layer(g):
        return layer if in_place else (layer + 1 + g) % n

    def row_spec(rows, width):
        return pl.BlockSpec((bs, rows, width), lambda g, b: (jnp.where(g == n_pass, b, 0), 0, 0))

    s_spec = pl.BlockSpec((None, bs, DN_HV, DN_DK, DN_DV), lambda g, b: (s_layer(g), b, 0, 0, 0))
    return pl.pallas_call(
        functools.partial(_dn_step_body, n_pass=n_pass),
        grid=(n_pass + 1, db // bs),
        in_specs=[row_spec(DN_HK, DN_DK), row_spec(DN_HK, DN_DK), row_spec(DN_HV, DN_DV),
                  row_spec(DN_HV, DN_DV), row_spec(1, LANES),
                  pl.BlockSpec((1, DN_DV), lambda g, b: (0, 0)), s_spec],
        out_specs=[row_spec(DN_HV, DN_DV), s_spec],
        out_shape=[jax.ShapeDtypeStruct((db, DN_HV, DN_DV), F32), jax.ShapeDtypeStruct(s_all.shape, F32)],
        input_output_aliases={6: 1} if in_place else {},
        compiler_params=pltpu.CompilerParams(dimension_semantics=("arbitrary", "arbitrary"),
                                             vmem_limit_bytes=VMEM_LIMIT),
        name="dn_step",
    )(q, k, v, z, gates, norm_g.reshape(1, DN_DV), s_all)


def _dn_layer(xp, xs, mod_p, mod_s, seq, conv_prev, s_all, layer, w_in, conv_w, a_log, dt_bias, norm_g, w_out,
              ln_g, ln_b, alpha):
    mp, d = xp.shape
    bp = mp // seq
    db = xs.shape[0]
    conv_dim = 2 * DN_HK * DN_DK + DN_HV * DN_DV
    v_w = DN_HV * DN_DV
    w_all = jnp.pad(w_in, ((0, 0), (0, LANES - 2 * DN_HV))).astype(BF16)
    w_o = w_out.astype(BF16)
    alog = jnp.pad(a_log, (0, LANES - DN_HV)).reshape(1, LANES)
    dtb = jnp.pad(dt_bias, (0, LANES - DN_HV)).reshape(1, LANES)
    lane_row = pl.BlockSpec((1, LANES), lambda i: (0, 0))
    segments = [conv_dim, v_w, (LANES, _dn_gates, [(alog, lane_row), (dtb, lane_row)])]

    qkv, z, gates = _in_proj(xp, mod_p, seq, w_all, segments, tm=PROJ_TM, name="dn_in_p")
    qkv3 = qkv.reshape(bp, seq, conv_dim)
    cv_p = qkv3[:, seq - (DN_CONV_W - 1):]
    qkvc = _dn_prep(qkv3, conv_w, tt=DN_PREP_ROWS, cb=512)
    o, s_p = _dn_chunks(qkvc, z.reshape(bp, seq, v_w), gates.reshape(bp, seq, LANES), norm_g)
    xp_new = _out_proj_norm(o.reshape(mp, v_w), w_o, xp, mod_p, seq, ln_g, ln_b, tm=PROJ_TM, alpha=alpha,
                            name="dn_out_p")

    qkv_s, z_s, gates_s = _in_proj(xs, mod_s, 1, w_all, segments, tm=db, name="dn_in_s")
    qkvc_s, cv_t = _dn_step_prep(qkv_s, jnp.swapaxes(conv_prev, 0, 1), conv_w, cb=512)
    k_w = DN_HK * DN_DK
    o_s, s_s = _dn_step(qkvc_s[:, :k_w].reshape(db, DN_HK, DN_DK),
                        qkvc_s[:, k_w:2 * k_w].reshape(db, DN_HK, DN_DK),
                        qkvc_s[:, 2 * k_w:].reshape(db, DN_HV, DN_DV),
                        z_s.reshape(db, DN_HV, DN_DV), gates_s.reshape(db, 1, LANES), norm_g, s_all, layer,
                        in_place=layer > 0)
    xs_new = _out_proj_norm(o_s.reshape(db, v_w), w_o, xs, mod_s, 1, ln_g, ln_b, tm=db, alpha=alpha,
                            name="dn_out_s")
    return xp_new, xs_new, s_p, cv_p, s_s, jnp.swapaxes(cv_t, 0, 1)


def _diff_lambda(lq1_ref, lk1_ref, lq2_ref, lk2_ref, lam_init):
    e1 = jnp.exp(jnp.sum(lq1_ref[...] * lk1_ref[...], -1, keepdims=True))
    e2 = jnp.exp(jnp.sum(lq2_ref[...] * lk2_ref[...], -1, keepdims=True))
    return e1 - e2 + lam_init


def _diff_head_out(o, sg, z, lam_init):
    on = o * lax.rsqrt(jnp.mean(o * o, -1, keepdims=True) + RMS_EPS) * sg * (1.0 - lam_init)
    return on * _silu(z)


def _diff_flash_body(qi_ref, kj_ref, q_ref, k_ref, v_ref, z_ref, lq1_ref, lk1_ref, lq2_ref, lk2_ref, sg_ref,
                     o_ref, m_ref, l_ref, acc_ref, *, lam_init):
    i = qi_ref[pl.program_id(2)]
    j = kj_ref[pl.program_id(2)]
    tq, tk = q_ref.shape[0], k_ref.shape[0]
    gw = DIFF_G * tq

    @pl.when(j == 0)
    def _():
        m_ref[...] = jnp.full_like(m_ref, -jnp.inf)
        l_ref[...] = jnp.zeros_like(l_ref)
        acc_ref[...] = jnp.zeros_like(acc_ref)

    def update(masked):
        v = v_ref[...].astype(BF16)
        if masked:
            keep = (lax.broadcasted_iota(jnp.int32, (tk, gw), 0)
                    <= (lax.broadcasted_iota(jnp.int32, (tk, gw), 1) & (tq - 1)))
        sts = []
        for half in range(2):
            qm = jnp.concatenate([q_ref[:, (g * 2 + half) * DIFF_HD:(g * 2 + half + 1) * DIFF_HD]
                                  for g in range(DIFF_G)], axis=0) * (DIFF_HD ** -0.5 * LOG2E)
            st = lax.dot_general(k_ref[:, half * DIFF_HD:(half + 1) * DIFF_HD], qm, _NT,
                                 preferred_element_type=F32)
            sts.append(jnp.where(keep, st, -jnp.inf) if masked else st)
        for half in range(2):
            st = sts[half]
            cols = slice(half * gw, (half + 1) * gw)
            m_prev = m_ref[:, cols]
            m_new = jnp.maximum(m_prev, jnp.max(st, 0, keepdims=True))
            alpha = jnp.exp2(m_prev - m_new)
            p = jnp.exp2(st - m_new)
            l_ref[:, cols] = alpha * l_ref[:, cols] + jnp.sum(p, 0, keepdims=True)
            acc_ref[:, cols] = alpha * acc_ref[:, cols] + lax.dot_general(
                v, p.astype(BF16), _TN, preferred_element_type=F32)
            m_ref[:, cols] = m_new

    @pl.when(j < i)
    def _():
        update(False)

    @pl.when(j == i)
    def _():
        update(True)
        lam = _diff_lambda(lq1_ref, lk1_ref, lq2_ref, lk2_ref, lam_init)
        sg = sg_ref[...]
        for g in range(DIFF_G):
            c0 = slice(g * tq, (g + 1) * tq)
            c1 = slice(gw + g * tq, gw + (g + 1) * tq)
            ot = acc_ref[:, c0] * (1.0 / l_ref[:, c0]) - lam * (acc_ref[:, c1] * (1.0 / l_ref[:, c1]))
            o_ref[:, g * DIFF_VD:(g + 1) * DIFF_VD] = _diff_head_out(
                ot.T, sg, z_ref[:, g * DIFF_VD:(g + 1) * DIFF_VD], lam_init).astype(o_ref.dtype)


def _diff_flash(q, k, v, z, lams, subln_g, lam_init, *, tq):
    b, t, _ = q.shape
    assert t % tq == 0 and tq & (tq - 1) == 0, (t, tq)
    nq = t // tq
    qw = DIFF_G * 2 * DIFF_HD
    ow = DIFF_G * DIFF_VD
    n_sub = 2 * DIFF_G
    pairs = [(i, j) for i in range(nq) for j in range(i + 1)]
    qi = jnp.asarray([p[0] for p in pairs], jnp.int32)
    kj = jnp.asarray([p[1] for p in pairs], jnp.int32)
    vec = pl.BlockSpec((1, DIFF_HD), lambda bi, h, s, qi, kj: (0, 0))
    grid_spec = pltpu.PrefetchScalarGridSpec(
        num_scalar_prefetch=2,
        grid=(b, DIFF_KVH, len(pairs)),
        in_specs=[pl.BlockSpec((None, tq, qw), lambda bi, h, s, qi, kj: (bi, qi[s], h)),
                  pl.BlockSpec((None, tq, 2 * DIFF_HD), lambda bi, h, s, qi, kj: (bi, kj[s], h)),
                  pl.BlockSpec((None, tq, DIFF_VD), lambda bi, h, s, qi, kj: (bi, kj[s], h)),
                  pl.BlockSpec((None, tq, ow), lambda bi, h, s, qi, kj: (bi, qi[s], h)),
                  vec, vec, vec, vec,
                  pl.BlockSpec((1, DIFF_VD), lambda bi, h, s, qi, kj: (0, 0))],
        out_specs=pl.BlockSpec((None, tq, ow), lambda bi, h, s, qi, kj: (bi, qi[s], h)),
        scratch_shapes=[pltpu.VMEM((1, n_sub * tq), F32), pltpu.VMEM((1, n_sub * tq), F32),
                        pltpu.VMEM((DIFF_VD, n_sub * tq), F32)],
    )
    return pl.pallas_call(
        functools.partial(_diff_flash_body, lam_init=lam_init),
        grid_spec=grid_spec,
        out_shape=jax.ShapeDtypeStruct((b, t, DIFF_H * DIFF_VD), BF16),
        compiler_params=pltpu.CompilerParams(
            dimension_semantics=("parallel", "parallel", "arbitrary"),
            vmem_limit_bytes=VMEM_LIMIT),
        name="diff_flash",
    )(qi, kj, q, k, v, z, *lams, subln_g.reshape(1, DIFF_VD))


def _diff_decode_body(pt_ref, q_ref, ks_ref, vs_ref, z_ref, lq1_ref, lk1_ref, lq2_ref, lk2_ref, sg_ref,
                      *rest, pp, lam_init):
    del pt_ref
    k_pages = rest[:pp]
    v_pages = rest[pp:2 * pp]
    o_ref = rest[2 * pp]
    qb_ref, m_ref, l_ref, acc_ref = rest[2 * pp + 1:]
    j = pl.program_id(1)
    n_rows = DIFF_KVH * DIFF_G * 2
    head_shift = (DIFF_G * 2).bit_length() - 1
    kw = 2 * DIFF_HD
    page_rows = k_pages[0].shape[0]

    @pl.when(j == 0)
    def _():
        d_i = lax.broadcasted_iota(jnp.int32, (DIFF_HD, kw), 0)
        c_i = lax.broadcasted_iota(jnp.int32, (DIFF_HD, kw), 1)
        spread = jnp.where((c_i & (DIFF_HD - 1)) == d_i, 1.0, 0.0)
        qt = jnp.dot(q_ref[...], spread, preferred_element_type=F32)
        r_i = lax.broadcasted_iota(jnp.int32, (n_rows, kw), 0)
        half = lax.broadcasted_iota(jnp.int32, (n_rows, kw), 1) >> (DIFF_HD.bit_length() - 1)
        qb_ref[...] = jnp.where(half == (r_i & 1), qt, 0.0) * (DIFF_HD ** -0.5)
        m_ref[...] = jnp.full_like(m_ref, -jnp.inf)
        l_ref[...] = jnp.zeros_like(l_ref)
        acc_ref[...] = jnp.zeros_like(acc_ref)

    qb = qb_ref[...]
    s = jnp.concatenate([lax.dot_general(qb, kp[...], _NT, preferred_element_type=F32) for kp in k_pages],
                        axis=1)
    row_head = lax.broadcasted_iota(jnp.int32, s.shape, 0) >> head_shift
    col_head = lax.broadcasted_iota(jnp.int32, s.shape, 1) & (DIFF_KVH - 1)
    s = jnp.where(row_head == col_head, s, -jnp.inf)
    m_prev = m_ref[...]
    m_new = jnp.maximum(m_prev, jnp.max(s, -1, keepdims=True))
    alpha = jnp.exp(m_prev - m_new)
    p = jnp.exp(s - m_new)
    l_new = alpha * l_ref[...] + jnp.sum(p, -1, keepdims=True)
    pv = jnp.dot(p[:, 0:page_rows], v_pages[0][...], preferred_element_type=F32)
    for t in range(1, pp):
        pv = pv + jnp.dot(p[:, t * page_rows:(t + 1) * page_rows], v_pages[t][...],
                          preferred_element_type=F32)
    acc_new = alpha * acc_ref[...] + pv
    m_ref[...] = m_new
    l_ref[...] = l_new
    acc_ref[...] = acc_new

    @pl.when(j == pl.num_programs(1) - 1)
    def _():
        rh = lax.broadcasted_iota(jnp.int32, (n_rows, kw), 0) >> head_shift
        ks = jnp.zeros((n_rows, kw), F32)
        vs = jnp.zeros((n_rows, kw), F32)
        for h in range(DIFF_KVH):
            ks = jnp.where(rh == h, ks_ref[h:h + 1, :], ks)
            vs = jnp.where(rh == h, vs_ref[h:h + 1, :], vs)
        s_self = jnp.sum(qb * ks, -1, keepdims=True)
        m_fin = jnp.maximum(m_new, s_self)
        a = jnp.exp(m_new - m_fin)
        p_self = jnp.exp(s_self - m_fin)
        l_fin = a * l_new + p_self
        acc = (a * acc_new + p_self * vs) / l_fin
        lam = _diff_lambda(lq1_ref, lk1_ref, lq2_ref, lk2_ref, lam_init)
        sg = sg_ref[...]
        for hh in range(DIFF_H):
            o = acc[2 * hh:2 * hh + 1] - lam * acc[2 * hh + 1:2 * hh + 2]
            o_ref[hh:hh + 1, :] = _diff_head_out(o, sg, z_ref[hh:hh + 1, :], lam_init)


def _diff_decode(q, k_self, v_self, z, cache_k, cache_v, layer, page_table, lams, subln_g, lam_init, *, pp):
    db = q.shape[0]
    n_pages = page_table.shape[1]
    assert n_pages % pp == 0
    page_rows, kw = cache_k.shape[2:]
    n_rows = DIFF_KVH * DIFF_G * 2

    def page_spec(t):
        return pl.BlockSpec((None, None, page_rows, kw), lambda b, j, pt: (layer, pt[b, j * pp + t], 0, 0))

    def seq_spec(shape):
        return pl.BlockSpec((None,) + shape, lambda b, j, pt: (b, 0, 0))

    vec = pl.BlockSpec((1, DIFF_HD), lambda b, j, pt: (0, 0))
    grid_spec = pltpu.PrefetchScalarGridSpec(
        num_scalar_prefetch=1,
        grid=(db, n_pages // pp),
        in_specs=[seq_spec((n_rows, DIFF_HD)), seq_spec((DIFF_KVH, kw)), seq_spec((DIFF_KVH, kw)),
                  seq_spec((DIFF_H, DIFF_VD)), vec, vec, vec, vec,
                  pl.BlockSpec((1, DIFF_VD), lambda b, j, pt: (0, 0))]
                 + [page_spec(t) for t in range(pp)] + [page_spec(t) for t in range(pp)],
        out_specs=seq_spec((DIFF_H, DIFF_VD)),
        scratch_shapes=[pltpu.VMEM((n_rows, kw), F32), pltpu.VMEM((n_rows, 1), F32),
                        pltpu.VMEM((n_rows, 1), F32), pltpu.VMEM((n_rows, DIFF_VD), F32)],
    )
    return pl.pallas_call(
        functools.partial(_diff_decode_body, pp=pp, lam_init=lam_init),
        grid_spec=grid_spec,
        out_shape=jax.ShapeDtypeStruct((db, DIFF_H, DIFF_VD), F32),
        compiler_params=pltpu.CompilerParams(dimension_semantics=("parallel", "arbitrary"),
                                             vmem_limit_bytes=VMEM_LIMIT),
        name="diff_decode",
    )(page_table, q, k_self, v_self, z, *lams, subln_g.reshape(1, DIFF_VD),
      *([cache_k] * pp), *([cache_v] * pp))


def _diff_layer(xp, xs, mod_p, mod_s, seq, layer_idx, j, cache_k, cache_v, page_table, w_in, lams, subln_g,
                w_out, ln_g, ln_b, alpha):
    mp, d = xp.shape
    bp = mp // seq
    db = xs.shape[0]
    lam_init = 0.8 - 0.6 * math.exp(-0.3 * layer_idx)
    q_w = DIFF_H * 2 * DIFF_HD
    k_w = DIFF_KVH * 2 * DIFF_HD
    v_w = DIFF_KVH * DIFF_VD
    w_all = w_in.astype(BF16)
    w_o = w_out.astype(BF16)
    lams = [a.reshape(1, DIFF_HD) for a in lams]
    segments = [q_w, k_w, v_w, w_in.shape[1] - q_w - k_w - v_w]

    q, k, v, z = _in_proj(xp, mod_p, seq, w_all, segments, tm=PROJ_TM, name="diff_in_p")
    o = _diff_flash(q.reshape(bp, seq, q_w), k.reshape(bp, seq, k_w), v.reshape(bp, seq, v_w),
                    z.reshape(bp, seq, -1), lams, subln_g, lam_init, tq=min(DIFF_TQ, seq))
    xp_new = _out_proj_norm(o.reshape(mp, -1), w_o, xp, mod_p, seq, ln_g, ln_b, tm=PROJ_TM, alpha=alpha,
                            name="diff_out_p")

    q_s, k_s, v_s, z_s = _in_proj(xs, mod_s, 1, w_all, segments, tm=db, name="diff_in_s")
    n_l, n_pool, page = cache_k.shape[:3]
    o_s = _diff_decode(q_s.reshape(db, DIFF_KVH * DIFF_G * 2, DIFF_HD), k_s.reshape(db, DIFF_KVH, 2 * DIFF_HD),
                       v_s.reshape(db, DIFF_KVH, DIFF_VD), z_s.reshape(db, DIFF_H, DIFF_VD),
                       cache_k.reshape(n_l, n_pool, page * DIFF_KVH, 2 * DIFF_HD),
                       cache_v.reshape(n_l, n_pool, page * DIFF_KVH, DIFF_VD),
                       j, page_table, lams, subln_g, lam_init, pp=DIFF_PAGES_PER_STEP)
    xs_new = _out_proj_norm(o_s.reshape(db, -1), w_o, xs, mod_s, 1, ln_g, ln_b, tm=db, alpha=alpha,
                            name="diff_out_s")
    return (xp_new, xs_new, k.reshape(bp, seq, DIFF_KVH, 2 * DIFF_HD), v.reshape(bp, seq, DIFF_KVH, DIFF_VD),
            k_s.reshape(db, 1, DIFF_KVH, 2 * DIFF_HD), v_s.reshape(db, 1, DIFF_KVH, DIFF_VD))


def _swa_band_body(q_ref, kc_ref, kp_ref, vc_ref, vp_ref, z_ref, sink_ref, o_ref):
    i = pl.program_id(1)
    blk = q_ref.shape[0]
    cols = SWA_G * blk
    kj = lax.broadcasted_iota(jnp.int32, (2 * blk, cols), 0)
    qi = lax.broadcasted_iota(jnp.int32, (2 * blk, cols), 1) & (blk - 1)
    dist = qi + blk - kj
    keep = (dist >= 0) & (dist <= WINDOW) & ((i > 0) | (kj >= blk))
    for h in range(SWA_KVH):
        hs = slice(h * SWA_HD, (h + 1) * SWA_HD)
        k2 = jnp.concatenate([kp_ref[:, hs], kc_ref[:, hs]], axis=0)
        v2 = jnp.concatenate([vp_ref[:, hs], vc_ref[:, hs]], axis=0)
        qs = jnp.concatenate([q_ref[:, (h * SWA_G + g) * SWA_HD:(h * SWA_G + g + 1) * SWA_HD]
                              for g in range(SWA_G)], axis=0) * (SWA_HD ** -0.5)
        sink = jnp.concatenate([jnp.broadcast_to(sink_ref[:, h * SWA_G + g:h * SWA_G + g + 1], (1, blk))
                                for g in range(SWA_G)], axis=1)
        st = lax.dot_general(k2, qs, _NT, preferred_element_type=F32)
        st = jnp.where(keep, st, -jnp.inf)
        m = jnp.maximum(jnp.max(st, 0, keepdims=True), sink)
        p = jnp.exp(st - m)
        den = jnp.sum(p, 0, keepdims=True) + jnp.exp(sink - m)
        ot = lax.dot_general(v2, p, _TN, preferred_element_type=F32) * (1.0 / den)
        for g in range(0, SWA_G, 2):
            cs = slice((h * SWA_G + g) * SWA_HD, (h * SWA_G + g + 2) * SWA_HD)
            pair = jnp.concatenate([ot[:, g * blk:(g + 1) * blk], ot[:, (g + 1) * blk:(g + 2) * blk]], axis=0)
            o_ref[:, cs] = (pair.T * _silu(z_ref[:, cs])).astype(o_ref.dtype)


def _swa_band(q, k, v, z, sinks):
    b, t, qw = q.shape
    blk = WINDOW
    kw = SWA_KVH * SWA_HD
    cur = pl.BlockSpec((None, blk, kw), lambda bi, i: (bi, i, 0))
    prev = pl.BlockSpec((None, blk, kw), lambda bi, i: (bi, jnp.maximum(i - 1, 0), 0))
    wide = pl.BlockSpec((None, blk, qw), lambda bi, i: (bi, i, 0))
    return pl.pallas_call(
        _swa_band_body,
        grid=(b, t // blk),
        in_specs=[wide, cur, prev, cur, prev, wide, pl.BlockSpec((1, SWA_H), lambda bi, i: (0, 0))],
        out_specs=wide,
        out_shape=jax.ShapeDtypeStruct((b, t, qw), BF16),
        compiler_params=pltpu.CompilerParams(dimension_semantics=("parallel", "parallel"),
                                             vmem_limit_bytes=VMEM_LIMIT),
        name="swa_band",
    )(q, k, k, v, v, z, sinks.reshape(1, SWA_H))


def _swa_decode_body(q_ref, kb_ref, vb_ref, kn_ref, vn_ref, z_ref, sink_ref, o_ref, ko_ref, vo_ref):
    win = kb_ref.shape[1]
    for h in range(SWA_KVH):
        hs = slice(h * SWA_HD, (h + 1) * SWA_HD)
        gs = slice(h * SWA_G, (h + 1) * SWA_G)
        qh = q_ref[:, gs, :]
        s = jnp.einsum("bqd,bkd->bqk", qh, kb_ref[:, :, hs], preferred_element_type=F32) * (SWA_HD ** -0.5)
        s_new = jnp.sum(qh * kn_ref[:, :, hs], -1, keepdims=True) * (SWA_HD ** -0.5)
        sink = sink_ref[gs, :][None]
        m = jnp.maximum(jnp.maximum(jnp.max(s, -1, keepdims=True), s_new), sink)
        p = jnp.exp(s - m)
        p_new = jnp.exp(s_new - m)
        den = jnp.sum(p, -1, keepdims=True) + p_new + jnp.exp(sink - m)
        o = jnp.einsum("bqk,bkd->bqd", p, vb_ref[:, :, hs], preferred_element_type=F32)
        o = (o + p_new * vn_ref[:, :, hs]) / den
        o_ref[:, gs, :] = o * _silu(z_ref[:, gs, :])
    ko_ref[:, 0:win - 1, :] = kb_ref[:, 1:win, :]
    ko_ref[:, win - 1:win, :] = kn_ref[...]
    vo_ref[:, 0:win - 1, :] = vb_ref[:, 1:win, :]
    vo_ref[:, win - 1:win, :] = vn_ref[...]


def _swa_decode(q, k_new, v_new, z, buf_k, buf_v, sinks, *, bb):
    db, win, kw = buf_k.shape
    qz = pl.BlockSpec((bb, SWA_H, SWA_HD), lambda b: (b, 0, 0))
    buf = pl.BlockSpec((bb, win, kw), lambda b: (b, 0, 0))
    new = pl.BlockSpec((bb, 1, kw), lambda b: (b, 0, 0))
    return pl.pallas_call(
        _swa_decode_body,
        grid=(db // bb,),
        in_specs=[qz, buf, buf, new, new, qz, pl.BlockSpec((SWA_H, 1), lambda b: (0, 0))],
        out_specs=[qz, buf, buf],
        out_shape=[jax.ShapeDtypeStruct(q.shape, F32), jax.ShapeDtypeStruct(buf_k.shape, F32),
                   jax.ShapeDtypeStruct(buf_v.shape, F32)],
        compiler_params=pltpu.CompilerParams(dimension_semantics=("parallel",), vmem_limit_bytes=VMEM_LIMIT),
        name="swa_decode",
    )(q, buf_k, buf_v, k_new, v_new, z, sinks.reshape(SWA_H, 1))


def _swa_layer(xp, xs, mod_p, mod_s, seq, buf_k, buf_v, sinks, w_in, w_out, ln_g, ln_b, alpha):
    mp, d = xp.shape
    bp = mp // seq
    db = xs.shape[0]
    q_w = SWA_H * SWA_HD
    kv_w = SWA_KVH * SWA_HD
    w_all = w_in.astype(BF16)
    w_o = w_out.astype(BF16)
    segments = [q_w, kv_w, kv_w, w_in.shape[1] - q_w - 2 * kv_w]

    q, k, v, z = _in_proj(xp, mod_p, seq, w_all, segments, tm=PROJ_TM, name="swa_in_p")
    k = k.reshape(bp, seq, kv_w)
    v = v.reshape(bp, seq, kv_w)
    o = _swa_band(q.reshape(bp, seq, q_w), k, v, z.reshape(bp, seq, q_w), sinks)
    xp_new = _out_proj_norm(o.reshape(mp, q_w), w_o, xp, mod_p, seq, ln_g, ln_b, tm=PROJ_TM, alpha=alpha,
                            name="swa_out_p")

    q_s, k_s, v_s, z_s = _in_proj(xs, mod_s, 1, w_all, segments, tm=db, name="swa_in_s")
    win = buf_k.shape[1]
    o_s, k_buf, v_buf = _swa_decode(q_s.reshape(db, SWA_H, SWA_HD), k_s.reshape(db, 1, kv_w),
                                    v_s.reshape(db, 1, kv_w), z_s.reshape(db, SWA_H, SWA_HD),
                                    buf_k.reshape(db, win, kv_w), buf_v.reshape(db, win, kv_w), sinks,
                                    bb=SWA_DEC_BB)
    xs_new = _out_proj_norm(o_s.reshape(db, q_w), w_o, xs, mod_s, 1, ln_g, ln_b, tm=db, alpha=alpha,
                            name="swa_out_s")
    kv_shape = (SWA_KVH, SWA_HD)
    return (xp_new, xs_new, k[:, seq - WINDOW:].reshape((bp, WINDOW) + kv_shape),
            v[:, seq - WINDOW:].reshape((bp, WINDOW) + kv_shape),
            k_buf.reshape((db, win) + kv_shape), v_buf.reshape((db, win) + kv_shape))


def kernel(x_prompt, x_sample, cache_diff_k, cache_diff_v, page_table, cache_swa_k, cache_swa_v, state_dn_S, state_dn_conv, c_prompt, c_sample, ada_w, ada_b, ln_g, ln_b, dn_w_in, dn_conv_w, dn_a_log, dn_dt_bias, dn_norm_g, dn_w_out, diff_w_in, diff_lq1, diff_lk1, diff_lq2, diff_lk2, diff_subln_g, diff_w_out, swa_w_in, swa_sinks, swa_w_out):
    bp, seq, d = x_prompt.shape
    db, dec_seq, _ = x_sample.shape
    assert dec_seq == 1 and bp <= SUBLANES
    depth = ada_w.shape[0]
    alpha = (2 * depth) ** 0.25
    xp = x_prompt.reshape(bp * seq, d)
    xs = x_sample.reshape(db, d)

    c_all = jnp.concatenate([c_prompt, jnp.zeros((SUBLANES - bp, d), F32), c_sample], axis=0)
    st = {n: [] for n in ("dk_p", "dv_p", "dk_s", "dv_s", "sk_p", "sv_p", "sk_s", "sv_s",
                          "S_p", "cv_p", "cv_s")}
    s_s = state_dn_S
    for l in range(depth):
        j = l // N_MIXERS
        bias_op = (ada_b[l].reshape(1, 3 * d), pl.BlockSpec((1, 3 * d), lambda i: (0, 0)))
        mod, = _proj(c_all, ada_w[l].astype(BF16), [(3 * d, _add_bias, [bias_op])], tm=c_all.shape[0],
                     pro=_silu, name="adaln")
        mod_p = mod[:bp].reshape(bp, 1, 3 * d)
        mod_s = mod[SUBLANES:]
        if l % N_MIXERS == 0:
            xp, xs, s_p, cv_p, s_s, cv_s = _dn_layer(
                xp, xs, mod_p, mod_s, seq, state_dn_conv[j], s_s, j, dn_w_in[j], dn_conv_w[j],
                dn_a_log[j], dn_dt_bias[j], dn_norm_g[j], dn_w_out[j], ln_g[l], ln_b[l], alpha)
            st["S_p"].append(s_p)
            st["cv_p"].append(cv_p)
            st["cv_s"].append(cv_s)
        elif l % N_MIXERS == 1:
            xp, xs, k_p, v_p, k_s, v_s = _diff_layer(
                xp, xs, mod_p, mod_s, seq, l, j, cache_diff_k, cache_diff_v, page_table, diff_w_in[j],
                (diff_lq1[j], diff_lk1[j], diff_lq2[j], diff_lk2[j]), diff_subln_g[j], diff_w_out[j],
                ln_g[l], ln_b[l], alpha)
            st["dk_p"].append(k_p)
            st["dv_p"].append(v_p)
            st["dk_s"].append(k_s)
            st["dv_s"].append(v_s)
        else:
            xp, xs, k_p, v_p, k_s, v_s = _swa_layer(
                xp, xs, mod_p, mod_s, seq, cache_swa_k[j], cache_swa_v[j], swa_sinks[j], swa_w_in[j],
                swa_w_out[j], ln_g[l], ln_b[l], alpha)
            st["sk_p"].append(k_p)
            st["sv_p"].append(v_p)
            st["sk_s"].append(k_s)
            st["sv_s"].append(v_s)
    return (xp.reshape(bp, seq, d), xs.reshape(db, 1, d),
            jnp.stack(st["dk_p"]), jnp.stack(st["dv_p"]), jnp.stack(st["dk_s"]), jnp.stack(st["dv_s"]),
            jnp.stack(st["sk_p"]), jnp.stack(st["sv_p"]), jnp.stack(st["sk_s"]), jnp.stack(st["sv_s"]),
            jnp.stack(st["S_p"]), jnp.stack(st["cv_p"]), s_s, jnp.stack(st["cv_s"]))
```

```python
import functools
import math

import jax
import jax.numpy as jnp
from jax import lax
from jax.experimental import pallas as pl
from jax.experimental.pallas import tpu as pltpu

F32 = jnp.float32
BF16 = jnp.bfloat16

LN_EPS = 1e-5
RMS_EPS = 1e-6
N_MIXERS = 3

DN_HK = 8
DN_HV = 16
DN_DK = 128
DN_DV = 128
DN_REP = DN_HV // DN_HK
DN_CONV_W = 4
DN_CHUNK = 128
DN_CONV_COLS = 512
DN_STEP_SEQS = 4
PROJ_TM = 512

DIFF_H = 8
DIFF_KVH = 4
DIFF_G = DIFF_H // DIFF_KVH
DIFF_HD = 64
DIFF_VD = 2 * DIFF_HD
DIFF_TQ = 512
LOG2E = math.log2(math.e)
DIFF_PAGES_PER_STEP = 32

SWA_H = 16
SWA_KVH = 2
SWA_G = SWA_H // SWA_KVH
SWA_HD = 64
WINDOW = 128
SWA_DEC_BB = 8

LANES = 128
SUBLANES = 8
VMEM_LIMIT = 48 * 1024 * 1024

_NT = (((1,), (1,)), ((), ()))
_TN = (((0,), (0,)), ((), ()))


def _silu(x):
    return (0.5 * x) * (1.0 + jnp.tanh(0.5 * x))


def _proj_body(x_ref, w_ref, *refs, n_pro, segs, pro):
    pro_refs = refs[:n_pro]
    n_epi = sum(s[3] for s in segs)
    epi_refs = refs[n_pro:n_pro + n_epi]
    o_refs = refs[n_pro + n_epi:]
    x = x_ref[...]
    if pro is not None:
        x = pro(x, *pro_refs)
    xb = x.astype(BF16)
    e = 0
    for (start, width, epi, n_e), o_ref in zip(segs, o_refs):
        acc = jnp.dot(xb, w_ref[:, start:start + width], preferred_element_type=F32)
        if epi is not None:
            acc = epi(acc, *epi_refs[e:e + n_e])
        e += n_e
        o_ref[...] = acc


def _proj(x, w, segments, *, tm, pro=None, pro_ops=(), name="proj"):
    m, k = x.shape
    n = w.shape[1]
    assert m % tm == 0, (m, tm)
    ops = [x, w]
    in_specs = [pl.BlockSpec((tm, k), lambda i: (i, 0)),
                pl.BlockSpec((k, n), lambda i: (0, 0), pipeline_mode=pl.Buffered(1))]
    for a, s in pro_ops:
        ops.append(a)
        in_specs.append(s)
    segs = []
    start = 0
    for width, epi, epi_ops in segments:
        segs.append((start, width, epi, len(epi_ops)))
        start += width
        for a, s in epi_ops:
            ops.append(a)
            in_specs.append(s)
    assert start <= n, (start, n)
    return pl.pallas_call(
        functools.partial(_proj_body, n_pro=len(pro_ops), segs=tuple(segs), pro=pro),
        grid=(m // tm,),
        in_specs=in_specs,
        out_specs=[pl.BlockSpec((tm, s[1]), lambda i: (i, 0)) for s in segs],
        out_shape=[jax.ShapeDtypeStruct((m, s[1]), F32) for s in segs],
        compiler_params=pltpu.CompilerParams(dimension_semantics=("parallel",), vmem_limit_bytes=VMEM_LIMIT),
        name=name,
    )(*ops)


def _modulate(x, mod_ref):
    d = x.shape[-1]
    return x * (1.0 + mod_ref[:, d:2 * d]) + mod_ref[:, 0:d]


def _deepnorm(y, xres_ref, mod_ref, g_ref, b_ref, *, alpha):
    d = y.shape[-1]
    h = alpha * xres_ref[...] + (1.0 + mod_ref[:, 2 * d:3 * d]) * y
    hc = h - jnp.mean(h, -1, keepdims=True)
    var = jnp.mean(hc * hc, -1, keepdims=True)
    return hc * lax.rsqrt(var + LN_EPS) * g_ref[...] + b_ref[...]


def _add_bias(y, b_ref):
    return y + b_ref[...]


def _dn_gates(y, alog_ref, dtb_ref):
    lane = lax.broadcasted_iota(jnp.int32, y.shape, 1)
    t = y + dtb_ref[...]
    softplus = jnp.maximum(t, 0.0) + jnp.log1p(jnp.exp(-jnp.abs(t)))
    g = -jnp.exp(alog_ref[...]) * softplus
    return jnp.where(lane < DN_HV, g, jax.nn.sigmoid(y))


def _mod_spec(mod, tm, rows_per_mod):
    w = mod.shape[-1]
    if mod.ndim == 3:
        assert rows_per_mod % tm == 0, (rows_per_mod, tm)
        return pl.BlockSpec((None, 1, w), lambda i: (i * tm // rows_per_mod, 0, 0))
    return pl.BlockSpec((tm, w), lambda i: (i, 0))


def _in_proj(x, mod, rows_per_mod, w, segments, *, tm, name="in_proj"):
    segments = [(s, None, ()) if isinstance(s, int) else s for s in segments]
    return _proj(x, w, segments, tm=tm, pro=_modulate, pro_ops=[(mod, _mod_spec(mod, tm, rows_per_mod))],
                 name=name)


def _out_proj_norm(y, w, xres, mod, rows_per_mod, ln_g, ln_b, *, tm, alpha, name="out_proj"):
    d = w.shape[1]
    row = pl.BlockSpec((1, d), lambda i: (0, 0))
    epi_ops = [(xres, pl.BlockSpec((tm, d), lambda i: (i, 0))), (mod, _mod_spec(mod, tm, rows_per_mod)),
               (ln_g.reshape(1, d), row), (ln_b.reshape(1, d), row)]
    return _proj(y, w, [(d, functools.partial(_deepnorm, alpha=alpha), epi_ops)], tm=tm, name=name)[0]


def _l2norm_heads(y, o_ref, scale):
    for h in range(y.shape[-1] // DN_DK):
        yh = y[:, h * DN_DK:(h + 1) * DN_DK]
        inv = lax.rsqrt(jnp.sum(yh * yh, -1, keepdims=True) + RMS_EPS)
        o_ref[:, h * DN_DK:(h + 1) * DN_DK] = yh * inv * scale


def _dn_in_conv_body(x_ref, w_ref, mod_ref, cw_ref, alog_ref, dtb_ref, qkv_ref, z_ref, g_ref, tail_ref,
                     halo_ref, *, tiles_per_seq, cb):
    i = pl.program_id(0)
    tm = x_ref.shape[0]
    conv_dim = qkv_ref.shape[1]
    v_w = z_ref.shape[1]
    xb = _modulate(x_ref[...], mod_ref).astype(BF16)
    first = i % tiles_per_seq == 0
    backs = range(1, DN_CONV_W)
    row = lax.broadcasted_iota(jnp.int32, (SUBLANES, cb), 0)

    def rolled(v):
        return [pltpu.roll(v, d, axis=0) for d in backs]

    def project(c0):
        return jnp.dot(xb, w_ref[:, c0:c0 + cb], preferred_element_type=F32)

    acc_next = project(0)
    for c0 in range(0, conv_dim, cb):
        cols = slice(c0, c0 + cb)
        acc = acc_next
        if c0 + cb < conv_dim:
            acc_next = project(c0 + cb)
        else:
            z_ref[...] = jnp.dot(xb, w_ref[:, conv_dim:conv_dim + v_w], preferred_element_type=F32)
            g_ref[...] = _dn_gates(jnp.dot(xb, w_ref[:, conv_dim + v_w:conv_dim + v_w + LANES],
                                           preferred_element_type=F32), alog_ref, dtb_ref)
        taps = [cw_ref[s:s + 1, cols] for s in range(DN_CONV_W)]
        prev_rolled = rolled(jnp.where(first, 0.0, halo_ref[:, cols]))
        for r0 in range(0, tm, SUBLANES):
            cur = acc[r0:r0 + SUBLANES]
            cur_rolled = rolled(cur)
            back = [jnp.where(row < d, p, q) for d, p, q in zip(backs, prev_rolled, cur_rolled)]
            conv = back[DN_CONV_W - 2] * taps[0]
            for s in range(1, DN_CONV_W - 1):
                conv = conv + back[DN_CONV_W - 2 - s] * taps[s]
            qkv_ref[r0:r0 + SUBLANES, cols] = _silu(conv + cur * taps[DN_CONV_W - 1])
            prev_rolled = cur_rolled
        last = acc[tm - SUBLANES:tm]
        halo_ref[:, cols] = last
        tail_ref[:, cols] = last


def _dn_in_conv(x, mod, seq, w, conv_w, alog, dtb, *, tm, cb):
    m, k = x.shape
    conv_dim = conv_w.shape[1]
    v_w = DN_HV * DN_DV
    assert seq % tm == 0 and m % seq == 0
    tiles_per_seq = seq // tm
    const = lambda i: (0, 0)
    rows = lambda i: (i, 0)
    return pl.pallas_call(
        functools.partial(_dn_in_conv_body, tiles_per_seq=tiles_per_seq, cb=cb),
        grid=(m // tm,),
        in_specs=[pl.BlockSpec((tm, k), rows),
                  pl.BlockSpec(w.shape, const, pipeline_mode=pl.Buffered(1)),
                  pl.BlockSpec((None, 1, mod.shape[-1]), lambda i: (i // tiles_per_seq, 0, 0)),
                  pl.BlockSpec(conv_w.shape, const), pl.BlockSpec((1, LANES), const),
                  pl.BlockSpec((1, LANES), const)],
        out_specs=[pl.BlockSpec((tm, conv_dim), rows), pl.BlockSpec((tm, v_w), rows),
                   pl.BlockSpec((tm, LANES), rows),
                   pl.BlockSpec((None, SUBLANES, conv_dim), lambda i: (i // tiles_per_seq, 0, 0))],
        out_shape=[jax.ShapeDtypeStruct((m, conv_dim), F32), jax.ShapeDtypeStruct((m, v_w), F32),
                   jax.ShapeDtypeStruct((m, LANES), F32),
                   jax.ShapeDtypeStruct((m // seq, SUBLANES, conv_dim), F32)],
        scratch_shapes=[pltpu.VMEM((SUBLANES, conv_dim), F32)],
        compiler_params=pltpu.CompilerParams(dimension_semantics=("arbitrary",), vmem_limit_bytes=VMEM_LIMIT),
        name="dn_in_conv_p",
    )(x, w, mod, conv_w, alog, dtb)


def _gated_rms(o, ng, z):
    return o * lax.rsqrt(jnp.mean(o * o, -1, keepdims=True) + RMS_EPS) * ng * _silu(z)


def _dn_chunk_body(x_ref, z_ref, gb_ref, ng_ref, o_ref, s_ref):
    ci = pl.program_id(1)
    c = x_ref.shape[0]
    hb = DN_HV

    @pl.when(ci == 0)
    def _():
        s_ref[...] = jnp.zeros_like(s_ref)

    k_w = DN_HK * DN_DK
    gb = gb_ref[...]
    row = lax.broadcasted_iota(jnp.int32, gb.shape, 0)
    gc = gb
    sh = 1
    while sh < c:
        gc = gc + jnp.where(row >= sh, pltpu.roll(gc, sh, axis=0), 0.0)
        sh *= 2
    gct = gc.T
    ri = lax.broadcasted_iota(jnp.int32, (c, c), 0)
    cj = lax.broadcasted_iota(jnp.int32, (c, c), 1)
    tril = ri >= cj
    strict = ri > cj
    merge_masks = []
    k = 0
    while (1 << k) < c:
        merge_masks.append(((ri >> (k + 1)) == (cj >> (k + 1))) & ((ri >> k) != (cj >> k)))
        k += 1
    ng = ng_ref[...]
    heads = range(hb)
    dot = functools.partial(jnp.dot, preferred_element_type=F32)
    def l2n(y, scale):
        return y * lax.rsqrt(jnp.sum(y * y, -1, keepdims=True) + RMS_EPS) * scale

    qh = [l2n(x_ref[:, i * DN_DK:(i + 1) * DN_DK], DN_DK ** -0.5) for i in range(DN_HK)]
    kh = [l2n(x_ref[:, k_w + i * DN_DK:k_w + (i + 1) * DN_DK], 1.0) for i in range(DN_HK)]
    kk = [lax.dot_general(a, a, _NT, preferred_element_type=F32) for a in kh]
    qk = [lax.dot_general(a, b, _NT, preferred_element_type=F32) for a, b in zip(qh, kh)]
    gcol = [gc[:, j:j + 1] for j in heads]
    bcol = [gb[:, hb + j:hb + j + 1] for j in heads]
    decay = [jnp.where(tril, jnp.exp(jnp.where(tril, gcol[j] - gct[j:j + 1, :], 0.0)), 0.0) for j in heads]
    lmat = [jnp.where(strict, bcol[j] * kk[j // DN_REP] * decay[j], 0.0) for j in heads]
    def odd_rows(a, blk):
        return jnp.concatenate([a[r:r + blk] for r in range(blk, c, 2 * blk)], axis=0)

    def spread_odd(a, blk):
        zero = jnp.zeros((blk, a.shape[1]), a.dtype)
        return jnp.concatenate([piece for t in range(c // (2 * blk))
                                for piece in (zero, a[t * blk:(t + 1) * blk])], axis=0)

    minv = [jnp.where(merge_masks[0], -lmat[j], 0.0) for j in heads]
    for lvl in range(1, len(merge_masks)):
        blk = 1 << lvl
        if blk < SUBLANES:
            loff = [jnp.where(merge_masks[lvl], lmat[j], 0.0) for j in heads]
            y = [loff[j] + dot(loff[j], minv[j]) for j in heads]
            minv = [minv[j] - y[j] - dot(minv[j], y[j]) for j in heads]
        else:
            rg = lax.broadcasted_iota(jnp.int32, (c // 2, c), 0)
            ro = ((rg >> lvl) << (lvl + 1)) + blk + (rg & (blk - 1))
            co = lax.broadcasted_iota(jnp.int32, (c // 2, c), 1)
            mask = ((ro >> (lvl + 1)) == (co >> (lvl + 1))) & ((ro >> lvl) != (co >> lvl))
            loff = [jnp.where(mask, odd_rows(lmat[j], blk), 0.0) for j in heads]
            y = [loff[j] + dot(loff[j], minv[j]) for j in heads]
            upd = [y[j] + dot(odd_rows(minv[j], blk), spread_odd(y[j], blk)) for j in heads]
            minv = [minv[j] - spread_odd(upd[j], blk) for j in heads]
    eg = [jnp.exp(gcol[j]) for j in heads]
    rhs = [jnp.concatenate([x_ref[:, 2 * k_w + j * DN_DV:2 * k_w + (j + 1) * DN_DV] * bcol[j],
                            kh[j // DN_REP] * (bcol[j] * eg[j])], axis=1) for j in heads]
    sol = [rhs[j] + dot(minv[j], rhs[j]) for j in heads]
    s_old = [s_ref[j] for j in heads]
    ws = [dot(jnp.concatenate([sol[j][:, DN_DV:], qh[j // DN_REP] * eg[j]], axis=0), s_old[j]) for j in heads]
    v_new = [sol[j][:, :DN_DV] - ws[j][:c] for j in heads]
    o = [ws[j][c:] + dot(qk[j // DN_REP] * decay[j], v_new[j]) for j in heads]
    for j in heads:
        glast = gc[c - 1:c, j:j + 1]
        kd = kh[j // DN_REP] * jnp.exp(glast - gcol[j])
        s_ref[j] = s_old[j] * jnp.exp(glast) + lax.dot_general(kd, v_new[j], _TN, preferred_element_type=F32)
        o_ref[:, j * DN_DV:(j + 1) * DN_DV] = _gated_rms(
            o[j], ng, z_ref[:, j * DN_DV:(j + 1) * DN_DV]).astype(o_ref.dtype)


def _dn_chunks(qkv, z, gates, norm_g):
    b, t, cd = qkv.shape
    c = DN_CHUNK
    assert t % c == 0
    v_w = DN_HV * DN_DV
    o, s = pl.pallas_call(
        _dn_chunk_body,
        grid=(b, t // c),
        in_specs=[pl.BlockSpec((None, c, cd), lambda bi, ci: (bi, ci, 0)),
                  pl.BlockSpec((None, c, v_w), lambda bi, ci: (bi, ci, 0)),
                  pl.BlockSpec((None, c, LANES), lambda bi, ci: (bi, ci, 0)),
                  pl.BlockSpec((1, DN_DV), lambda bi, ci: (0, 0))],
        out_specs=[pl.BlockSpec((None, c, v_w), lambda bi, ci: (bi, ci, 0)),
                   pl.BlockSpec((None, DN_HV, DN_DK, DN_DV), lambda bi, ci: (bi, 0, 0, 0))],
        out_shape=[jax.ShapeDtypeStruct((b, t, v_w), BF16),
                   jax.ShapeDtypeStruct((b, DN_HV, DN_DK, DN_DV), F32)],
        compiler_params=pltpu.CompilerParams(dimension_semantics=("parallel", "arbitrary"),
                                             vmem_limit_bytes=VMEM_LIMIT),
        name="dn_chunks",
    )(qkv, z, gates, norm_g.reshape(1, DN_DV))
    return o, s


def _dn_step_prep_body(x_ref, prev_ref, w_ref, o_ref, cv_ref, *, n_q_blk, n_k_blk):
    c = pl.program_id(0)
    x = x_ref[...]
    w = w_ref[...]
    conv = prev_ref[0] * w[0:1]
    for s in range(1, DN_CONV_W - 1):
        conv = conv + prev_ref[s] * w[s:s + 1]
    conv = conv + x * w[DN_CONV_W - 1:DN_CONV_W]
    y = _silu(conv)
    for s in range(DN_CONV_W - 2):
        cv_ref[s] = prev_ref[s + 1]
    cv_ref[DN_CONV_W - 2] = x

    @pl.when(c >= n_q_blk + n_k_blk)
    def _():
        o_ref[...] = y

    @pl.when(c < n_q_blk + n_k_blk)
    def _():
        _l2norm_heads(y, o_ref, jnp.where(c < n_q_blk, DN_DK ** -0.5, 1.0))


def _dn_step_prep(qkv, conv_prev_t, conv_w, *, cb):
    db, c = qkv.shape
    k_w = DN_HK * DN_DK
    nw = DN_CONV_W - 1
    return pl.pallas_call(
        functools.partial(_dn_step_prep_body, n_q_blk=k_w // cb, n_k_blk=k_w // cb),
        grid=(c // cb,),
        in_specs=[pl.BlockSpec((db, cb), lambda ci: (0, ci)),
                  pl.BlockSpec((nw, db, cb), lambda ci: (0, 0, ci)),
                  pl.BlockSpec((DN_CONV_W, cb), lambda ci: (0, ci))],
        out_specs=[pl.BlockSpec((db, cb), lambda ci: (0, ci)),
                   pl.BlockSpec((nw, db, cb), lambda ci: (0, 0, ci))],
        out_shape=[jax.ShapeDtypeStruct((db, c), F32), jax.ShapeDtypeStruct((nw, db, c), F32)],
        compiler_params=pltpu.CompilerParams(dimension_semantics=("parallel",), vmem_limit_bytes=VMEM_LIMIT),
        name="dn_step_prep",
    )(qkv, conv_prev_t, conv_w)


def _dn_step_body(q_ref, k_ref, v_ref, z_ref, gb_ref, ng_ref, s_ref, o_ref, so_ref, *, n_pass):
    g = pl.program_id(0)

    if n_pass:
        @pl.when(g < n_pass)
        def _():
            so_ref[...] = s_ref[...]

    @pl.when(g == n_pass)
    def _():
        ng = ng_ref[...]
        rows = lax.broadcasted_iota(jnp.int32, (SUBLANES, DN_DK), 0)
        heads = [(t, h) for t in range(q_ref.shape[0]) for h in range(DN_HV)]
        s_old = [s_ref[t, h] for t, h in heads]
        kh = [k_ref[t, h // DN_REP:h // DN_REP + 1, :] for t, h in heads]
        qh = [q_ref[t, h // DN_REP:h // DN_REP + 1, :] for t, h in heads]
        ks_qs = [jnp.dot(jnp.where(rows == 0, kh[n], jnp.where(rows == 1, qh[n], 0.0)), s_old[n],
                         preferred_element_type=F32) for n in range(len(heads))]
        eg = [jnp.exp(gb_ref[t, :, h:h + 1]) for t, h in heads]
        v_new = [gb_ref[t, :, DN_HV + h:DN_HV + h + 1] * (v_ref[t, h:h + 1, :] - eg[n] * ks_qs[n][0:1])
                 for n, (t, h) in enumerate(heads)]
        outer = [lax.dot_general(jnp.where(rows == 0, kh[n], 0.0), jnp.where(rows == 0, v_new[n], 0.0), _TN,
                                 preferred_element_type=F32) for n in range(len(heads))]
        for n, (t, h) in enumerate(heads):
            so_ref[t, h] = s_old[n] * eg[n] + outer[n]
            o = eg[n] * ks_qs[n][1:2] + jnp.sum(qh[n] * kh[n], -1, keepdims=True) * v_new[n]
            o_ref[t, h:h + 1, :] = _gated_rms(o, ng, z_ref[t, h:h + 1, :])


def _dn_step(q, k, v, z, gates, norm_g, s_all, layer, *, in_place):
    n, db = s_all.shape[:2]
    bs = DN_STEP_SEQS
    assert db % bs == 0
    n_pass = 0 if in_place else n - 1

    def s_layer(g):
        return layer if in_place else (layer + 1 + g) % n

    def row_spec(rows, width):
        return pl.BlockSpec((bs, rows, width), lambda g, b: (jnp.where(g == n_pass, b, 0), 0, 0))

    s_spec = pl.BlockSpec((None, bs, DN_HV, DN_DK, DN_DV), lambda g, b: (s_layer(g), b, 0, 0, 0))
    return pl.pallas_call(
        functools.partial(_dn_step_body, n_pass=n_pass),
        grid=(n_pass + 1, db // bs),
        in_specs=[row_spec(DN_HK, DN_DK), row_spec(DN_HK, DN_DK), row_spec(DN_HV, DN_DV),
                  row_spec(DN_HV, DN_DV), row_spec(1, LANES),
                  pl.BlockSpec((1, DN_DV), lambda g, b: (0, 0)), s_spec],
        out_specs=[row_spec(DN_HV, DN_DV), s_spec],
        out_shape=[jax.ShapeDtypeStruct((db, DN_HV, DN_DV), F32), jax.ShapeDtypeStruct(s_all.shape, F32)],
        input_output_aliases={6: 1} if in_place else {},
        compiler_params=pltpu.CompilerParams(dimension_semantics=("arbitrary", "arbitrary"),
                                             vmem_limit_bytes=VMEM_LIMIT),
        name="dn_step",
    )(q, k, v, z, gates, norm_g.reshape(1, DN_DV), s_all)


def _dn_layer(xp, xs, mod_p, mod_s, seq, conv_prev, s_all, layer, w_in, conv_w, a_log, dt_bias, norm_g, w_out,
              ln_g, ln_b, alpha):
    mp, d = xp.shape
    bp = mp // seq
    db = xs.shape[0]
    conv_dim = 2 * DN_HK * DN_DK + DN_HV * DN_DV
    v_w = DN_HV * DN_DV
    w_all = jnp.pad(w_in, ((0, 0), (0, LANES - 2 * DN_HV))).astype(BF16)
    w_o = w_out.astype(BF16)
    alog = jnp.pad(a_log, (0, LANES - DN_HV)).reshape(1, LANES)
    dtb = jnp.pad(dt_bias, (0, LANES - DN_HV)).reshape(1, LANES)
    lane_row = pl.BlockSpec((1, LANES), lambda i: (0, 0))
    segments = [conv_dim, v_w, (LANES, _dn_gates, [(alog, lane_row), (dtb, lane_row)])]

    qkvc, z, gates, tail = _dn_in_conv(xp, mod_p, seq, w_all, conv_w, alog, dtb, tm=PROJ_TM, cb=DN_CONV_COLS)
    cv_p = tail[:, SUBLANES - (DN_CONV_W - 1):]
    o, s_p = _dn_chunks(qkvc.reshape(bp, seq, conv_dim), z.reshape(bp, seq, v_w), gates.reshape(bp, seq, LANES), norm_g)
    xp_new = _out_proj_norm(o.reshape(mp, v_w), w_o, xp, mod_p, seq, ln_g, ln_b, tm=PROJ_TM, alpha=alpha,
                            name="dn_out_p")

    qkv_s, z_s, gates_s = _in_proj(xs, mod_s, 1, w_all, segments, tm=db, name="dn_in_s")
    qkvc_s, cv_t = _dn_step_prep(qkv_s, jnp.swapaxes(conv_prev, 0, 1), conv_w, cb=512)
    k_w = DN_HK * DN_DK
    o_s, s_s = _dn_step(qkvc_s[:, :k_w].reshape(db, DN_HK, DN_DK),
                        qkvc_s[:, k_w:2 * k_w].reshape(db, DN_HK, DN_DK),
                        qkvc_s[:, 2 * k_w:].reshape(db, DN_HV, DN_DV),
                        z_s.reshape(db, DN_HV, DN_DV), gates_s.reshape(db, 1, LANES), norm_g, s_all, layer,
                        in_place=layer > 0)
    xs_new = _out_proj_norm(o_s.reshape(db, v_w), w_o, xs, mod_s, 1, ln_g, ln_b, tm=db, alpha=alpha,
                            name="dn_out_s")
    return xp_new, xs_new, s_p, cv_p, s_s, jnp.swapaxes(cv_t, 0, 1)


def _diff_lambda(lq1_ref, lk1_ref, lq2_ref, lk2_ref, lam_init):
    e1 = jnp.exp(jnp.sum(lq1_ref[...] * lk1_ref[...], -1, keepdims=True))
    e2 = jnp.exp(jnp.sum(lq2_ref[...] * lk2_ref[...], -1, keepdims=True))
    return e1 - e2 + lam_init


def _diff_head_out(o, sg, z, lam_init):
    on = o * lax.rsqrt(jnp.mean(o * o, -1, keepdims=True) + RMS_EPS) * sg * (1.0 - lam_init)
    return on * _silu(z)


def _diff_flash_body(qi_ref, kj_ref, q_ref, k_ref, v_ref, z_ref, lq1_ref, lk1_ref, lq2_ref, lk2_ref, sg_ref,
                     o_ref, m_ref, l_ref, acc_ref, *, lam_init):
    i = qi_ref[pl.program_id(2)]
    j = kj_ref[pl.program_id(2)]
    tq, tk = q_ref.shape[0], k_ref.shape[0]
    gw = DIFF_G * tq

    @pl.when(j == 0)
    def _():
        m_ref[...] = jnp.full_like(m_ref, -jnp.inf)
        l_ref[...] = jnp.zeros_like(l_ref)
        acc_ref[...] = jnp.zeros_like(acc_ref)

    def update(masked):
        v = v_ref[...].astype(BF16)
        if masked:
            keep = (lax.broadcasted_iota(jnp.int32, (tk, gw), 0)
                    <= (lax.broadcasted_iota(jnp.int32, (tk, gw), 1) & (tq - 1)))
        sts = []
        for half in range(2):
            qm = jnp.concatenate([q_ref[:, (g * 2 + half) * DIFF_HD:(g * 2 + half + 1) * DIFF_HD]
                                  for g in range(DIFF_G)], axis=0) * (DIFF_HD ** -0.5 * LOG2E)
            st = lax.dot_general(k_ref[:, half * DIFF_HD:(half + 1) * DIFF_HD], qm, _NT,
                                 preferred_element_type=F32)
            sts.append(jnp.where(keep, st, -jnp.inf) if masked else st)
        for half in range(2):
            st = sts[half]
            cols = slice(half * gw, (half + 1) * gw)
            m_prev = m_ref[:, cols]
            m_new = jnp.maximum(m_prev, jnp.max(st, 0, keepdims=True))
            alpha = jnp.exp2(m_prev - m_new)
            p = jnp.exp2(st - m_new)
            l_ref[:, cols] = alpha * l_ref[:, cols] + jnp.sum(p, 0, keepdims=True)
            acc_ref[:, cols] = alpha * acc_ref[:, cols] + lax.dot_general(
                v, p.astype(BF16), _TN, preferred_element_type=F32)
            m_ref[:, cols] = m_new

    @pl.when(j < i)
    def _():
        update(False)

    @pl.when(j == i)
    def _():
        update(True)
        lam = _diff_lambda(lq1_ref, lk1_ref, lq2_ref, lk2_ref, lam_init)
        sg = sg_ref[...]
        for g in range(DIFF_G):
            c0 = slice(g * tq, (g + 1) * tq)
            c1 = slice(gw + g * tq, gw + (g + 1) * tq)
            ot = acc_ref[:, c0] * (1.0 / l_ref[:, c0]) - lam * (acc_ref[:, c1] * (1.0 / l_ref[:, c1]))
            o_ref[:, g * DIFF_VD:(g + 1) * DIFF_VD] = _diff_head_out(
                ot.T, sg, z_ref[:, g * DIFF_VD:(g + 1) * DIFF_VD], lam_init).astype(o_ref.dtype)


def _diff_flash(q, k, v, z, lams, subln_g, lam_init, *, tq):
    b, t, _ = q.shape
    assert t % tq == 0 and tq & (tq - 1) == 0, (t, tq)
    nq = t // tq
    qw = DIFF_G * 2 * DIFF_HD
    ow = DIFF_G * DIFF_VD
    n_sub = 2 * DIFF_G
    pairs = [(i, j) for i in range(nq) for j in range(i + 1)]
    qi = jnp.asarray([p[0] for p in pairs], jnp.int32)
    kj = jnp.asarray([p[1] for p in pairs], jnp.int32)
    vec = pl.BlockSpec((1, DIFF_HD), lambda bi, h, s, qi, kj: (0, 0))
    grid_spec = pltpu.PrefetchScalarGridSpec(
        num_scalar_prefetch=2,
        grid=(b, DIFF_KVH, len(pairs)),
        in_specs=[pl.BlockSpec((None, tq, qw), lambda bi, h, s, qi, kj: (bi, qi[s], h)),
                  pl.BlockSpec((None, tq, 2 * DIFF_HD), lambda bi, h, s, qi, kj: (bi, kj[s], h)),
                  pl.BlockSpec((None, tq, DIFF_VD), lambda bi, h, s, qi, kj: (bi, kj[s], h)),
                  pl.BlockSpec((None, tq, ow), lambda bi, h, s, qi, kj: (bi, qi[s], h)),
                  vec, vec, vec, vec,
                  pl.BlockSpec((1, DIFF_VD), lambda bi, h, s, qi, kj: (0, 0))],
        out_specs=pl.BlockSpec((None, tq, ow), lambda bi, h, s, qi, kj: (bi, qi[s], h)),
        scratch_shapes=[pltpu.VMEM((1, n_sub * tq), F32), pltpu.VMEM((1, n_sub * tq), F32),
                        pltpu.VMEM((DIFF_VD, n_sub * tq), F32)],
    )
    return pl.pallas_call(
        functools.partial(_diff_flash_body, lam_init=lam_init),
        grid_spec=grid_spec,
        out_shape=jax.ShapeDtypeStruct((b, t, DIFF_H * DIFF_VD), BF16),
        compiler_params=pltpu.CompilerParams(
            dimension_semantics=("parallel", "parallel", "arbitrary"),
            vmem_limit_bytes=VMEM_LIMIT),
        name="diff_flash",
    )(qi, kj, q, k, v, z, *lams, subln_g.reshape(1, DIFF_VD))


def _diff_decode_body(pt_ref, q_ref, ks_ref, vs_ref, z_ref, lq1_ref, lk1_ref, lq2_ref, lk2_ref, sg_ref,
                      *rest, pp, lam_init):
    del pt_ref
    k_pages = rest[:pp]
    v_pages = rest[pp:2 * pp]
    o_ref = rest[2 * pp]
    qb_ref, m_ref, l_ref, acc_ref = rest[2 * pp + 1:]
    j = pl.program_id(1)
    n_rows = DIFF_KVH * DIFF_G * 2
    head_shift = (DIFF_G * 2).bit_length() - 1
    kw = 2 * DIFF_HD
    page_rows = k_pages[0].shape[0]

    @pl.when(j == 0)
    def _():
        d_i = lax.broadcasted_iota(jnp.int32, (DIFF_HD, kw), 0)
        c_i = lax.broadcasted_iota(jnp.int32, (DIFF_HD, kw), 1)
        spread = jnp.where((c_i & (DIFF_HD - 1)) == d_i, 1.0, 0.0)
        qt = jnp.dot(q_ref[...], spread, preferred_element_type=F32)
        r_i = lax.broadcasted_iota(jnp.int32, (n_rows, kw), 0)
        half = lax.broadcasted_iota(jnp.int32, (n_rows, kw), 1) >> (DIFF_HD.bit_length() - 1)
        qb_ref[...] = jnp.where(half == (r_i & 1), qt, 0.0) * (DIFF_HD ** -0.5)
        m_ref[...] = jnp.full_like(m_ref, -jnp.inf)
        l_ref[...] = jnp.zeros_like(l_ref)
        acc_ref[...] = jnp.zeros_like(acc_ref)

    qb = qb_ref[...]
    s = jnp.concatenate([lax.dot_general(qb, kp[...], _NT, preferred_element_type=F32) for kp in k_pages],
                        axis=1)
    row_head = lax.broadcasted_iota(jnp.int32, s.shape, 0) >> head_shift
    col_head = lax.broadcasted_iota(jnp.int32, s.shape, 1) & (DIFF_KVH - 1)
    s = jnp.where(row_head == col_head, s, -jnp.inf)
    m_prev = m_ref[...]
    m_new = jnp.maximum(m_prev, jnp.max(s, -1, keepdims=True))
    alpha = jnp.exp(m_prev - m_new)
    p = jnp.exp(s - m_new)
    l_new = alpha * l_ref[...] + jnp.sum(p, -1, keepdims=True)
    pv = jnp.dot(p[:, 0:page_rows], v_pages[0][...], preferred_element_type=F32)
    for t in range(1, pp):
        pv = pv + jnp.dot(p[:, t * page_rows:(t + 1) * page_rows], v_pages[t][...],
                          preferred_element_type=F32)
    acc_new = alpha * acc_ref[...] + pv
    m_ref[...] = m_new
    l_ref[...] = l_new
    acc_ref[...] = acc_new

    @pl.when(j == pl.num_programs(1) - 1)
    def _():
        rh = lax.broadcasted_iota(jnp.int32, (n_rows, kw), 0) >> head_shift
        ks = jnp.zeros((n_rows, kw), F32)
        vs = jnp.zeros((n_rows, kw), F32)
        for h in range(DIFF_KVH):
            ks = jnp.where(rh == h, ks_ref[h:h + 1, :], ks)
            vs = jnp.where(rh == h, vs_ref[h:h + 1, :], vs)
        s_self = jnp.sum(qb * ks, -1, keepdims=True)
        m_fin = jnp.maximum(m_new, s_self)
        a = jnp.exp(m_new - m_fin)
        p_self = jnp.exp(s_self - m_fin)
        l_fin = a * l_new + p_self
        acc = (a * acc_new + p_self * vs) / l_fin
        lam = _diff_lambda(lq1_ref, lk1_ref, lq2_ref, lk2_ref, lam_init)
        sg = sg_ref[...]
        for hh in range(DIFF_H):
            o = acc[2 * hh:2 * hh + 1] - lam * acc[2 * hh + 1:2 * hh + 2]
            o_ref[hh:hh + 1, :] = _diff_head_out(o, sg, z_ref[hh:hh + 1, :], lam_init)


def _diff_decode(q, k_self, v_self, z, cache_k, cache_v, layer, page_table, lams, subln_g, lam_init, *, pp):
    db = q.shape[0]
    n_pages = page_table.shape[1]
    assert n_pages % pp == 0
    page_rows, kw = cache_k.shape[2:]
    n_rows = DIFF_KVH * DIFF_G * 2

    def page_spec(t):
        return pl.BlockSpec((None, None, page_rows, kw), lambda b, j, pt: (layer, pt[b, j * pp + t], 0, 0))

    def seq_spec(shape):
        return pl.BlockSpec((None,) + shape, lambda b, j, pt: (b, 0, 0))

    vec = pl.BlockSpec((1, DIFF_HD), lambda b, j, pt: (0, 0))
    grid_spec = pltpu.PrefetchScalarGridSpec(
        num_scalar_prefetch=1,
        grid=(db, n_pages // pp),
        in_specs=[seq_spec((n_rows, DIFF_HD)), seq_spec((DIFF_KVH, kw)), seq_spec((DIFF_KVH, kw)),
                  seq_spec((DIFF_H, DIFF_VD)), vec, vec, vec, vec,
                  pl.BlockSpec((1, DIFF_VD), lambda b, j, pt: (0, 0))]
                 + [page_spec(t) for t in range(pp)] + [page_spec(t) for t in range(pp)],
        out_specs=seq_spec((DIFF_H, DIFF_VD)),
        scratch_shapes=[pltpu.VMEM((n_rows, kw), F32), pltpu.VMEM((n_rows, 1), F32),
                        pltpu.VMEM((n_rows, 1), F32), pltpu.VMEM((n_rows, DIFF_VD), F32)],
    )
    return pl.pallas_call(
        functools.partial(_diff_decode_body, pp=pp, lam_init=lam_init),
        grid_spec=grid_spec,
        out_shape=jax.ShapeDtypeStruct((db, DIFF_H, DIFF_VD), F32),
        compiler_params=pltpu.CompilerParams(dimension_semantics=("parallel", "arbitrary"),
                                             vmem_limit_bytes=VMEM_LIMIT),
        name="diff_decode",
    )(page_table, q, k_self, v_self, z, *lams, subln_g.reshape(1, DIFF_VD),
      *([cache_k] * pp), *([cache_v] * pp))


def _diff_layer(xp, xs, mod_p, mod_s, seq, layer_idx, j, cache_k, cache_v, page_table, w_in, lams, subln_g,
                w_out, ln_g, ln_b, alpha):
    mp, d = xp.shape
    bp = mp // seq
    db = xs.shape[0]
    lam_init = 0.8 - 0.6 * math.exp(-0.3 * layer_idx)
    q_w = DIFF_H * 2 * DIFF_HD
    k_w = DIFF_KVH * 2 * DIFF_HD
    v_w = DIFF_KVH * DIFF_VD
    w_all = w_in.astype(BF16)
    w_o = w_out.astype(BF16)
    lams = [a.reshape(1, DIFF_HD) for a in lams]
    segments = [q_w, k_w, v_w, w_in.shape[1] - q_w - k_w - v_w]

    q, k, v, z = _in_proj(xp, mod_p, seq, w_all, segments, tm=PROJ_TM, name="diff_in_p")
    o = _diff_flash(q.reshape(bp, seq, q_w), k.reshape(bp, seq, k_w), v.reshape(bp, seq, v_w),
                    z.reshape(bp, seq, -1), lams, subln_g, lam_init, tq=min(DIFF_TQ, seq))
    xp_new = _out_proj_norm(o.reshape(mp, -1), w_o, xp, mod_p, seq, ln_g, ln_b, tm=PROJ_TM, alpha=alpha,
                            name="diff_out_p")

    q_s, k_s, v_s, z_s = _in_proj(xs, mod_s, 1, w_all, segments, tm=db, name="diff_in_s")
    n_l, n_pool, page = cache_k.shape[:3]
    o_s = _diff_decode(q_s.reshape(db, DIFF_KVH * DIFF_G * 2, DIFF_HD), k_s.reshape(db, DIFF_KVH, 2 * DIFF_HD),
                       v_s.reshape(db, DIFF_KVH, DIFF_VD), z_s.reshape(db, DIFF_H, DIFF_VD),
                       cache_k.reshape(n_l, n_pool, page * DIFF_KVH, 2 * DIFF_HD),
                       cache_v.reshape(n_l, n_pool, page * DIFF_KVH, DIFF_VD),
                       j, page_table, lams, subln_g, lam_init, pp=DIFF_PAGES_PER_STEP)
    xs_new = _out_proj_norm(o_s.reshape(db, -1), w_o, xs, mod_s, 1, ln_g, ln_b, tm=db, alpha=alpha,
                            name="diff_out_s")
    return (xp_new, xs_new, k.reshape(bp, seq, DIFF_KVH, 2 * DIFF_HD), v.reshape(bp, seq, DIFF_KVH, DIFF_VD),
            k_s.reshape(db, 1, DIFF_KVH, 2 * DIFF_HD), v_s.reshape(db, 1, DIFF_KVH, DIFF_VD))


def _swa_band_body(q_ref, kc_ref, kp_ref, vc_ref, vp_ref, z_ref, sink_ref, o_ref):
    i = pl.program_id(1)
    blk = q_ref.shape[0]
    cols = SWA_G * blk
    kj = lax.broadcasted_iota(jnp.int32, (2 * blk, cols), 0)
    qi = lax.broadcasted_iota(jnp.int32, (2 * blk, cols), 1) & (blk - 1)
    dist = qi + blk - kj
    keep = (dist >= 0) & (dist <= WINDOW) & ((i > 0) | (kj >= blk))
    for h in range(SWA_KVH):
        hs = slice(h * SWA_HD, (h + 1) * SWA_HD)
        k2 = jnp.concatenate([kp_ref[:, hs], kc_ref[:, hs]], axis=0)
        v2 = jnp.concatenate([vp_ref[:, hs], vc_ref[:, hs]], axis=0)
        qs = jnp.concatenate([q_ref[:, (h * SWA_G + g) * SWA_HD:(h * SWA_G + g + 1) * SWA_HD]
                              for g in range(SWA_G)], axis=0) * (SWA_HD ** -0.5)
        sink = jnp.concatenate([jnp.broadcast_to(sink_ref[:, h * SWA_G + g:h * SWA_G + g + 1], (1, blk))
                                for g in range(SWA_G)], axis=1)
        st = lax.dot_general(k2, qs, _NT, preferred_element_type=F32)
        st = jnp.where(keep, st, -jnp.inf)
        m = jnp.maximum(jnp.max(st, 0, keepdims=True), sink)
        p = jnp.exp(st - m)
        den = jnp.sum(p, 0, keepdims=True) + jnp.exp(sink - m)
        ot = lax.dot_general(v2, p, _TN, preferred_element_type=F32) * (1.0 / den)
        for g in range(0, SWA_G, 2):
            cs = slice((h * SWA_G + g) * SWA_HD, (h * SWA_G + g + 2) * SWA_HD)
            pair = jnp.concatenate([ot[:, g * blk:(g + 1) * blk], ot[:, (g + 1) * blk:(g + 2) * blk]], axis=0)
            o_ref[:, cs] = (pair.T * _silu(z_ref[:, cs])).astype(o_ref.dtype)


def _swa_band(q, k, v, z, sinks):
    b, t, qw = q.shape
    blk = WINDOW
    kw = SWA_KVH * SWA_HD
    cur = pl.BlockSpec((None, blk, kw), lambda bi, i: (bi, i, 0))
    prev = pl.BlockSpec((None, blk, kw), lambda bi, i: (bi, jnp.maximum(i - 1, 0), 0))
    wide = pl.BlockSpec((None, blk, qw), lambda bi, i: (bi, i, 0))
    return pl.pallas_call(
        _swa_band_body,
        grid=(b, t // blk),
        in_specs=[wide, cur, prev, cur, prev, wide, pl.BlockSpec((1, SWA_H), lambda bi, i: (0, 0))],
        out_specs=wide,
        out_shape=jax.ShapeDtypeStruct((b, t, qw), BF16),
        compiler_params=pltpu.CompilerParams(dimension_semantics=("parallel", "parallel"),
                                             vmem_limit_bytes=VMEM_LIMIT),
        name="swa_band",
    )(q, k, k, v, v, z, sinks.reshape(1, SWA_H))


def _swa_decode_body(q_ref, kb_ref, vb_ref, kn_ref, vn_ref, z_ref, sink_ref, o_ref, ko_ref, vo_ref):
    win = kb_ref.shape[1]
    for h in range(SWA_KVH):
        hs = slice(h * SWA_HD, (h + 1) * SWA_HD)
        gs = slice(h * SWA_G, (h + 1) * SWA_G)
        qh = q_ref[:, gs, :]
        s = jnp.einsum("bqd,bkd->bqk", qh, kb_ref[:, :, hs], preferred_element_type=F32) * (SWA_HD ** -0.5)
        s_new = jnp.sum(qh * kn_ref[:, :, hs], -1, keepdims=True) * (SWA_HD ** -0.5)
        sink = sink_ref[gs, :][None]
        m = jnp.maximum(jnp.maximum(jnp.max(s, -1, keepdims=True), s_new), sink)
        p = jnp.exp(s - m)
        p_new = jnp.exp(s_new - m)
        den = jnp.sum(p, -1, keepdims=True) + p_new + jnp.exp(sink - m)
        o = jnp.einsum("bqk,bkd->bqd", p, vb_ref[:, :, hs], preferred_element_type=F32)
        o = (o + p_new * vn_ref[:, :, hs]) / den
        o_ref[:, gs, :] = o * _silu(z_ref[:, gs, :])
    ko_ref[:, 0:win - 1, :] = kb_ref[:, 1:win, :]
    ko_ref[:, win - 1:win, :] = kn_ref[...]
    vo_ref[:, 0:win - 1, :] = vb_ref[:, 1:win, :]
    vo_ref[:, win - 1:win, :] = vn_ref[...]


def _swa_decode(q, k_new, v_new, z, buf_k, buf_v, sinks, *, bb):
    db, win, kw = buf_k.shape
    qz = pl.BlockSpec((bb, SWA_H, SWA_HD), lambda b: (b, 0, 0))
    buf = pl.BlockSpec((bb, win, kw), lambda b: (b, 0, 0))
    new = pl.BlockSpec((bb, 1, kw), lambda b: (b, 0, 0))
    return pl.pallas_call(
        _swa_decode_body,
        grid=(db // bb,),
        in_specs=[qz, buf, buf, new, new, qz, pl.BlockSpec((SWA_H, 1), lambda b: (0, 0))],
        out_specs=[qz, buf, buf],
        out_shape=[jax.ShapeDtypeStruct(q.shape, F32), jax.ShapeDtypeStruct(buf_k.shape, F32),
                   jax.ShapeDtypeStruct(buf_v.shape, F32)],
        compiler_params=pltpu.CompilerParams(dimension_semantics=("parallel",), vmem_limit_bytes=VMEM_LIMIT),
        name="swa_decode",
    )(q, buf_k, buf_v, k_new, v_new, z, sinks.reshape(SWA_H, 1))


def _swa_layer(xp, xs, mod_p, mod_s, seq, buf_k, buf_v, sinks, w_in, w_out, ln_g, ln_b, alpha):
    mp, d = xp.shape
    bp = mp // seq
    db = xs.shape[0]
    q_w = SWA_H * SWA_HD
    kv_w = SWA_KVH * SWA_HD
    w_all = w_in.astype(BF16)
    w_o = w_out.astype(BF16)
    segments = [q_w, kv_w, kv_w, w_in.shape[1] - q_w - 2 * kv_w]

    q, k, v, z = _in_proj(xp, mod_p, seq, w_all, segments, tm=PROJ_TM, name="swa_in_p")
    k = k.reshape(bp, seq, kv_w)
    v = v.reshape(bp, seq, kv_w)
    o = _swa_band(q.reshape(bp, seq, q_w), k, v, z.reshape(bp, seq, q_w), sinks)
    xp_new = _out_proj_norm(o.reshape(mp, q_w), w_o, xp, mod_p, seq, ln_g, ln_b, tm=PROJ_TM, alpha=alpha,
                            name="swa_out_p")

    q_s, k_s, v_s, z_s = _in_proj(xs, mod_s, 1, w_all, segments, tm=db, name="swa_in_s")
    win = buf_k.shape[1]
    o_s, k_buf, v_buf = _swa_decode(q_s.reshape(db, SWA_H, SWA_HD), k_s.reshape(db, 1, kv_w),
                                    v_s.reshape(db, 1, kv_w), z_s.reshape(db, SWA_H, SWA_HD),
                                    buf_k.reshape(db, win, kv_w), buf_v.reshape(db, win, kv_w), sinks,
                                    bb=SWA_DEC_BB)
    xs_new = _out_proj_norm(o_s.reshape(db, q_w), w_o, xs, mod_s, 1, ln_g, ln_b, tm=db, alpha=alpha,
                            name="swa_out_s")
    kv_shape = (SWA_KVH, SWA_HD)
    return (xp_new, xs_new, k[:, seq - WINDOW:].reshape((bp, WINDOW) + kv_shape),
            v[:, seq - WINDOW:].reshape((bp, WINDOW) + kv_shape),
            k_buf.reshape((db, win) + kv_shape), v_buf.reshape((db, win) + kv_shape))


def kernel(x_prompt, x_sample, cache_diff_k, cache_diff_v, page_table, cache_swa_k, cache_swa_v, state_dn_S, state_dn_conv, c_prompt, c_sample, ada_w, ada_b, ln_g, ln_b, dn_w_in, dn_conv_w, dn_a_log, dn_dt_bias, dn_norm_g, dn_w_out, diff_w_in, diff_lq1, diff_lk1, diff_lq2, diff_lk2, diff_subln_g, diff_w_out, swa_w_in, swa_sinks, swa_w_out):
    bp, seq, d = x_prompt.shape
    db, dec_seq, _ = x_sample.shape
    assert dec_seq == 1 and bp <= SUBLANES
    depth = ada_w.shape[0]
    alpha = (2 * depth) ** 0.25
    xp = x_prompt.reshape(bp * seq, d)
    xs = x_sample.reshape(db, d)

    c_all = jnp.concatenate([c_prompt, jnp.zeros((SUBLANES - bp, d), F32), c_sample], axis=0)
    st = {n: [] for n in ("dk_p", "dv_p", "dk_s", "dv_s", "sk_p", "sv_p", "sk_s", "sv_s",
                          "S_p", "cv_p", "cv_s")}
    s_s = state_dn_S
    for l in range(depth):
        j = l // N_MIXERS
        bias_op = (ada_b[l].reshape(1, 3 * d), pl.BlockSpec((1, 3 * d), lambda i: (0, 0)))
        mod, = _proj(c_all, ada_w[l].astype(BF16), [(3 * d, _add_bias, [bias_op])], tm=c_all.shape[0],
                     pro=_silu, name="adaln")
        mod_p = mod[:bp].reshape(bp, 1, 3 * d)
        mod_s = mod[SUBLANES:]
        if l % N_MIXERS == 0:
            xp, xs, s_p, cv_p, s_s, cv_s = _dn_layer(
                xp, xs, mod_p, mod_s, seq, state_dn_conv[j], s_s, j, dn_w_in[j], dn_conv_w[j],
                dn_a_log[j], dn_dt_bias[j], dn_norm_g[j], dn_w_out[j], ln_g[l], ln_b[l], alpha)
            st["S_p"].append(s_p)
            st["cv_p"].append(cv_p)
            st["cv_s"].append(cv_s)
        elif l % N_MIXERS == 1:
            xp, xs, k_p, v_p, k_s, v_s = _diff_layer(
                xp, xs, mod_p, mod_s, seq, l, j, cache_diff_k, cache_diff_v, page_table, diff_w_in[j],
                (diff_lq1[j], diff_lk1[j], diff_lq2[j], diff_lk2[j]), diff_subln_g[j], diff_w_out[j],
                ln_g[l], ln_b[l], alpha)
            st["dk_p"].append(k_p)
            st["dv_p"].append(v_p)
            st["dk_s"].append(k_s)
            st["dv_s"].append(v_s)
        else:
            xp, xs, k_p, v_p, k_s, v_s = _swa_layer(
                xp, xs, mod_p, mod_s, seq, cache_swa_k[j], cache_swa_v[j], swa_sinks[j], swa_w_in[j],
                swa_w_out[j], ln_g[l], ln_b[l], alpha)
            st["sk_p"].append(k_p)
            st["sv_p"].append(v_p)
            st["sk_s"].append(k_s)
            st["sv_s"].append(v_s)
    return (xp.reshape(bp, seq, d), xs.reshape(db, 1, d),
            jnp.stack(st["dk_p"]), jnp.stack(st["dv_p"]), jnp.stack(st["dk_s"]), jnp.stack(st["dv_s"]),
            jnp.stack(st["sk_p"]), jnp.stack(st["sv_p"]), jnp.stack(st["sk_s"]), jnp.stack(st["sv_s"]),
            jnp.stack(st["S_p"]), jnp.stack(st["cv_p"]), s_s, jnp.stack(st["cv_s"]))
```

```python
import functools
import math

import jax
import jax.numpy as jnp
from jax import lax
from jax.experimental import pallas as pl
from jax.experimental.pallas import tpu as pltpu

F32 = jnp.float32
BF16 = jnp.bfloat16

LN_EPS = 1e-5
RMS_EPS = 1e-6
N_MIXERS = 3

DN_HK = 8
DN_HV = 16
DN_DK = 128
DN_DV = 128
DN_REP = DN_HV // DN_HK
DN_CONV_W = 4
DN_CHUNK = 128
DN_CONV_COLS = 512
DN_STEP_SEQS = 4
PROJ_TM = 512

DIFF_H = 8
DIFF_KVH = 4
DIFF_G = DIFF_H // DIFF_KVH
DIFF_HD = 64
DIFF_VD = 2 * DIFF_HD
DIFF_TQ = 512
FLASH_QUERY_COLS = 256
LOG2E = math.log2(math.e)
DIFF_PAGES_PER_STEP = 32

SWA_H = 16
SWA_KVH = 2
SWA_G = SWA_H // SWA_KVH
SWA_HD = 64
WINDOW = 128
SWA_DEC_BB = 8

LANES = 128
SUBLANES = 8
VMEM_LIMIT = 48 * 1024 * 1024

_NT = (((1,), (1,)), ((), ()))
_TN = (((0,), (0,)), ((), ()))


def _silu(x):
    return (0.5 * x) * (1.0 + jnp.tanh(0.5 * x))


def _proj_body(x_ref, w_ref, *refs, n_pro, segs, pro):
    pro_refs = refs[:n_pro]
    n_epi = sum(s[3] for s in segs)
    epi_refs = refs[n_pro:n_pro + n_epi]
    o_refs = refs[n_pro + n_epi:]
    x = x_ref[...]
    if pro is not None:
        x = pro(x, *pro_refs)
    xb = x.astype(BF16)
    e = 0
    for (start, width, epi, n_e), o_ref in zip(segs, o_refs):
        acc = jnp.dot(xb, w_ref[:, start:start + width], preferred_element_type=F32)
        if epi is not None:
            acc = epi(acc, *epi_refs[e:e + n_e])
        e += n_e
        o_ref[...] = acc


def _proj(x, w, segments, *, tm, pro=None, pro_ops=(), name="proj"):
    m, k = x.shape
    n = w.shape[1]
    assert m % tm == 0, (m, tm)
    ops = [x, w]
    in_specs = [pl.BlockSpec((tm, k), lambda i: (i, 0)),
                pl.BlockSpec((k, n), lambda i: (0, 0), pipeline_mode=pl.Buffered(1))]
    for a, s in pro_ops:
        ops.append(a)
        in_specs.append(s)
    segs = []
    start = 0
    for width, epi, epi_ops in segments:
        segs.append((start, width, epi, len(epi_ops)))
        start += width
        for a, s in epi_ops:
            ops.append(a)
            in_specs.append(s)
    assert start <= n, (start, n)
    return pl.pallas_call(
        functools.partial(_proj_body, n_pro=len(pro_ops), segs=tuple(segs), pro=pro),
        grid=(m // tm,),
        in_specs=in_specs,
        out_specs=[pl.BlockSpec((tm, s[1]), lambda i: (i, 0)) for s in segs],
        out_shape=[jax.ShapeDtypeStruct((m, s[1]), F32) for s in segs],
        compiler_params=pltpu.CompilerParams(dimension_semantics=("parallel",), vmem_limit_bytes=VMEM_LIMIT),
        name=name,
    )(*ops)


def _modulate(x, mod_ref):
    d = x.shape[-1]
    return x * (1.0 + mod_ref[:, d:2 * d]) + mod_ref[:, 0:d]


def _deepnorm(y, xres_ref, mod_ref, g_ref, b_ref, *, alpha):
    d = y.shape[-1]
    h = alpha * xres_ref[...] + (1.0 + mod_ref[:, 2 * d:3 * d]) * y
    hc = h - jnp.mean(h, -1, keepdims=True)
    var = jnp.mean(hc * hc, -1, keepdims=True)
    return hc * lax.rsqrt(var + LN_EPS) * g_ref[...] + b_ref[...]


def _add_bias(y, b_ref):
    return y + b_ref[...]


def _dn_gates(y, alog_ref, dtb_ref):
    lane = lax.broadcasted_iota(jnp.int32, y.shape, 1)
    t = y + dtb_ref[...]
    softplus = jnp.maximum(t, 0.0) + jnp.log1p(jnp.exp(-jnp.abs(t)))
    g = -jnp.exp(alog_ref[...]) * softplus
    return jnp.where(lane < DN_HV, g, jax.nn.sigmoid(y))


def _mod_spec(mod, tm, rows_per_mod):
    w = mod.shape[-1]
    if mod.ndim == 3:
        assert rows_per_mod % tm == 0, (rows_per_mod, tm)
        return pl.BlockSpec((None, 1, w), lambda i: (i * tm // rows_per_mod, 0, 0))
    return pl.BlockSpec((tm, w), lambda i: (i, 0))


def _in_proj(x, mod, rows_per_mod, w, segments, *, tm, name="in_proj"):
    segments = [(s, None, ()) if isinstance(s, int) else s for s in segments]
    return _proj(x, w, segments, tm=tm, pro=_modulate, pro_ops=[(mod, _mod_spec(mod, tm, rows_per_mod))],
                 name=name)


def _out_proj_norm(y, w, xres, mod, rows_per_mod, ln_g, ln_b, *, tm, alpha, name="out_proj"):
    d = w.shape[1]
    row = pl.BlockSpec((1, d), lambda i: (0, 0))
    epi_ops = [(xres, pl.BlockSpec((tm, d), lambda i: (i, 0))), (mod, _mod_spec(mod, tm, rows_per_mod)),
               (ln_g.reshape(1, d), row), (ln_b.reshape(1, d), row)]
    return _proj(y, w, [(d, functools.partial(_deepnorm, alpha=alpha), epi_ops)], tm=tm, name=name)[0]


def _l2norm_heads(y, o_ref, scale):
    for h in range(y.shape[-1] // DN_DK):
        yh = y[:, h * DN_DK:(h + 1) * DN_DK]
        inv = lax.rsqrt(jnp.sum(yh * yh, -1, keepdims=True) + RMS_EPS)
        o_ref[:, h * DN_DK:(h + 1) * DN_DK] = yh * inv * scale


def _dn_in_conv_body(x_ref, w_ref, mod_ref, cw_ref, alog_ref, dtb_ref, qkv_ref, z_ref, g_ref, tail_ref,
                     halo_ref, *, tiles_per_seq, cb):
    i = pl.program_id(0)
    tm = x_ref.shape[0]
    conv_dim = qkv_ref.shape[1]
    v_w = z_ref.shape[1]
    xb = _modulate(x_ref[...], mod_ref).astype(BF16)
    first = i % tiles_per_seq == 0
    backs = range(1, DN_CONV_W)
    row = lax.broadcasted_iota(jnp.int32, (SUBLANES, cb), 0)

    def rolled(v):
        return [pltpu.roll(v, d, axis=0) for d in backs]

    def project(c0):
        return jnp.dot(xb, w_ref[:, c0:c0 + cb], preferred_element_type=F32)

    acc_next = project(0)
    for c0 in range(0, conv_dim, cb):
        cols = slice(c0, c0 + cb)
        acc = acc_next
        if c0 + cb < conv_dim:
            acc_next = project(c0 + cb)
        else:
            z_ref[...] = jnp.dot(xb, w_ref[:, conv_dim:conv_dim + v_w], preferred_element_type=F32)
            g_ref[...] = _dn_gates(jnp.dot(xb, w_ref[:, conv_dim + v_w:conv_dim + v_w + LANES],
                                           preferred_element_type=F32), alog_ref, dtb_ref)
        taps = [cw_ref[s:s + 1, cols] for s in range(DN_CONV_W)]
        prev_rolled = rolled(jnp.where(first, 0.0, halo_ref[:, cols]))
        for r0 in range(0, tm, SUBLANES):
            cur = acc[r0:r0 + SUBLANES]
            cur_rolled = rolled(cur)
            back = [jnp.where(row < d, p, q) for d, p, q in zip(backs, prev_rolled, cur_rolled)]
            conv = back[DN_CONV_W - 2] * taps[0]
            for s in range(1, DN_CONV_W - 1):
                conv = conv + back[DN_CONV_W - 2 - s] * taps[s]
            qkv_ref[r0:r0 + SUBLANES, cols] = _silu(conv + cur * taps[DN_CONV_W - 1])
            prev_rolled = cur_rolled
        last = acc[tm - SUBLANES:tm]
        halo_ref[:, cols] = last
        tail_ref[:, cols] = last


def _dn_in_conv(x, mod, seq, w, conv_w, alog, dtb, *, tm, cb):
    m, k = x.shape
    conv_dim = conv_w.shape[1]
    v_w = DN_HV * DN_DV
    assert seq % tm == 0 and m % seq == 0
    tiles_per_seq = seq // tm
    const = lambda i: (0, 0)
    rows = lambda i: (i, 0)
    return pl.pallas_call(
        functools.partial(_dn_in_conv_body, tiles_per_seq=tiles_per_seq, cb=cb),
        grid=(m // tm,),
        in_specs=[pl.BlockSpec((tm, k), rows),
                  pl.BlockSpec(w.shape, const, pipeline_mode=pl.Buffered(1)),
                  pl.BlockSpec((None, 1, mod.shape[-1]), lambda i: (i // tiles_per_seq, 0, 0)),
                  pl.BlockSpec(conv_w.shape, const), pl.BlockSpec((1, LANES), const),
                  pl.BlockSpec((1, LANES), const)],
        out_specs=[pl.BlockSpec((tm, conv_dim), rows), pl.BlockSpec((tm, v_w), rows),
                   pl.BlockSpec((tm, LANES), rows),
                   pl.BlockSpec((None, SUBLANES, conv_dim), lambda i: (i // tiles_per_seq, 0, 0))],
        out_shape=[jax.ShapeDtypeStruct((m, conv_dim), F32), jax.ShapeDtypeStruct((m, v_w), F32),
                   jax.ShapeDtypeStruct((m, LANES), F32),
                   jax.ShapeDtypeStruct((m // seq, SUBLANES, conv_dim), F32)],
        scratch_shapes=[pltpu.VMEM((SUBLANES, conv_dim), F32)],
        compiler_params=pltpu.CompilerParams(dimension_semantics=("arbitrary",), vmem_limit_bytes=VMEM_LIMIT),
        name="dn_in_conv_p",
    )(x, w, mod, conv_w, alog, dtb)


def _gated_rms(o, ng, z):
    return o * lax.rsqrt(jnp.mean(o * o, -1, keepdims=True) + RMS_EPS) * ng * _silu(z)


def _dn_chunk_body(x_ref, z_ref, gb_ref, ng_ref, o_ref, s_ref):
    ci = pl.program_id(1)
    c = x_ref.shape[0]
    hb = DN_HV

    @pl.when(ci == 0)
    def _():
        s_ref[...] = jnp.zeros_like(s_ref)

    k_w = DN_HK * DN_DK
    gb = gb_ref[...]
    row = lax.broadcasted_iota(jnp.int32, gb.shape, 0)
    gc = gb
    sh = 1
    while sh < c:
        gc = gc + jnp.where(row >= sh, pltpu.roll(gc, sh, axis=0), 0.0)
        sh *= 2
    gct = gc.T
    ri = lax.broadcasted_iota(jnp.int32, (c, c), 0)
    cj = lax.broadcasted_iota(jnp.int32, (c, c), 1)
    tril = ri >= cj
    strict = ri > cj
    merge_masks = []
    k = 0
    while (1 << k) < c:
        merge_masks.append(((ri >> (k + 1)) == (cj >> (k + 1))) & ((ri >> k) != (cj >> k)))
        k += 1
    ng = ng_ref[...]
    heads = range(hb)
    dot = functools.partial(jnp.dot, preferred_element_type=F32)
    def l2n(y, scale):
        return y * lax.rsqrt(jnp.sum(y * y, -1, keepdims=True) + RMS_EPS) * scale

    qh = [l2n(x_ref[:, i * DN_DK:(i + 1) * DN_DK], DN_DK ** -0.5) for i in range(DN_HK)]
    kh = [l2n(x_ref[:, k_w + i * DN_DK:k_w + (i + 1) * DN_DK], 1.0) for i in range(DN_HK)]
    kk = [lax.dot_general(a, a, _NT, preferred_element_type=F32) for a in kh]
    qk = [lax.dot_general(a, b, _NT, preferred_element_type=F32) for a, b in zip(qh, kh)]
    gcol = [gc[:, j:j + 1] for j in heads]
    bcol = [gb[:, hb + j:hb + j + 1] for j in heads]
    decay = [jnp.where(tril, jnp.exp(jnp.where(tril, gcol[j] - gct[j:j + 1, :], 0.0)), 0.0) for j in heads]
    lmat = [jnp.where(strict, bcol[j] * kk[j // DN_REP] * decay[j], 0.0) for j in heads]
    def odd_rows(a, blk):
        return jnp.concatenate([a[r:r + blk] for r in range(blk, c, 2 * blk)], axis=0)

    def spread_odd(a, blk):
        zero = jnp.zeros((blk, a.shape[1]), a.dtype)
        return jnp.concatenate([piece for t in range(c // (2 * blk))
                                for piece in (zero, a[t * blk:(t + 1) * blk])], axis=0)

    minv = [jnp.where(merge_masks[0], -lmat[j], 0.0) for j in heads]
    for lvl in range(1, len(merge_masks)):
        blk = 1 << lvl
        if blk < SUBLANES:
            loff = [jnp.where(merge_masks[lvl], lmat[j], 0.0) for j in heads]
            y = [loff[j] + dot(loff[j], minv[j]) for j in heads]
            minv = [minv[j] - y[j] - dot(minv[j], y[j]) for j in heads]
        else:
            rg = lax.broadcasted_iota(jnp.int32, (c // 2, c), 0)
            ro = ((rg >> lvl) << (lvl + 1)) + blk + (rg & (blk - 1))
            co = lax.broadcasted_iota(jnp.int32, (c // 2, c), 1)
            mask = ((ro >> (lvl + 1)) == (co >> (lvl + 1))) & ((ro >> lvl) != (co >> lvl))
            loff = [jnp.where(mask, odd_rows(lmat[j], blk), 0.0) for j in heads]
            y = [loff[j] + dot(loff[j], minv[j]) for j in heads]
            upd = [y[j] + dot(odd_rows(minv[j], blk), spread_odd(y[j], blk)) for j in heads]
            minv = [minv[j] - spread_odd(upd[j], blk) for j in heads]
    eg = [jnp.exp(gcol[j]) for j in heads]
    rhs = [jnp.concatenate([x_ref[:, 2 * k_w + j * DN_DV:2 * k_w + (j + 1) * DN_DV] * bcol[j],
                            kh[j // DN_REP] * (bcol[j] * eg[j])], axis=1) for j in heads]
    sol = [rhs[j] + dot(minv[j], rhs[j]) for j in heads]
    s_old = [s_ref[j] for j in heads]
    ws = [dot(jnp.concatenate([sol[j][:, DN_DV:], qh[j // DN_REP] * eg[j]], axis=0), s_old[j]) for j in heads]
    v_new = [sol[j][:, :DN_DV] - ws[j][:c] for j in heads]
    o = [ws[j][c:] + dot(qk[j // DN_REP] * decay[j], v_new[j]) for j in heads]
    for j in heads:
        glast = gc[c - 1:c, j:j + 1]
        kd = kh[j // DN_REP] * jnp.exp(glast - gcol[j])
        s_ref[j] = s_old[j] * jnp.exp(glast) + lax.dot_general(kd, v_new[j], _TN, preferred_element_type=F32)
        o_ref[:, j * DN_DV:(j + 1) * DN_DV] = _gated_rms(
            o[j], ng, z_ref[:, j * DN_DV:(j + 1) * DN_DV]).astype(o_ref.dtype)


def _dn_chunks(qkv, z, gates, norm_g):
    b, t, cd = qkv.shape
    c = DN_CHUNK
    assert t % c == 0
    v_w = DN_HV * DN_DV
    o, s = pl.pallas_call(
        _dn_chunk_body,
        grid=(b, t // c),
        in_specs=[pl.BlockSpec((None, c, cd), lambda bi, ci: (bi, ci, 0)),
                  pl.BlockSpec((None, c, v_w), lambda bi, ci: (bi, ci, 0)),
                  pl.BlockSpec((None, c, LANES), lambda bi, ci: (bi, ci, 0)),
                  pl.BlockSpec((1, DN_DV), lambda bi, ci: (0, 0))],
        out_specs=[pl.BlockSpec((None, c, v_w), lambda bi, ci: (bi, ci, 0)),
                   pl.BlockSpec((None, DN_HV, DN_DK, DN_DV), lambda bi, ci: (bi, 0, 0, 0))],
        out_shape=[jax.ShapeDtypeStruct((b, t, v_w), BF16),
                   jax.ShapeDtypeStruct((b, DN_HV, DN_DK, DN_DV), F32)],
        compiler_params=pltpu.CompilerParams(dimension_semantics=("parallel", "arbitrary"),
                                             vmem_limit_bytes=VMEM_LIMIT),
        name="dn_chunks",
    )(qkv, z, gates, norm_g.reshape(1, DN_DV))
    return o, s


def _dn_step_prep_body(x_ref, prev_ref, w_ref, o_ref, cv_ref, *, n_q_blk, n_k_blk):
    c = pl.program_id(0)
    x = x_ref[...]
    w = w_ref[...]
    conv = prev_ref[0] * w[0:1]
    for s in range(1, DN_CONV_W - 1):
        conv = conv + prev_ref[s] * w[s:s + 1]
    conv = conv + x * w[DN_CONV_W - 1:DN_CONV_W]
    y = _silu(conv)
    for s in range(DN_CONV_W - 2):
        cv_ref[s] = prev_ref[s + 1]
    cv_ref[DN_CONV_W - 2] = x

    @pl.when(c >= n_q_blk + n_k_blk)
    def _():
        o_ref[...] = y

    @pl.when(c < n_q_blk + n_k_blk)
    def _():
        _l2norm_heads(y, o_ref, jnp.where(c < n_q_blk, DN_DK ** -0.5, 1.0))


def _dn_step_prep(qkv, conv_prev_t, conv_w, *, cb):
    db, c = qkv.shape
    k_w = DN_HK * DN_DK
    nw = DN_CONV_W - 1
    return pl.pallas_call(
        functools.partial(_dn_step_prep_body, n_q_blk=k_w // cb, n_k_blk=k_w // cb),
        grid=(c // cb,),
        in_specs=[pl.BlockSpec((db, cb), lambda ci: (0, ci)),
                  pl.BlockSpec((nw, db, cb), lambda ci: (0, 0, ci)),
                  pl.BlockSpec((DN_CONV_W, cb), lambda ci: (0, ci))],
        out_specs=[pl.BlockSpec((db, cb), lambda ci: (0, ci)),
                   pl.BlockSpec((nw, db, cb), lambda ci: (0, 0, ci))],
        out_shape=[jax.ShapeDtypeStruct((db, c), F32), jax.ShapeDtypeStruct((nw, db, c), F32)],
        compiler_params=pltpu.CompilerParams(dimension_semantics=("parallel",), vmem_limit_bytes=VMEM_LIMIT),
        name="dn_step_prep",
    )(qkv, conv_prev_t, conv_w)


def _dn_step_body(q_ref, k_ref, v_ref, z_ref, gb_ref, ng_ref, s_ref, o_ref, so_ref, *, n_pass):
    g = pl.program_id(0)

    if n_pass:
        @pl.when(g < n_pass)
        def _():
            so_ref[...] = s_ref[...]

    @pl.when(g == n_pass)
    def _():
        ng = ng_ref[...]
        rows = lax.broadcasted_iota(jnp.int32, (SUBLANES, DN_DK), 0)
        heads = [(t, h) for t in range(q_ref.shape[0]) for h in range(DN_HV)]
        s_old = [s_ref[t, h] for t, h in heads]
        kh = [k_ref[t, h // DN_REP:h // DN_REP + 1, :] for t, h in heads]
        qh = [q_ref[t, h // DN_REP:h // DN_REP + 1, :] for t, h in heads]
        ks_qs = [jnp.dot(jnp.where(rows == 0, kh[n], jnp.where(rows == 1, qh[n], 0.0)), s_old[n],
                         preferred_element_type=F32) for n in range(len(heads))]
        eg = [jnp.exp(gb_ref[t, :, h:h + 1]) for t, h in heads]
        v_new = [gb_ref[t, :, DN_HV + h:DN_HV + h + 1] * (v_ref[t, h:h + 1, :] - eg[n] * ks_qs[n][0:1])
                 for n, (t, h) in enumerate(heads)]
        outer = [lax.dot_general(jnp.where(rows == 0, kh[n], 0.0), jnp.where(rows == 0, v_new[n], 0.0), _TN,
                                 preferred_element_type=F32) for n in range(len(heads))]
        for n, (t, h) in enumerate(heads):
            so_ref[t, h] = s_old[n] * eg[n] + outer[n]
            o = eg[n] * ks_qs[n][1:2] + jnp.sum(qh[n] * kh[n], -1, keepdims=True) * v_new[n]
            o_ref[t, h:h + 1, :] = _gated_rms(o, ng, z_ref[t, h:h + 1, :])


def _dn_step(q, k, v, z, gates, norm_g, s_all, layer, *, in_place):
    n, db = s_all.shape[:2]
    bs = DN_STEP_SEQS
    assert db % bs == 0
    n_pass = 0 if in_place else n - 1

    def s_layer(g):
        return layer if in_place else (layer + 1 + g) % n

    def row_spec(rows, width):
        return pl.BlockSpec((bs, rows, width), lambda g, b: (jnp.where(g == n_pass, b, 0), 0, 0))

    s_spec = pl.BlockSpec((None, bs, DN_HV, DN_DK, DN_DV), lambda g, b: (s_layer(g), b, 0, 0, 0))
    return pl.pallas_call(
        functools.partial(_dn_step_body, n_pass=n_pass),
        grid=(n_pass + 1, db // bs),
        in_specs=[row_spec(DN_HK, DN_DK), row_spec(DN_HK, DN_DK), row_spec(DN_HV, DN_DV),
                  row_spec(DN_HV, DN_DV), row_spec(1, LANES),
                  pl.BlockSpec((1, DN_DV), lambda g, b: (0, 0)), s_spec],
        out_specs=[row_spec(DN_HV, DN_DV), s_spec],
        out_shape=[jax.ShapeDtypeStruct((db, DN_HV, DN_DV), F32), jax.ShapeDtypeStruct(s_all.shape, F32)],
        input_output_aliases={6: 1} if in_place else {},
        compiler_params=pltpu.CompilerParams(dimension_semantics=("arbitrary", "arbitrary"),
                                             vmem_limit_bytes=VMEM_LIMIT),
        name="dn_step",
    )(q, k, v, z, gates, norm_g.reshape(1, DN_DV), s_all)


def _dn_layer(xp, xs, mod_p, mod_s, seq, conv_prev, s_all, layer, w_in, conv_w, a_log, dt_bias, norm_g, w_out,
              ln_g, ln_b, alpha):
    mp, d = xp.shape
    bp = mp // seq
    db = xs.shape[0]
    conv_dim = 2 * DN_HK * DN_DK + DN_HV * DN_DV
    v_w = DN_HV * DN_DV
    w_all = jnp.pad(w_in, ((0, 0), (0, LANES - 2 * DN_HV))).astype(BF16)
    w_o = w_out.astype(BF16)
    alog = jnp.pad(a_log, (0, LANES - DN_HV)).reshape(1, LANES)
    dtb = jnp.pad(dt_bias, (0, LANES - DN_HV)).reshape(1, LANES)
    lane_row = pl.BlockSpec((1, LANES), lambda i: (0, 0))
    segments = [conv_dim, v_w, (LANES, _dn_gates, [(alog, lane_row), (dtb, lane_row)])]

    qkvc, z, gates, tail = _dn_in_conv(xp, mod_p, seq, w_all, conv_w, alog, dtb, tm=PROJ_TM, cb=DN_CONV_COLS)
    cv_p = tail[:, SUBLANES - (DN_CONV_W - 1):]
    o, s_p = _dn_chunks(qkvc.reshape(bp, seq, conv_dim), z.reshape(bp, seq, v_w), gates.reshape(bp, seq, LANES), norm_g)
    xp_new = _out_proj_norm(o.reshape(mp, v_w), w_o, xp, mod_p, seq, ln_g, ln_b, tm=PROJ_TM, alpha=alpha,
                            name="dn_out_p")

    qkv_s, z_s, gates_s = _in_proj(xs, mod_s, 1, w_all, segments, tm=db, name="dn_in_s")
    qkvc_s, cv_t = _dn_step_prep(qkv_s, jnp.swapaxes(conv_prev, 0, 1), conv_w, cb=512)
    k_w = DN_HK * DN_DK
    o_s, s_s = _dn_step(qkvc_s[:, :k_w].reshape(db, DN_HK, DN_DK),
                        qkvc_s[:, k_w:2 * k_w].reshape(db, DN_HK, DN_DK),
                        qkvc_s[:, 2 * k_w:].reshape(db, DN_HV, DN_DV),
                        z_s.reshape(db, DN_HV, DN_DV), gates_s.reshape(db, 1, LANES), norm_g, s_all, layer,
                        in_place=layer > 0)
    xs_new = _out_proj_norm(o_s.reshape(db, v_w), w_o, xs, mod_s, 1, ln_g, ln_b, tm=db, alpha=alpha,
                            name="dn_out_s")
    return xp_new, xs_new, s_p, cv_p, s_s, jnp.swapaxes(cv_t, 0, 1)


def _diff_lambda(lq1_ref, lk1_ref, lq2_ref, lk2_ref, lam_init):
    e1 = jnp.exp(jnp.sum(lq1_ref[...] * lk1_ref[...], -1, keepdims=True))
    e2 = jnp.exp(jnp.sum(lq2_ref[...] * lk2_ref[...], -1, keepdims=True))
    return e1 - e2 + lam_init


def _diff_head_out(o, sg, z, lam_init):
    on = o * lax.rsqrt(jnp.mean(o * o, -1, keepdims=True) + RMS_EPS) * sg * (1.0 - lam_init)
    return on * _silu(z)


def _diff_flash_body(qi_ref, kj_ref, q_ref, k_ref, v_ref, z_ref, lq1_ref, lk1_ref, lq2_ref, lk2_ref, sg_ref,
                     o_ref, m_ref, l_ref, acc_ref, *, lam_init):
    i = qi_ref[pl.program_id(2)]
    j = kj_ref[pl.program_id(2)]
    tq, tk = q_ref.shape[0], k_ref.shape[0]
    gw = DIFF_G * tq

    @pl.when(j == 0)
    def _():
        m_ref[...] = jnp.full_like(m_ref, -jnp.inf)
        l_ref[...] = jnp.zeros_like(l_ref)
        acc_ref[...] = jnp.zeros_like(acc_ref)

    def update(masked):
        v = v_ref[...].astype(BF16)
        if masked:
            keep = lax.broadcasted_iota(jnp.int32, (tk, tq), 0) <= lax.broadcasted_iota(jnp.int32, (tk, tq), 1)
        qc = FLASH_QUERY_COLS
        groups = [(half, g, q0) for half in range(2) for g in range(DIFF_G) for q0 in range(0, tq, qc)]
        sts = []
        for half, g, q0 in groups:
            qm = q_ref[q0:q0 + qc, (g * 2 + half) * DIFF_HD:(g * 2 + half + 1) * DIFF_HD] * (
                DIFF_HD ** -0.5 * LOG2E)
            st = lax.dot_general(k_ref[:, half * DIFF_HD:(half + 1) * DIFF_HD], qm, _NT,
                                 preferred_element_type=F32)
            sts.append(jnp.where(keep[:, q0:q0 + qc], st, -jnp.inf) if masked else st)
        for n, (half, g, q0) in enumerate(groups):
            st = sts[n]
            cols = slice(half * gw + g * tq + q0, half * gw + g * tq + q0 + qc)
            m_prev = m_ref[:, cols]
            m_new = jnp.maximum(m_prev, jnp.max(st, 0, keepdims=True))
            alpha = jnp.exp2(m_prev - m_new)
            p = jnp.exp2(st - m_new)
            l_ref[:, cols] = alpha * l_ref[:, cols] + jnp.sum(p, 0, keepdims=True)
            acc_ref[:, cols] = alpha * acc_ref[:, cols] + lax.dot_general(
                v, p.astype(BF16), _TN, preferred_element_type=F32)
            m_ref[:, cols] = m_new

    @pl.when(j < i)
    def _():
        update(False)

    @pl.when(j == i)
    def _():
        update(True)
        lam = _diff_lambda(lq1_ref, lk1_ref, lq2_ref, lk2_ref, lam_init)
        sg = sg_ref[...]
        for g in range(DIFF_G):
            c0 = slice(g * tq, (g + 1) * tq)
            c1 = slice(gw + g * tq, gw + (g + 1) * tq)
            ot = acc_ref[:, c0] * (1.0 / l_ref[:, c0]) - lam * (acc_ref[:, c1] * (1.0 / l_ref[:, c1]))
            o_ref[:, g * DIFF_VD:(g + 1) * DIFF_VD] = _diff_head_out(
                ot.T, sg, z_ref[:, g * DIFF_VD:(g + 1) * DIFF_VD], lam_init).astype(o_ref.dtype)


def _diff_flash(q, k, v, z, lams, subln_g, lam_init, *, tq):
    b, t, _ = q.shape
    assert t % tq == 0 and tq & (tq - 1) == 0, (t, tq)
    nq = t // tq
    qw = DIFF_G * 2 * DIFF_HD
    ow = DIFF_G * DIFF_VD
    n_sub = 2 * DIFF_G
    pairs = [(i, j) for i in range(nq) for j in range(i + 1)]
    qi = jnp.asarray([p[0] for p in pairs], jnp.int32)
    kj = jnp.asarray([p[1] for p in pairs], jnp.int32)
    vec = pl.BlockSpec((1, DIFF_HD), lambda bi, h, s, qi, kj: (0, 0))
    grid_spec = pltpu.PrefetchScalarGridSpec(
        num_scalar_prefetch=2,
        grid=(b, DIFF_KVH, len(pairs)),
        in_specs=[pl.BlockSpec((None, tq, qw), lambda bi, h, s, qi, kj: (bi, qi[s], h)),
                  pl.BlockSpec((None, tq, 2 * DIFF_HD), lambda bi, h, s, qi, kj: (bi, kj[s], h)),
                  pl.BlockSpec((None, tq, DIFF_VD), lambda bi, h, s, qi, kj: (bi, kj[s], h)),
                  pl.BlockSpec((None, tq, ow), lambda bi, h, s, qi, kj: (bi, qi[s], h)),
                  vec, vec, vec, vec,
                  pl.BlockSpec((1, DIFF_VD), lambda bi, h, s, qi, kj: (0, 0))],
        out_specs=pl.BlockSpec((None, tq, ow), lambda bi, h, s, qi, kj: (bi, qi[s], h)),
        scratch_shapes=[pltpu.VMEM((1, n_sub * tq), F32), pltpu.VMEM((1, n_sub * tq), F32),
                        pltpu.VMEM((DIFF_VD, n_sub * tq), F32)],
    )
    return pl.pallas_call(
        functools.partial(_diff_flash_body, lam_init=lam_init),
        grid_spec=grid_spec,
        out_shape=jax.ShapeDtypeStruct((b, t, DIFF_H * DIFF_VD), BF16),
        compiler_params=pltpu.CompilerParams(
            dimension_semantics=("parallel", "parallel", "arbitrary"),
            vmem_limit_bytes=VMEM_LIMIT),
        name="diff_flash",
    )(qi, kj, q, k, v, z, *lams, subln_g.reshape(1, DIFF_VD))


def _diff_decode_body(pt_ref, q_ref, ks_ref, vs_ref, z_ref, lq1_ref, lk1_ref, lq2_ref, lk2_ref, sg_ref,
                      *rest, pp, lam_init):
    del pt_ref
    k_pages = rest[:pp]
    v_pages = rest[pp:2 * pp]
    o_ref = rest[2 * pp]
    qb_ref, m_ref, l_ref, acc_ref = rest[2 * pp + 1:]
    j = pl.program_id(1)
    n_rows = DIFF_KVH * DIFF_G * 2
    head_shift = (DIFF_G * 2).bit_length() - 1
    kw = 2 * DIFF_HD
    page_rows = k_pages[0].shape[0]

    @pl.when(j == 0)
    def _():
        d_i = lax.broadcasted_iota(jnp.int32, (DIFF_HD, kw), 0)
        c_i = lax.broadcasted_iota(jnp.int32, (DIFF_HD, kw), 1)
        spread = jnp.where((c_i & (DIFF_HD - 1)) == d_i, 1.0, 0.0)
        qt = jnp.dot(q_ref[...], spread, preferred_element_type=F32)
        r_i = lax.broadcasted_iota(jnp.int32, (n_rows, kw), 0)
        half = lax.broadcasted_iota(jnp.int32, (n_rows, kw), 1) >> (DIFF_HD.bit_length() - 1)
        qb_ref[...] = jnp.where(half == (r_i & 1), qt, 0.0) * (DIFF_HD ** -0.5)
        m_ref[...] = jnp.full_like(m_ref, -jnp.inf)
        l_ref[...] = jnp.zeros_like(l_ref)
        acc_ref[...] = jnp.zeros_like(acc_ref)

    qb = qb_ref[...]
    s = jnp.concatenate([lax.dot_general(qb, kp[...], _NT, preferred_element_type=F32) for kp in k_pages],
                        axis=1)
    row_head = lax.broadcasted_iota(jnp.int32, s.shape, 0) >> head_shift
    col_head = lax.broadcasted_iota(jnp.int32, s.shape, 1) & (DIFF_KVH - 1)
    s = jnp.where(row_head == col_head, s, -jnp.inf)
    m_prev = m_ref[...]
    m_new = jnp.maximum(m_prev, jnp.max(s, -1, keepdims=True))
    alpha = jnp.exp(m_prev - m_new)
    p = jnp.exp(s - m_new)
    l_new = alpha * l_ref[...] + jnp.sum(p, -1, keepdims=True)
    pv = jnp.dot(p[:, 0:page_rows], v_pages[0][...], preferred_element_type=F32)
    for t in range(1, pp):
        pv = pv + jnp.dot(p[:, t * page_rows:(t + 1) * page_rows], v_pages[t][...],
                          preferred_element_type=F32)
    acc_new = alpha * acc_ref[...] + pv
    m_ref[...] = m_new
    l_ref[...] = l_new
    acc_ref[...] = acc_new

    @pl.when(j == pl.num_programs(1) - 1)
    def _():
        rh = lax.broadcasted_iota(jnp.int32, (n_rows, kw), 0) >> head_shift
        ks = jnp.zeros((n_rows, kw), F32)
        vs = jnp.zeros((n_rows, kw), F32)
        for h in range(DIFF_KVH):
            ks = jnp.where(rh == h, ks_ref[h:h + 1, :], ks)
            vs = jnp.where(rh == h, vs_ref[h:h + 1, :], vs)
        s_self = jnp.sum(qb * ks, -1, keepdims=True)
        m_fin = jnp.maximum(m_new, s_self)
        a = jnp.exp(m_new - m_fin)
        p_self = jnp.exp(s_self - m_fin)
        l_fin = a * l_new + p_self
        acc = (a * acc_new + p_self * vs) / l_fin
        lam = _diff_lambda(lq1_ref, lk1_ref, lq2_ref, lk2_ref, lam_init)
        sg = sg_ref[...]
        for hh in range(DIFF_H):
            o = acc[2 * hh:2 * hh + 1] - lam * acc[2 * hh + 1:2 * hh + 2]
            o_ref[hh:hh + 1, :] = _diff_head_out(o, sg, z_ref[hh:hh + 1, :], lam_init)


def _diff_decode(q, k_self, v_self, z, cache_k, cache_v, layer, page_table, lams, subln_g, lam_init, *, pp):
    db = q.shape[0]
    n_pages = page_table.shape[1]
    assert n_pages % pp == 0
    page_rows, kw = cache_k.shape[2:]
    n_rows = DIFF_KVH * DIFF_G * 2

    def page_spec(t):
        return pl.BlockSpec((None, None, page_rows, kw), lambda b, j, pt: (layer, pt[b, j * pp + t], 0, 0))

    def seq_spec(shape):
        return pl.BlockSpec((None,) + shape, lambda b, j, pt: (b, 0, 0))

    vec = pl.BlockSpec((1, DIFF_HD), lambda b, j, pt: (0, 0))
    grid_spec = pltpu.PrefetchScalarGridSpec(
        num_scalar_prefetch=1,
        grid=(db, n_pages // pp),
        in_specs=[seq_spec((n_rows, DIFF_HD)), seq_spec((DIFF_KVH, kw)), seq_spec((DIFF_KVH, kw)),
                  seq_spec((DIFF_H, DIFF_VD)), vec, vec, vec, vec,
                  pl.BlockSpec((1, DIFF_VD), lambda b, j, pt: (0, 0))]
                 + [page_spec(t) for t in range(pp)] + [page_spec(t) for t in range(pp)],
        out_specs=seq_spec((DIFF_H, DIFF_VD)),
        scratch_shapes=[pltpu.VMEM((n_rows, kw), F32), pltpu.VMEM((n_rows, 1), F32),
                        pltpu.VMEM((n_rows, 1), F32), pltpu.VMEM((n_rows, DIFF_VD), F32)],
    )
    return pl.pallas_call(
        functools.partial(_diff_decode_body, pp=pp, lam_init=lam_init),
        grid_spec=grid_spec,
        out_shape=jax.ShapeDtypeStruct((db, DIFF_H, DIFF_VD), F32),
        compiler_params=pltpu.CompilerParams(dimension_semantics=("parallel", "arbitrary"),
                                             vmem_limit_bytes=VMEM_LIMIT),
        name="diff_decode",
    )(page_table, q, k_self, v_self, z, *lams, subln_g.reshape(1, DIFF_VD),
      *([cache_k] * pp), *([cache_v] * pp))


def _diff_layer(xp, xs, mod_p, mod_s, seq, layer_idx, j, cache_k, cache_v, page_table, w_in, lams, subln_g,
                w_out, ln_g, ln_b, alpha):
    mp, d = xp.shape
    bp = mp // seq
    db = xs.shape[0]
    lam_init = 0.8 - 0.6 * math.exp(-0.3 * layer_idx)
    q_w = DIFF_H * 2 * DIFF_HD
    k_w = DIFF_KVH * 2 * DIFF_HD
    v_w = DIFF_KVH * DIFF_VD
    w_all = w_in.astype(BF16)
    w_o = w_out.astype(BF16)
    lams = [a.reshape(1, DIFF_HD) for a in lams]
    segments = [q_w, k_w, v_w, w_in.shape[1] - q_w - k_w - v_w]

    q, k, v, z = _in_proj(xp, mod_p, seq, w_all, segments, tm=PROJ_TM, name="diff_in_p")
    o = _diff_flash(q.reshape(bp, seq, q_w), k.reshape(bp, seq, k_w), v.reshape(bp, seq, v_w),
                    z.reshape(bp, seq, -1), lams, subln_g, lam_init, tq=min(DIFF_TQ, seq))
    xp_new = _out_proj_norm(o.reshape(mp, -1), w_o, xp, mod_p, seq, ln_g, ln_b, tm=PROJ_TM, alpha=alpha,
                            name="diff_out_p")

    q_s, k_s, v_s, z_s = _in_proj(xs, mod_s, 1, w_all, segments, tm=db, name="diff_in_s")
    n_l, n_pool, page = cache_k.shape[:3]
    o_s = _diff_decode(q_s.reshape(db, DIFF_KVH * DIFF_G * 2, DIFF_HD), k_s.reshape(db, DIFF_KVH, 2 * DIFF_HD),
                       v_s.reshape(db, DIFF_KVH, DIFF_VD), z_s.reshape(db, DIFF_H, DIFF_VD),
                       cache_k.reshape(n_l, n_pool, page * DIFF_KVH, 2 * DIFF_HD),
                       cache_v.reshape(n_l, n_pool, page * DIFF_KVH, DIFF_VD),
                       j, page_table, lams, subln_g, lam_init, pp=DIFF_PAGES_PER_STEP)
    xs_new = _out_proj_norm(o_s.reshape(db, -1), w_o, xs, mod_s, 1, ln_g, ln_b, tm=db, alpha=alpha,
                            name="diff_out_s")
    return (xp_new, xs_new, k.reshape(bp, seq, DIFF_KVH, 2 * DIFF_HD), v.reshape(bp, seq, DIFF_KVH, DIFF_VD),
            k_s.reshape(db, 1, DIFF_KVH, 2 * DIFF_HD), v_s.reshape(db, 1, DIFF_KVH, DIFF_VD))


def _swa_band_body(q_ref, kc_ref, kp_ref, vc_ref, vp_ref, z_ref, sink_ref, o_ref):
    i = pl.program_id(1)
    blk = q_ref.shape[0]
    cols = SWA_G * blk
    kj = lax.broadcasted_iota(jnp.int32, (2 * blk, cols), 0)
    qi = lax.broadcasted_iota(jnp.int32, (2 * blk, cols), 1) & (blk - 1)
    dist = qi + blk - kj
    keep = (dist >= 0) & (dist <= WINDOW) & ((i > 0) | (kj >= blk))
    for h in range(SWA_KVH):
        hs = slice(h * SWA_HD, (h + 1) * SWA_HD)
        k2 = jnp.concatenate([kp_ref[:, hs], kc_ref[:, hs]], axis=0)
        v2 = jnp.concatenate([vp_ref[:, hs], vc_ref[:, hs]], axis=0)
        qs = jnp.concatenate([q_ref[:, (h * SWA_G + g) * SWA_HD:(h * SWA_G + g + 1) * SWA_HD]
                              for g in range(SWA_G)], axis=0) * (SWA_HD ** -0.5)
        sink = jnp.concatenate([jnp.broadcast_to(sink_ref[:, h * SWA_G + g:h * SWA_G + g + 1], (1, blk))
                                for g in range(SWA_G)], axis=1)
        st = lax.dot_general(k2, qs, _NT, preferred_element_type=F32)
        st = jnp.where(keep, st, -jnp.inf)
        m = jnp.maximum(jnp.max(st, 0, keepdims=True), sink)
        p = jnp.exp(st - m)
        den = jnp.sum(p, 0, keepdims=True) + jnp.exp(sink - m)
        ot = lax.dot_general(v2, p, _TN, preferred_element_type=F32) * (1.0 / den)
        for g in range(0, SWA_G, 2):
            cs = slice((h * SWA_G + g) * SWA_HD, (h * SWA_G + g + 2) * SWA_HD)
            pair = jnp.concatenate([ot[:, g * blk:(g + 1) * blk], ot[:, (g + 1) * blk:(g + 2) * blk]], axis=0)
            o_ref[:, cs] = (pair.T * _silu(z_ref[:, cs])).astype(o_ref.dtype)


def _swa_band(q, k, v, z, sinks):
    b, t, qw = q.shape
    blk = WINDOW
    kw = SWA_KVH * SWA_HD
    cur = pl.BlockSpec((None, blk, kw), lambda bi, i: (bi, i, 0))
    prev = pl.BlockSpec((None, blk, kw), lambda bi, i: (bi, jnp.maximum(i - 1, 0), 0))
    wide = pl.BlockSpec((None, blk, qw), lambda bi, i: (bi, i, 0))
    return pl.pallas_call(
        _swa_band_body,
        grid=(b, t // blk),
        in_specs=[wide, cur, prev, cur, prev, wide, pl.BlockSpec((1, SWA_H), lambda bi, i: (0, 0))],
        out_specs=wide,
        out_shape=jax.ShapeDtypeStruct((b, t, qw), BF16),
        compiler_params=pltpu.CompilerParams(dimension_semantics=("parallel", "parallel"),
                                             vmem_limit_bytes=VMEM_LIMIT),
        name="swa_band",
    )(q, k, k, v, v, z, sinks.reshape(1, SWA_H))


def _swa_decode_body(q_ref, kb_ref, vb_ref, kn_ref, vn_ref, z_ref, sink_ref, o_ref, ko_ref, vo_ref):
    win = kb_ref.shape[1]
    for h in range(SWA_KVH):
        hs = slice(h * SWA_HD, (h + 1) * SWA_HD)
        gs = slice(h * SWA_G, (h + 1) * SWA_G)
        qh = q_ref[:, gs, :]
        s = jnp.einsum("bqd,bkd->bqk", qh, kb_ref[:, :, hs], preferred_element_type=F32) * (SWA_HD ** -0.5)
        s_new = jnp.sum(qh * kn_ref[:, :, hs], -1, keepdims=True) * (SWA_HD ** -0.5)
        sink = sink_ref[gs, :][None]
        m = jnp.maximum(jnp.maximum(jnp.max(s, -1, keepdims=True), s_new), sink)
        p = jnp.exp(s - m)
        p_new = jnp.exp(s_new - m)
        den = jnp.sum(p, -1, keepdims=True) + p_new + jnp.exp(sink - m)
        o = jnp.einsum("bqk,bkd->bqd", p, vb_ref[:, :, hs], preferred_element_type=F32)
        o = (o + p_new * vn_ref[:, :, hs]) / den
        o_ref[:, gs, :] = o * _silu(z_ref[:, gs, :])
    ko_ref[:, 0:win - 1, :] = kb_ref[:, 1:win, :]
    ko_ref[:, win - 1:win, :] = kn_ref[...]
    vo_ref[:, 0:win - 1, :] = vb_ref[:, 1:win, :]
    vo_ref[:, win - 1:win, :] = vn_ref[...]


def _swa_decode(q, k_new, v_new, z, buf_k, buf_v, sinks, *, bb):
    db, win, kw = buf_k.shape
    qz = pl.BlockSpec((bb, SWA_H, SWA_HD), lambda b: (b, 0, 0))
    buf = pl.BlockSpec((bb, win, kw), lambda b: (b, 0, 0))
    new = pl.BlockSpec((bb, 1, kw), lambda b: (b, 0, 0))
    return pl.pallas_call(
        _swa_decode_body,
        grid=(db // bb,),
        in_specs=[qz, buf, buf, new, new, qz, pl.BlockSpec((SWA_H, 1), lambda b: (0, 0))],
        out_specs=[qz, buf, buf],
        out_shape=[jax.ShapeDtypeStruct(q.shape, F32), jax.ShapeDtypeStruct(buf_k.shape, F32),
                   jax.ShapeDtypeStruct(buf_v.shape, F32)],
        compiler_params=pltpu.CompilerParams(dimension_semantics=("parallel",), vmem_limit_bytes=VMEM_LIMIT),
        name="swa_decode",
    )(q, buf_k, buf_v, k_new, v_new, z, sinks.reshape(SWA_H, 1))


def _swa_layer(xp, xs, mod_p, mod_s, seq, buf_k, buf_v, sinks, w_in, w_out, ln_g, ln_b, alpha):
    mp, d = xp.shape
    bp = mp // seq
    db = xs.shape[0]
    q_w = SWA_H * SWA_HD
    kv_w = SWA_KVH * SWA_HD
    w_all = w_in.astype(BF16)
    w_o = w_out.astype(BF16)
    segments = [q_w, kv_w, kv_w, w_in.shape[1] - q_w - 2 * kv_w]

    q, k, v, z = _in_proj(xp, mod_p, seq, w_all, segments, tm=PROJ_TM, name="swa_in_p")
    k = k.reshape(bp, seq, kv_w)
    v = v.reshape(bp, seq, kv_w)
    o = _swa_band(q.reshape(bp, seq, q_w), k, v, z.reshape(bp, seq, q_w), sinks)
    xp_new = _out_proj_norm(o.reshape(mp, q_w), w_o, xp, mod_p, seq, ln_g, ln_b, tm=PROJ_TM, alpha=alpha,
                            name="swa_out_p")

    q_s, k_s, v_s, z_s = _in_proj(xs, mod_s, 1, w_all, segments, tm=db, name="swa_in_s")
    win = buf_k.shape[1]
    o_s, k_buf, v_buf = _swa_decode(q_s.reshape(db, SWA_H, SWA_HD), k_s.reshape(db, 1, kv_w),
                                    v_s.reshape(db, 1, kv_w), z_s.reshape(db, SWA_H, SWA_HD),
                                    buf_k.reshape(db, win, kv_w), buf_v.reshape(db, win, kv_w), sinks,
                                    bb=SWA_DEC_BB)
    xs_new = _out_proj_norm(o_s.reshape(db, q_w), w_o, xs, mod_s, 1, ln_g, ln_b, tm=db, alpha=alpha,
                            name="swa_out_s")
    kv_shape = (SWA_KVH, SWA_HD)
    return (xp_new, xs_new, k[:, seq - WINDOW:].reshape((bp, WINDOW) + kv_shape),
            v[:, seq - WINDOW:].reshape((bp, WINDOW) + kv_shape),
            k_buf.reshape((db, win) + kv_shape), v_buf.reshape((db, win) + kv_shape))


def kernel(x_prompt, x_sample, cache_diff_k, cache_diff_v, page_table, cache_swa_k, cache_swa_v, state_dn_S, state_dn_conv, c_prompt, c_sample, ada_w, ada_b, ln_g, ln_b, dn_w_in, dn_conv_w, dn_a_log, dn_dt_bias, dn_norm_g, dn_w_out, diff_w_in, diff_lq1, diff_lk1, diff_lq2, diff_lk2, diff_subln_g, diff_w_out, swa_w_in, swa_sinks, swa_w_out):
    bp, seq, d = x_prompt.shape
    db, dec_seq, _ = x_sample.shape
    assert dec_seq == 1 and bp <= SUBLANES
    depth = ada_w.shape[0]
    alpha = (2 * depth) ** 0.25
    xp = x_prompt.reshape(bp * seq, d)
    xs = x_sample.reshape(db, d)

    c_all = jnp.concatenate([c_prompt, jnp.zeros((SUBLANES - bp, d), F32), c_sample], axis=0)
    st = {n: [] for n in ("dk_p", "dv_p", "dk_s", "dv_s", "sk_p", "sv_p", "sk_s", "sv_s",
                          "S_p", "cv_p", "cv_s")}
    s_s = state_dn_S
    for l in range(depth):
        j = l // N_MIXERS
        bias_op = (ada_b[l].reshape(1, 3 * d), pl.BlockSpec((1, 3 * d), lambda i: (0, 0)))
        mod, = _proj(c_all, ada_w[l].astype(BF16), [(3 * d, _add_bias, [bias_op])], tm=c_all.shape[0],
                     pro=_silu, name="adaln")
        mod_p = mod[:bp].reshape(bp, 1, 3 * d)
        mod_s = mod[SUBLANES:]
        if l % N_MIXERS == 0:
            xp, xs, s_p, cv_p, s_s, cv_s = _dn_layer(
                xp, xs, mod_p, mod_s, seq, state_dn_conv[j], s_s, j, dn_w_in[j], dn_conv_w[j],
                dn_a_log[j], dn_dt_bias[j], dn_norm_g[j], dn_w_out[j], ln_g[l], ln_b[l], alpha)
            st["S_p"].append(s_p)
            st["cv_p"].append(cv_p)
            st["cv_s"].append(cv_s)
        elif l % N_MIXERS == 1:
            xp, xs, k_p, v_p, k_s, v_s = _diff_layer(
                xp, xs, mod_p, mod_s, seq, l, j, cache_diff_k, cache_diff_v, page_table, diff_w_in[j],
                (diff_lq1[j], diff_lk1[j], diff_lq2[j], diff_lk2[j]), diff_subln_g[j], diff_w_out[j],
                ln_g[l], ln_b[l], alpha)
            st["dk_p"].append(k_p)
            st["dv_p"].append(v_p)
            st["dk_s"].append(k_s)
            st["dv_s"].append(v_s)
        else:
            xp, xs, k_p, v_p, k_s, v_s = _swa_layer(
                xp, xs, mod_p, mod_s, seq, cache_swa_k[j], cache_swa_v[j], swa_sinks[j], swa_w_in[j],
                swa_w_out[j], ln_g[l], ln_b[l], alpha)
            st["sk_p"].append(k_p)
            st["sv_p"].append(v_p)
            st["sk_s"].append(k_s)
            st["sv_s"].append(v_s)
    return (xp.reshape(bp, seq, d), xs.reshape(db, 1, d),
            jnp.stack(st["dk_p"]), jnp.stack(st["dv_p"]), jnp.stack(st["dk_s"]), jnp.stack(st["dv_s"]),
            jnp.stack(st["sk_p"]), jnp.stack(st["sv_p"]), jnp.stack(st["sk_s"]), jnp.stack(st["sv_s"]),
            jnp.stack(st["S_p"]), jnp.stack(st["cv_p"]), s_s, jnp.stack(st["cv_s"]))
```

```python
import functools
import math

import jax
import jax.numpy as jnp
from jax import lax
from jax.experimental import pallas as pl
from jax.experimental.pallas import tpu as pltpu

F32 = jnp.float32
BF16 = jnp.bfloat16

LN_EPS = 1e-5
RMS_EPS = 1e-6
N_MIXERS = 3

DN_HK = 8
DN_HV = 16
DN_DK = 128
DN_DV = 128
DN_REP = DN_HV // DN_HK
DN_CONV_W = 4
DN_CHUNK = 128
DN_CONV_COLS = 512
DN_STEP_SEQS = 4
PROJ_TM = 512

DIFF_H = 8
DIFF_KVH = 4
DIFF_G = DIFF_H // DIFF_KVH
DIFF_HD = 64
DIFF_VD = 2 * DIFF_HD
DIFF_TQ = 512
FLASH_QUERY_COLS = 256
LOG2E = math.log2(math.e)
DIFF_PAGES_PER_STEP = 32

SWA_H = 16
SWA_KVH = 2
SWA_G = SWA_H // SWA_KVH
SWA_HD = 64
WINDOW = 128
SWA_DEC_BB = 8

LANES = 128
SUBLANES = 8
VMEM_LIMIT = 48 * 1024 * 1024

_NT = (((1,), (1,)), ((), ()))
_TN = (((0,), (0,)), ((), ()))


def _silu(x):
    return (0.5 * x) * (1.0 + jnp.tanh(0.5 * x))


def _proj_body(x_ref, w_ref, *refs, n_pro, segs, pro):
    pro_refs = refs[:n_pro]
    n_epi = sum(s[3] for s in segs)
    epi_refs = refs[n_pro:n_pro + n_epi]
    o_refs = refs[n_pro + n_epi:]
    x = x_ref[...]
    if pro is not None:
        x = pro(x, *pro_refs)
    xb = x.astype(BF16)
    e = 0
    for (start, width, epi, n_e), o_ref in zip(segs, o_refs):
        acc = jnp.dot(xb, w_ref[:, start:start + width], preferred_element_type=F32)
        if epi is not None:
            acc = epi(acc, *epi_refs[e:e + n_e])
        e += n_e
        o_ref[...] = acc


def _proj(x, w, segments, *, tm, pro=None, pro_ops=(), name="proj"):
    m, k = x.shape
    n = w.shape[1]
    assert m % tm == 0, (m, tm)
    ops = [x, w]
    in_specs = [pl.BlockSpec((tm, k), lambda i: (i, 0)),
                pl.BlockSpec((k, n), lambda i: (0, 0), pipeline_mode=pl.Buffered(1))]
    for a, s in pro_ops:
        ops.append(a)
        in_specs.append(s)
    segs = []
    start = 0
    for width, epi, epi_ops in segments:
        segs.append((start, width, epi, len(epi_ops)))
        start += width
        for a, s in epi_ops:
            ops.append(a)
            in_specs.append(s)
    assert start <= n, (start, n)
    return pl.pallas_call(
        functools.partial(_proj_body, n_pro=len(pro_ops), segs=tuple(segs), pro=pro),
        grid=(m // tm,),
        in_specs=in_specs,
        out_specs=[pl.BlockSpec((tm, s[1]), lambda i: (i, 0)) for s in segs],
        out_shape=[jax.ShapeDtypeStruct((m, s[1]), F32) for s in segs],
        compiler_params=pltpu.CompilerParams(dimension_semantics=("parallel",), vmem_limit_bytes=VMEM_LIMIT),
        name=name,
    )(*ops)


def _modulate(x, mod_ref):
    d = x.shape[-1]
    return x * (1.0 + mod_ref[:, d:2 * d]) + mod_ref[:, 0:d]


def _deepnorm(y, xres_ref, mod_ref, g_ref, b_ref, *, alpha):
    d = y.shape[-1]
    h = alpha * xres_ref[...] + (1.0 + mod_ref[:, 2 * d:3 * d]) * y
    hc = h - jnp.mean(h, -1, keepdims=True)
    var = jnp.mean(hc * hc, -1, keepdims=True)
    return hc * lax.rsqrt(var + LN_EPS) * g_ref[...] + b_ref[...]


def _add_bias(y, b_ref):
    return y + b_ref[...]


def _dn_gates(y, alog_ref, dtb_ref):
    lane = lax.broadcasted_iota(jnp.int32, y.shape, 1)
    t = y + dtb_ref[...]
    softplus = jnp.maximum(t, 0.0) + jnp.log1p(jnp.exp(-jnp.abs(t)))
    g = -jnp.exp(alog_ref[...]) * softplus
    return jnp.where(lane < DN_HV, g, jax.nn.sigmoid(y))


def _mod_spec(mod, tm, rows_per_mod):
    w = mod.shape[-1]
    if mod.ndim == 3:
        assert rows_per_mod % tm == 0, (rows_per_mod, tm)
        return pl.BlockSpec((None, 1, w), lambda i: (i * tm // rows_per_mod, 0, 0))
    return pl.BlockSpec((tm, w), lambda i: (i, 0))


def _in_proj(x, mod, rows_per_mod, w, segments, *, tm, name="in_proj"):
    segments = [(s, None, ()) if isinstance(s, int) else s for s in segments]
    return _proj(x, w, segments, tm=tm, pro=_modulate, pro_ops=[(mod, _mod_spec(mod, tm, rows_per_mod))],
                 name=name)


def _out_proj_norm(y, w, xres, mod, rows_per_mod, ln_g, ln_b, *, tm, alpha, name="out_proj"):
    d = w.shape[1]
    row = pl.BlockSpec((1, d), lambda i: (0, 0))
    epi_ops = [(xres, pl.BlockSpec((tm, d), lambda i: (i, 0))), (mod, _mod_spec(mod, tm, rows_per_mod)),
               (ln_g.reshape(1, d), row), (ln_b.reshape(1, d), row)]
    return _proj(y, w, [(d, functools.partial(_deepnorm, alpha=alpha), epi_ops)], tm=tm, name=name)[0]


def _l2norm_heads(y, o_ref, scale):
    for h in range(y.shape[-1] // DN_DK):
        yh = y[:, h * DN_DK:(h + 1) * DN_DK]
        inv = lax.rsqrt(jnp.sum(yh * yh, -1, keepdims=True) + RMS_EPS)
        o_ref[:, h * DN_DK:(h + 1) * DN_DK] = yh * inv * scale


def _dn_in_conv_body(x_ref, w_ref, mod_ref, cw_ref, alog_ref, dtb_ref, qkv_ref, z_ref, g_ref, tail_ref,
                     halo_ref, *, tiles_per_seq, cb):
    i = pl.program_id(0)
    tm = x_ref.shape[0]
    conv_dim = qkv_ref.shape[1]
    v_w = z_ref.shape[1]
    xb = _modulate(x_ref[...], mod_ref).astype(BF16)
    first = i % tiles_per_seq == 0
    backs = range(1, DN_CONV_W)
    row = lax.broadcasted_iota(jnp.int32, (SUBLANES, cb), 0)

    def rolled(v):
        return [pltpu.roll(v, d, axis=0) for d in backs]

    def project(c0):
        return jnp.dot(xb, w_ref[:, c0:c0 + cb], preferred_element_type=F32)

    acc_next = project(0)
    for c0 in range(0, conv_dim, cb):
        cols = slice(c0, c0 + cb)
        acc = acc_next
        if c0 + cb < conv_dim:
            acc_next = project(c0 + cb)
        else:
            z_ref[...] = jnp.dot(xb, w_ref[:, conv_dim:conv_dim + v_w], preferred_element_type=F32)
            g_ref[...] = _dn_gates(jnp.dot(xb, w_ref[:, conv_dim + v_w:conv_dim + v_w + LANES],
                                           preferred_element_type=F32), alog_ref, dtb_ref)
        taps = [cw_ref[s:s + 1, cols] for s in range(DN_CONV_W)]
        prev_rolled = rolled(jnp.where(first, 0.0, halo_ref[:, cols]))
        for r0 in range(0, tm, SUBLANES):
            cur = acc[r0:r0 + SUBLANES]
            cur_rolled = rolled(cur)
            back = [jnp.where(row < d, p, q) for d, p, q in zip(backs, prev_rolled, cur_rolled)]
            conv = back[DN_CONV_W - 2] * taps[0]
            for s in range(1, DN_CONV_W - 1):
                conv = conv + back[DN_CONV_W - 2 - s] * taps[s]
            qkv_ref[r0:r0 + SUBLANES, cols] = _silu(conv + cur * taps[DN_CONV_W - 1])
            prev_rolled = cur_rolled
        last = acc[tm - SUBLANES:tm]
        halo_ref[:, cols] = last
        tail_ref[:, cols] = last


def _dn_in_conv(x, mod, seq, w, conv_w, alog, dtb, *, tm, cb):
    m, k = x.shape
    conv_dim = conv_w.shape[1]
    v_w = DN_HV * DN_DV
    assert seq % tm == 0 and m % seq == 0
    tiles_per_seq = seq // tm
    const = lambda i: (0, 0)
    rows = lambda i: (i, 0)
    return pl.pallas_call(
        functools.partial(_dn_in_conv_body, tiles_per_seq=tiles_per_seq, cb=cb),
        grid=(m // tm,),
        in_specs=[pl.BlockSpec((tm, k), rows),
                  pl.BlockSpec(w.shape, const, pipeline_mode=pl.Buffered(1)),
                  pl.BlockSpec((None, 1, mod.shape[-1]), lambda i: (i // tiles_per_seq, 0, 0)),
                  pl.BlockSpec(conv_w.shape, const), pl.BlockSpec((1, LANES), const),
                  pl.BlockSpec((1, LANES), const)],
        out_specs=[pl.BlockSpec((tm, conv_dim), rows), pl.BlockSpec((tm, v_w), rows),
                   pl.BlockSpec((tm, LANES), rows),
                   pl.BlockSpec((None, SUBLANES, conv_dim), lambda i: (i // tiles_per_seq, 0, 0))],
        out_shape=[jax.ShapeDtypeStruct((m, conv_dim), F32), jax.ShapeDtypeStruct((m, v_w), F32),
                   jax.ShapeDtypeStruct((m, LANES), F32),
                   jax.ShapeDtypeStruct((m // seq, SUBLANES, conv_dim), F32)],
        scratch_shapes=[pltpu.VMEM((SUBLANES, conv_dim), F32)],
        compiler_params=pltpu.CompilerParams(dimension_semantics=("arbitrary",), vmem_limit_bytes=VMEM_LIMIT),
        name="dn_in_conv_p",
    )(x, w, mod, conv_w, alog, dtb)


def _gated_rms(o, ng, z):
    return o * lax.rsqrt(jnp.mean(o * o, -1, keepdims=True) + RMS_EPS) * ng * _silu(z)


def _dn_chunk_body(x_ref, z_ref, gb_ref, ng_ref, o_ref, s_ref):
    ci = pl.program_id(1)
    c = x_ref.shape[0]
    hb = DN_HV

    @pl.when(ci == 0)
    def _():
        s_ref[...] = jnp.zeros_like(s_ref)

    k_w = DN_HK * DN_DK
    gb = gb_ref[...]
    row = lax.broadcasted_iota(jnp.int32, gb.shape, 0)
    gc = gb
    sh = 1
    while sh < c:
        gc = gc + jnp.where(row >= sh, pltpu.roll(gc, sh, axis=0), 0.0)
        sh *= 2
    gct = gc.T
    ri = lax.broadcasted_iota(jnp.int32, (c, c), 0)
    cj = lax.broadcasted_iota(jnp.int32, (c, c), 1)
    tril = ri >= cj
    strict = ri > cj
    merge_masks = []
    k = 0
    while (1 << k) < c:
        merge_masks.append(((ri >> (k + 1)) == (cj >> (k + 1))) & ((ri >> k) != (cj >> k)))
        k += 1
    ng = ng_ref[...]
    heads = range(hb)
    dot = functools.partial(jnp.dot, preferred_element_type=F32)
    def l2n(y, scale):
        return y * lax.rsqrt(jnp.sum(y * y, -1, keepdims=True) + RMS_EPS) * scale

    qh = [l2n(x_ref[:, i * DN_DK:(i + 1) * DN_DK], DN_DK ** -0.5) for i in range(DN_HK)]
    kh = [l2n(x_ref[:, k_w + i * DN_DK:k_w + (i + 1) * DN_DK], 1.0) for i in range(DN_HK)]
    kk = [lax.dot_general(a, a, _NT, preferred_element_type=F32) for a in kh]
    qk = [lax.dot_general(a, b, _NT, preferred_element_type=F32) for a, b in zip(qh, kh)]
    gcol = [gc[:, j:j + 1] for j in heads]
    bcol = [gb[:, hb + j:hb + j + 1] for j in heads]
    decay = [jnp.where(tril, jnp.exp(jnp.where(tril, gcol[j] - gct[j:j + 1, :], 0.0)), 0.0) for j in heads]
    lmat = [jnp.where(strict, bcol[j] * kk[j // DN_REP] * decay[j], 0.0) for j in heads]
    def odd_rows(a, blk):
        return jnp.concatenate([a[r:r + blk] for r in range(blk, c, 2 * blk)], axis=0)

    def spread_odd(a, blk):
        zero = jnp.zeros((blk, a.shape[1]), a.dtype)
        return jnp.concatenate([piece for t in range(c // (2 * blk))
                                for piece in (zero, a[t * blk:(t + 1) * blk])], axis=0)

    minv = [jnp.where(merge_masks[0], -lmat[j], 0.0) for j in heads]
    for lvl in range(1, len(merge_masks)):
        blk = 1 << lvl
        if blk < SUBLANES:
            loff = [jnp.where(merge_masks[lvl], lmat[j], 0.0) for j in heads]
            y = [loff[j] + dot(loff[j], minv[j]) for j in heads]
            minv = [minv[j] - y[j] - dot(minv[j], y[j]) for j in heads]
        else:
            rg = lax.broadcasted_iota(jnp.int32, (c // 2, c), 0)
            ro = ((rg >> lvl) << (lvl + 1)) + blk + (rg & (blk - 1))
            co = lax.broadcasted_iota(jnp.int32, (c // 2, c), 1)
            mask = ((ro >> (lvl + 1)) == (co >> (lvl + 1))) & ((ro >> lvl) != (co >> lvl))
            loff = [jnp.where(mask, odd_rows(lmat[j], blk), 0.0) for j in heads]
            y = [loff[j] + dot(loff[j], minv[j]) for j in heads]
            upd = [y[j] + dot(odd_rows(minv[j], blk), spread_odd(y[j], blk)) for j in heads]
            minv = [minv[j] - spread_odd(upd[j], blk) for j in heads]
    eg = [jnp.exp(gcol[j]) for j in heads]
    rhs = [jnp.concatenate([x_ref[:, 2 * k_w + j * DN_DV:2 * k_w + (j + 1) * DN_DV] * bcol[j],
                            kh[j // DN_REP] * (bcol[j] * eg[j])], axis=1) for j in heads]
    sol = [rhs[j] + dot(minv[j], rhs[j]) for j in heads]
    s_old = [s_ref[j] for j in heads]
    ws = [dot(jnp.concatenate([sol[j][:, DN_DV:], qh[j // DN_REP] * eg[j]], axis=0), s_old[j]) for j in heads]
    v_new = [sol[j][:, :DN_DV] - ws[j][:c] for j in heads]
    o = [ws[j][c:] + dot(qk[j // DN_REP] * decay[j], v_new[j]) for j in heads]
    for j in heads:
        glast = gc[c - 1:c, j:j + 1]
        kd = kh[j // DN_REP] * jnp.exp(glast - gcol[j])
        s_ref[j] = s_old[j] * jnp.exp(glast) + lax.dot_general(kd, v_new[j], _TN, preferred_element_type=F32)
        o_ref[:, j * DN_DV:(j + 1) * DN_DV] = _gated_rms(
            o[j], ng, z_ref[:, j * DN_DV:(j + 1) * DN_DV]).astype(o_ref.dtype)


def _dn_chunks(qkv, z, gates, norm_g):
    b, t, cd = qkv.shape
    c = DN_CHUNK
    assert t % c == 0
    v_w = DN_HV * DN_DV
    o, s = pl.pallas_call(
        _dn_chunk_body,
        grid=(b, t // c),
        in_specs=[pl.BlockSpec((None, c, cd), lambda bi, ci: (bi, ci, 0)),
                  pl.BlockSpec((None, c, v_w), lambda bi, ci: (bi, ci, 0)),
                  pl.BlockSpec((None, c, LANES), lambda bi, ci: (bi, ci, 0)),
                  pl.BlockSpec((1, DN_DV), lambda bi, ci: (0, 0))],
        out_specs=[pl.BlockSpec((None, c, v_w), lambda bi, ci: (bi, ci, 0)),
                   pl.BlockSpec((None, DN_HV, DN_DK, DN_DV), lambda bi, ci: (bi, 0, 0, 0))],
        out_shape=[jax.ShapeDtypeStruct((b, t, v_w), BF16),
                   jax.ShapeDtypeStruct((b, DN_HV, DN_DK, DN_DV), F32)],
        compiler_params=pltpu.CompilerParams(dimension_semantics=("parallel", "arbitrary"),
                                             vmem_limit_bytes=VMEM_LIMIT),
        name="dn_chunks",
    )(qkv, z, gates, norm_g.reshape(1, DN_DV))
    return o, s


def _dn_step_prep_body(x_ref, prev_ref, w_ref, o_ref, cv_ref, *, n_q_blk, n_k_blk):
    c = pl.program_id(0)
    x = x_ref[...]
    w = w_ref[...]
    conv = prev_ref[0] * w[0:1]
    for s in range(1, DN_CONV_W - 1):
        conv = conv + prev_ref[s] * w[s:s + 1]
    conv = conv + x * w[DN_CONV_W - 1:DN_CONV_W]
    y = _silu(conv)
    for s in range(DN_CONV_W - 2):
        cv_ref[s] = prev_ref[s + 1]
    cv_ref[DN_CONV_W - 2] = x

    @pl.when(c >= n_q_blk + n_k_blk)
    def _():
        o_ref[...] = y

    @pl.when(c < n_q_blk + n_k_blk)
    def _():
        _l2norm_heads(y, o_ref, jnp.where(c < n_q_blk, DN_DK ** -0.5, 1.0))


def _dn_step_prep(qkv, conv_prev_t, conv_w, *, cb):
    db, c = qkv.shape
    k_w = DN_HK * DN_DK
    nw = DN_CONV_W - 1
    return pl.pallas_call(
        functools.partial(_dn_step_prep_body, n_q_blk=k_w // cb, n_k_blk=k_w // cb),
        grid=(c // cb,),
        in_specs=[pl.BlockSpec((db, cb), lambda ci: (0, ci)),
                  pl.BlockSpec((nw, db, cb), lambda ci: (0, 0, ci)),
                  pl.BlockSpec((DN_CONV_W, cb), lambda ci: (0, ci))],
        out_specs=[pl.BlockSpec((db, cb), lambda ci: (0, ci)),
                   pl.BlockSpec((nw, db, cb), lambda ci: (0, 0, ci))],
        out_shape=[jax.ShapeDtypeStruct((db, c), F32), jax.ShapeDtypeStruct((nw, db, c), F32)],
        compiler_params=pltpu.CompilerParams(dimension_semantics=("parallel",), vmem_limit_bytes=VMEM_LIMIT),
        name="dn_step_prep",
    )(qkv, conv_prev_t, conv_w)


def _dn_step_body(q_ref, k_ref, v_ref, z_ref, gb_ref, ng_ref, s_ref, o_ref, so_ref, *, n_pass):
    g = pl.program_id(0)

    if n_pass:
        @pl.when(g < n_pass)
        def _():
            so_ref[...] = s_ref[...]

    @pl.when(g == n_pass)
    def _():
        ng = ng_ref[...]
        rows = lax.broadcasted_iota(jnp.int32, (SUBLANES, DN_DK), 0)
        heads = [(t, h) for t in range(q_ref.shape[0]) for h in range(DN_HV)]
        s_old = [s_ref[t, h] for t, h in heads]
        kh = [k_ref[t, h // DN_REP:h // DN_REP + 1, :] for t, h in heads]
        qh = [q_ref[t, h // DN_REP:h // DN_REP + 1, :] for t, h in heads]
        ks_qs = [jnp.dot(jnp.where(rows == 0, kh[n], jnp.where(rows == 1, qh[n], 0.0)), s_old[n],
                         preferred_element_type=F32) for n in range(len(heads))]
        eg = [jnp.exp(gb_ref[t, :, h:h + 1]) for t, h in heads]
        v_new = [gb_ref[t, :, DN_HV + h:DN_HV + h + 1] * (v_ref[t, h:h + 1, :] - eg[n] * ks_qs[n][0:1])
                 for n, (t, h) in enumerate(heads)]
        outer = [lax.dot_general(jnp.where(rows == 0, kh[n], 0.0), jnp.where(rows == 0, v_new[n], 0.0), _TN,
                                 preferred_element_type=F32) for n in range(len(heads))]
        for n, (t, h) in enumerate(heads):
            so_ref[t, h] = s_old[n] * eg[n] + outer[n]
            o = eg[n] * ks_qs[n][1:2] + jnp.sum(qh[n] * kh[n], -1, keepdims=True) * v_new[n]
            o_ref[t, h:h + 1, :] = _gated_rms(o, ng, z_ref[t, h:h + 1, :])


def _dn_step(q, k, v, z, gates, norm_g, s_all, layer, *, in_place):
    n, db = s_all.shape[:2]
    bs = DN_STEP_SEQS
    assert db % bs == 0
    n_pass = 0 if in_place else n - 1

    def s_layer(g):
        return layer if in_place else (layer + 1 + g) % n

    def row_spec(rows, width):
        return pl.BlockSpec((bs, rows, width), lambda g, b: (jnp.where(g == n_pass, b, 0), 0, 0))

    s_spec = pl.BlockSpec((None, bs, DN_HV, DN_DK, DN_DV), lambda g, b: (s_layer(g), b, 0, 0, 0))
    return pl.pallas_call(
        functools.partial(_dn_step_body, n_pass=n_pass),
        grid=(n_pass + 1, db // bs),
        in_specs=[row_spec(DN_HK, DN_DK), row_spec(DN_HK, DN_DK), row_spec(DN_HV, DN_DV),
                  row_spec(DN_HV, DN_DV), row_spec(1, LANES),
                  pl.BlockSpec((1, DN_DV), lambda g, b: (0, 0)), s_spec],
        out_specs=[row_spec(DN_HV, DN_DV), s_spec],
        out_shape=[jax.ShapeDtypeStruct((db, DN_HV, DN_DV), F32), jax.ShapeDtypeStruct(s_all.shape, F32)],
        input_output_aliases={6: 1} if in_place else {},
        compiler_params=pltpu.CompilerParams(dimension_semantics=("arbitrary", "arbitrary"),
                                             vmem_limit_bytes=VMEM_LIMIT),
        name="dn_step",
    )(q, k, v, z, gates, norm_g.reshape(1, DN_DV), s_all)


def _dn_layer(xp, xs, mod_p, mod_s, seq, conv_prev, s_all, layer, w_in, conv_w, a_log, dt_bias, norm_g, w_out,
              ln_g, ln_b, alpha):
    mp, d = xp.shape
    bp = mp // seq
    db = xs.shape[0]
    conv_dim = 2 * DN_HK * DN_DK + DN_HV * DN_DV
    v_w = DN_HV * DN_DV
    w_all = jnp.pad(w_in, ((0, 0), (0, LANES - 2 * DN_HV))).astype(BF16)
    w_o = w_out.astype(BF16)
    alog = jnp.pad(a_log, (0, LANES - DN_HV)).reshape(1, LANES)
    dtb = jnp.pad(dt_bias, (0, LANES - DN_HV)).reshape(1, LANES)
    lane_row = pl.BlockSpec((1, LANES), lambda i: (0, 0))
    segments = [conv_dim, v_w, (LANES, _dn_gates, [(alog, lane_row), (dtb, lane_row)])]

    qkvc, z, gates, tail = _dn_in_conv(xp, mod_p, seq, w_all, conv_w, alog, dtb, tm=PROJ_TM, cb=DN_CONV_COLS)
    cv_p = tail[:, SUBLANES - (DN_CONV_W - 1):]
    o, s_p = _dn_chunks(qkvc.reshape(bp, seq, conv_dim), z.reshape(bp, seq, v_w), gates.reshape(bp, seq, LANES), norm_g)
    xp_new = _out_proj_norm(o.reshape(mp, v_w), w_o, xp, mod_p, seq, ln_g, ln_b, tm=PROJ_TM, alpha=alpha,
                            name="dn_out_p")

    qkv_s, z_s, gates_s = _in_proj(xs, mod_s, 1, w_all, segments, tm=db, name="dn_in_s")
    qkvc_s, cv_t = _dn_step_prep(qkv_s, jnp.swapaxes(conv_prev, 0, 1), conv_w, cb=512)
    k_w = DN_HK * DN_DK
    o_s, s_s = _dn_step(qkvc_s[:, :k_w].reshape(db, DN_HK, DN_DK),
                        qkvc_s[:, k_w:2 * k_w].reshape(db, DN_HK, DN_DK),
                        qkvc_s[:, 2 * k_w:].reshape(db, DN_HV, DN_DV),
                        z_s.reshape(db, DN_HV, DN_DV), gates_s.reshape(db, 1, LANES), norm_g, s_all, layer,
                        in_place=layer > 0)
    xs_new = _out_proj_norm(o_s.reshape(db, v_w), w_o, xs, mod_s, 1, ln_g, ln_b, tm=db, alpha=alpha,
                            name="dn_out_s")
    return xp_new, xs_new, s_p, cv_p, s_s, jnp.swapaxes(cv_t, 0, 1)


def _diff_lambda(lq1_ref, lk1_ref, lq2_ref, lk2_ref, lam_init):
    e1 = jnp.exp(jnp.sum(lq1_ref[...] * lk1_ref[...], -1, keepdims=True))
    e2 = jnp.exp(jnp.sum(lq2_ref[...] * lk2_ref[...], -1, keepdims=True))
    return e1 - e2 + lam_init


def _diff_head_out(o, sg, z, lam_init):
    on = o * lax.rsqrt(jnp.mean(o * o, -1, keepdims=True) + RMS_EPS) * sg * (1.0 - lam_init)
    return on * _silu(z)


def _diff_flash_body(qi_ref, kj_ref, q_ref, k_ref, v_ref, z_ref, lq1_ref, lk1_ref, lq2_ref, lk2_ref, sg_ref,
                     o_ref, m_ref, l_ref, acc_ref, *, lam_init):
    i = qi_ref[pl.program_id(2)]
    j = kj_ref[pl.program_id(2)]
    tq, tk = q_ref.shape[0], k_ref.shape[0]
    gw = DIFF_G * tq

    @pl.when(j == 0)
    def _():
        m_ref[...] = jnp.full_like(m_ref, -jnp.inf)
        l_ref[...] = jnp.zeros_like(l_ref)
        acc_ref[...] = jnp.zeros_like(acc_ref)

    def update(masked):
        v = v_ref[...].astype(BF16)
        if masked:
            keep = lax.broadcasted_iota(jnp.int32, (tk, tq), 0) <= lax.broadcasted_iota(jnp.int32, (tk, tq), 1)
        qc = FLASH_QUERY_COLS
        groups = [(half, g, q0) for half in range(2) for g in range(DIFF_G) for q0 in range(0, tq, qc)]
        sts = []
        for half, g, q0 in groups:
            qm = q_ref[q0:q0 + qc, (g * 2 + half) * DIFF_HD:(g * 2 + half + 1) * DIFF_HD] * (
                DIFF_HD ** -0.5 * LOG2E)
            st = lax.dot_general(k_ref[:, half * DIFF_HD:(half + 1) * DIFF_HD], qm, _NT,
                                 preferred_element_type=F32)
            sts.append(jnp.where(keep[:, q0:q0 + qc], st, -jnp.inf) if masked else st)
        for n, (half, g, q0) in enumerate(groups):
            st = sts[n]
            cols = slice(half * gw + g * tq + q0, half * gw + g * tq + q0 + qc)
            m_prev = m_ref[:, cols]
            m_new = jnp.maximum(m_prev, jnp.max(st, 0, keepdims=True))
            alpha = jnp.exp2(m_prev - m_new)
            p = jnp.exp2(st - m_new)
            l_ref[:, cols] = alpha * l_ref[:, cols] + jnp.sum(p, 0, keepdims=True)
            acc_ref[:, cols] = alpha * acc_ref[:, cols] + lax.dot_general(
                v, p.astype(BF16), _TN, preferred_element_type=F32)
            m_ref[:, cols] = m_new

    @pl.when(j < i)
    def _():
        update(False)

    @pl.when(j == i)
    def _():
        update(True)
        lam = _diff_lambda(lq1_ref, lk1_ref, lq2_ref, lk2_ref, lam_init)
        sg = sg_ref[...]
        for g in range(DIFF_G):
            c0 = slice(g * tq, (g + 1) * tq)
            c1 = slice(gw + g * tq, gw + (g + 1) * tq)
            ot = acc_ref[:, c0] * (1.0 / l_ref[:, c0]) - lam * (acc_ref[:, c1] * (1.0 / l_ref[:, c1]))
            o_ref[:, g * DIFF_VD:(g + 1) * DIFF_VD] = _diff_head_out(
                ot.T, sg, z_ref[:, g * DIFF_VD:(g + 1) * DIFF_VD], lam_init).astype(o_ref.dtype)


def _diff_flash(q, k, v, z, lams, subln_g, lam_init, *, tq):
    b, t, _ = q.shape
    assert t % tq == 0 and tq & (tq - 1) == 0, (t, tq)
    nq = t // tq
    qw = DIFF_G * 2 * DIFF_HD
    ow = DIFF_G * DIFF_VD
    n_sub = 2 * DIFF_G
    pairs = [(i, j) for i in range(nq) for j in range(i + 1)]
    qi = jnp.asarray([p[0] for p in pairs], jnp.int32)
    kj = jnp.asarray([p[1] for p in pairs], jnp.int32)
    vec = pl.BlockSpec((1, DIFF_HD), lambda bi, h, s, qi, kj: (0, 0))
    grid_spec = pltpu.PrefetchScalarGridSpec(
        num_scalar_prefetch=2,
        grid=(b, DIFF_KVH, len(pairs)),
        in_specs=[pl.BlockSpec((None, tq, qw), lambda bi, h, s, qi, kj: (bi, qi[s], h)),
                  pl.BlockSpec((None, tq, 2 * DIFF_HD), lambda bi, h, s, qi, kj: (bi, kj[s], h)),
                  pl.BlockSpec((None, tq, DIFF_VD), lambda bi, h, s, qi, kj: (bi, kj[s], h)),
                  pl.BlockSpec((None, tq, ow), lambda bi, h, s, qi, kj: (bi, qi[s], h)),
                  vec, vec, vec, vec,
                  pl.BlockSpec((1, DIFF_VD), lambda bi, h, s, qi, kj: (0, 0))],
        out_specs=pl.BlockSpec((None, tq, ow), lambda bi, h, s, qi, kj: (bi, qi[s], h)),
        scratch_shapes=[pltpu.VMEM((1, n_sub * tq), F32), pltpu.VMEM((1, n_sub * tq), F32),
                        pltpu.VMEM((DIFF_VD, n_sub * tq), F32)],
    )
    return pl.pallas_call(
        functools.partial(_diff_flash_body, lam_init=lam_init),
        grid_spec=grid_spec,
        out_shape=jax.ShapeDtypeStruct((b, t, DIFF_H * DIFF_VD), BF16),
        compiler_params=pltpu.CompilerParams(
            dimension_semantics=("parallel", "parallel", "arbitrary"),
            vmem_limit_bytes=VMEM_LIMIT),
        name="diff_flash",
    )(qi, kj, q, k, v, z, *lams, subln_g.reshape(1, DIFF_VD))


def _diff_decode_body(pt_ref, q_ref, ks_ref, vs_ref, z_ref, lq1_ref, lk1_ref, lq2_ref, lk2_ref, sg_ref,
                      *rest, pp, lam_init):
    del pt_ref
    k_pages = rest[:pp]
    v_pages = rest[pp:2 * pp]
    o_ref = rest[2 * pp]
    qb_ref, m_ref, l_ref, acc_ref = rest[2 * pp + 1:]
    j = pl.program_id(1)
    n_rows = DIFF_KVH * DIFF_G * 2
    head_shift = (DIFF_G * 2).bit_length() - 1
    kw = 2 * DIFF_HD
    page_rows = k_pages[0].shape[0]

    @pl.when(j == 0)
    def _():
        d_i = lax.broadcasted_iota(jnp.int32, (DIFF_HD, kw), 0)
        c_i = lax.broadcasted_iota(jnp.int32, (DIFF_HD, kw), 1)
        spread = jnp.where((c_i & (DIFF_HD - 1)) == d_i, 1.0, 0.0)
        qt = jnp.dot(q_ref[...], spread, preferred_element_type=F32)
        r_i = lax.broadcasted_iota(jnp.int32, (n_rows, kw), 0)
        half = lax.broadcasted_iota(jnp.int32, (n_rows, kw), 1) >> (DIFF_HD.bit_length() - 1)
        qb_ref[...] = jnp.where(half == (r_i & 1), qt, 0.0) * (DIFF_HD ** -0.5)
        m_ref[...] = jnp.full_like(m_ref, -jnp.inf)
        l_ref[...] = jnp.zeros_like(l_ref)
        acc_ref[...] = jnp.zeros_like(acc_ref)

    qb = qb_ref[...]
    s = jnp.concatenate([lax.dot_general(qb, kp[...], _NT, preferred_element_type=F32) for kp in k_pages],
                        axis=1)
    row_head = lax.broadcasted_iota(jnp.int32, s.shape, 0) >> head_shift
    col_head = lax.broadcasted_iota(jnp.int32, s.shape, 1) & (DIFF_KVH - 1)
    s = jnp.where(row_head == col_head, s, -jnp.inf)
    m_prev = m_ref[...]
    m_new = jnp.maximum(m_prev, jnp.max(s, -1, keepdims=True))
    alpha = jnp.exp(m_prev - m_new)
    p = jnp.exp(s - m_new)
    l_new = alpha * l_ref[...] + jnp.sum(p, -1, keepdims=True)
    pv = jnp.dot(p[:, 0:page_rows], v_pages[0][...], preferred_element_type=F32)
    for t in range(1, pp):
        pv = pv + jnp.dot(p[:, t * page_rows:(t + 1) * page_rows], v_pages[t][...],
                          preferred_element_type=F32)
    acc_new = alpha * acc_ref[...] + pv
    m_ref[...] = m_new
    l_ref[...] = l_new
    acc_ref[...] = acc_new

    @pl.when(j == pl.num_programs(1) - 1)
    def _():
        rh = lax.broadcasted_iota(jnp.int32, (n_rows, kw), 0) >> head_shift
        ks = jnp.zeros((n_rows, kw), F32)
        vs = jnp.zeros((n_rows, kw), F32)
        for h in range(DIFF_KVH):
            ks = jnp.where(rh == h, ks_ref[h:h + 1, :], ks)
            vs = jnp.where(rh == h, vs_ref[h:h + 1, :], vs)
        s_self = jnp.sum(qb * ks, -1, keepdims=True)
        m_fin = jnp.maximum(m_new, s_self)
        a = jnp.exp(m_new - m_fin)
        p_self = jnp.exp(s_self - m_fin)
        l_fin = a * l_new + p_self
        acc = (a * acc_new + p_self * vs) / l_fin
        lam = _diff_lambda(lq1_ref, lk1_ref, lq2_ref, lk2_ref, lam_init)
        sg = sg_ref[...]
        for hh in range(DIFF_H):
            o = acc[2 * hh:2 * hh + 1] - lam * acc[2 * hh + 1:2 * hh + 2]
            o_ref[hh:hh + 1, :] = _diff_head_out(o, sg, z_ref[hh:hh + 1, :], lam_init)


def _diff_decode(q, k_self, v_self, z, cache_k, cache_v, layer, page_table, lams, subln_g, lam_init, *, pp):
    db = q.shape[0]
    n_pages = page_table.shape[1]
    assert n_pages % pp == 0
    page_rows, kw = cache_k.shape[2:]
    n_rows = DIFF_KVH * DIFF_G * 2

    def page_spec(t):
        return pl.BlockSpec((None, None, page_rows, kw), lambda b, j, pt: (layer, pt[b, j * pp + t], 0, 0))

    def seq_spec(shape):
        return pl.BlockSpec((None,) + shape, lambda b, j, pt: (b, 0, 0))

    vec = pl.BlockSpec((1, DIFF_HD), lambda b, j, pt: (0, 0))
    grid_spec = pltpu.PrefetchScalarGridSpec(
        num_scalar_prefetch=1,
        grid=(db, n_pages // pp),
        in_specs=[seq_spec((n_rows, DIFF_HD)), seq_spec((DIFF_KVH, kw)), seq_spec((DIFF_KVH, kw)),
                  seq_spec((DIFF_H, DIFF_VD)), vec, vec, vec, vec,
                  pl.BlockSpec((1, DIFF_VD), lambda b, j, pt: (0, 0))]
                 + [page_spec(t) for t in range(pp)] + [page_spec(t) for t in range(pp)],
        out_specs=seq_spec((DIFF_H, DIFF_VD)),
        scratch_shapes=[pltpu.VMEM((n_rows, kw), F32), pltpu.VMEM((n_rows, 1), F32),
                        pltpu.VMEM((n_rows, 1), F32), pltpu.VMEM((n_rows, DIFF_VD), F32)],
    )
    return pl.pallas_call(
        functools.partial(_diff_decode_body, pp=pp, lam_init=lam_init),
        grid_spec=grid_spec,
        out_shape=jax.ShapeDtypeStruct((db, DIFF_H, DIFF_VD), F32),
        compiler_params=pltpu.CompilerParams(dimension_semantics=("parallel", "arbitrary"),
                                             vmem_limit_bytes=VMEM_LIMIT),
        name="diff_decode",
    )(page_table, q, k_self, v_self, z, *lams, subln_g.reshape(1, DIFF_VD),
      *([cache_k] * pp), *([cache_v] * pp))


def _diff_layer(xp, xs, mod_p, mod_s, seq, layer_idx, j, cache_k, cache_v, page_table, w_in, lams, subln_g,
                w_out, ln_g, ln_b, alpha):
    mp, d = xp.shape
    bp = mp // seq
    db = xs.shape[0]
    lam_init = 0.8 - 0.6 * math.exp(-0.3 * layer_idx)
    q_w = DIFF_H * 2 * DIFF_HD
    k_w = DIFF_KVH * 2 * DIFF_HD
    v_w = DIFF_KVH * DIFF_VD
    w_all = w_in.astype(BF16)
    w_o = w_out.astype(BF16)
    lams = [a.reshape(1, DIFF_HD) for a in lams]
    segments = [q_w, k_w, v_w, w_in.shape[1] - q_w - k_w - v_w]

    q, k, v, z = _in_proj(xp, mod_p, seq, w_all, segments, tm=PROJ_TM, name="diff_in_p")
    o = _diff_flash(q.reshape(bp, seq, q_w), k.reshape(bp, seq, k_w), v.reshape(bp, seq, v_w),
                    z.reshape(bp, seq, -1), lams, subln_g, lam_init, tq=min(DIFF_TQ, seq))
    xp_new = _out_proj_norm(o.reshape(mp, -1), w_o, xp, mod_p, seq, ln_g, ln_b, tm=PROJ_TM, alpha=alpha,
                            name="diff_out_p")

    q_s, k_s, v_s, z_s = _in_proj(xs, mod_s, 1, w_all, segments, tm=db, name="diff_in_s")
    n_l, n_pool, page = cache_k.shape[:3]
    o_s = _diff_decode(q_s.reshape(db, DIFF_KVH * DIFF_G * 2, DIFF_HD), k_s.reshape(db, DIFF_KVH, 2 * DIFF_HD),
                       v_s.reshape(db, DIFF_KVH, DIFF_VD), z_s.reshape(db, DIFF_H, DIFF_VD),
                       cache_k.reshape(n_l, n_pool, page * DIFF_KVH, 2 * DIFF_HD),
                       cache_v.reshape(n_l, n_pool, page * DIFF_KVH, DIFF_VD),
                       j, page_table, lams, subln_g, lam_init, pp=DIFF_PAGES_PER_STEP)
    xs_new = _out_proj_norm(o_s.reshape(db, -1), w_o, xs, mod_s, 1, ln_g, ln_b, tm=db, alpha=alpha,
                            name="diff_out_s")
    return (xp_new, xs_new, k.reshape(bp, seq, DIFF_KVH, 2 * DIFF_HD), v.reshape(bp, seq, DIFF_KVH, DIFF_VD),
            k_s.reshape(db, 1, DIFF_KVH, 2 * DIFF_HD), v_s.reshape(db, 1, DIFF_KVH, DIFF_VD))


def _swa_band_body(q_ref, kc_ref, kp_ref, vc_ref, vp_ref, z_ref, sink_ref, o_ref):
    i = pl.program_id(1)
    blk = q_ref.shape[0]
    kj = lax.broadcasted_iota(jnp.int32, (2 * blk, 2 * blk), 0)
    qi = lax.broadcasted_iota(jnp.int32, (2 * blk, 2 * blk), 1) & (blk - 1)
    dist = qi + blk - kj
    keep = (dist >= 0) & (dist <= WINDOW) & ((i > 0) | (kj >= blk))
    chains = [(h, g) for h in range(SWA_KVH) for g in range(0, SWA_G, 2)]
    k2 = [jnp.concatenate([kp_ref[:, h * SWA_HD:(h + 1) * SWA_HD], kc_ref[:, h * SWA_HD:(h + 1) * SWA_HD]],
                          axis=0) for h in range(SWA_KVH)]
    v2 = [jnp.concatenate([vp_ref[:, h * SWA_HD:(h + 1) * SWA_HD], vc_ref[:, h * SWA_HD:(h + 1) * SWA_HD]],
                          axis=0) for h in range(SWA_KVH)]
    sts = []
    for h, g in chains:
        c0 = (h * SWA_G + g) * SWA_HD
        qs = jnp.concatenate([q_ref[:, c0:c0 + SWA_HD], q_ref[:, c0 + SWA_HD:c0 + 2 * SWA_HD]],
                             axis=0) * (SWA_HD ** -0.5)
        st = lax.dot_general(k2[h], qs, _NT, preferred_element_type=F32)
        sts.append(jnp.where(keep, st, -jnp.inf))
    for n, (h, g) in enumerate(chains):
        hd = h * SWA_G + g
        sink = jnp.concatenate([jnp.broadcast_to(sink_ref[:, hd + t:hd + t + 1], (1, blk)) for t in range(2)],
                               axis=1)
        m = jnp.maximum(jnp.max(sts[n], 0, keepdims=True), sink)
        p = jnp.exp(sts[n] - m)
        den = jnp.sum(p, 0, keepdims=True) + jnp.exp(sink - m)
        ot = lax.dot_general(v2[h], p, _TN, preferred_element_type=F32) * (1.0 / den)
        cs = slice(hd * SWA_HD, (hd + 2) * SWA_HD)
        pair = jnp.concatenate([ot[:, 0:blk], ot[:, blk:2 * blk]], axis=0)
        o_ref[:, cs] = (pair.T * _silu(z_ref[:, cs])).astype(o_ref.dtype)


def _swa_band(q, k, v, z, sinks):
    b, t, qw = q.shape
    blk = WINDOW
    kw = SWA_KVH * SWA_HD
    cur = pl.BlockSpec((None, blk, kw), lambda bi, i: (bi, i, 0))
    prev = pl.BlockSpec((None, blk, kw), lambda bi, i: (bi, jnp.maximum(i - 1, 0), 0))
    wide = pl.BlockSpec((None, blk, qw), lambda bi, i: (bi, i, 0))
    return pl.pallas_call(
        _swa_band_body,
        grid=(b, t // blk),
        in_specs=[wide, cur, prev, cur, prev, wide, pl.BlockSpec((1, SWA_H), lambda bi, i: (0, 0))],
        out_specs=wide,
        out_shape=jax.ShapeDtypeStruct((b, t, qw), BF16),
        compiler_params=pltpu.CompilerParams(dimension_semantics=("parallel", "parallel"),
                                             vmem_limit_bytes=VMEM_LIMIT),
        name="swa_band",
    )(q, k, k, v, v, z, sinks.reshape(1, SWA_H))


def _swa_decode_body(q_ref, kb_ref, vb_ref, kn_ref, vn_ref, z_ref, sink_ref, o_ref, ko_ref, vo_ref):
    win = kb_ref.shape[1]
    for h in range(SWA_KVH):
        hs = slice(h * SWA_HD, (h + 1) * SWA_HD)
        gs = slice(h * SWA_G, (h + 1) * SWA_G)
        qh = q_ref[:, gs, :]
        s = jnp.einsum("bqd,bkd->bqk", qh, kb_ref[:, :, hs], preferred_element_type=F32) * (SWA_HD ** -0.5)
        s_new = jnp.sum(qh * kn_ref[:, :, hs], -1, keepdims=True) * (SWA_HD ** -0.5)
        sink = sink_ref[gs, :][None]
        m = jnp.maximum(jnp.maximum(jnp.max(s, -1, keepdims=True), s_new), sink)
        p = jnp.exp(s - m)
        p_new = jnp.exp(s_new - m)
        den = jnp.sum(p, -1, keepdims=True) + p_new + jnp.exp(sink - m)
        o = jnp.einsum("bqk,bkd->bqd", p, vb_ref[:, :, hs], preferred_element_type=F32)
        o = (o + p_new * vn_ref[:, :, hs]) / den
        o_ref[:, gs, :] = o * _silu(z_ref[:, gs, :])
    ko_ref[:, 0:win - 1, :] = kb_ref[:, 1:win, :]
    ko_ref[:, win - 1:win, :] = kn_ref[...]
    vo_ref[:, 0:win - 1, :] = vb_ref[:, 1:win, :]
    vo_ref[:, win - 1:win, :] = vn_ref[...]


def _swa_decode(q, k_new, v_new, z, buf_k, buf_v, sinks, *, bb):
    db, win, kw = buf_k.shape
    qz = pl.BlockSpec((bb, SWA_H, SWA_HD), lambda b: (b, 0, 0))
    buf = pl.BlockSpec((bb, win, kw), lambda b: (b, 0, 0))
    new = pl.BlockSpec((bb, 1, kw), lambda b: (b, 0, 0))
    return pl.pallas_call(
        _swa_decode_body,
        grid=(db // bb,),
        in_specs=[qz, buf, buf, new, new, qz, pl.BlockSpec((SWA_H, 1), lambda b: (0, 0))],
        out_specs=[qz, buf, buf],
        out_shape=[jax.ShapeDtypeStruct(q.shape, F32), jax.ShapeDtypeStruct(buf_k.shape, F32),
                   jax.ShapeDtypeStruct(buf_v.shape, F32)],
        compiler_params=pltpu.CompilerParams(dimension_semantics=("parallel",), vmem_limit_bytes=VMEM_LIMIT),
        name="swa_decode",
    )(q, buf_k, buf_v, k_new, v_new, z, sinks.reshape(SWA_H, 1))


def _swa_layer(xp, xs, mod_p, mod_s, seq, buf_k, buf_v, sinks, w_in, w_out, ln_g, ln_b, alpha):
    mp, d = xp.shape
    bp = mp // seq
    db = xs.shape[0]
    q_w = SWA_H * SWA_HD
    kv_w = SWA_KVH * SWA_HD
    w_all = w_in.astype(BF16)
    w_o = w_out.astype(BF16)
    segments = [q_w, kv_w, kv_w, w_in.shape[1] - q_w - 2 * kv_w]

    q, k, v, z = _in_proj(xp, mod_p, seq, w_all, segments, tm=PROJ_TM, name="swa_in_p")
    k = k.reshape(bp, seq, kv_w)
    v = v.reshape(bp, seq, kv_w)
    o = _swa_band(q.reshape(bp, seq, q_w), k, v, z.reshape(bp, seq, q_w), sinks)
    xp_new = _out_proj_norm(o.reshape(mp, q_w), w_o, xp, mod_p, seq, ln_g, ln_b, tm=PROJ_TM, alpha=alpha,
                            name="swa_out_p")

    q_s, k_s, v_s, z_s = _in_proj(xs, mod_s, 1, w_all, segments, tm=db, name="swa_in_s")
    win = buf_k.shape[1]
    o_s, k_buf, v_buf = _swa_decode(q_s.reshape(db, SWA_H, SWA_HD), k_s.reshape(db, 1, kv_w),
                                    v_s.reshape(db, 1, kv_w), z_s.reshape(db, SWA_H, SWA_HD),
                                    buf_k.reshape(db, win, kv_w), buf_v.reshape(db, win, kv_w), sinks,
                                    bb=SWA_DEC_BB)
    xs_new = _out_proj_norm(o_s.reshape(db, q_w), w_o, xs, mod_s, 1, ln_g, ln_b, tm=db, alpha=alpha,
                            name="swa_out_s")
    kv_shape = (SWA_KVH, SWA_HD)
    return (xp_new, xs_new, k[:, seq - WINDOW:].reshape((bp, WINDOW) + kv_shape),
            v[:, seq - WINDOW:].reshape((bp, WINDOW) + kv_shape),
            k_buf.reshape((db, win) + kv_shape), v_buf.reshape((db, win) + kv_shape))


def kernel(x_prompt, x_sample, cache_diff_k, cache_diff_v, page_table, cache_swa_k, cache_swa_v, state_dn_S, state_dn_conv, c_prompt, c_sample, ada_w, ada_b, ln_g, ln_b, dn_w_in, dn_conv_w, dn_a_log, dn_dt_bias, dn_norm_g, dn_w_out, diff_w_in, diff_lq1, diff_lk1, diff_lq2, diff_lk2, diff_subln_g, diff_w_out, swa_w_in, swa_sinks, swa_w_out):
    bp, seq, d = x_prompt.shape
    db, dec_seq, _ = x_sample.shape
    assert dec_seq == 1 and bp <= SUBLANES
    depth = ada_w.shape[0]
    alpha = (2 * depth) ** 0.25
    xp = x_prompt.reshape(bp * seq, d)
    xs = x_sample.reshape(db, d)

    c_all = jnp.concatenate([c_prompt, jnp.zeros((SUBLANES - bp, d), F32), c_sample], axis=0)
    st = {n: [] for n in ("dk_p", "dv_p", "dk_s", "dv_s", "sk_p", "sv_p", "sk_s", "sv_s",
                          "S_p", "cv_p", "cv_s")}
    s_s = state_dn_S
    for l in range(depth):
        j = l // N_MIXERS
        bias_op = (ada_b[l].reshape(1, 3 * d), pl.BlockSpec((1, 3 * d), lambda i: (0, 0)))
        mod, = _proj(c_all, ada_w[l].astype(BF16), [(3 * d, _add_bias, [bias_op])], tm=c_all.shape[0],
                     pro=_silu, name="adaln")
        mod_p = mod[:bp].reshape(bp, 1, 3 * d)
        mod_s = mod[SUBLANES:]
        if l % N_MIXERS == 0:
            xp, xs, s_p, cv_p, s_s, cv_s = _dn_layer(
                xp, xs, mod_p, mod_s, seq, state_dn_conv[j], s_s, j, dn_w_in[j], dn_conv_w[j],
                dn_a_log[j], dn_dt_bias[j], dn_norm_g[j], dn_w_out[j], ln_g[l], ln_b[l], alpha)
            st["S_p"].append(s_p)
            st["cv_p"].append(cv_p)
            st["cv_s"].append(cv_s)
        elif l % N_MIXERS == 1:
            xp, xs, k_p, v_p, k_s, v_s = _diff_layer(
                xp, xs, mod_p, mod_s, seq, l, j, cache_diff_k, cache_diff_v, page_table, diff_w_in[j],
                (diff_lq1[j], diff_lk1[j], diff_lq2[j], diff_lk2[j]), diff_subln_g[j], diff_w_out[j],
                ln_g[l], ln_b[l], alpha)
            st["dk_p"].append(k_p)
            st["dv_p"].append(v_p)
            st["dk_s"].append(k_s)
            st["dv_s"].append(v_s)
        else:
            xp, xs, k_p, v_p, k_s, v_s = _swa_layer(
                xp, xs, mod_p, mod_s, seq, cache_swa_k[j], cache_swa_v[j], swa_sinks[j], swa_w_in[j],
                swa_w_out[j], ln_g[l], ln_b[l], alpha)
            st["sk_p"].append(k_p)
            st["sv_p"].append(v_p)
            st["sk_s"].append(k_s)
            st["sv_s"].append(v_s)
    return (xp.reshape(bp, seq, d), xs.reshape(db, 1, d),
            jnp.stack(st["dk_p"]), jnp.stack(st["dv_p"]), jnp.stack(st["dk_s"]), jnp.stack(st["dv_s"]),
            jnp.stack(st["sk_p"]), jnp.stack(st["sv_p"]), jnp.stack(st["sk_s"]), jnp.stack(st["sv_s"]),
            jnp.stack(st["S_p"]), jnp.stack(st["cv_p"]), s_s, jnp.stack(st["cv_s"]))
```

```python
import functools
import math

import jax
import jax.numpy as jnp
from jax import lax
from jax.experimental import pallas as pl
from jax.experimental.pallas import tpu as pltpu

F32 = jnp.float32
BF16 = jnp.bfloat16

LN_EPS = 1e-5
RMS_EPS = 1e-6
N_MIXERS = 3

DN_HK = 8
DN_HV = 16
DN_DK = 128
DN_DV = 128
DN_REP = DN_HV // DN_HK
DN_CONV_W = 4
DN_CHUNK = 128
DN_CONV_COLS = 512
DN_STEP_SEQS = 4
PROJ_TM = 512

DIFF_H = 8
DIFF_KVH = 4
DIFF_G = DIFF_H // DIFF_KVH
DIFF_HD = 64
DIFF_VD = 2 * DIFF_HD
DIFF_TQ = 512
FLASH_QUERY_COLS = 256
LOG2E = math.log2(math.e)
DIFF_PAGES_PER_STEP = 32

SWA_H = 16
SWA_KVH = 2
SWA_G = SWA_H // SWA_KVH
SWA_HD = 64
WINDOW = 128
SWA_DEC_BB = 8

LANES = 128
SUBLANES = 8
VMEM_LIMIT = 48 * 1024 * 1024

_NT = (((1,), (1,)), ((), ()))
_TN = (((0,), (0,)), ((), ()))


def _silu(x):
    return (0.5 * x) * (1.0 + jnp.tanh(0.5 * x))


def _proj_body(x_ref, w_ref, *refs, n_pro, segs, pro):
    pro_refs = refs[:n_pro]
    n_epi = sum(s[3] for s in segs)
    epi_refs = refs[n_pro:n_pro + n_epi]
    o_refs = refs[n_pro + n_epi:]
    x = x_ref[...]
    if pro is not None:
        x = pro(x, *pro_refs)
    xb = x.astype(BF16)
    e = 0
    for (start, width, epi, n_e), o_ref in zip(segs, o_refs):
        acc = jnp.dot(xb, w_ref[:, start:start + width], preferred_element_type=F32)
        if epi is not None:
            acc = epi(acc, *epi_refs[e:e + n_e])
        e += n_e
        o_ref[...] = acc


def _proj(x, w, segments, *, tm, pro=None, pro_ops=(), name="proj"):
    m, k = x.shape
    n = w.shape[1]
    assert m % tm == 0, (m, tm)
    ops = [x, w]
    in_specs = [pl.BlockSpec((tm, k), lambda i: (i, 0)),
                pl.BlockSpec((k, n), lambda i: (0, 0), pipeline_mode=pl.Buffered(1))]
    for a, s in pro_ops:
        ops.append(a)
        in_specs.append(s)
    segs = []
    start = 0
    for width, epi, epi_ops in segments:
        segs.append((start, width, epi, len(epi_ops)))
        start += width
        for a, s in epi_ops:
            ops.append(a)
            in_specs.append(s)
    assert start <= n, (start, n)
    return pl.pallas_call(
        functools.partial(_proj_body, n_pro=len(pro_ops), segs=tuple(segs), pro=pro),
        grid=(m // tm,),
        in_specs=in_specs,
        out_specs=[pl.BlockSpec((tm, s[1]), lambda i: (i, 0)) for s in segs],
        out_shape=[jax.ShapeDtypeStruct((m, s[1]), F32) for s in segs],
        compiler_params=pltpu.CompilerParams(dimension_semantics=("parallel",), vmem_limit_bytes=VMEM_LIMIT),
        name=name,
    )(*ops)


def _modulate(x, mod_ref):
    d = x.shape[-1]
    return x * (1.0 + mod_ref[:, d:2 * d]) + mod_ref[:, 0:d]


def _deepnorm(y, xres_ref, mod_ref, g_ref, b_ref, *, alpha):
    d = y.shape[-1]
    h = alpha * xres_ref[...] + (1.0 + mod_ref[:, 2 * d:3 * d]) * y
    hc = h - jnp.mean(h, -1, keepdims=True)
    var = jnp.mean(hc * hc, -1, keepdims=True)
    return hc * lax.rsqrt(var + LN_EPS) * g_ref[...] + b_ref[...]


def _add_bias(y, b_ref):
    return y + b_ref[...]


def _dn_gates(y, alog_ref, dtb_ref):
    lane = lax.broadcasted_iota(jnp.int32, y.shape, 1)
    t = y + dtb_ref[...]
    softplus = jnp.maximum(t, 0.0) + jnp.log1p(jnp.exp(-jnp.abs(t)))
    g = -jnp.exp(alog_ref[...]) * softplus
    return jnp.where(lane < DN_HV, g, jax.nn.sigmoid(y))


def _mod_spec(mod, tm, rows_per_mod):
    w = mod.shape[-1]
    if mod.ndim == 3:
        assert rows_per_mod % tm == 0, (rows_per_mod, tm)
        return pl.BlockSpec((None, 1, w), lambda i: (i * tm // rows_per_mod, 0, 0))
    return pl.BlockSpec((tm, w), lambda i: (i, 0))


def _in_proj(x, mod, rows_per_mod, w, segments, *, tm, name="in_proj"):
    segments = [(s, None, ()) if isinstance(s, int) else s for s in segments]
    return _proj(x, w, segments, tm=tm, pro=_modulate, pro_ops=[(mod, _mod_spec(mod, tm, rows_per_mod))],
                 name=name)


def _out_proj_norm(y, w, xres, mod, rows_per_mod, ln_g, ln_b, *, tm, alpha, name="out_proj"):
    d = w.shape[1]
    row = pl.BlockSpec((1, d), lambda i: (0, 0))
    epi_ops = [(xres, pl.BlockSpec((tm, d), lambda i: (i, 0))), (mod, _mod_spec(mod, tm, rows_per_mod)),
               (ln_g.reshape(1, d), row), (ln_b.reshape(1, d), row)]
    return _proj(y, w, [(d, functools.partial(_deepnorm, alpha=alpha), epi_ops)], tm=tm, name=name)[0]


def _l2norm_heads(y, o_ref, scale):
    for h in range(y.shape[-1] // DN_DK):
        yh = y[:, h * DN_DK:(h + 1) * DN_DK]
        inv = lax.rsqrt(jnp.sum(yh * yh, -1, keepdims=True) + RMS_EPS)
        o_ref[:, h * DN_DK:(h + 1) * DN_DK] = yh * inv * scale


def _dn_in_conv_body(x_ref, w_ref, mod_ref, cw_ref, alog_ref, dtb_ref, qkv_ref, z_ref, g_ref, tail_ref,
                     halo_ref, *, tiles_per_seq, cb):
    i = pl.program_id(0)
    tm = x_ref.shape[0]
    conv_dim = qkv_ref.shape[1]
    v_w = z_ref.shape[1]
    xb = _modulate(x_ref[...], mod_ref).astype(BF16)
    first = i % tiles_per_seq == 0
    backs = range(1, DN_CONV_W)
    row = lax.broadcasted_iota(jnp.int32, (SUBLANES, cb), 0)

    def rolled(v):
        return [pltpu.roll(v, d, axis=0) for d in backs]

    def project(c0):
        return jnp.dot(xb, w_ref[:, c0:c0 + cb], preferred_element_type=F32)

    acc_next = project(0)
    for c0 in range(0, conv_dim, cb):
        cols = slice(c0, c0 + cb)
        acc = acc_next
        if c0 + cb < conv_dim:
            acc_next = project(c0 + cb)
        else:
            z_ref[...] = jnp.dot(xb, w_ref[:, conv_dim:conv_dim + v_w], preferred_element_type=F32)
            g_ref[...] = _dn_gates(jnp.dot(xb, w_ref[:, conv_dim + v_w:conv_dim + v_w + LANES],
                                           preferred_element_type=F32), alog_ref, dtb_ref)
        taps = [cw_ref[s:s + 1, cols] for s in range(DN_CONV_W)]
        prev_rolled = rolled(jnp.where(first, 0.0, halo_ref[:, cols]))
        for r0 in range(0, tm, SUBLANES):
            cur = acc[r0:r0 + SUBLANES]
            cur_rolled = rolled(cur)
            back = [jnp.where(row < d, p, q) for d, p, q in zip(backs, prev_rolled, cur_rolled)]
            conv = back[DN_CONV_W - 2] * taps[0]
            for s in range(1, DN_CONV_W - 1):
                conv = conv + back[DN_CONV_W - 2 - s] * taps[s]
            qkv_ref[r0:r0 + SUBLANES, cols] = _silu(conv + cur * taps[DN_CONV_W - 1])
            prev_rolled = cur_rolled
        last = acc[tm - SUBLANES:tm]
        halo_ref[:, cols] = last
        tail_ref[:, cols] = last


def _dn_in_conv(x, mod, seq, w, conv_w, alog, dtb, *, tm, cb):
    m, k = x.shape
    conv_dim = conv_w.shape[1]
    v_w = DN_HV * DN_DV
    assert seq % tm == 0 and m % seq == 0
    tiles_per_seq = seq // tm
    const = lambda i: (0, 0)
    rows = lambda i: (i, 0)
    return pl.pallas_call(
        functools.partial(_dn_in_conv_body, tiles_per_seq=tiles_per_seq, cb=cb),
        grid=(m // tm,),
        in_specs=[pl.BlockSpec((tm, k), rows),
                  pl.BlockSpec(w.shape, const, pipeline_mode=pl.Buffered(1)),
                  pl.BlockSpec((None, 1, mod.shape[-1]), lambda i: (i // tiles_per_seq, 0, 0)),
                  pl.BlockSpec(conv_w.shape, const), pl.BlockSpec((1, LANES), const),
                  pl.BlockSpec((1, LANES), const)],
        out_specs=[pl.BlockSpec((tm, conv_dim), rows), pl.BlockSpec((tm, v_w), rows),
                   pl.BlockSpec((tm, LANES), rows),
                   pl.BlockSpec((None, SUBLANES, conv_dim), lambda i: (i // tiles_per_seq, 0, 0))],
        out_shape=[jax.ShapeDtypeStruct((m, conv_dim), F32), jax.ShapeDtypeStruct((m, v_w), F32),
                   jax.ShapeDtypeStruct((m, LANES), F32),
                   jax.ShapeDtypeStruct((m // seq, SUBLANES, conv_dim), F32)],
        scratch_shapes=[pltpu.VMEM((SUBLANES, conv_dim), F32)],
        compiler_params=pltpu.CompilerParams(dimension_semantics=("arbitrary",), vmem_limit_bytes=VMEM_LIMIT),
        name="dn_in_conv_p",
    )(x, w, mod, conv_w, alog, dtb)


def _gated_rms(o, ng, z):
    return o * lax.rsqrt(jnp.mean(o * o, -1, keepdims=True) + RMS_EPS) * ng * _silu(z)


def _dn_chunk_body(x_ref, z_ref, gb_ref, ng_ref, o_ref, s_ref):
    ci = pl.program_id(1)
    c = x_ref.shape[0]
    hb = DN_HV

    @pl.when(ci == 0)
    def _():
        s_ref[...] = jnp.zeros_like(s_ref)

    k_w = DN_HK * DN_DK
    gb = gb_ref[...]
    row = lax.broadcasted_iota(jnp.int32, gb.shape, 0)
    gc = gb
    sh = 1
    while sh < c:
        gc = gc + jnp.where(row >= sh, pltpu.roll(gc, sh, axis=0), 0.0)
        sh *= 2
    gct = gc.T
    ri = lax.broadcasted_iota(jnp.int32, (c, c), 0)
    cj = lax.broadcasted_iota(jnp.int32, (c, c), 1)
    tril = ri >= cj
    strict = ri > cj
    merge_masks = []
    k = 0
    while (1 << k) < c:
        merge_masks.append(((ri >> (k + 1)) == (cj >> (k + 1))) & ((ri >> k) != (cj >> k)))
        k += 1
    ng = ng_ref[...]
    heads = range(hb)
    dot = functools.partial(jnp.dot, preferred_element_type=F32)
    def l2n(y, scale):
        return y * lax.rsqrt(jnp.sum(y * y, -1, keepdims=True) + RMS_EPS) * scale

    qh = [l2n(x_ref[:, i * DN_DK:(i + 1) * DN_DK], DN_DK ** -0.5) for i in range(DN_HK)]
    kh = [l2n(x_ref[:, k_w + i * DN_DK:k_w + (i + 1) * DN_DK], 1.0) for i in range(DN_HK)]
    kk = [lax.dot_general(a, a, _NT, preferred_element_type=F32) for a in kh]
    qk = [lax.dot_general(a, b, _NT, preferred_element_type=F32) for a, b in zip(qh, kh)]
    gcol = [gc[:, j:j + 1] for j in heads]
    bcol = [gb[:, hb + j:hb + j + 1] for j in heads]
    decay = [jnp.where(tril, jnp.exp(jnp.where(tril, gcol[j] - gct[j:j + 1, :], 0.0)), 0.0) for j in heads]
    lmat = [jnp.where(strict, bcol[j] * kk[j // DN_REP] * decay[j], 0.0) for j in heads]
    def odd_rows(a, blk):
        return jnp.concatenate([a[r:r + blk] for r in range(blk, c, 2 * blk)], axis=0)

    def spread_odd(a, blk):
        zero = jnp.zeros((blk, a.shape[1]), a.dtype)
        return jnp.concatenate([piece for t in range(c // (2 * blk))
                                for piece in (zero, a[t * blk:(t + 1) * blk])], axis=0)

    minv = [jnp.where(merge_masks[0], -lmat[j], 0.0) for j in heads]
    for lvl in range(1, len(merge_masks)):
        blk = 1 << lvl
        if blk < SUBLANES:
            loff = [jnp.where(merge_masks[lvl], lmat[j], 0.0) for j in heads]
            y = [loff[j] + dot(loff[j], minv[j]) for j in heads]
            minv = [minv[j] - y[j] - dot(minv[j], y[j]) for j in heads]
        else:
            rg = lax.broadcasted_iota(jnp.int32, (c // 2, c), 0)
            ro = ((rg >> lvl) << (lvl + 1)) + blk + (rg & (blk - 1))
            co = lax.broadcasted_iota(jnp.int32, (c // 2, c), 1)
            mask = ((ro >> (lvl + 1)) == (co >> (lvl + 1))) & ((ro >> lvl) != (co >> lvl))
            loff = [jnp.where(mask, odd_rows(lmat[j], blk), 0.0) for j in heads]
            y = [loff[j] + dot(loff[j], minv[j]) for j in heads]
            upd = [y[j] + dot(odd_rows(minv[j], blk), spread_odd(y[j], blk)) for j in heads]
            minv = [minv[j] - spread_odd(upd[j], blk) for j in heads]
    eg = [jnp.exp(gcol[j]) for j in heads]
    rhs = [jnp.concatenate([x_ref[:, 2 * k_w + j * DN_DV:2 * k_w + (j + 1) * DN_DV] * bcol[j],
                            kh[j // DN_REP] * (bcol[j] * eg[j])], axis=1) for j in heads]
    sol = [rhs[j] + dot(minv[j], rhs[j]) for j in heads]
    s_old = [s_ref[j] for j in heads]
    ws = [dot(jnp.concatenate([sol[j][:, DN_DV:], qh[j // DN_REP] * eg[j]], axis=0), s_old[j]) for j in heads]
    v_new = [sol[j][:, :DN_DV] - ws[j][:c] for j in heads]
    o = [ws[j][c:] + dot(qk[j // DN_REP] * decay[j], v_new[j]) for j in heads]
    for j in heads:
        glast = gc[c - 1:c, j:j + 1]
        kd = kh[j // DN_REP] * jnp.exp(glast - gcol[j])
        s_ref[j] = s_old[j] * jnp.exp(glast) + lax.dot_general(kd, v_new[j], _TN, preferred_element_type=F32)
        o_ref[:, j * DN_DV:(j + 1) * DN_DV] = _gated_rms(
            o[j], ng, z_ref[:, j * DN_DV:(j + 1) * DN_DV]).astype(o_ref.dtype)


def _dn_chunks(qkv, z, gates, norm_g):
    b, t, cd = qkv.shape
    c = DN_CHUNK
    assert t % c == 0
    v_w = DN_HV * DN_DV
    o, s = pl.pallas_call(
        _dn_chunk_body,
        grid=(b, t // c),
        in_specs=[pl.BlockSpec((None, c, cd), lambda bi, ci: (bi, ci, 0)),
                  pl.BlockSpec((None, c, v_w), lambda bi, ci: (bi, ci, 0)),
                  pl.BlockSpec((None, c, LANES), lambda bi, ci: (bi, ci, 0)),
                  pl.BlockSpec((1, DN_DV), lambda bi, ci: (0, 0))],
        out_specs=[pl.BlockSpec((None, c, v_w), lambda bi, ci: (bi, ci, 0)),
                   pl.BlockSpec((None, DN_HV, DN_DK, DN_DV), lambda bi, ci: (bi, 0, 0, 0))],
        out_shape=[jax.ShapeDtypeStruct((b, t, v_w), BF16),
                   jax.ShapeDtypeStruct((b, DN_HV, DN_DK, DN_DV), F32)],
        compiler_params=pltpu.CompilerParams(dimension_semantics=("parallel", "arbitrary"),
                                             vmem_limit_bytes=VMEM_LIMIT),
        name="dn_chunks",
    )(qkv, z, gates, norm_g.reshape(1, DN_DV))
    return o, s


def _dn_step_prep_body(x_ref, prev_ref, w_ref, o_ref, cv_ref, *, n_q_blk, n_k_blk):
    c = pl.program_id(0)
    x = x_ref[...]
    w = w_ref[...]
    conv = prev_ref[0] * w[0:1]
    for s in range(1, DN_CONV_W - 1):
        conv = conv + prev_ref[s] * w[s:s + 1]
    conv = conv + x * w[DN_CONV_W - 1:DN_CONV_W]
    y = _silu(conv)
    for s in range(DN_CONV_W - 2):
        cv_ref[s] = prev_ref[s + 1]
    cv_ref[DN_CONV_W - 2] = x

    @pl.when(c >= n_q_blk + n_k_blk)
    def _():
        o_ref[...] = y

    @pl.when(c < n_q_blk + n_k_blk)
    def _():
        _l2norm_heads(y, o_ref, jnp.where(c < n_q_blk, DN_DK ** -0.5, 1.0))


def _dn_step_prep(qkv, conv_prev_t, conv_w, *, cb):
    db, c = qkv.shape
    k_w = DN_HK * DN_DK
    nw = DN_CONV_W - 1
    return pl.pallas_call(
        functools.partial(_dn_step_prep_body, n_q_blk=k_w // cb, n_k_blk=k_w // cb),
        grid=(c // cb,),
        in_specs=[pl.BlockSpec((db, cb), lambda ci: (0, ci)),
                  pl.BlockSpec((nw, db, cb), lambda ci: (0, 0, ci)),
                  pl.BlockSpec((DN_CONV_W, cb), lambda ci: (0, ci))],
        out_specs=[pl.BlockSpec((db, cb), lambda ci: (0, ci)),
                   pl.BlockSpec((nw, db, cb), lambda ci: (0, 0, ci))],
        out_shape=[jax.ShapeDtypeStruct((db, c), F32), jax.ShapeDtypeStruct((nw, db, c), F32)],
        compiler_params=pltpu.CompilerParams(dimension_semantics=("parallel",), vmem_limit_bytes=VMEM_LIMIT),
        name="dn_step_prep",
    )(qkv, conv_prev_t, conv_w)


def _dn_step_body(q_ref, k_ref, v_ref, z_ref, gb_ref, ng_ref, s_ref, o_ref, so_ref, *, n_pass):
    g = pl.program_id(0)

    if n_pass:
        @pl.when(g < n_pass)
        def _():
            so_ref[...] = s_ref[...]

    @pl.when(g == n_pass)
    def _():
        ng = ng_ref[...]
        rows = lax.broadcasted_iota(jnp.int32, (SUBLANES, DN_DK), 0)
        heads = [(t, h) for t in range(q_ref.shape[0]) for h in range(DN_HV)]
        s_old = [s_ref[t, h] for t, h in heads]
        kh = [k_ref[t, h // DN_REP:h // DN_REP + 1, :] for t, h in heads]
        qh = [q_ref[t, h // DN_REP:h // DN_REP + 1, :] for t, h in heads]
        ks_qs = [jnp.dot(jnp.where(rows == 0, kh[n], jnp.where(rows == 1, qh[n], 0.0)), s_old[n],
                         preferred_element_type=F32) for n in range(len(heads))]
        eg = [jnp.exp(gb_ref[t, :, h:h + 1]) for t, h in heads]
        v_new = [gb_ref[t, :, DN_HV + h:DN_HV + h + 1] * (v_ref[t, h:h + 1, :] - eg[n] * ks_qs[n][0:1])
                 for n, (t, h) in enumerate(heads)]
        outer = [lax.dot_general(jnp.where(rows == 0, kh[n], 0.0), jnp.where(rows == 0, v_new[n], 0.0), _TN,
                                 preferred_element_type=F32) for n in range(len(heads))]
        for n, (t, h) in enumerate(heads):
            so_ref[t, h] = s_old[n] * eg[n] + outer[n]
            o = eg[n] * ks_qs[n][1:2] + jnp.sum(qh[n] * kh[n], -1, keepdims=True) * v_new[n]
            o_ref[t, h:h + 1, :] = _gated_rms(o, ng, z_ref[t, h:h + 1, :])


def _dn_step(q, k, v, z, gates, norm_g, s_all, layer, *, in_place):
    n, db = s_all.shape[:2]
    bs = DN_STEP_SEQS
    assert db % bs == 0
    n_pass = 0 if in_place else n - 1

    def s_layer(g):
        return layer if in_place else (layer + 1 + g) % n

    def row_spec(rows, width):
        return pl.BlockSpec((bs, rows, width), lambda g, b: (jnp.where(g == n_pass, b, 0), 0, 0))

    s_spec = pl.BlockSpec((None, bs, DN_HV, DN_DK, DN_DV), lambda g, b: (s_layer(g), b, 0, 0, 0))
    return pl.pallas_call(
        functools.partial(_dn_step_body, n_pass=n_pass),
        grid=(n_pass + 1, db // bs),
        in_specs=[row_spec(DN_HK, DN_DK), row_spec(DN_HK, DN_DK), row_spec(DN_HV, DN_DV),
                  row_spec(DN_HV, DN_DV), row_spec(1, LANES),
                  pl.BlockSpec((1, DN_DV), lambda g, b: (0, 0)), s_spec],
        out_specs=[row_spec(DN_HV, DN_DV), s_spec],
        out_shape=[jax.ShapeDtypeStruct((db, DN_HV, DN_DV), F32), jax.ShapeDtypeStruct(s_all.shape, F32)],
        input_output_aliases={6: 1} if in_place else {},
        compiler_params=pltpu.CompilerParams(dimension_semantics=("arbitrary", "arbitrary"),
                                             vmem_limit_bytes=VMEM_LIMIT),
        name="dn_step",
    )(q, k, v, z, gates, norm_g.reshape(1, DN_DV), s_all)


def _dn_layer(xp, xs, mod_p, mod_s, seq, conv_prev, s_all, layer, w_in, conv_w, a_log, dt_bias, norm_g, w_out,
              ln_g, ln_b, alpha):
    mp, d = xp.shape
    bp = mp // seq
    db = xs.shape[0]
    conv_dim = 2 * DN_HK * DN_DK + DN_HV * DN_DV
    v_w = DN_HV * DN_DV
    w_all = jnp.pad(w_in, ((0, 0), (0, LANES - 2 * DN_HV))).astype(BF16)
    w_o = w_out.astype(BF16)
    alog = jnp.pad(a_log, (0, LANES - DN_HV)).reshape(1, LANES)
    dtb = jnp.pad(dt_bias, (0, LANES - DN_HV)).reshape(1, LANES)
    lane_row = pl.BlockSpec((1, LANES), lambda i: (0, 0))
    segments = [conv_dim, v_w, (LANES, _dn_gates, [(alog, lane_row), (dtb, lane_row)])]

    qkvc, z, gates, tail = _dn_in_conv(xp, mod_p, seq, w_all, conv_w, alog, dtb, tm=PROJ_TM, cb=DN_CONV_COLS)
    cv_p = tail[:, SUBLANES - (DN_CONV_W - 1):]
    o, s_p = _dn_chunks(qkvc.reshape(bp, seq, conv_dim), z.reshape(bp, seq, v_w), gates.reshape(bp, seq, LANES), norm_g)
    xp_new = _out_proj_norm(o.reshape(mp, v_w), w_o, xp, mod_p, seq, ln_g, ln_b, tm=PROJ_TM, alpha=alpha,
                            name="dn_out_p")

    qkv_s, z_s, gates_s = _in_proj(xs, mod_s, 1, w_all, segments, tm=db, name="dn_in_s")
    qkvc_s, cv_t = _dn_step_prep(qkv_s, jnp.swapaxes(conv_prev, 0, 1), conv_w, cb=512)
    k_w = DN_HK * DN_DK
    o_s, s_s = _dn_step(qkvc_s[:, :k_w].reshape(db, DN_HK, DN_DK),
                        qkvc_s[:, k_w:2 * k_w].reshape(db, DN_HK, DN_DK),
                        qkvc_s[:, 2 * k_w:].reshape(db, DN_HV, DN_DV),
                        z_s.reshape(db, DN_HV, DN_DV), gates_s.reshape(db, 1, LANES), norm_g, s_all, layer,
                        in_place=layer > 0)
    xs_new = _out_proj_norm(o_s.reshape(db, v_w), w_o, xs, mod_s, 1, ln_g, ln_b, tm=db, alpha=alpha,
                            name="dn_out_s")
    return xp_new, xs_new, s_p, cv_p, s_s, jnp.swapaxes(cv_t, 0, 1)


def _diff_lambda(lq1_ref, lk1_ref, lq2_ref, lk2_ref, lam_init):
    e1 = jnp.exp(jnp.sum(lq1_ref[...] * lk1_ref[...], -1, keepdims=True))
    e2 = jnp.exp(jnp.sum(lq2_ref[...] * lk2_ref[...], -1, keepdims=True))
    return e1 - e2 + lam_init


def _diff_head_out(o, sg, z, lam_init):
    on = o * lax.rsqrt(jnp.mean(o * o, -1, keepdims=True) + RMS_EPS) * sg * (1.0 - lam_init)
    return on * _silu(z)


def _diff_flash_body(qi_ref, kj_ref, q_ref, k_ref, v_ref, z_ref, lq1_ref, lk1_ref, lq2_ref, lk2_ref, sg_ref,
                     o_ref, m_ref, l_ref, acc_ref, *, lam_init):
    i = qi_ref[pl.program_id(2)]
    j = kj_ref[pl.program_id(2)]
    tq, tk = q_ref.shape[0], k_ref.shape[0]
    gw = DIFF_G * tq

    @pl.when(j == 0)
    def _():
        m_ref[...] = jnp.full_like(m_ref, -jnp.inf)
        l_ref[...] = jnp.zeros_like(l_ref)
        acc_ref[...] = jnp.zeros_like(acc_ref)

    def update(masked):
        v_all = v_ref[...].astype(BF16)
        if masked:
            keep = lax.broadcasted_iota(jnp.int32, (tk, tq), 0) <= lax.broadcasted_iota(jnp.int32, (tk, tq), 1)
        qc = FLASH_QUERY_COLS
        groups = [(half, g, q0) for half in range(2) for g in range(DIFF_G) for q0 in range(0, tq, qc)]
        n_keys = [min(q0 + qc, tk) if masked else tk for _, _, q0 in groups]
        sts = []
        for (half, g, q0), nk in zip(groups, n_keys):
            qm = q_ref[q0:q0 + qc, (g * 2 + half) * DIFF_HD:(g * 2 + half + 1) * DIFF_HD] * (
                DIFF_HD ** -0.5 * LOG2E)
            st = lax.dot_general(k_ref[0:nk, half * DIFF_HD:(half + 1) * DIFF_HD], qm, _NT,
                                 preferred_element_type=F32)
            sts.append(jnp.where(keep[0:nk, q0:q0 + qc], st, -jnp.inf) if masked else st)
        for n, (half, g, q0) in enumerate(groups):
            st = sts[n]
            v = v_all[0:n_keys[n]]
            cols = slice(half * gw + g * tq + q0, half * gw + g * tq + q0 + qc)
            m_prev = m_ref[:, cols]
            m_new = jnp.maximum(m_prev, jnp.max(st, 0, keepdims=True))
            alpha = jnp.exp2(m_prev - m_new)
            p = jnp.exp2(st - m_new)
            l_ref[:, cols] = alpha * l_ref[:, cols] + jnp.sum(p, 0, keepdims=True)
            acc_ref[:, cols] = alpha * acc_ref[:, cols] + lax.dot_general(
                v, p.astype(BF16), _TN, preferred_element_type=F32)
            m_ref[:, cols] = m_new

    @pl.when(j < i)
    def _():
        update(False)

    @pl.when(j == i)
    def _():
        update(True)
        lam = _diff_lambda(lq1_ref, lk1_ref, lq2_ref, lk2_ref, lam_init)
        sg = sg_ref[...]
        for g in range(DIFF_G):
            c0 = slice(g * tq, (g + 1) * tq)
            c1 = slice(gw + g * tq, gw + (g + 1) * tq)
            ot = acc_ref[:, c0] * (1.0 / l_ref[:, c0]) - lam * (acc_ref[:, c1] * (1.0 / l_ref[:, c1]))
            o_ref[:, g * DIFF_VD:(g + 1) * DIFF_VD] = _diff_head_out(
                ot.T, sg, z_ref[:, g * DIFF_VD:(g + 1) * DIFF_VD], lam_init).astype(o_ref.dtype)


def _diff_flash(q, k, v, z, lams, subln_g, lam_init, *, tq):
    b, t, _ = q.shape
    assert t % tq == 0 and tq & (tq - 1) == 0, (t, tq)
    nq = t // tq
    qw = DIFF_G * 2 * DIFF_HD
    ow = DIFF_G * DIFF_VD
    n_sub = 2 * DIFF_G
    pairs = [(i, j) for i in range(nq) for j in range(i + 1)]
    qi = jnp.asarray([p[0] for p in pairs], jnp.int32)
    kj = jnp.asarray([p[1] for p in pairs], jnp.int32)
    vec = pl.BlockSpec((1, DIFF_HD), lambda bi, h, s, qi, kj: (0, 0))
    grid_spec = pltpu.PrefetchScalarGridSpec(
        num_scalar_prefetch=2,
        grid=(b, DIFF_KVH, len(pairs)),
        in_specs=[pl.BlockSpec((None, tq, qw), lambda bi, h, s, qi, kj: (bi, qi[s], h)),
                  pl.BlockSpec((None, tq, 2 * DIFF_HD), lambda bi, h, s, qi, kj: (bi, kj[s], h)),
                  pl.BlockSpec((None, tq, DIFF_VD), lambda bi, h, s, qi, kj: (bi, kj[s], h)),
                  pl.BlockSpec((None, tq, ow), lambda bi, h, s, qi, kj: (bi, qi[s], h)),
                  vec, vec, vec, vec,
                  pl.BlockSpec((1, DIFF_VD), lambda bi, h, s, qi, kj: (0, 0))],
        out_specs=pl.BlockSpec((None, tq, ow), lambda bi, h, s, qi, kj: (bi, qi[s], h)),
        scratch_shapes=[pltpu.VMEM((1, n_sub * tq), F32), pltpu.VMEM((1, n_sub * tq), F32),
                        pltpu.VMEM((DIFF_VD, n_sub * tq), F32)],
    )
    return pl.pallas_call(
        functools.partial(_diff_flash_body, lam_init=lam_init),
        grid_spec=grid_spec,
        out_shape=jax.ShapeDtypeStruct((b, t, DIFF_H * DIFF_VD), BF16),
        compiler_params=pltpu.CompilerParams(
            dimension_semantics=("parallel", "parallel", "arbitrary"),
            vmem_limit_bytes=VMEM_LIMIT),
        name="diff_flash",
    )(qi, kj, q, k, v, z, *lams, subln_g.reshape(1, DIFF_VD))


def _diff_decode_body(pt_ref, q_ref, ks_ref, vs_ref, z_ref, lq1_ref, lk1_ref, lq2_ref, lk2_ref, sg_ref,
                      *rest, pp, lam_init):
    del pt_ref
    k_pages = rest[:pp]
    v_pages = rest[pp:2 * pp]
    o_ref = rest[2 * pp]
    qb_ref, m_ref, l_ref, acc_ref = rest[2 * pp + 1:]
    j = pl.program_id(1)
    n_rows = DIFF_KVH * DIFF_G * 2
    head_shift = (DIFF_G * 2).bit_length() - 1
    kw = 2 * DIFF_HD
    page_rows = k_pages[0].shape[0]

    @pl.when(j == 0)
    def _():
        d_i = lax.broadcasted_iota(jnp.int32, (DIFF_HD, kw), 0)
        c_i = lax.broadcasted_iota(jnp.int32, (DIFF_HD, kw), 1)
        spread = jnp.where((c_i & (DIFF_HD - 1)) == d_i, 1.0, 0.0)
        qt = jnp.dot(q_ref[...], spread, preferred_element_type=F32)
        r_i = lax.broadcasted_iota(jnp.int32, (n_rows, kw), 0)
        half = lax.broadcasted_iota(jnp.int32, (n_rows, kw), 1) >> (DIFF_HD.bit_length() - 1)
        qb_ref[...] = jnp.where(half == (r_i & 1), qt, 0.0) * (DIFF_HD ** -0.5)
        m_ref[...] = jnp.full_like(m_ref, -jnp.inf)
        l_ref[...] = jnp.zeros_like(l_ref)
        acc_ref[...] = jnp.zeros_like(acc_ref)

    qb = qb_ref[...]
    s = jnp.concatenate([lax.dot_general(qb, kp[...], _NT, preferred_element_type=F32) for kp in k_pages],
                        axis=1)
    row_head = lax.broadcasted_iota(jnp.int32, s.shape, 0) >> head_shift
    col_head = lax.broadcasted_iota(jnp.int32, s.shape, 1) & (DIFF_KVH - 1)
    s = jnp.where(row_head == col_head, s, -jnp.inf)
    m_prev = m_ref[...]
    m_new = jnp.maximum(m_prev, jnp.max(s, -1, keepdims=True))
    alpha = jnp.exp(m_prev - m_new)
    p = jnp.exp(s - m_new)
    l_new = alpha * l_ref[...] + jnp.sum(p, -1, keepdims=True)
    pv = jnp.dot(p[:, 0:page_rows], v_pages[0][...], preferred_element_type=F32)
    for t in range(1, pp):
        pv = pv + jnp.dot(p[:, t * page_rows:(t + 1) * page_rows], v_pages[t][...],
                          preferred_element_type=F32)
    acc_new = alpha * acc_ref[...] + pv
    m_ref[...] = m_new
    l_ref[...] = l_new
    acc_ref[...] = acc_new

    @pl.when(j == pl.num_programs(1) - 1)
    def _():
        rh = lax.broadcasted_iota(jnp.int32, (n_rows, kw), 0) >> head_shift
        ks = jnp.zeros((n_rows, kw), F32)
        vs = jnp.zeros((n_rows, kw), F32)
        for h in range(DIFF_KVH):
            ks = jnp.where(rh == h, ks_ref[h:h + 1, :], ks)
            vs = jnp.where(rh == h, vs_ref[h:h + 1, :], vs)
        s_self = jnp.sum(qb * ks, -1, keepdims=True)
        m_fin = jnp.maximum(m_new, s_self)
        a = jnp.exp(m_new - m_fin)
        p_self = jnp.exp(s_self - m_fin)
        l_fin = a * l_new + p_self
        acc = (a * acc_new + p_self * vs) / l_fin
        lam = _diff_lambda(lq1_ref, lk1_ref, lq2_ref, lk2_ref, lam_init)
        sg = sg_ref[...]
        for hh in range(DIFF_H):
            o = acc[2 * hh:2 * hh + 1] - lam * acc[2 * hh + 1:2 * hh + 2]
            o_ref[hh:hh + 1, :] = _diff_head_out(o, sg, z_ref[hh:hh + 1, :], lam_init)


def _diff_decode(q, k_self, v_self, z, cache_k, cache_v, layer, page_table, lams, subln_g, lam_init, *, pp):
    db = q.shape[0]
    n_pages = page_table.shape[1]
    assert n_pages % pp == 0
    page_rows, kw = cache_k.shape[2:]
    n_rows = DIFF_KVH * DIFF_G * 2

    def page_spec(t):
        return pl.BlockSpec((None, None, page_rows, kw), lambda b, j, pt: (layer, pt[b, j * pp + t], 0, 0))

    def seq_spec(shape):
        return pl.BlockSpec((None,) + shape, lambda b, j, pt: (b, 0, 0))

    vec = pl.BlockSpec((1, DIFF_HD), lambda b, j, pt: (0, 0))
    grid_spec = pltpu.PrefetchScalarGridSpec(
        num_scalar_prefetch=1,
        grid=(db, n_pages // pp),
        in_specs=[seq_spec((n_rows, DIFF_HD)), seq_spec((DIFF_KVH, kw)), seq_spec((DIFF_KVH, kw)),
                  seq_spec((DIFF_H, DIFF_VD)), vec, vec, vec, vec,
                  pl.BlockSpec((1, DIFF_VD), lambda b, j, pt: (0, 0))]
                 + [page_spec(t) for t in range(pp)] + [page_spec(t) for t in range(pp)],
        out_specs=seq_spec((DIFF_H, DIFF_VD)),
        scratch_shapes=[pltpu.VMEM((n_rows, kw), F32), pltpu.VMEM((n_rows, 1), F32),
                        pltpu.VMEM((n_rows, 1), F32), pltpu.VMEM((n_rows, DIFF_VD), F32)],
    )
    return pl.pallas_call(
        functools.partial(_diff_decode_body, pp=pp, lam_init=lam_init),
        grid_spec=grid_spec,
        out_shape=jax.ShapeDtypeStruct((db, DIFF_H, DIFF_VD), F32),
        compiler_params=pltpu.CompilerParams(dimension_semantics=("parallel", "arbitrary"),
                                             vmem_limit_bytes=VMEM_LIMIT),
        name="diff_decode",
    )(page_table, q, k_self, v_self, z, *lams, subln_g.reshape(1, DIFF_VD),
      *([cache_k] * pp), *([cache_v] * pp))


def _diff_layer(xp, xs, mod_p, mod_s, seq, layer_idx, j, cache_k, cache_v, page_table, w_in, lams, subln_g,
                w_out, ln_g, ln_b, alpha):
    mp, d = xp.shape
    bp = mp // seq
    db = xs.shape[0]
    lam_init = 0.8 - 0.6 * math.exp(-0.3 * layer_idx)
    q_w = DIFF_H * 2 * DIFF_HD
    k_w = DIFF_KVH * 2 * DIFF_HD
    v_w = DIFF_KVH * DIFF_VD
    w_all = w_in.astype(BF16)
    w_o = w_out.astype(BF16)
    lams = [a.reshape(1, DIFF_HD) for a in lams]
    segments = [q_w, k_w, v_w, w_in.shape[1] - q_w - k_w - v_w]

    q, k, v, z = _in_proj(xp, mod_p, seq, w_all, segments, tm=PROJ_TM, name="diff_in_p")
    o = _diff_flash(q.reshape(bp, seq, q_w), k.reshape(bp, seq, k_w), v.reshape(bp, seq, v_w),
                    z.reshape(bp, seq, -1), lams, subln_g, lam_init, tq=min(DIFF_TQ, seq))
    xp_new = _out_proj_norm(o.reshape(mp, -1), w_o, xp, mod_p, seq, ln_g, ln_b, tm=PROJ_TM, alpha=alpha,
                            name="diff_out_p")

    q_s, k_s, v_s, z_s = _in_proj(xs, mod_s, 1, w_all, segments, tm=db, name="diff_in_s")
    n_l, n_pool, page = cache_k.shape[:3]
    o_s = _diff_decode(q_s.reshape(db, DIFF_KVH * DIFF_G * 2, DIFF_HD), k_s.reshape(db, DIFF_KVH, 2 * DIFF_HD),
                       v_s.reshape(db, DIFF_KVH, DIFF_VD), z_s.reshape(db, DIFF_H, DIFF_VD),
                       cache_k.reshape(n_l, n_pool, page * DIFF_KVH, 2 * DIFF_HD),
                       cache_v.reshape(n_l, n_pool, page * DIFF_KVH, DIFF_VD),
                       j, page_table, lams, subln_g, lam_init, pp=DIFF_PAGES_PER_STEP)
    xs_new = _out_proj_norm(o_s.reshape(db, -1), w_o, xs, mod_s, 1, ln_g, ln_b, tm=db, alpha=alpha,
                            name="diff_out_s")
    return (xp_new, xs_new, k.reshape(bp, seq, DIFF_KVH, 2 * DIFF_HD), v.reshape(bp, seq, DIFF_KVH, DIFF_VD),
            k_s.reshape(db, 1, DIFF_KVH, 2 * DIFF_HD), v_s.reshape(db, 1, DIFF_KVH, DIFF_VD))


def _swa_band_body(q_ref, kc_ref, kp_ref, vc_ref, vp_ref, z_ref, sink_ref, o_ref):
    i = pl.program_id(1)
    blk = q_ref.shape[0]
    kj = lax.broadcasted_iota(jnp.int32, (2 * blk, 2 * blk), 0)
    qi = lax.broadcasted_iota(jnp.int32, (2 * blk, 2 * blk), 1) & (blk - 1)
    dist = qi + blk - kj
    keep = (dist >= 0) & (dist <= WINDOW) & ((i > 0) | (kj >= blk))
    chains = [(h, g) for h in range(SWA_KVH) for g in range(0, SWA_G, 2)]
    k2 = [jnp.concatenate([kp_ref[:, h * SWA_HD:(h + 1) * SWA_HD], kc_ref[:, h * SWA_HD:(h + 1) * SWA_HD]],
                          axis=0) for h in range(SWA_KVH)]
    v2 = [jnp.concatenate([vp_ref[:, h * SWA_HD:(h + 1) * SWA_HD], vc_ref[:, h * SWA_HD:(h + 1) * SWA_HD]],
                          axis=0) for h in range(SWA_KVH)]
    sts = []
    for h, g in chains:
        c0 = (h * SWA_G + g) * SWA_HD
        qs = jnp.concatenate([q_ref[:, c0:c0 + SWA_HD], q_ref[:, c0 + SWA_HD:c0 + 2 * SWA_HD]],
                             axis=0) * (SWA_HD ** -0.5)
        st = lax.dot_general(k2[h], qs, _NT, preferred_element_type=F32)
        sts.append(jnp.where(keep, st, -jnp.inf))
    for n, (h, g) in enumerate(chains):
        hd = h * SWA_G + g
        sink = jnp.concatenate([jnp.broadcast_to(sink_ref[:, hd + t:hd + t + 1], (1, blk)) for t in range(2)],
                               axis=1)
        m = jnp.maximum(jnp.max(sts[n], 0, keepdims=True), sink)
        p = jnp.exp(sts[n] - m)
        den = jnp.sum(p, 0, keepdims=True) + jnp.exp(sink - m)
        ot = lax.dot_general(v2[h], p, _TN, preferred_element_type=F32) * (1.0 / den)
        cs = slice(hd * SWA_HD, (hd + 2) * SWA_HD)
        pair = jnp.concatenate([ot[:, 0:blk], ot[:, blk:2 * blk]], axis=0)
        o_ref[:, cs] = (pair.T * _silu(z_ref[:, cs])).astype(o_ref.dtype)


def _swa_band(q, k, v, z, sinks):
    b, t, qw = q.shape
    blk = WINDOW
    kw = SWA_KVH * SWA_HD
    cur = pl.BlockSpec((None, blk, kw), lambda bi, i: (bi, i, 0))
    prev = pl.BlockSpec((None, blk, kw), lambda bi, i: (bi, jnp.maximum(i - 1, 0), 0))
    wide = pl.BlockSpec((None, blk, qw), lambda bi, i: (bi, i, 0))
    return pl.pallas_call(
        _swa_band_body,
        grid=(b, t // blk),
        in_specs=[wide, cur, prev, cur, prev, wide, pl.BlockSpec((1, SWA_H), lambda bi, i: (0, 0))],
        out_specs=wide,
        out_shape=jax.ShapeDtypeStruct((b, t, qw), BF16),
        compiler_params=pltpu.CompilerParams(dimension_semantics=("parallel", "parallel"),
                                             vmem_limit_bytes=VMEM_LIMIT),
        name="swa_band",
    )(q, k, k, v, v, z, sinks.reshape(1, SWA_H))


def _swa_decode_body(q_ref, kb_ref, vb_ref, kn_ref, vn_ref, z_ref, sink_ref, o_ref, ko_ref, vo_ref):
    win = kb_ref.shape[1]
    for h in range(SWA_KVH):
        hs = slice(h * SWA_HD, (h + 1) * SWA_HD)
        gs = slice(h * SWA_G, (h + 1) * SWA_G)
        qh = q_ref[:, gs, :]
        s = jnp.einsum("bqd,bkd->bqk", qh, kb_ref[:, :, hs], preferred_element_type=F32) * (SWA_HD ** -0.5)
        s_new = jnp.sum(qh * kn_ref[:, :, hs], -1, keepdims=True) * (SWA_HD ** -0.5)
        sink = sink_ref[gs, :][None]
        m = jnp.maximum(jnp.maximum(jnp.max(s, -1, keepdims=True), s_new), sink)
        p = jnp.exp(s - m)
        p_new = jnp.exp(s_new - m)
        den = jnp.sum(p, -1, keepdims=True) + p_new + jnp.exp(sink - m)
        o = jnp.einsum("bqk,bkd->bqd", p, vb_ref[:, :, hs], preferred_element_type=F32)
        o = (o + p_new * vn_ref[:, :, hs]) / den
        o_ref[:, gs, :] = o * _silu(z_ref[:, gs, :])
    ko_ref[:, 0:win - 1, :] = kb_ref[:, 1:win, :]
    ko_ref[:, win - 1:win, :] = kn_ref[...]
    vo_ref[:, 0:win - 1, :] = vb_ref[:, 1:win, :]
    vo_ref[:, win - 1:win, :] = vn_ref[...]


def _swa_decode(q, k_new, v_new, z, buf_k, buf_v, sinks, *, bb):
    db, win, kw = buf_k.shape
    qz = pl.BlockSpec((bb, SWA_H, SWA_HD), lambda b: (b, 0, 0))
    buf = pl.BlockSpec((bb, win, kw), lambda b: (b, 0, 0))
    new = pl.BlockSpec((bb, 1, kw), lambda b: (b, 0, 0))
    return pl.pallas_call(
        _swa_decode_body,
        grid=(db // bb,),
        in_specs=[qz, buf, buf, new, new, qz, pl.BlockSpec((SWA_H, 1), lambda b: (0, 0))],
        out_specs=[qz, buf, buf],
        out_shape=[jax.ShapeDtypeStruct(q.shape, F32), jax.ShapeDtypeStruct(buf_k.shape, F32),
                   jax.ShapeDtypeStruct(buf_v.shape, F32)],
        compiler_params=pltpu.CompilerParams(dimension_semantics=("parallel",), vmem_limit_bytes=VMEM_LIMIT),
        name="swa_decode",
    )(q, buf_k, buf_v, k_new, v_new, z, sinks.reshape(SWA_H, 1))


def _swa_layer(xp, xs, mod_p, mod_s, seq, buf_k, buf_v, sinks, w_in, w_out, ln_g, ln_b, alpha):
    mp, d = xp.shape
    bp = mp // seq
    db = xs.shape[0]
    q_w = SWA_H * SWA_HD
    kv_w = SWA_KVH * SWA_HD
    w_all = w_in.astype(BF16)
    w_o = w_out.astype(BF16)
    segments = [q_w, kv_w, kv_w, w_in.shape[1] - q_w - 2 * kv_w]

    q, k, v, z = _in_proj(xp, mod_p, seq, w_all, segments, tm=PROJ_TM, name="swa_in_p")
    k = k.reshape(bp, seq, kv_w)
    v = v.reshape(bp, seq, kv_w)
    o = _swa_band(q.reshape(bp, seq, q_w), k, v, z.reshape(bp, seq, q_w), sinks)
    xp_new = _out_proj_norm(o.reshape(mp, q_w), w_o, xp, mod_p, seq, ln_g, ln_b, tm=PROJ_TM, alpha=alpha,
                            name="swa_out_p")

    q_s, k_s, v_s, z_s = _in_proj(xs, mod_s, 1, w_all, segments, tm=db, name="swa_in_s")
    win = buf_k.shape[1]
    o_s, k_buf, v_buf = _swa_decode(q_s.reshape(db, SWA_H, SWA_HD), k_s.reshape(db, 1, kv_w),
                                    v_s.reshape(db, 1, kv_w), z_s.reshape(db, SWA_H, SWA_HD),
                                    buf_k.reshape(db, win, kv_w), buf_v.reshape(db, win, kv_w), sinks,
                                    bb=SWA_DEC_BB)
    xs_new = _out_proj_norm(o_s.reshape(db, q_w), w_o, xs, mod_s, 1, ln_g, ln_b, tm=db, alpha=alpha,
                            name="swa_out_s")
    kv_shape = (SWA_KVH, SWA_HD)
    return (xp_new, xs_new, k[:, seq - WINDOW:].reshape((bp, WINDOW) + kv_shape),
            v[:, seq - WINDOW:].reshape((bp, WINDOW) + kv_shape),
            k_buf.reshape((db, win) + kv_shape), v_buf.reshape((db, win) + kv_shape))


def kernel(x_prompt, x_sample, cache_diff_k, cache_diff_v, page_table, cache_swa_k, cache_swa_v, state_dn_S, state_dn_conv, c_prompt, c_sample, ada_w, ada_b, ln_g, ln_b, dn_w_in, dn_conv_w, dn_a_log, dn_dt_bias, dn_norm_g, dn_w_out, diff_w_in, diff_lq1, diff_lk1, diff_lq2, diff_lk2, diff_subln_g, diff_w_out, swa_w_in, swa_sinks, swa_w_out):
    bp, seq, d = x_prompt.shape
    db, dec_seq, _ = x_sample.shape
    assert dec_seq == 1 and bp <= SUBLANES
    depth = ada_w.shape[0]
    alpha = (2 * depth) ** 0.25
    xp = x_prompt.reshape(bp * seq, d)
    xs = x_sample.reshape(db, d)

    c_all = jnp.concatenate([c_prompt, jnp.zeros((SUBLANES - bp, d), F32), c_sample], axis=0)
    st = {n: [] for n in ("dk_p", "dv_p", "dk_s", "dv_s", "sk_p", "sv_p", "sk_s", "sv_s",
                          "S_p", "cv_p", "cv_s")}
    s_s = state_dn_S
    for l in range(depth):
        j = l // N_MIXERS
        bias_op = (ada_b[l].reshape(1, 3 * d), pl.BlockSpec((1, 3 * d), lambda i: (0, 0)))
        mod, = _proj(c_all, ada_w[l].astype(BF16), [(3 * d, _add_bias, [bias_op])], tm=c_all.shape[0],
                     pro=_silu, name="adaln")
        mod_p = mod[:bp].reshape(bp, 1, 3 * d)
        mod_s = mod[SUBLANES:]
        if l % N_MIXERS == 0:
            xp, xs, s_p, cv_p, s_s, cv_s = _dn_layer(
                xp, xs, mod_p, mod_s, seq, state_dn_conv[j], s_s, j, dn_w_in[j], dn_conv_w[j],
                dn_a_log[j], dn_dt_bias[j], dn_norm_g[j], dn_w_out[j], ln_g[l], ln_b[l], alpha)
            st["S_p"].append(s_p)
            st["cv_p"].append(cv_p)
            st["cv_s"].append(cv_s)
        elif l % N_MIXERS == 1:
            xp, xs, k_p, v_p, k_s, v_s = _diff_layer(
                xp, xs, mod_p, mod_s, seq, l, j, cache_diff_k, cache_diff_v, page_table, diff_w_in[j],
                (diff_lq1[j], diff_lk1[j], diff_lq2[j], diff_lk2[j]), diff_subln_g[j], diff_w_out[j],
                ln_g[l], ln_b[l], alpha)
            st["dk_p"].append(k_p)
            st["dv_p"].append(v_p)
            st["dk_s"].append(k_s)
            st["dv_s"].append(v_s)
        else:
            xp, xs, k_p, v_p, k_s, v_s = _swa_layer(
                xp, xs, mod_p, mod_s, seq, cache_swa_k[j], cache_swa_v[j], swa_sinks[j], swa_w_in[j],
                swa_w_out[j], ln_g[l], ln_b[l], alpha)
            st["sk_p"].append(k_p)
            st["sv_p"].append(v_p)
            st["sk_s"].append(k_s)
            st["sv_s"].append(v_s)
    return (xp.reshape(bp, seq, d), xs.reshape(db, 1, d),
            jnp.stack(st["dk_p"]), jnp.stack(st["dv_p"]), jnp.stack(st["dk_s"]), jnp.stack(st["dv_s"]),
            jnp.stack(st["sk_p"]), jnp.stack(st["sv_p"]), jnp.stack(st["sk_s"]), jnp.stack(st["sv_s"]),
            jnp.stack(st["S_p"]), jnp.stack(st["cv_p"]), s_s, jnp.stack(st["cv_s"]))
```
